```python
import math
import jax, jax.numpy as jnp
from jax import lax
import numpy as np

D_MODEL = 1024
BATCH = 8
SEQ = 2048
DEPTH = 4

N_A_LAYERS = DEPTH // 2
N_B_LAYERS = DEPTH - N_A_LAYERS
D_FF = -(-8 * D_MODEL // (3 * 256)) * 256

ML_HEADS = 4
ML_DQK = D_MODEL // (2 * ML_HEADS)
ML_DV = D_MODEL // ML_HEADS
ML_CHUNK = 64
ML_PROJ = 2 * ML_HEADS * ML_DQK + 2 * ML_HEADS * ML_DV + 2 * ML_HEADS

NSA_HEADS = D_MODEL // 64
NSA_GROUPS = 4
NSA_HPG = NSA_HEADS // NSA_GROUPS
NSA_HD = 64
CMP_BLOCK = 32
CMP_STRIDE = 16
CMP_HIDDEN = 4 * NSA_HD
SLC_BLOCK = 64
SLC_TOPK = 8
WINDOW = 512
NSA_QBLOCK = 64
FORCE_SCORE = 1e4
NSA_Q_PROJ = NSA_HEADS * NSA_HD + 3 * NSA_HEADS
NSA_KV_PROJ = 6 * NSA_GROUPS * NSA_HD

REL_BUCKETS = 32
REL_MAX_DIST = 128

kernel_name = "hybrid_mlstm_nsa_yoco"


def rms_norm(x, g, eps=1e-6):
    xf = x.astype(jnp.float32)
    y = xf * lax.rsqrt(jnp.mean(xf * xf, axis=-1, keepdims=True) + eps)
    return (y * g.astype(jnp.float32)).astype(x.dtype)


def modulate(h, shift, scale):
    return h * (1 + scale[:, None, :]) + shift[:, None, :]


def swiglu(h, w_in, w_out):
    g, u = jnp.split(h @ w_in, 2, axis=-1)
    return (jax.nn.silu(g) * u) @ w_out


def masked_softmax(s, valid):
    s = jnp.where(valid, s.astype(jnp.float32), -jnp.inf)
    m = jnp.max(s, axis=-1, keepdims=True)
    m = jnp.where(jnp.isfinite(m), m, 0.0)
    e = jnp.exp(s - m)
    return e / jnp.maximum(jnp.sum(e, axis=-1, keepdims=True), jnp.finfo(jnp.float32).tiny)


def t5_bucket(dist):
    n = jnp.maximum(dist, 0)
    max_exact = REL_BUCKETS // 2
    nf = jnp.maximum(n, 1).astype(jnp.float32)
    large = max_exact + (jnp.log(nf / max_exact) / math.log(REL_MAX_DIST / max_exact)
                         * (REL_BUCKETS - max_exact)).astype(jnp.int32)
    large = jnp.minimum(large, REL_BUCKETS - 1)
    return jnp.where(n < max_exact, n, large)


def dense_bias(dist, rel_table):
    b = rel_table[t5_bucket(dist)]
    q, k = dist.shape
    return b.reshape(q, k, NSA_GROUPS, NSA_HPG).transpose(2, 3, 0, 1).astype(jnp.float32)


def mlstm_mixer(h, w_in, b_if, g_out, w_out):
    B, S, _ = h.shape
    NC = S // ML_CHUNK
    L = ML_CHUNK
    qk = ML_HEADS * ML_DQK
    vd = ML_HEADS * ML_DV
    proj = h @ w_in
    q, k, v, o, gif = jnp.split(proj, [qk, 2 * qk, 2 * qk + vd, 2 * qk + 2 * vd], axis=-1)
    gif = (gif + b_if).astype(jnp.float32)

    def heads(t, d):
        return t.reshape(B, NC, L, ML_HEADS, d).transpose(0, 3, 1, 2, 4)

    def gate_heads(t):
        return t.reshape(B, NC, L, ML_HEADS).transpose(0, 3, 1, 2)

    q = heads(q, ML_DQK) * (ML_DQK ** -0.5)
    k = heads(k, ML_DQK)
    v = heads(v, ML_DV)
    logi = gate_heads(gif[..., :ML_HEADS])
    logf = jax.nn.log_sigmoid(gate_heads(gif[..., ML_HEADS:]))
    b = jnp.cumsum(logf, axis=-1)
    b_last = b[..., -1]

    w_state = b_last[..., None] - b + logi
    m_loc = jnp.max(w_state, axis=-1)
    e_state = jnp.exp(w_state - m_loc[..., None])
    kv_loc = jnp.einsum('bhcs,bhcsd,bhcsv->bhcdv', e_state, k, v)
    n_loc = jnp.einsum('bhcs,bhcsd->bhcd', e_state, k)

    def step(carry, xs):
        C, n, m = carry
        bl, ml, kvl, nl = xs
        m_new = jnp.maximum(bl + m, ml)
        a = jnp.exp(bl + m - m_new)
        bb = jnp.exp(ml - m_new)
        C_new = a[..., None, None] * C + bb[..., None, None] * kvl
        n_new = a[..., None] * n + bb[..., None] * nl
        return (C_new, n_new, m_new), (C, n, m)

    init = (jnp.zeros((B, ML_HEADS, ML_DQK, ML_DV), jnp.float32),
            jnp.zeros((B, ML_HEADS, ML_DQK), jnp.float32),
            jnp.zeros((B, ML_HEADS), jnp.float32))
    xs = (jnp.moveaxis(b_last, 2, 0), jnp.moveaxis(m_loc, 2, 0),
          jnp.moveaxis(kv_loc, 2, 0).astype(jnp.float32), jnp.moveaxis(n_loc, 2, 0).astype(jnp.float32))
    _, (C_prev, n_prev, m_prev) = lax.scan(step, init, xs)
    C_prev = jnp.moveaxis(C_prev, 0, 2)
    n_prev = jnp.moveaxis(n_prev, 0, 2)
    m_prev = jnp.moveaxis(m_prev, 0, 2)

    causal = np.tril(np.ones((L, L), dtype=bool))
    d_intra = jnp.where(causal, b[..., :, None] - b[..., None, :] + logi[..., None, :], -jnp.inf)
    inter_log = b + m_prev[..., None]
    m = jnp.maximum(inter_log, jnp.max(d_intra, axis=-1))
    w_intra = jnp.exp(d_intra - m[..., None])
    w_inter = jnp.exp(inter_log - m)
    s = jnp.einsum('bhcjd,bhcsd->bhcjs', q, k) * w_intra
    num = jnp.einsum('bhcjs,bhcsv->bhcjv', s, v) \
        + w_inter[..., None] * jnp.einsum('bhcjd,bhcdv->bhcjv', q, C_prev)
    qn = jnp.sum(s, axis=-1) + w_inter * jnp.einsum('bhcjd,bhcd->bhcj', q, n_prev)
    denom = jnp.maximum(jnp.abs(qn), jnp.exp(-m))
    hout = num / denom[..., None]
    hout = hout.transpose(0, 2, 3, 1, 4).reshape(B, S, ML_HEADS, ML_DV)
    hout = rms_norm(hout, g_out.reshape(ML_HEADS, ML_DV)).reshape(B, S, vd)
    return (hout * jax.nn.sigmoid(o)) @ w_out


def nsa_shared_kv(h, w_kv, pos_cmp_k, w_cmp_k1, w_cmp_k2, pos_cmp_v, w_cmp_v1, w_cmp_v2, g_knorm):
    B, S, _ = h.shape
    G = NSA_GROUPS
    kv = (h @ w_kv).reshape(B, S, 6, G, NSA_HD).transpose(2, 0, 3, 1, 4)
    kc, vc, ks, vs, kw, vw = kv[0], kv[1], kv[2], kv[3], kv[4], kv[5]
    ncmp = (S - CMP_BLOCK) // CMP_STRIDE + 1
    idx = np.arange(ncmp)[:, None] * CMP_STRIDE + np.arange(CMP_BLOCK)[None, :]

    def compress(t, pos, w1, w2):
        blk = t[:, :, idx] + pos
        return jax.nn.gelu(blk.reshape(B, G, ncmp, CMP_BLOCK * NSA_HD) @ w1) @ w2

    kc_cmp = rms_norm(compress(kc, pos_cmp_k, w_cmp_k1, w_cmp_k2), g_knorm[0])
    vc_cmp = compress(vc, pos_cmp_v, w_cmp_v1, w_cmp_v2)
    nsb = S // SLC_BLOCK
    ks_blk = rms_norm(ks, g_knorm[1]).reshape(B, G, nsb, SLC_BLOCK, NSA_HD)
    vs_blk = vs.reshape(B, G, nsb, SLC_BLOCK, NSA_HD)
    pad = ((0, 0), (0, 0), (WINDOW, 0), (0, 0))
    kw_pad = jnp.pad(rms_norm(kw, g_knorm[2]), pad)
    vw_pad = jnp.pad(vw, pad)
    return kc_cmp, vc_cmp, ks_blk, vs_blk, kw_pad, vw_pad


def cmp_to_slc_overlap(ncmp, nsb):
    start = np.arange(ncmp) * CMP_STRIDE
    sj = np.arange(nsb) * SLC_BLOCK
    ov = np.minimum(start[:, None] + CMP_BLOCK, sj[None, :] + SLC_BLOCK) - np.maximum(start[:, None], sj[None, :])
    return (np.clip(ov, 0, None) / CMP_BLOCK).astype(np.float32)


def nsa_mixer(h, kc, vc, ks_blk, vs_blk, kw_pad, vw_pad, w_q, b_gate, g_qnorm, rel_table, w_out):
    B, S, _ = h.shape
    G, HPG, HD = NSA_GROUPS, NSA_HPG, NSA_HD
    QB = NSA_QBLOCK
    proj = h @ w_q
    q = rms_norm(proj[..., :NSA_HEADS * HD].reshape(B, S, G, HPG, HD), g_qnorm) * (HD ** -0.5)
    q = q.transpose(0, 2, 3, 1, 4)
    gates = jax.nn.sigmoid((proj[..., NSA_HEADS * HD:] + b_gate).astype(jnp.float32))
    gates = gates.reshape(B, S, G, HPG, 3).transpose(0, 2, 3, 1, 4)
    ncmp = kc.shape[2]
    nsb = S // SLC_BLOCK
    topk = min(SLC_TOPK, nsb)
    overlap = cmp_to_slc_overlap(ncmp, nsb)
    cmp_end = np.arange(ncmp) * CMP_STRIDE + CMP_BLOCK - 1
    table_g = rel_table.reshape(REL_BUCKETS, G, HPG).transpose(1, 0, 2)
    bi = jnp.arange(B)[:, None, None]
    gi = jnp.arange(G)[None, :, None]

    def block_step(qb):
        t0 = qb * QB
        qt = t0 + jnp.arange(QB)
        qblk = lax.dynamic_slice_in_dim(q, t0, QB, axis=3)
        gblk = lax.dynamic_slice_in_dim(gates, t0, QB, axis=3)

        dist_c = qt[:, None] - cmp_end[None, :]
        s_c = jnp.einsum('bghqd,bgkd->bghqk', qblk, kc) + dense_bias(dist_c, rel_table)
        p_c = masked_softmax(s_c, dist_c >= 0)
        o_c = jnp.einsum('bghqk,bgkd->bghqd', p_c.astype(vc.dtype), vc)

        imp = jnp.einsum('bghqk,kj->bgqj', p_c, overlap)
        jb = jnp.arange(nsb)[None, :]
        qblk_id = (qt // SLC_BLOCK)[:, None]
        forced = (jb == 0) | (jb == qblk_id) | (jb == qblk_id - 1)
        score = jnp.where(forced, FORCE_SCORE, imp)
        score = jnp.where(jb <= qblk_id, score, -jnp.inf)
        top_s, top_idx = lax.top_k(score, topk)
        sel_valid = jnp.isfinite(top_s)

        flat_idx = top_idx.reshape(B, G, QB * topk)
        k_sel = ks_blk[bi, gi, flat_idx].reshape(B, G, QB, topk * SLC_BLOCK, HD)
        v_sel = vs_blk[bi, gi, flat_idx].reshape(B, G, QB, topk * SLC_BLOCK, HD)
        pos_s = (top_idx[..., None] * SLC_BLOCK + jnp.arange(SLC_BLOCK)).reshape(B, G, QB, topk * SLC_BLOCK)
        dist_s = qt[None, None, :, None] - pos_s
        valid_s = (dist_s >= 0) & jnp.repeat(sel_valid, SLC_BLOCK, axis=-1)
        bias_s = table_g[jnp.arange(G)[None, :, None, None], t5_bucket(dist_s)]
        s_s = jnp.einsum('bghqd,bgqkd->bghqk', qblk, k_sel) + bias_s.transpose(0, 1, 4, 2, 3).astype(jnp.float32)
        p_s = masked_softmax(s_s, valid_s[:, :, None])
        o_s = jnp.einsum('bghqk,bgqkd->bghqd', p_s.astype(v_sel.dtype), v_sel)

        k_w = lax.dynamic_slice_in_dim(kw_pad, t0, WINDOW + QB, axis=2)
        v_w = lax.dynamic_slice_in_dim(vw_pad, t0, WINDOW + QB, axis=2)
        pos_w = t0 - WINDOW + jnp.arange(WINDOW + QB)
        dist_w = qt[:, None] - pos_w[None, :]
        valid_w = (dist_w >= 0) & (dist_w < WINDOW) & (pos_w[None, :] >= 0)
        s_w = jnp.einsum('bghqd,bgkd->bghqk', qblk, k_w) + dense_bias(dist_w, rel_table)
        p_w = masked_softmax(s_w, valid_w)
        o_w = jnp.einsum('bghqk,bgkd->bghqd', p_w.astype(v_w.dtype), v_w)

        return gblk[..., 0:1] * o_c + gblk[..., 1:2] * o_s + gblk[..., 2:3] * o_w

    out = lax.map(block_step, jnp.arange(S // QB))
    out = out.transpose(1, 0, 4, 2, 3, 5).reshape(B, S, NSA_HEADS * HD)
    return out @ w_out


def setup_inputs(seed: int = 0) -> dict:
    key = jax.random.key(seed)
    ks = jax.random.split(key, 32)
    D = D_MODEL

    def nrm(k, shape, scale):
        return jax.random.normal(k, shape, jnp.float32) * scale

    def gain(k, shape):
        return 1.0 + 0.02 * jax.random.normal(k, shape, jnp.float32)

    forget_bias = 3.0 + jnp.linspace(0.0, 3.0, ML_HEADS, dtype=jnp.float32)
    b_a_if = jnp.concatenate([nrm(ks[9], (N_A_LAYERS, ML_HEADS), 0.1),
                              forget_bias[None, :] + nrm(ks[10], (N_A_LAYERS, ML_HEADS), 0.1)], axis=-1)
    return {
        "x": nrm(ks[0], (BATCH, SEQ, D), 1.0),
        "c": nrm(ks[1], (BATCH, D), 1.0),
        "w_ada": nrm(ks[2], (DEPTH, D, 6 * D), 0.5 * D ** -0.5),
        "b_ada": nrm(ks[3], (DEPTH, 6 * D), 0.02),
        "g_norm_mix": gain(ks[4], (DEPTH, D)),
        "g_norm_ffn": gain(ks[5], (DEPTH, D)),
        "w_ffn_in": nrm(ks[6], (DEPTH, D, 2 * D_FF), D ** -0.5),
        "w_ffn_out": nrm(ks[7], (DEPTH, D_FF, D), D_FF ** -0.5),
        "w_a_in": nrm(ks[8], (N_A_LAYERS, D, ML_PROJ), D ** -0.5),
        "b_a_if": b_a_if,
        "g_a_out": gain(ks[11], (N_A_LAYERS, ML_HEADS * ML_DV)),
        "w_a_out": nrm(ks[12], (N_A_LAYERS, ML_HEADS * ML_DV, D), (ML_HEADS * ML_DV) ** -0.5),
        "w_kv_ada": nrm(ks[13], (D, 2 * D), 0.5 * D ** -0.5),
        "b_kv_ada": nrm(ks[14], (2 * D,), 0.02),
        "g_kv_norm": gain(ks[15], (D,)),
        "w_kv": nrm(ks[16], (D, NSA_KV_PROJ), D ** -0.5),
        "pos_cmp_k": nrm(ks[17], (CMP_BLOCK, NSA_HD), 0.1),
        "w_cmp_k1": nrm(ks[18], (CMP_BLOCK * NSA_HD, CMP_HIDDEN), (CMP_BLOCK * NSA_HD) ** -0.5),
        "w_cmp_k2": nrm(ks[19], (CMP_HIDDEN, NSA_HD), CMP_HIDDEN ** -0.5),
        "pos_cmp_v": nrm(ks[20], (CMP_BLOCK, NSA_HD), 0.1),
        "w_cmp_v1": nrm(ks[21], (CMP_BLOCK * NSA_HD, CMP_HIDDEN), (CMP_BLOCK * NSA_HD) ** -0.5),
        "w_cmp_v2": nrm(ks[22], (CMP_HIDDEN, NSA_HD), CMP_HIDDEN ** -0.5),
        "g_knorm": gain(ks[23], (3, NSA_HD)),
        "w_b_q": nrm(ks[24], (N_B_LAYERS, D, NSA_Q_PROJ), D ** -0.5),
        "b_b_gate": nrm(ks[25], (N_B_LAYERS, 3 * NSA_HEADS), 0.1),
        "g_qnorm": gain(ks[26], (N_B_LAYERS, NSA_HD)),
        "w_b_out": nrm(ks[27], (N_B_LAYERS, NSA_HEADS * NSA_HD, D), (NSA_HEADS * NSA_HD) ** -0.5),
        "rel_table": nrm(ks[28], (REL_BUCKETS, NSA_HEADS), 0.5),
    }


def reference(x, c, w_ada, b_ada, g_norm_mix, g_norm_ffn, w_ffn_in, w_ffn_out,
              w_a_in, b_a_if, g_a_out, w_a_out,
              w_kv_ada, b_kv_ada, g_kv_norm, w_kv, pos_cmp_k, w_cmp_k1, w_cmp_k2,
              pos_cmp_v, w_cmp_v1, w_cmp_v2, g_knorm,
              w_b_q, b_b_gate, g_qnorm, w_b_out, rel_table):
    c_act = jax.nn.silu(c)
    shared = None
    for layer in range(DEPTH):
        mod = c_act @ w_ada[layer] + b_ada[layer]
        sh1, sc1, ga1, sh2, sc2, ga2 = jnp.split(mod, 6, axis=-1)
        h = modulate(rms_norm(x, g_norm_mix[layer]), sh1, sc1)
        if layer < N_A_LAYERS:
            y = mlstm_mixer(h, w_a_in[layer], b_a_if[layer], g_a_out[layer], w_a_out[layer])
        else:
            if layer == N_A_LAYERS:
                kv_sh, kv_sc = jnp.split(c_act @ w_kv_ada + b_kv_ada, 2, axis=-1)
                h_kv = modulate(rms_norm(x, g_kv_norm), kv_sh, kv_sc)
                shared = nsa_shared_kv(h_kv, w_kv, pos_cmp_k, w_cmp_k1, w_cmp_k2,
                                       pos_cmp_v, w_cmp_v1, w_cmp_v2, g_knorm)
            j = layer - N_A_LAYERS
            kc, vc, ks_blk, vs_blk, kw_pad, vw_pad = shared
            y = nsa_mixer(h, kc, vc, ks_blk, vs_blk, kw_pad, vw_pad,
                          w_b_q[j], b_b_gate[j], g_qnorm[j], rel_table, w_b_out[j])
        x = x + (ga1[:, None, :] * y).astype(x.dtype)
        h = modulate(rms_norm(x, g_norm_ffn[layer]), sh2, sc2)
        x = x + (ga2[:, None, :] * swiglu(h, w_ffn_in[layer], w_ffn_out[layer])).astype(x.dtype)
    return x
```

```python
import functools
import math

import jax
import jax.numpy as jnp
import numpy as np
from jax import lax
from jax.experimental import pallas as pl
from jax.experimental.pallas import tpu as pltpu

F32 = jnp.float32
BF16 = jnp.bfloat16
NEG_INF = float("-inf")

RMS_EPS = 1e-6

ML_HEADS = 4
ML_DQK = 128
ML_DV = 256
ML_LC = 256

NSA_HEADS = 16
NSA_GROUPS = 4
NSA_HPG = 4
NSA_HD = 64
CMP_BLOCK = 32
CMP_STRIDE = 16
SLC_BLOCK = 64
SLC_TOPK = 8
WINDOW = 512
FORCE_SCORE = 1e4
REL_BUCKETS = 32
REL_MAX_DIST = 128
ATT_QB = 128
ATT_TK = 128
NCMP_PAD = 128

VMEM_LIMIT = 48 * 1024 * 1024


def _cparams(sem):
    return pltpu.CompilerParams(dimension_semantics=sem, vmem_limit_bytes=VMEM_LIMIT)


def _dot(a, b):
    return jnp.dot(a, b, preferred_element_type=F32)


def _dot_nt(a, b):
    return lax.dot_general(a, b, (((1,), (1,)), ((), ())), preferred_element_type=F32)


def _dot_tn(a, b):
    return lax.dot_general(a, b, (((0,), (0,)), ((), ())), preferred_element_type=F32)


def _norm_mod(x, g, sh, sc):
    var = jnp.mean(x * x, axis=-1, keepdims=True)
    y = x * lax.rsqrt(var + RMS_EPS) * g
    return y * (1.0 + sc) + sh


def _ada_body(c_ref, w_ref, b_ref, o_ref):
    c = c_ref[...]
    ca = c * jax.nn.sigmoid(c)
    o_ref[...] = jnp.dot(ca, w_ref[...], preferred_element_type=F32,
                         precision=lax.Precision.HIGHEST) + b_ref[...]


def ada_mod(c, w, b, tn=1024):
    L, D, N = w.shape
    B = c.shape[0]
    return pl.pallas_call(
        _ada_body,
        grid=(L, N // tn),
        in_specs=[pl.BlockSpec((B, D), lambda l, j: (0, 0)),
                  pl.BlockSpec((None, D, tn), lambda l, j: (l, 0, j)),
                  pl.BlockSpec((None, 1, tn), lambda l, j: (l, 0, j))],
        out_specs=pl.BlockSpec((None, B, tn), lambda l, j: (l, 0, j)),
        out_shape=jax.ShapeDtypeStruct((L, B, N), F32),
        compiler_params=_cparams(("parallel", "parallel")),
        name="ada_mod",
    )(c, w, b.reshape(L, 1, N))


def _norm_proj_body(n_out, has_bias, tn, x_ref, g_ref, sh_ref, sc_ref, *refs):
    w_refs = refs[:n_out]
    b_refs = refs[n_out:2 * n_out]
    o_refs = refs[2 * n_out:3 * n_out]
    h = _norm_mod(x_ref[...], g_ref[...], sh_ref[...], sc_ref[...]).astype(BF16)
    for w_ref, b_ref, o_ref, hb in zip(w_refs, b_refs, o_refs, has_bias):
        n = w_ref.shape[1]
        step = min(tn, n)
        for n0 in range(0, n, step):
            acc = _dot(h, w_ref[:, n0:n0 + step])
            if hb:
                acc = acc + b_ref[:, n0:n0 + step]
            o_ref[:, n0:n0 + step] = acc.astype(o_ref.dtype)


def norm_proj(x2d, g, sh, sc, ws, biases, out_dtypes, tm=512, tn=512):
    M, D = x2d.shape
    B = sh.shape[0]
    tiles_per_batch = (M // B) // tm
    n_out = len(ws)
    has_bias = tuple(b is not None for b in biases)
    bias_args = [(b if b is not None else jnp.zeros((w.shape[1],), F32)).reshape(1, -1)
                 for b, w in zip(biases, ws)]
    in_specs = [pl.BlockSpec((tm, D), lambda i: (i, 0)),
                pl.BlockSpec((1, D), lambda i: (0, 0)),
                pl.BlockSpec((None, 1, D), lambda i: (i // tiles_per_batch, 0, 0)),
                pl.BlockSpec((None, 1, D), lambda i: (i // tiles_per_batch, 0, 0))]
    in_specs += [pl.BlockSpec(w.shape, lambda i: (0, 0)) for w in ws]
    in_specs += [pl.BlockSpec(b.shape, lambda i: (0, 0)) for b in bias_args]
    out_specs = [pl.BlockSpec((tm, w.shape[1]), lambda i: (i, 0)) for w in ws]
    out_shape = [jax.ShapeDtypeStruct((M, w.shape[1]), dt) for w, dt in zip(ws, out_dtypes)]
    return pl.pallas_call(
        functools.partial(_norm_proj_body, n_out, has_bias, tn),
        grid=(M // tm,),
        in_specs=in_specs, out_specs=out_specs, out_shape=out_shape,
        compiler_params=_cparams(("parallel",)),
        name="norm_proj",
    )(x2d, g.reshape(1, D), sh, sc, *ws, *bias_args)


def _gate_proj_t_body(x_ref, g_ref, sh_ref, sc_ref, wt_ref, b_ref, o_ref):
    h = _norm_mod(x_ref[...], g_ref[...], sh_ref[...], sc_ref[...]).astype(BF16)
    o_ref[...] = _dot_nt(wt_ref[...], h) + b_ref[...]


def gate_proj_t(x2d, g, sh, sc, wt, bias, seq, tm=512):
    M, D = x2d.shape
    B = sh.shape[0]
    NG = wt.shape[0]
    tiles_per_batch = seq // tm
    return pl.pallas_call(
        _gate_proj_t_body,
        grid=(M // tm,),
        in_specs=[pl.BlockSpec((tm, D), lambda i: (i, 0)),
                  pl.BlockSpec((1, D), lambda i: (0, 0)),
                  pl.BlockSpec((None, 1, D), lambda i: (i // tiles_per_batch, 0, 0)),
                  pl.BlockSpec((None, 1, D), lambda i: (i // tiles_per_batch, 0, 0)),
                  pl.BlockSpec((NG, D), lambda i: (0, 0)),
                  pl.BlockSpec((NG, 1), lambda i: (0, 0))],
        out_specs=pl.BlockSpec((None, NG, tm),
                               lambda i: (i // tiles_per_batch, 0, i % tiles_per_batch)),
        out_shape=jax.ShapeDtypeStruct((B, NG, seq), F32),
        compiler_params=_cparams(("parallel",)),
        name="gate_proj_t",
    )(x2d, g.reshape(1, D), sh, sc, wt, bias.reshape(NG, 1))


def _proj_residual_body(a_ref, w_ref, x_ref, ga_ref, o_ref):
    o_ref[...] = x_ref[...] + ga_ref[...] * _dot(a_ref[...], w_ref[...])


def proj_residual(a, w, x2d, ga, tm=512):
    M, K = a.shape
    D = w.shape[1]
    B = ga.shape[0]
    tiles_per_batch = (M // B) // tm
    return pl.pallas_call(
        _proj_residual_body,
        grid=(M // tm,),
        in_specs=[pl.BlockSpec((tm, K), lambda i: (i, 0)),
                  pl.BlockSpec((K, D), lambda i: (0, 0)),
                  pl.BlockSpec((tm, D), lambda i: (i, 0)),
                  pl.BlockSpec((None, 1, D), lambda i: (i // tiles_per_batch, 0, 0))],
        out_specs=pl.BlockSpec((tm, D), lambda i: (i, 0)),
        out_shape=jax.ShapeDtypeStruct((M, D), F32),
        compiler_params=_cparams(("parallel",)),
        name="proj_residual",
    )(a, w, x2d, ga)


def _ffn_body(x_ref, g_ref, sh_ref, sc_ref, ga_ref, wg_ref, wu_ref, wo_ref, o_ref, h_s, acc_s):
    f = pl.program_id(1)

    @pl.when(f == 0)
    def _():
        h_s[...] = _norm_mod(x_ref[...], g_ref[...], sh_ref[...], sc_ref[...]).astype(BF16)

    h = h_s[...]
    gate = _dot(h, wg_ref[...])
    up = _dot(h, wu_ref[...])
    act = (gate * jax.nn.sigmoid(gate) * up).astype(BF16)
    part = _dot(act, wo_ref[...])

    @pl.when(f == 0)
    def _():
        acc_s[...] = part

    @pl.when(f > 0)
    def _():
        acc_s[...] += part

    @pl.when(f == pl.num_programs(1) - 1)
    def _():
        o_ref[...] = x_ref[...] + ga_ref[...] * acc_s[...]


def ffn(x2d, g, sh, sc, ga, w_in, w_out, tm=1024, tf=256):
    M, D = x2d.shape
    F = w_out.shape[0]
    B = sh.shape[0]
    tiles_per_batch = (M // B) // tm
    nf = F // tf
    bvec = pl.BlockSpec((None, 1, D), lambda i, f: (i // tiles_per_batch, 0, 0))
    return pl.pallas_call(
        _ffn_body,
        grid=(M // tm, nf),
        in_specs=[pl.BlockSpec((tm, D), lambda i, f: (i, 0)),
                  pl.BlockSpec((1, D), lambda i, f: (0, 0)),
                  bvec, bvec, bvec,
                  pl.BlockSpec((D, tf), lambda i, f: (0, f)),
                  pl.BlockSpec((D, tf), lambda i, f: (0, nf + f)),
                  pl.BlockSpec((tf, D), lambda i, f: (f, 0))],
        out_specs=pl.BlockSpec((tm, D), lambda i, f: (i, 0)),
        out_shape=jax.ShapeDtypeStruct((M, D), F32),
        scratch_shapes=[pltpu.VMEM((tm, D), BF16), pltpu.VMEM((tm, D), F32)],
        compiler_params=_cparams(("parallel", "arbitrary")),
        name="ffn",
    )(x2d, g.reshape(1, D), sh, sc, ga, w_in, w_in, w_out)


def _lane_cumsum(x):
    n = x.shape[-1]
    lane = lax.broadcasted_iota(jnp.int32, x.shape, 1)
    sh = 1
    while sh < n:
        x = x + jnp.where(lane >= sh, pltpu.roll(x, sh, axis=1), 0.0)
        sh *= 2
    return x


def _mlstm_body(q_ref, k_ref, v_ref, o_ref, gt_ref, gout_ref, out_ref, c_s, n_s, m_s):
    c_idx = pl.program_id(1)
    LC = q_ref.shape[0]
    scale = ML_DQK ** -0.5

    @pl.when(c_idx == 0)
    def _():
        c_s[...] = jnp.zeros_like(c_s)
        n_s[...] = jnp.zeros_like(n_s)
        m_s[...] = jnp.zeros_like(m_s)

    gt = gt_ref[...]
    fpre = gt[ML_HEADS:2 * ML_HEADS, :]
    logf = jnp.minimum(fpre, 0.0) - jnp.log1p(jnp.exp(-jnp.abs(fpre)))
    bcum = _lane_cumsum(logf)

    row = lax.broadcasted_iota(jnp.int32, (LC, LC), 0)
    col = lax.broadcasted_iota(jnp.int32, (LC, LC), 1)
    causal = col <= row
    diag = col == row

    for h in range(ML_HEADS):
        li = gt[h:h + 1, :]
        lf = logf[h:h + 1, :]
        b_r = bcum[h:h + 1, :]
        a_r = li - b_r
        b_last = b_r[:, LC - 1:LC]
        m_prev = m_s[h:h + 1, 0:1]
        m_loc = b_last + jnp.max(a_r, axis=-1, keepdims=True)
        m_new = jnp.maximum(b_last + m_prev, m_loc)
        e_r = jnp.exp(b_last + a_r - m_new)
        decay = jnp.exp(b_last + m_prev - m_new)

        cm_col = jnp.max(jnp.where(causal, a_r, NEG_INF), axis=-1, keepdims=True)
        b_col = jnp.sum(jnp.where(causal, lf, 0.0), axis=-1, keepdims=True)
        e_col = jnp.sum(jnp.where(diag, e_r, 0.0), axis=-1, keepdims=True)
        g_col = jnp.maximum(m_prev, cm_col)
        w_intra = jnp.exp(jnp.where(causal, a_r - g_col, NEG_INF))
        w_inter = jnp.exp(m_prev - g_col)
        floor = jnp.exp(-(b_col + g_col))

        qh = q_ref[:, h * ML_DQK:(h + 1) * ML_DQK]
        kh = k_ref[:, h * ML_DQK:(h + 1) * ML_DQK]
        vh = v_ref[:, h * ML_DV:(h + 1) * ML_DV]
        c_prev = c_s[h]
        n_prev = n_s[h:h + 1, :]

        s = _dot_nt(qh, kh) * scale * w_intra
        inter = _dot(qh, c_prev.astype(BF16)) * scale
        num = _dot(s.astype(BF16), vh) + w_inter * inter
        qn_inter = jnp.sum(qh.astype(F32) * n_prev, axis=-1, keepdims=True) * scale
        qn = jnp.sum(s, axis=-1, keepdims=True) + w_inter * qn_inter
        denom = jnp.maximum(jnp.abs(qn), floor)
        hout = num / denom

        ke = kh.astype(F32) * e_col
        c_s[h] = decay * c_prev + _dot_tn(ke.astype(BF16), vh)
        n_s[h:h + 1, :] = decay * n_prev + jnp.sum(ke, axis=0, keepdims=True)
        m_s[h:h + 1, :] = jnp.broadcast_to(m_new, (1, m_s.shape[1]))

        var = jnp.mean(hout * hout, axis=-1, keepdims=True)
        hn = hout * lax.rsqrt(var + RMS_EPS) * gout_ref[:, h * ML_DV:(h + 1) * ML_DV]
        og = jax.nn.sigmoid(o_ref[:, h * ML_DV:(h + 1) * ML_DV].astype(F32))
        out_ref[:, h * ML_DV:(h + 1) * ML_DV] = (hn * og).astype(out_ref.dtype)


def mlstm_core(proj, gates_t, g_out, batch, seq):
    LC = ML_LC
    nc = seq // LC
    qk = ML_HEADS * ML_DQK
    vd = ML_HEADS * ML_DV
    row = lambda b, c: b * nc + c
    return pl.pallas_call(
        _mlstm_body,
        grid=(batch, nc),
        in_specs=[pl.BlockSpec((LC, qk), lambda b, c: (row(b, c), 0)),
                  pl.BlockSpec((LC, qk), lambda b, c: (row(b, c), 1)),
                  pl.BlockSpec((LC, vd), lambda b, c: (row(b, c), 1)),
                  pl.BlockSpec((LC, vd), lambda b, c: (row(b, c), 2)),
                  pl.BlockSpec((None, 2 * ML_HEADS, LC), lambda b, c: (b, 0, c)),
                  pl.BlockSpec((1, vd), lambda b, c: (0, 0))],
        out_specs=pl.BlockSpec((LC, vd), lambda b, c: (row(b, c), 0)),
        out_shape=jax.ShapeDtypeStruct((batch * seq, vd), BF16),
        scratch_shapes=[pltpu.VMEM((ML_HEADS, ML_DQK, ML_DV), F32),
                        pltpu.VMEM((8, ML_DQK), F32),
                        pltpu.VMEM((8, 128), F32)],
        compiler_params=_cparams(("parallel", "arbitrary")),
        name="mlstm_core",
    )(proj, proj, proj, proj, gates_t, g_out.reshape(1, vd))


def _gelu_tanh(x):
    c = math.sqrt(2.0 / math.pi)
    return 0.5 * x * (1.0 + jnp.tanh(c * (x + 0.044715 * (x * x * x))))


def _compress_body(x_ref, pos_ref, w1_ref, w2_ref, g_ref, o_ref):
    kind = pl.program_id(0)
    tm = x_ref.shape[0]
    half = x_ref.shape[1]
    x = x_ref[...]
    xlo = (x + pos_ref[:, :half]).astype(BF16)
    xhi = (x + pos_ref[:, half:]).astype(BF16)
    u = _dot(xlo, w1_ref[:half, :])
    v = _dot(xhi, w1_ref[half:, :])
    pre = u + pltpu.roll(v, tm - 1, axis=0)
    y = _dot(_gelu_tanh(pre).astype(BF16), w2_ref[...])
    var = jnp.mean(y * y, axis=-1, keepdims=True)
    yn = y * lax.rsqrt(var + RMS_EPS) * g_ref[...]
    o_ref[...] = jnp.where(kind == 0, yn, y).astype(o_ref.dtype)


def compress(x16, pos, w1, w2, g, tm=512):
    _, R, half = x16.shape
    hid = w1.shape[2]
    return pl.pallas_call(
        _compress_body,
        grid=(2, R // tm),
        in_specs=[pl.BlockSpec((None, tm, half), lambda k, i: (k, i, 0)),
                  pl.BlockSpec((None, 1, 2 * half), lambda k, i: (k, 0, 0)),
                  pl.BlockSpec((None, 2 * half, hid), lambda k, i: (k, 0, 0)),
                  pl.BlockSpec((None, hid, NSA_HD), lambda k, i: (k, 0, 0)),
                  pl.BlockSpec((1, NSA_HD), lambda k, i: (0, 0))],
        out_specs=pl.BlockSpec((None, tm, NSA_HD), lambda k, i: (k, i, 0)),
        out_shape=jax.ShapeDtypeStruct((2, R, NSA_HD), BF16),
        compiler_params=_cparams(("parallel", "parallel")),
        name="compress",
    )(x16, pos, w1, w2, g)


def _kv_norm_body(x_ref, g_ref, o_ref):
    x = x_ref[...]
    var = jnp.mean(x * x, axis=-1, keepdims=True)
    o_ref[...] = (x * lax.rsqrt(var + RMS_EPS) * g_ref[...]).astype(o_ref.dtype)


def kv_norm(x, g, tm=2048):
    _, R, hd = x.shape
    return pl.pallas_call(
        _kv_norm_body,
        grid=(2, R // tm),
        in_specs=[pl.BlockSpec((None, tm, hd), lambda k, i: (k, i, 0)),
                  pl.BlockSpec((None, 1, hd), lambda k, i: (k, 0, 0))],
        out_specs=pl.BlockSpec((None, tm, hd), lambda k, i: (k, i, 0)),
        out_shape=jax.ShapeDtypeStruct(x.shape, BF16),
        compiler_params=_cparams(("parallel", "parallel")),
        name="kv_norm",
    )(x, g)


def _t5_bucket(dist):
    n = jnp.maximum(dist, 0)
    max_exact = REL_BUCKETS // 2
    nf = jnp.maximum(n, 1).astype(F32)
    large = max_exact + (jnp.log(nf / max_exact) / math.log(REL_MAX_DIST / max_exact)
                         * (REL_BUCKETS - max_exact)).astype(jnp.int32)
    large = jnp.minimum(large, REL_BUCKETS - 1)
    return jnp.where(n < max_exact, n, large)


def _table_lookup(bucket, tab_ref, h):
    out = jnp.zeros(bucket.shape, F32)
    for k in range(REL_BUCKETS):
        out = jnp.where(bucket == k, tab_ref[k, h], out)
    return out


def _bias_prep_body(tab_ref, bc_ref, tp_ref):
    h = pl.program_id(0)
    S = bc_ref.shape[0]
    rows = 128

    def chunk(r, carry):
        start = pl.multiple_of(r * rows, rows)
        t = start + lax.broadcasted_iota(jnp.int32, (rows, NCMP_PAD), 0)
        n = lax.broadcasted_iota(jnp.int32, (rows, NCMP_PAD), 1)
        dist_c = t - (n * CMP_STRIDE + CMP_BLOCK - 1)
        bias_c = _table_lookup(_t5_bucket(dist_c), tab_ref, h)
        bc_ref[pl.ds(start, rows), :] = jnp.where(dist_c >= 0, bias_c, NEG_INF)
        return carry

    lax.fori_loop(0, S // rows, chunk, 0)
    i = lax.broadcasted_iota(jnp.int32, (ATT_QB, ATT_TK), 0)
    j = lax.broadcasted_iota(jnp.int32, (ATT_QB, ATT_TK), 1)
    for d in range(3):
        tp_ref[d] = _table_lookup(_t5_bucket(d * ATT_TK + i - j), tab_ref, h)


def bias_prep(rel_table, seq):
    assert ATT_TK == ATT_QB and 2 * ATT_TK - ATT_QB + 1 > 113
    return pl.pallas_call(
        _bias_prep_body,
        grid=(NSA_HEADS,),
        in_specs=[pl.BlockSpec(memory_space=pltpu.SMEM)],
        out_specs=[pl.BlockSpec((None, seq, NCMP_PAD), lambda h: (h, 0, 0)),
                   pl.BlockSpec((None, 3, ATT_QB, ATT_TK), lambda h: (h, 0, 0, 0))],
        out_shape=[jax.ShapeDtypeStruct((NSA_HEADS, seq, NCMP_PAD), F32),
                   jax.ShapeDtypeStruct((NSA_HEADS, 3, ATT_QB, ATT_TK), F32)],
        compiler_params=_cparams(("parallel",)),
        name="bias_prep",
    )(rel_table)


def _softmax_tile(s, v_tile, m_s, l_s, acc_s, first):
    tmax = jnp.max(s, axis=-1, keepdims=True)
    if first:
        m_new = tmax
        p = jnp.exp(s - m_new)
        l_s[...] = jnp.sum(p, axis=-1, keepdims=True)
        acc_s[...] = _dot(p.astype(BF16), v_tile)
    else:
        m_old = m_s[...]
        m_new = jnp.maximum(m_old, tmax)
        alpha = jnp.exp(m_old - m_new)
        p = jnp.exp(s - m_new)
        l_s[...] = alpha * l_s[...] + jnp.sum(p, axis=-1, keepdims=True)
        acc_s[...] = alpha * acc_s[...] + _dot(p.astype(BF16), v_tile)
    m_s[...] = m_new


def _nsa_body(q_ref, gate_ref, kc_ref, vc_ref, ks_ref, vs_ref, kw_ref, vw_ref,
              bc_ref, tp_ref, gq_ref, ov_ref, ex_ref, out_ref,
              q_s, m_s, l_s, acc_s, osel_s):
    qb = pl.program_id(2)
    QB, TK, HD, HPG = ATT_QB, ATT_TK, NSA_HD, NSA_HPG
    t0 = qb * QB

    for h in range(HPG):
        qh = q_ref[:, h * HD:(h + 1) * HD].astype(F32)
        var = jnp.mean(qh * qh, axis=-1, keepdims=True)
        qn = qh * lax.rsqrt(var + RMS_EPS) * gq_ref[...] * (HD ** -0.5)
        q_s[h * QB:(h + 1) * QB, :] = qn.astype(BF16)
    q = q_s[...]

    s_c = _dot_nt(q, kc_ref[...])
    s_c = s_c + jnp.concatenate([bc_ref[h] for h in range(HPG)], axis=0)
    m_c = jnp.max(s_c, axis=-1, keepdims=True)
    m_c = jnp.where(m_c == NEG_INF, 0.0, m_c)
    e_c = jnp.exp(s_c - m_c)
    p_c = e_c / jnp.maximum(jnp.sum(e_c, axis=-1, keepdims=True), jnp.finfo(F32).tiny)
    o_cmp = _dot(p_c.astype(BF16), vc_ref[...])

    p_sum = p_c[0:QB]
    for h in range(1, HPG):
        p_sum = p_sum + p_c[h * QB:(h + 1) * QB]
    imp = jnp.dot(p_sum, ov_ref[...], preferred_element_type=F32,
                  precision=lax.Precision.HIGHEST)
    jb = lax.broadcasted_iota(jnp.int32, (QB, 128), 1)
    tq = t0 + lax.broadcasted_iota(jnp.int32, (QB, 128), 0)
    qid = jnp.right_shift(tq, 6)
    forced = (jb == 0) | (jb == qid) | (jb == qid - 1)
    score = jnp.where(forced, FORCE_SCORE, imp)
    score = jnp.where(jb <= qid, score, NEG_INF)
    jbf = jb.astype(F32)
    sel = jnp.zeros((QB, 128), F32)
    for _ in range(SLC_TOPK):
        mx = jnp.max(score, axis=-1, keepdims=True)
        first = jnp.min(jnp.where(score == mx, jbf, 128.0), axis=-1, keepdims=True)
        pick = jbf == first
        sel = jnp.where(pick & (mx > NEG_INF), 1.0, sel)
        score = jnp.where(pick, NEG_INF, score)
    sel_b = sel.astype(BF16)

    ri = lax.broadcasted_iota(jnp.int32, (QB, TK), 0)
    ci = lax.broadcasted_iota(jnp.int32, (QB, TK), 1)

    def sel_tile_mask(kt):
        start = pl.multiple_of(kt * TK, TK)
        hit = _dot(sel_b, ex_ref[:, pl.ds(start, TK)])
        return jnp.where(hit > 0.5, 0.0, NEG_INF)

    def scores(k_ref, kt, extra):
        start = pl.multiple_of(kt * TK, TK)
        s = _dot_nt(q, k_ref[pl.ds(start, TK), :])
        d = jnp.minimum(qb - kt, 2)
        bias = jnp.concatenate([tp_ref[h, d] + extra for h in range(HPG)], axis=0)
        return s + bias

    def v_tile(v_ref, kt):
        start = pl.multiple_of(kt * TK, TK)
        return v_ref[pl.ds(start, TK), :]

    causal_mask = jnp.where(ci <= ri, 0.0, NEG_INF)

    _softmax_tile(scores(ks_ref, qb, causal_mask + sel_tile_mask(qb)), v_tile(vs_ref, qb),
                  m_s, l_s, acc_s, True)

    def sel_step(kt, carry):
        _softmax_tile(scores(ks_ref, kt, sel_tile_mask(kt)), v_tile(vs_ref, kt),
                      m_s, l_s, acc_s, False)
        return carry

    lax.fori_loop(0, qb, sel_step, 0)
    osel_s[...] = acc_s[...] / l_s[...]

    _softmax_tile(scores(kw_ref, qb, causal_mask), v_tile(vw_ref, qb), m_s, l_s, acc_s, True)

    def win_step(kt, carry):
        dist = (qb - kt) * TK + ri - ci
        _softmax_tile(scores(kw_ref, kt, jnp.where(dist < WINDOW, 0.0, NEG_INF)),
                      v_tile(vw_ref, kt), m_s, l_s, acc_s, False)
        return carry

    lax.fori_loop(jnp.maximum(qb - WINDOW // TK, 0), qb, win_step, 0)
    o_win = acc_s[...] / l_s[...]
    o_sel = osel_s[...]

    gates = jax.nn.sigmoid(gate_ref[...])
    for h in range(HPG):
        rows = slice(h * QB, (h + 1) * QB)
        o = (gates[:, 3 * h:3 * h + 1] * o_cmp[rows]
             + gates[:, 3 * h + 1:3 * h + 2] * o_sel[rows]
             + gates[:, 3 * h + 2:3 * h + 3] * o_win[rows])
        out_ref[:, h * HD:(h + 1) * HD] = o.astype(out_ref.dtype)


def _overlap_matrix():
    start = np.arange(NCMP_PAD) * CMP_STRIDE
    sj = np.arange(128) * SLC_BLOCK
    ov = (np.minimum(start[:, None] + CMP_BLOCK, sj[None, :] + SLC_BLOCK)
          - np.maximum(start[:, None], sj[None, :]))
    ov = np.clip(ov, 0, None) / CMP_BLOCK
    ov[NCMP_PAD - 1, :] = 0.0
    return ov.astype(np.float32)


def _expand_matrix(seq):
    blk = np.arange(seq) // SLC_BLOCK
    return (np.arange(128)[:, None] == blk[None, :]).astype(np.float32)


def nsa_attn(q, gates, kc, vc, ks, vs, kw, vw, bias_c, toep, g_qnorm, batch, seq):
    QB, TK, HD, HPG, G = ATT_QB, ATT_TK, NSA_HD, NSA_HPG, NSA_GROUPS
    assert seq // SLC_BLOCK <= 128 and seq % QB == 0 and seq // CMP_STRIDE == NCMP_PAD
    nq = seq // QB
    ov = jnp.asarray(_overlap_matrix())
    ex = jnp.asarray(_expand_matrix(seq), dtype=BF16)
    kv_small = pl.BlockSpec((None, None, NCMP_PAD, HD), lambda b, g, i: (b, g, 0, 0))
    kv_full = pl.BlockSpec((None, None, seq, HD), lambda b, g, i: (b, g, 0, 0))
    R = HPG * QB
    return pl.pallas_call(
        _nsa_body,
        grid=(batch, G, nq),
        in_specs=[pl.BlockSpec((QB, HPG * HD), lambda b, g, i: (b * nq + i, g)),
                  pl.BlockSpec((QB, 128), lambda b, g, i: (b * nq + i, g)),
                  kv_small, kv_small, kv_full, kv_full, kv_full, kv_full,
                  pl.BlockSpec((HPG, QB, NCMP_PAD), lambda b, g, i: (g, i, 0)),
                  pl.BlockSpec((HPG, 3, QB, TK), lambda b, g, i: (g, 0, 0, 0)),
                  pl.BlockSpec((1, HD), lambda b, g, i: (0, 0)),
                  pl.BlockSpec((NCMP_PAD, 128), lambda b, g, i: (0, 0)),
                  pl.BlockSpec((128, seq), lambda b, g, i: (0, 0))],
        out_specs=pl.BlockSpec((QB, HPG * HD), lambda b, g, i: (b * nq + i, g)),
        out_shape=jax.ShapeDtypeStruct((batch * seq, NSA_HEADS * HD), BF16),
        scratch_shapes=[pltpu.VMEM((R, HD), BF16),
                        pltpu.VMEM((R, 1), F32),
                        pltpu.VMEM((R, 1), F32),
                        pltpu.VMEM((R, HD), F32),
                        pltpu.VMEM((R, HD), F32)],
        compiler_params=_cparams(("parallel", "parallel", "arbitrary")),
        name="nsa_attn",
    )(q, gates, kc, vc, ks, vs, kw, vw, bias_c, toep, g_qnorm.reshape(1, HD), ov, ex)


def _mlstm_layer(x2d, g_mix, sh, sc, ga, w_in, b_if, g_out, w_out, batch, seq):
    nbig = 2 * ML_HEADS * ML_DQK + 2 * ML_HEADS * ML_DV
    (proj,) = norm_proj(x2d, g_mix, sh, sc, [w_in[:, :nbig].astype(BF16)], [None], [BF16])
    gates_t = gate_proj_t(x2d, g_mix, sh, sc, w_in[:, nbig:].T.astype(BF16), b_if, seq)
    hg = mlstm_core(proj, gates_t, g_out, batch, seq)
    return proj_residual(hg, w_out.astype(BF16), x2d, ga)


def _nsa_shared(x2d, g_kv, kv_sh, kv_sc, w_kv, pos_k, w_k1, w_k2, pos_v, w_v1, w_v2,
                g_knorm, batch, seq):
    G, HD = NSA_GROUPS, NSA_HD
    (kv,) = norm_proj(x2d, g_kv, kv_sh, kv_sc, [w_kv.astype(BF16)], [None], [F32])
    kv = kv.reshape(batch, seq, 6, G, HD).transpose(2, 0, 3, 1, 4)
    grp = CMP_STRIDE * HD
    x16 = kv[0:2].reshape(2, batch * G * (seq // CMP_STRIDE), grp)
    pos = jnp.stack([pos_k.reshape(1, -1), pos_v.reshape(1, -1)])
    w1 = jnp.stack([w_k1, w_v1]).astype(BF16)
    w2 = jnp.stack([w_k2, w_v2]).astype(BF16)
    cmp = compress(x16, pos, w1, w2, g_knorm[0:1])
    cmp = cmp.reshape(2, batch, G, seq // CMP_STRIDE, HD)
    keys = jnp.stack([kv[2], kv[4]]).reshape(2, batch * G * seq, HD)
    keys = kv_norm(keys, g_knorm[1:3].reshape(2, 1, HD)).reshape(2, batch, G, seq, HD)
    return cmp[0], cmp[1], keys[0], kv[3].astype(BF16), keys[1], kv[5].astype(BF16)


def _gate_weights(w_q, b_gate):
    nq = NSA_HEADS * NSA_HD
    per = 3 * NSA_HPG
    wg = w_q[:, nq:].reshape(-1, NSA_GROUPS, per)
    wg = jnp.pad(wg, ((0, 0), (0, 0), (0, 128 - per))).reshape(-1, NSA_GROUPS * 128)
    bg = jnp.pad(b_gate.reshape(NSA_GROUPS, per), ((0, 0), (0, 128 - per))).reshape(-1)
    return wg.astype(BF16), bg


def _nsa_layer(x2d, g_mix, sh, sc, ga, shared, w_q, b_gate, g_qnorm, w_out, bias_c, toep,
               batch, seq):
    nq = NSA_HEADS * NSA_HD
    wg, bg = _gate_weights(w_q, b_gate)
    q, gates = norm_proj(x2d, g_mix, sh, sc, [w_q[:, :nq].astype(BF16), wg], [None, bg],
                         [BF16, F32])
    kc, vc, ks, vs, kw, vw = shared
    att = nsa_attn(q, gates, kc, vc, ks, vs, kw, vw, bias_c, toep, g_qnorm, batch, seq)
    return proj_residual(att, w_out.astype(BF16), x2d, ga)


def kernel(x, c, w_ada, b_ada, g_norm_mix, g_norm_ffn, w_ffn_in, w_ffn_out, w_a_in, b_a_if, g_a_out, w_a_out, w_kv_ada, b_kv_ada, g_kv_norm, w_kv, pos_cmp_k, w_cmp_k1, w_cmp_k2, pos_cmp_v, w_cmp_v1, w_cmp_v2, g_knorm, w_b_q, b_b_gate, g_qnorm, w_b_out, rel_table):
    B, S, D = x.shape
    depth = w_ada.shape[0]
    n_a = w_a_in.shape[0]
    x2d = x.reshape(B * S, D)
    mods = ada_mod(c, w_ada, b_ada)
    kv_mod = ada_mod(c, w_kv_ada[None], b_kv_ada[None])[0]
    shared = None
    bias_c = toep = None
    for layer in range(depth):
        sh1, sc1, ga1, sh2, sc2, ga2 = [mods[layer, :, i * D:(i + 1) * D].reshape(B, 1, D)
                                        for i in range(6)]
        if layer < n_a:
            x2d = _mlstm_layer(x2d, g_norm_mix[layer], sh1, sc1, ga1, w_a_in[layer],
                               b_a_if[layer], g_a_out[layer], w_a_out[layer], B, S)
        else:
            j = layer - n_a
            if shared is None:
                kv_sh = kv_mod[:, :D].reshape(B, 1, D)
                kv_sc = kv_mod[:, D:].reshape(B, 1, D)
                shared = _nsa_shared(x2d, g_kv_norm, kv_sh, kv_sc, w_kv, pos_cmp_k, w_cmp_k1,
                                     w_cmp_k2, pos_cmp_v, w_cmp_v1, w_cmp_v2, g_knorm, B, S)
                bias_c, toep = bias_prep(rel_table, S)
            x2d = _nsa_layer(x2d, g_norm_mix[layer], sh1, sc1, ga1, shared, w_b_q[j],
                             b_b_gate[j], g_qnorm[j], w_b_out[j], bias_c, toep, B, S)
        x2d = ffn(x2d, g_norm_ffn[layer], sh2, sc2, ga2, w_ffn_in[layer].astype(BF16),
                  w_ffn_out[layer].astype(BF16))
    return x2d.reshape(B, S, D)
```

```python
import functools
import math

import jax
import jax.numpy as jnp
import numpy as np
from jax import lax
from jax.experimental import pallas as pl
from jax.experimental.pallas import tpu as pltpu

F32 = jnp.float32
BF16 = jnp.bfloat16
NEG_INF = float("-inf")
LOG2E = math.log2(math.e)

RMS_EPS = 1e-6

ML_HEADS = 4
ML_DQK = 128
ML_DV = 256
ML_LC = 256

NSA_HEADS = 16
NSA_GROUPS = 4
NSA_HPG = 4
NSA_HD = 64
CMP_BLOCK = 32
CMP_STRIDE = 16
SLC_BLOCK = 64
SLC_TOPK = 8
WINDOW = 512
FORCE_SCORE = 1e4
REL_BUCKETS = 32
REL_MAX_DIST = 128
ATT_QB = 128
ATT_TK = 128
NCMP_PAD = 128
KV_PAD = WINDOW
FAR_CHUNK = 512
VT_ROWS = 80
SEL_ROWS = 2 * KV_PAD // SLC_BLOCK + 2 * 16

VMEM_LIMIT = 48 * 1024 * 1024


def _cparams(sem):
    return pltpu.CompilerParams(dimension_semantics=sem, vmem_limit_bytes=VMEM_LIMIT)


def _dot(a, b):
    return jnp.dot(a, b, preferred_element_type=F32)


def _dot_nt(a, b):
    return lax.dot_general(a, b, (((1,), (1,)), ((), ())), preferred_element_type=F32)


def _dot_tn(a, b):
    return lax.dot_general(a, b, (((0,), (0,)), ((), ())), preferred_element_type=F32)


def _norm_mod(x, g, sh, sc):
    var = jnp.mean(x * x, axis=-1, keepdims=True)
    y = x * lax.rsqrt(var + RMS_EPS) * g
    return y * (1.0 + sc) + sh


def _ada_body(c_ref, w_ref, b_ref, o_ref):
    c = c_ref[...]
    ca = c * jax.nn.sigmoid(c)
    o_ref[...] = jnp.dot(ca, w_ref[...], preferred_element_type=F32,
                         precision=lax.Precision.HIGHEST) + b_ref[...]


def ada_mod(c, w, b, tn=1024):
    L, D, N = w.shape
    B = c.shape[0]
    return pl.pallas_call(
        _ada_body,
        grid=(L, N // tn),
        in_specs=[pl.BlockSpec((B, D), lambda l, j: (0, 0)),
                  pl.BlockSpec((None, D, tn), lambda l, j: (l, 0, j)),
                  pl.BlockSpec((None, 1, tn), lambda l, j: (l, 0, j))],
        out_specs=pl.BlockSpec((None, B, tn), lambda l, j: (l, 0, j)),
        out_shape=jax.ShapeDtypeStruct((L, B, N), F32),
        compiler_params=_cparams(("parallel", "parallel")),
        name="ada_mod",
    )(c, w, b.reshape(L, 1, N))


def _norm_proj_body(n_out, has_bias, tn, x_ref, g_ref, sh_ref, sc_ref, *refs):
    w_refs = refs[:n_out]
    b_refs = refs[n_out:2 * n_out]
    o_refs = refs[2 * n_out:3 * n_out]
    h = _norm_mod(x_ref[...], g_ref[...], sh_ref[...], sc_ref[...]).astype(BF16)
    for w_ref, b_ref, o_ref, hb in zip(w_refs, b_refs, o_refs, has_bias):
        n = w_ref.shape[1]
        step = min(tn, n)
        for n0 in range(0, n, step):
            acc = _dot(h, w_ref[:, n0:n0 + step])
            if hb:
                acc = acc + b_ref[:, n0:n0 + step]
            o_ref[:, n0:n0 + step] = acc.astype(o_ref.dtype)


def norm_proj(x2d, g, sh, sc, ws, biases, out_dtypes, tm=512, tn=512):
    M, D = x2d.shape
    B = sh.shape[0]
    tiles_per_batch = (M // B) // tm
    n_out = len(ws)
    has_bias = tuple(b is not None for b in biases)
    bias_args = [(b if b is not None else jnp.zeros((w.shape[1],), F32)).reshape(1, -1)
                 for b, w in zip(biases, ws)]
    in_specs = [pl.BlockSpec((tm, D), lambda i: (i, 0)),
                pl.BlockSpec((1, D), lambda i: (0, 0)),
                pl.BlockSpec((None, 1, D), lambda i: (i // tiles_per_batch, 0, 0)),
                pl.BlockSpec((None, 1, D), lambda i: (i // tiles_per_batch, 0, 0))]
    in_specs += [pl.BlockSpec(w.shape, lambda i: (0, 0)) for w in ws]
    in_specs += [pl.BlockSpec(b.shape, lambda i: (0, 0)) for b in bias_args]
    out_specs = [pl.BlockSpec((tm, w.shape[1]), lambda i: (i, 0)) for w in ws]
    out_shape = [jax.ShapeDtypeStruct((M, w.shape[1]), dt) for w, dt in zip(ws, out_dtypes)]
    return pl.pallas_call(
        functools.partial(_norm_proj_body, n_out, has_bias, tn),
        grid=(M // tm,),
        in_specs=in_specs, out_specs=out_specs, out_shape=out_shape,
        compiler_params=_cparams(("parallel",)),
        name="norm_proj",
    )(x2d, g.reshape(1, D), sh, sc, *ws, *bias_args)


def _gate_proj_t_body(x_ref, g_ref, sh_ref, sc_ref, wt_ref, b_ref, o_ref):
    h = _norm_mod(x_ref[...], g_ref[...], sh_ref[...], sc_ref[...]).astype(BF16)
    o_ref[...] = _dot_nt(wt_ref[...], h) + b_ref[...]


def gate_proj_t(x2d, g, sh, sc, wt, bias, seq, tm=512):
    M, D = x2d.shape
    B = sh.shape[0]
    NG = wt.shape[0]
    tiles_per_batch = seq // tm
    return pl.pallas_call(
        _gate_proj_t_body,
        grid=(M // tm,),
        in_specs=[pl.BlockSpec((tm, D), lambda i: (i, 0)),
                  pl.BlockSpec((1, D), lambda i: (0, 0)),
                  pl.BlockSpec((None, 1, D), lambda i: (i // tiles_per_batch, 0, 0)),
                  pl.BlockSpec((None, 1, D), lambda i: (i // tiles_per_batch, 0, 0)),
                  pl.BlockSpec((NG, D), lambda i: (0, 0)),
                  pl.BlockSpec((NG, 1), lambda i: (0, 0))],
        out_specs=pl.BlockSpec((None, NG, tm),
                               lambda i: (i // tiles_per_batch, 0, i % tiles_per_batch)),
        out_shape=jax.ShapeDtypeStruct((B, NG, seq), F32),
        compiler_params=_cparams(("parallel",)),
        name="gate_proj_t",
    )(x2d, g.reshape(1, D), sh, sc, wt, bias.reshape(NG, 1))


def _proj_residual_body(a_ref, w_ref, x_ref, ga_ref, o_ref):
    o_ref[...] = x_ref[...] + ga_ref[...] * _dot(a_ref[...], w_ref[...])


def proj_residual(a, w, x2d, ga, tm=512):
    M, K = a.shape
    D = w.shape[1]
    B = ga.shape[0]
    tiles_per_batch = (M // B) // tm
    return pl.pallas_call(
        _proj_residual_body,
        grid=(M // tm,),
        in_specs=[pl.BlockSpec((tm, K), lambda i: (i, 0)),
                  pl.BlockSpec((K, D), lambda i: (0, 0)),
                  pl.BlockSpec((tm, D), lambda i: (i, 0)),
                  pl.BlockSpec((None, 1, D), lambda i: (i // tiles_per_batch, 0, 0))],
        out_specs=pl.BlockSpec((tm, D), lambda i: (i, 0)),
        out_shape=jax.ShapeDtypeStruct((M, D), F32),
        compiler_params=_cparams(("parallel",)),
        name="proj_residual",
    )(a, w, x2d, ga)


def _ffn_body(x_ref, g_ref, sh_ref, sc_ref, ga_ref, wg_ref, wu_ref, wo_ref, o_ref, h_s, acc_s):
    f = pl.program_id(1)

    @pl.when(f == 0)
    def _():
        h_s[...] = _norm_mod(x_ref[...], g_ref[...], sh_ref[...], sc_ref[...]).astype(BF16)

    h = h_s[...]
    gate = _dot(h, wg_ref[...])
    up = _dot(h, wu_ref[...])
    act = (gate * jax.nn.sigmoid(gate) * up).astype(BF16)
    part = _dot(act, wo_ref[...])

    @pl.when(f == 0)
    def _():
        acc_s[...] = part

    @pl.when(f > 0)
    def _():
        acc_s[...] += part

    @pl.when(f == pl.num_programs(1) - 1)
    def _():
        o_ref[...] = x_ref[...] + ga_ref[...] * acc_s[...]


def ffn(x2d, g, sh, sc, ga, w_in, w_out, tm=1024, tf=256):
    M, D = x2d.shape
    F = w_out.shape[0]
    B = sh.shape[0]
    tiles_per_batch = (M // B) // tm
    nf = F // tf
    bvec = pl.BlockSpec((None, 1, D), lambda i, f: (i // tiles_per_batch, 0, 0))
    return pl.pallas_call(
        _ffn_body,
        grid=(M // tm, nf),
        in_specs=[pl.BlockSpec((tm, D), lambda i, f: (i, 0)),
                  pl.BlockSpec((1, D), lambda i, f: (0, 0)),
                  bvec, bvec, bvec,
                  pl.BlockSpec((D, tf), lambda i, f: (0, f)),
                  pl.BlockSpec((D, tf), lambda i, f: (0, nf + f)),
                  pl.BlockSpec((tf, D), lambda i, f: (f, 0))],
        out_specs=pl.BlockSpec((tm, D), lambda i, f: (i, 0)),
        out_shape=jax.ShapeDtypeStruct((M, D), F32),
        scratch_shapes=[pltpu.VMEM((tm, D), BF16), pltpu.VMEM((tm, D), F32)],
        compiler_params=_cparams(("parallel", "arbitrary")),
        name="ffn",
    )(x2d, g.reshape(1, D), sh, sc, ga, w_in, w_in, w_out)


def _lane_cumsum(x):
    n = x.shape[-1]
    lane = lax.broadcasted_iota(jnp.int32, x.shape, 1)
    sh = 1
    while sh < n:
        x = x + jnp.where(lane >= sh, pltpu.roll(x, sh, axis=1), 0.0)
        sh *= 2
    return x


def _mlstm_body(q_ref, k_ref, v_ref, o_ref, gt_ref, gout_ref, out_ref, c_s, n_s, m_s):
    c_idx = pl.program_id(1)
    LC = q_ref.shape[0]
    scale = ML_DQK ** -0.5

    @pl.when(c_idx == 0)
    def _():
        c_s[...] = jnp.zeros_like(c_s)
        n_s[...] = jnp.zeros_like(n_s)
        m_s[...] = jnp.zeros_like(m_s)

    gt = gt_ref[...]
    fpre = gt[ML_HEADS:2 * ML_HEADS, :]
    logf = jnp.minimum(fpre, 0.0) - jnp.log1p(jnp.exp(-jnp.abs(fpre)))
    bcum = _lane_cumsum(logf)

    row = lax.broadcasted_iota(jnp.int32, (LC, LC), 0)
    col = lax.broadcasted_iota(jnp.int32, (LC, LC), 1)
    causal = col <= row
    diag = col == row

    for h in range(ML_HEADS):
        li = gt[h:h + 1, :]
        lf = logf[h:h + 1, :]
        b_r = bcum[h:h + 1, :]
        a_r = li - b_r
        b_last = b_r[:, LC - 1:LC]
        m_prev = m_s[h:h + 1, 0:1]
        m_loc = b_last + jnp.max(a_r, axis=-1, keepdims=True)
        m_new = jnp.maximum(b_last + m_prev, m_loc)
        e_r = jnp.exp(b_last + a_r - m_new)
        decay = jnp.exp(b_last + m_prev - m_new)

        cm_col = jnp.max(jnp.where(causal, a_r, NEG_INF), axis=-1, keepdims=True)
        b_col = jnp.sum(jnp.where(causal, lf, 0.0), axis=-1, keepdims=True)
        e_col = jnp.sum(jnp.where(diag, e_r, 0.0), axis=-1, keepdims=True)
        g_col = jnp.maximum(m_prev, cm_col)
        w_intra = jnp.exp(jnp.where(causal, a_r - g_col, NEG_INF))
        w_inter = jnp.exp(m_prev - g_col)
        floor = jnp.exp(-(b_col + g_col))

        qh = q_ref[:, h * ML_DQK:(h + 1) * ML_DQK]
        kh = k_ref[:, h * ML_DQK:(h + 1) * ML_DQK]
        vh = v_ref[:, h * ML_DV:(h + 1) * ML_DV]
        c_prev = c_s[h]
        n_prev = n_s[h:h + 1, :]

        s = _dot_nt(qh, kh) * scale * w_intra
        inter = _dot(qh, c_prev.astype(BF16)) * scale
        num = _dot(s.astype(BF16), vh) + w_inter * inter
        qn_inter = jnp.sum(qh.astype(F32) * n_prev, axis=-1, keepdims=True) * scale
        qn = jnp.sum(s, axis=-1, keepdims=True) + w_inter * qn_inter
        denom = jnp.maximum(jnp.abs(qn), floor)
        hout = num / denom

        ke = kh.astype(F32) * e_col
        c_s[h] = decay * c_prev + _dot_tn(ke.astype(BF16), vh)
        n_s[h:h + 1, :] = decay * n_prev + jnp.sum(ke, axis=0, keepdims=True)
        m_s[h:h + 1, :] = jnp.broadcast_to(m_new, (1, m_s.shape[1]))

        var = jnp.mean(hout * hout, axis=-1, keepdims=True)
        hn = hout * lax.rsqrt(var + RMS_EPS) * gout_ref[:, h * ML_DV:(h + 1) * ML_DV]
        og = jax.nn.sigmoid(o_ref[:, h * ML_DV:(h + 1) * ML_DV].astype(F32))
        out_ref[:, h * ML_DV:(h + 1) * ML_DV] = (hn * og).astype(out_ref.dtype)


def mlstm_core(proj, gates_t, g_out, batch, seq):
    LC = ML_LC
    nc = seq // LC
    qk = ML_HEADS * ML_DQK
    vd = ML_HEADS * ML_DV
    row = lambda b, c: b * nc + c
    return pl.pallas_call(
        _mlstm_body,
        grid=(batch, nc),
        in_specs=[pl.BlockSpec((LC, qk), lambda b, c: (row(b, c), 0)),
                  pl.BlockSpec((LC, qk), lambda b, c: (row(b, c), 1)),
                  pl.BlockSpec((LC, vd), lambda b, c: (row(b, c), 1)),
                  pl.BlockSpec((LC, vd), lambda b, c: (row(b, c), 2)),
                  pl.BlockSpec((None, 2 * ML_HEADS, LC), lambda b, c: (b, 0, c)),
                  pl.BlockSpec((1, vd), lambda b, c: (0, 0))],
        out_specs=pl.BlockSpec((LC, vd), lambda b, c: (row(b, c), 0)),
        out_shape=jax.ShapeDtypeStruct((batch * seq, vd), BF16),
        scratch_shapes=[pltpu.VMEM((ML_HEADS, ML_DQK, ML_DV), F32),
                        pltpu.VMEM((8, ML_DQK), F32),
                        pltpu.VMEM((8, 128), F32)],
        compiler_params=_cparams(("parallel", "arbitrary")),
        name="mlstm_core",
    )(proj, proj, proj, proj, gates_t, g_out.reshape(1, vd))


def _kv_proj_body(x_ref, g_ref, sh_ref, sc_ref, wk_ref, wvt_ref, wcc_ref, gk_ref,
                  ks_ref, kw_ref, vst_ref, vwt_ref, cc_ref):
    i = pl.program_id(1)
    G, HD = NSA_GROUPS, NSA_HD
    tm = x_ref.shape[0]

    @pl.when(i == 0)
    def _():
        ks_ref[...] = jnp.zeros_like(ks_ref)
        kw_ref[...] = jnp.zeros_like(kw_ref)
        vst_ref[...] = jnp.zeros_like(vst_ref)
        vwt_ref[...] = jnp.zeros_like(vwt_ref)

    @pl.when(i > 0)
    def _():
        h = _norm_mod(x_ref[...], g_ref[...], sh_ref[...], sc_ref[...]).astype(BF16)
        cc_ref[...] = _dot(h, wcc_ref[...])
        kk = _dot(h, wk_ref[...])
        lane = lax.broadcasted_iota(jnp.int32, (tm, 128 - HD), 1)
        tail = jnp.where(lane < 2, 1.0, 0.0).astype(BF16)
        for kind, o_ref in enumerate((ks_ref, kw_ref)):
            for g in range(G):
                seg = kk[:, (kind * G + g) * HD:(kind * G + g + 1) * HD]
                var = jnp.mean(seg * seg, axis=-1, keepdims=True)
                segn = seg * lax.rsqrt(var + RMS_EPS) * gk_ref[kind:kind + 1, :]
                o_ref[:, g * 128:g * 128 + HD] = segn.astype(BF16)
                o_ref[:, g * 128 + HD:(g + 1) * 128] = tail
        vt = _dot_nt(wvt_ref[...], h)
        srow = lax.broadcasted_iota(jnp.int32, (VT_ROWS - HD, tm), 0)
        ones_blk = jnp.where(srow == 0, 1.0, 0.0).astype(BF16)
        for kind, o_ref in enumerate((vst_ref, vwt_ref)):
            for g in range(G):
                r0 = (kind * G + g) * HD
                o_ref[g * VT_ROWS:g * VT_ROWS + HD, :] = vt[r0:r0 + HD, :].astype(BF16)
                o_ref[g * VT_ROWS + HD:(g + 1) * VT_ROWS, :] = ones_blk


def kv_proj(x2d, g, sh, sc, wk, wvt, wcc, gk, batch, seq, tm=KV_PAD):
    assert tm == KV_PAD and seq % tm == 0
    M, D = x2d.shape
    G = NSA_GROUPS
    nt = seq // tm
    sp = seq + KV_PAD
    xrow = lambda b, i: (b * nt + jnp.maximum(i - 1, 0), 0)
    bvec = pl.BlockSpec((None, 1, D), lambda b, i: (b, 0, 0))
    full = lambda a: pl.BlockSpec(a.shape, lambda b, i: (0,) * a.ndim)
    return pl.pallas_call(
        _kv_proj_body,
        grid=(batch, nt + 1),
        in_specs=[pl.BlockSpec((tm, D), xrow),
                  pl.BlockSpec((1, D), lambda b, i: (0, 0)),
                  bvec, bvec, full(wk), full(wvt), full(wcc), full(gk)],
        out_specs=[pl.BlockSpec((None, tm, G * 128), lambda b, i: (b, i, 0)),
                   pl.BlockSpec((None, tm, G * 128), lambda b, i: (b, i, 0)),
                   pl.BlockSpec((None, G * VT_ROWS, tm), lambda b, i: (b, 0, i)),
                   pl.BlockSpec((None, G * VT_ROWS, tm), lambda b, i: (b, 0, i)),
                   pl.BlockSpec((tm, wcc.shape[1]), xrow)],
        out_shape=[jax.ShapeDtypeStruct((batch, sp, G * 128), BF16),
                   jax.ShapeDtypeStruct((batch, sp, G * 128), BF16),
                   jax.ShapeDtypeStruct((batch, G * VT_ROWS, sp), BF16),
                   jax.ShapeDtypeStruct((batch, G * VT_ROWS, sp), BF16),
                   jax.ShapeDtypeStruct((M, wcc.shape[1]), F32)],
        compiler_params=_cparams(("parallel", "arbitrary")),
        name="kv_proj",
    )(x2d, g.reshape(1, D), sh, sc, wk, wvt, wcc, gk)


def _gelu_tanh(x):
    c = math.sqrt(2.0 / math.pi)
    return 0.5 * x * (1.0 + jnp.tanh(c * (x + 0.044715 * (x * x * x))))


def _compress_body(cc0_ref, cc1_ref, cc2_ref, cc3_ref, pos_ref, w1_ref, w2_ref, w2t_ref, g_ref,
                   kc_ref, vct_ref, u_s, v_s):
    G, HD = NSA_GROUPS, NSA_HD
    nwin = u_s.shape[1]
    u_s[...] = jnp.zeros_like(u_s)
    v_s[...] = jnp.zeros_like(v_s)

    def step(i, carry):
        xi = jnp.concatenate([r[pl.ds(i, nwin, stride=CMP_STRIDE), :]
                              for r in (cc0_ref, cc1_ref, cc2_ref, cc3_ref)], axis=1)
        r0 = pl.multiple_of(i * HD, HD)
        r1 = pl.multiple_of((i + CMP_STRIDE) * HD, HD)
        for kind in range(2):
            p_lo = pos_ref[kind, pl.ds(i, 1), :]
            p_hi = pos_ref[kind, pl.ds(i + CMP_STRIDE, 1), :]
            w_lo = w1_ref[kind, pl.ds(r0, HD), :]
            w_hi = w1_ref[kind, pl.ds(r1, HD), :]
            for g in range(G):
                j = kind * G + g
                seg = xi[:, j * HD:(j + 1) * HD]
                u_s[j] += _dot((seg + p_lo).astype(BF16), w_lo)
                v_s[j] += _dot((seg + p_hi).astype(BF16), w_hi)
        return carry

    lax.fori_loop(0, CMP_STRIDE, step, 0)

    for kind in range(2):
        for g in range(G):
            j = kind * G + g
            pre = u_s[j] + pltpu.roll(v_s[j], nwin - 1, axis=0)
            hmid = _gelu_tanh(pre).astype(BF16)
            if kind == 0:
                y = _dot(hmid, w2_ref[...])
                var = jnp.mean(y * y, axis=-1, keepdims=True)
                yn = y * lax.rsqrt(var + RMS_EPS) * g_ref[...]
                kc_ref[:, g * 128:g * 128 + HD] = yn.astype(BF16)
                kc_ref[:, g * 128 + HD:(g + 1) * 128] = jnp.zeros((nwin, 128 - HD), BF16)
            else:
                vct_ref[g * HD:(g + 1) * HD, :] = _dot_nt(w2t_ref[...], hmid).astype(BF16)


def compress(cc, pos, w1, w2k, w2vt, g, batch, seq):
    G, HD = NSA_GROUPS, NSA_HD
    nwin = seq // CMP_STRIDE
    hid = w1.shape[2]
    full = lambda a: pl.BlockSpec(a.shape, lambda b: (0,) * a.ndim)
    return pl.pallas_call(
        _compress_body,
        grid=(batch,),
        in_specs=[pl.BlockSpec((seq, 128), functools.partial(lambda q, b: (b, q), q))
                  for q in range(4)]
                 + [full(pos), full(w1), full(w2k), full(w2vt), full(g)],
        out_specs=[pl.BlockSpec((None, nwin, G * 128), lambda b: (b, 0, 0)),
                   pl.BlockSpec((None, G * HD, nwin), lambda b: (b, 0, 0))],
        out_shape=[jax.ShapeDtypeStruct((batch, nwin, G * 128), BF16),
                   jax.ShapeDtypeStruct((batch, G * HD, nwin), BF16)],
        scratch_shapes=[pltpu.VMEM((2 * G, nwin, hid), F32), pltpu.VMEM((2 * G, nwin, hid), F32)],
        compiler_params=_cparams(("parallel",)),
        name="compress",
    )(cc, cc, cc, cc, pos, w1, w2k, w2vt, g)


def _t5_bucket(dist):
    n = jnp.maximum(dist, 0)
    max_exact = REL_BUCKETS // 2
    nf = jnp.maximum(n, 1).astype(F32)
    large = max_exact + (jnp.log(nf / max_exact) / math.log(REL_MAX_DIST / max_exact)
                         * (REL_BUCKETS - max_exact)).astype(jnp.int32)
    large = jnp.minimum(large, REL_BUCKETS - 1)
    return jnp.where(n < max_exact, n, large)


def _table_lookup(bucket, tab_ref, h):
    out = jnp.zeros(bucket.shape, F32)
    for k in range(REL_BUCKETS):
        out = jnp.where(bucket == k, tab_ref[k, h], out)
    return out


def _bias_prep_body(tab_ref, bc_ref, tp_ref, far_ref):
    h = pl.program_id(0)
    S = bc_ref.shape[1]
    cols = 128
    far = tab_ref[REL_BUCKETS - 1, h]
    far_ref[...] = jnp.full(far_ref.shape, far * LOG2E, F32)

    def chunk(r, carry):
        start = pl.multiple_of(r * cols, cols)
        n = lax.broadcasted_iota(jnp.int32, (NCMP_PAD, cols), 0)
        t = start + lax.broadcasted_iota(jnp.int32, (NCMP_PAD, cols), 1)
        dist_c = t - (n * CMP_STRIDE + CMP_BLOCK - 1)
        bias_c = _table_lookup(_t5_bucket(dist_c), tab_ref, h) * LOG2E
        bc_ref[:, pl.ds(start, cols)] = jnp.where(dist_c >= 0, bias_c, NEG_INF)
        return carry

    lax.fori_loop(0, S // cols, chunk, 0)
    j = lax.broadcasted_iota(jnp.int32, (ATT_TK, ATT_QB), 0)
    i = lax.broadcasted_iota(jnp.int32, (ATT_TK, ATT_QB), 1)
    for d in range(2):
        dist = d * ATT_TK + i - j
        rel = (_table_lookup(_t5_bucket(dist), tab_ref, h) - far) * LOG2E
        tp_ref[d] = jnp.where(dist >= 0, rel, NEG_INF)
    tp_ref[2] = jnp.where(i < j, 0.0, NEG_INF)


def bias_prep(rel_table, seq):
    assert ATT_TK == ATT_QB and ATT_TK + 1 > 113
    return pl.pallas_call(
        _bias_prep_body,
        grid=(NSA_HEADS,),
        in_specs=[pl.BlockSpec(memory_space=pltpu.SMEM)],
        out_specs=[pl.BlockSpec((None, NCMP_PAD, seq), lambda h: (h, 0, 0)),
                   pl.BlockSpec((None, 3, ATT_TK, ATT_QB), lambda h: (h, 0, 0, 0)),
                   pl.BlockSpec((None, 8, 128), lambda h: (h, 0, 0))],
        out_shape=[jax.ShapeDtypeStruct((NSA_HEADS, NCMP_PAD, seq), F32),
                   jax.ShapeDtypeStruct((NSA_HEADS, 3, ATT_TK, ATT_QB), F32),
                   jax.ShapeDtypeStruct((NSA_HEADS, 8, 128), F32)],
        compiler_params=_cparams(("parallel",)),
        name="bias_prep",
    )(rel_table)


def _q_proj_body(x_ref, g_ref, sh_ref, sc_ref, wqt_ref, wgt_ref, bg_ref, gq_ref, qt_ref, gt_ref):
    HD = NSA_HD
    h = _norm_mod(x_ref[...], g_ref[...], sh_ref[...], sc_ref[...]).astype(BF16)
    gt_ref[...] = _dot_nt(wgt_ref[...], h) + bg_ref[...]
    qt = _dot_nt(wqt_ref[...], h)
    scale = gq_ref[...] * (HD ** -0.5 * LOG2E)
    for hh in range(NSA_HEADS):
        seg = qt[hh * HD:(hh + 1) * HD, :]
        var = jnp.mean(seg * seg, axis=0, keepdims=True)
        qt_ref[hh * HD:(hh + 1) * HD, :] = (seg * lax.rsqrt(var + RMS_EPS) * scale).astype(BF16)


def q_proj(x2d, g, sh, sc, wqt, wgt, bg, gq, batch, seq, tm=512):
    M, D = x2d.shape
    nt = seq // tm
    nq = wqt.shape[0]
    ng = wgt.shape[0]
    bvec = pl.BlockSpec((None, 1, D), lambda i: (i // nt, 0, 0))
    full = lambda a: pl.BlockSpec(a.shape, lambda i: (0,) * a.ndim)
    return pl.pallas_call(
        _q_proj_body,
        grid=(M // tm,),
        in_specs=[pl.BlockSpec((tm, D), lambda i: (i, 0)),
                  pl.BlockSpec((1, D), lambda i: (0, 0)),
                  bvec, bvec, full(wqt), full(wgt), full(bg), full(gq)],
        out_specs=[pl.BlockSpec((None, nq, tm), lambda i: (i // nt, 0, i % nt)),
                   pl.BlockSpec((None, ng, tm), lambda i: (i // nt, 0, i % nt))],
        out_shape=[jax.ShapeDtypeStruct((batch, nq, seq), BF16),
                   jax.ShapeDtypeStruct((batch, ng, seq), F32)],
        compiler_params=_cparams(("parallel",)),
        name="q_proj",
    )(x2d, g.reshape(1, D), sh, sc, wqt, wgt, bg, gq)


def _heads_on_lanes(pieces):
    return jnp.concatenate(pieces, axis=1)


def _nsa_body(qt_ref, gt_ref, kc_ref, vct_ref, ks_ref, kw_ref, vst_ref, vwt_ref,
              bct_ref, tp_ref, far_ref, ovt_ref, out_ref, qm_s, msel_s):
    qb = pl.program_id(2)
    QB, HD, HPG = ATT_QB, NSA_HD, NSA_HPG
    R = HPG * QB
    t0 = pl.multiple_of(qb * QB, QB)

    srow = lax.broadcasted_iota(jnp.int32, (16, QB), 0)
    for h in range(HPG):
        qm_s[0:HD, h * QB:(h + 1) * QB] = qt_ref[h * HD:(h + 1) * HD, :]
        far = far_ref[h, 0:1, :]
        hi = far.astype(BF16).astype(F32)
        lo = far - hi
        extra = jnp.where(srow == 0, hi, jnp.where(srow == 1, lo, 0.0))
        qm_s[HD:HD + 16, h * QB:(h + 1) * QB] = extra.astype(BF16)
    qm_s[HD + 16:, :] = jnp.zeros((128 - HD - 16, R), BF16)
    qm = qm_s[...]

    s_c = _dot(kc_ref[...], qm) + _heads_on_lanes([bct_ref[h] for h in range(HPG)])
    m_c = jnp.max(s_c, axis=0, keepdims=True)
    m_c = jnp.where(m_c == NEG_INF, 0.0, m_c)
    e_c = jnp.exp2(s_c - m_c)
    p_c = e_c / jnp.maximum(jnp.sum(e_c, axis=0, keepdims=True), jnp.finfo(F32).tiny)
    o_cmp = _dot(vct_ref[...], p_c.astype(BF16))

    p_sum = p_c[:, 0:QB]
    for h in range(1, HPG):
        p_sum = p_sum + p_c[:, h * QB:(h + 1) * QB]
    nblk = ovt_ref.shape[0]
    imp = jnp.dot(ovt_ref[...], p_sum, preferred_element_type=F32,
                  precision=lax.Precision.HIGHEST)
    jb = lax.broadcasted_iota(jnp.int32, (nblk, QB), 0)
    qid = jnp.right_shift(t0 + lax.broadcasted_iota(jnp.int32, (nblk, QB), 1), 6)
    forced = (jb == 0) | (jb == qid) | (jb == qid - 1)
    score = jnp.where(forced, FORCE_SCORE, imp)
    score = jnp.where(jb <= qid, score, NEG_INF)
    jbf = jb.astype(F32)
    msel = jnp.full((nblk, QB), NEG_INF, F32)
    for _ in range(SLC_TOPK):
        mx = jnp.max(score, axis=0, keepdims=True)
        first = jnp.min(jnp.where(score == mx, jbf, float(nblk)), axis=0, keepdims=True)
        pick = jbf == first
        msel = jnp.where(pick & (mx > NEG_INF), 0.0, msel)
        score = jnp.where(pick, NEG_INF, score)
    pad_blocks = KV_PAD // SLC_BLOCK
    msel_s[0:pad_blocks, :] = jnp.full((pad_blocks, QB), NEG_INF, F32)
    msel_s[pad_blocks:pad_blocks + nblk, :] = msel
    msel_s[pad_blocks + nblk:, :] = jnp.full((msel_s.shape[0] - pad_blocks - nblk, QB), NEG_INF, F32)

    def block_mask(first_row, nblocks):
        rows = msel_s[pl.ds(first_row, nblocks), :]
        one_head = jnp.concatenate(
            [jnp.broadcast_to(rows[r:r + 1, :], (SLC_BLOCK, QB)) for r in range(nblocks)], axis=0)
        return _heads_on_lanes([one_head] * HPG)

    def finish(acc):
        return acc[0:HD, :] / acc[HD:HD + 1, :]

    wlen = WINDOW + QB
    s_w = _dot(kw_ref[pl.ds(t0, wlen), :], qm)
    pens = [jnp.where(qb >= d, 0.0, NEG_INF) for d in range(5)]

    def win_bias(h):
        return jnp.concatenate([
            tp_ref[h, 2] + pens[4],
            jnp.full((ATT_TK, QB), pens[3], F32),
            jnp.full((ATT_TK, QB), pens[2], F32),
            tp_ref[h, 1] + pens[1],
            tp_ref[h, 0]], axis=0)

    s_w = s_w + _heads_on_lanes([win_bias(h) for h in range(HPG)])
    m_w = jnp.max(s_w, axis=0, keepdims=True)
    p_w = jnp.exp2(s_w - m_w).astype(BF16)
    o_win = finish(_dot(vwt_ref[:, pl.ds(t0, wlen)], p_w))

    near0 = pl.multiple_of(t0 + WINDOW - ATT_TK, ATT_TK)
    s_n = _dot(ks_ref[pl.ds(near0, 2 * ATT_TK), :], qm)
    prev_blocks = msel_s[pl.ds(2 * qb + pad_blocks - 2, 2), :]
    prev_mask = jnp.concatenate(
        [jnp.broadcast_to(prev_blocks[r:r + 1, :], (SLC_BLOCK, QB)) for r in range(2)], axis=0)
    s_n = s_n + _heads_on_lanes(
        [jnp.concatenate([tp_ref[h, 1] + prev_mask, tp_ref[h, 0]], axis=0) for h in range(HPG)])
    m_n = jnp.max(s_n, axis=0, keepdims=True)
    p_n = jnp.exp2(s_n - m_n).astype(BF16)
    acc_n = _dot(vst_ref[:, pl.ds(near0, 2 * ATT_TK)], p_n)

    def far_step(c, carry):
        m_run, acc = carry
        tile0 = qb + (WINDOW - ATT_TK) // ATT_TK - (c + 1) * (FAR_CHUNK // ATT_TK)
        row0 = pl.multiple_of(tile0 * ATT_TK, ATT_TK)
        s_f = _dot(ks_ref[pl.ds(row0, FAR_CHUNK), :], qm)
        s_f = s_f + block_mask(tile0 * (ATT_TK // SLC_BLOCK), FAR_CHUNK // SLC_BLOCK)
        m_new = jnp.maximum(m_run, jnp.max(s_f, axis=0, keepdims=True))
        alpha = jnp.exp2(m_run - m_new)
        p_f = jnp.exp2(s_f - m_new).astype(BF16)
        acc = alpha * acc + _dot(vst_ref[:, pl.ds(row0, FAR_CHUNK)], p_f)
        return m_new, acc

    n_far = (qb + 2) // 4
    _, acc_s = lax.fori_loop(0, n_far, far_step, (m_n, acc_n))
    o_sel = finish(acc_s)

    gates = jax.nn.sigmoid(gt_ref[...])
    outs = []
    for h in range(HPG):
        lanes = slice(h * QB, (h + 1) * QB)
        outs.append(gates[3 * h:3 * h + 1, :] * o_cmp[:, lanes]
                    + gates[3 * h + 1:3 * h + 2, :] * o_sel[:, lanes]
                    + gates[3 * h + 2:3 * h + 3, :] * o_win[:, lanes])
    for pair in range(HPG // 2):
        two = jnp.concatenate(outs[2 * pair:2 * pair + 2], axis=0)
        out_ref[:, pair * 2 * HD:(pair + 1) * 2 * HD] = two.T.astype(out_ref.dtype)


def _overlap_matrix_t(nblk):
    start = np.arange(NCMP_PAD) * CMP_STRIDE
    sj = np.arange(nblk) * SLC_BLOCK
    ov = (np.minimum(start[None, :] + CMP_BLOCK, sj[:, None] + SLC_BLOCK)
          - np.maximum(start[None, :], sj[:, None]))
    ov = np.clip(ov, 0, None) / CMP_BLOCK
    ov[:, NCMP_PAD - 1] = 0.0
    return ov.astype(np.float32)


def nsa_attn(qt, gt, kc, vct, ks, kw, vst, vwt, bias_ct, tiles, far, batch, seq):
    QB, HD, HPG, G = ATT_QB, NSA_HD, NSA_HPG, NSA_GROUPS
    nblk = seq // SLC_BLOCK
    assert nblk % 8 == 0 and seq % QB == 0 and seq // CMP_STRIDE == NCMP_PAD
    assert WINDOW % ATT_TK == 0 and FAR_CHUNK == 4 * ATT_TK and KV_PAD >= FAR_CHUNK - ATT_TK
    assert SEL_ROWS >= KV_PAD // SLC_BLOCK + nblk
    nq = seq // QB
    sp = seq + KV_PAD
    R = HPG * QB
    ovt = jnp.asarray(_overlap_matrix_t(nblk))
    return pl.pallas_call(
        _nsa_body,
        grid=(batch, G, nq),
        in_specs=[pl.BlockSpec((None, HPG * HD, QB), lambda b, g, i: (b, g, i)),
                  pl.BlockSpec((None, 16, QB), lambda b, g, i: (b, g, i)),
                  pl.BlockSpec((None, NCMP_PAD, 128), lambda b, g, i: (b, 0, g)),
                  pl.BlockSpec((None, HD, NCMP_PAD), lambda b, g, i: (b, g, 0)),
                  pl.BlockSpec((None, sp, 128), lambda b, g, i: (b, 0, g)),
                  pl.BlockSpec((None, sp, 128), lambda b, g, i: (b, 0, g)),
                  pl.BlockSpec((None, VT_ROWS, sp), lambda b, g, i: (b, g, 0)),
                  pl.BlockSpec((None, VT_ROWS, sp), lambda b, g, i: (b, g, 0)),
                  pl.BlockSpec((HPG, NCMP_PAD, QB), lambda b, g, i: (g, 0, i)),
                  pl.BlockSpec((HPG, 3, ATT_TK, QB), lambda b, g, i: (g, 0, 0, 0)),
                  pl.BlockSpec((HPG, 8, 128), lambda b, g, i: (g, 0, 0)),
                  pl.BlockSpec((nblk, NCMP_PAD), lambda b, g, i: (0, 0))],
        out_specs=pl.BlockSpec((QB, HPG * HD), lambda b, g, i: (b * nq + i, g)),
        out_shape=jax.ShapeDtypeStruct((batch * seq, NSA_HEADS * HD), BF16),
        scratch_shapes=[pltpu.VMEM((128, R), BF16),
                        pltpu.VMEM((SEL_ROWS, QB), F32)],
        compiler_params=_cparams(("parallel", "parallel", "arbitrary")),
        name="nsa_attn",
    )(qt, gt, kc, vct, ks, kw, vst, vwt, bias_ct, tiles, far, ovt)


def _mlstm_layer(x2d, g_mix, sh, sc, ga, w_in, b_if, g_out, w_out, batch, seq):
    nbig = 2 * ML_HEADS * ML_DQK + 2 * ML_HEADS * ML_DV
    (proj,) = norm_proj(x2d, g_mix, sh, sc, [w_in[:, :nbig].astype(BF16)], [None], [BF16])
    gates_t = gate_proj_t(x2d, g_mix, sh, sc, w_in[:, nbig:].T.astype(BF16), b_if, seq)
    hg = mlstm_core(proj, gates_t, g_out, batch, seq)
    return proj_residual(hg, w_out.astype(BF16), x2d, ga)


def _nsa_shared(x2d, g_kv, kv_sh, kv_sc, w_kv, pos_k, w_k1, w_k2, pos_v, w_v1, w_v2,
                g_knorm, batch, seq):
    gw = NSA_GROUPS * NSA_HD
    part = lambda i: w_kv[:, i * gw:(i + 1) * gw]
    wk = jnp.concatenate([part(2), part(4)], axis=1).astype(BF16)
    wvt = jnp.concatenate([part(3), part(5)], axis=1).T.astype(BF16)
    wcc = jnp.concatenate([part(0), part(1)], axis=1).astype(BF16)
    ks, kw, vst, vwt, cc = kv_proj(x2d, g_kv, kv_sh, kv_sc, wk, wvt, wcc, g_knorm[1:3],
                                   batch, seq)
    pos = jnp.stack([pos_k, pos_v])
    w1 = jnp.stack([w_k1, w_v1]).astype(BF16)
    kc, vct = compress(cc, pos, w1, w_k2.astype(BF16), w_v2.T.astype(BF16), g_knorm[0:1],
                       batch, seq)
    return kc, vct, ks, kw, vst, vwt


def _gate_weights_t(w_q, b_gate):
    nq = NSA_HEADS * NSA_HD
    per = 3 * NSA_HPG
    wg = w_q[:, nq:].T.reshape(NSA_GROUPS, per, -1)
    wg = jnp.pad(wg, ((0, 0), (0, 16 - per), (0, 0))).reshape(NSA_GROUPS * 16, -1)
    bg = jnp.pad(b_gate.reshape(NSA_GROUPS, per), ((0, 0), (0, 16 - per))).reshape(-1, 1)
    return wg.astype(BF16), bg


def _nsa_layer(x2d, g_mix, sh, sc, ga, shared, w_q, b_gate, g_qnorm, w_out, bias_ct, tiles, far,
               batch, seq):
    nq = NSA_HEADS * NSA_HD
    wgt, bg = _gate_weights_t(w_q, b_gate)
    qt, gt = q_proj(x2d, g_mix, sh, sc, w_q[:, :nq].T.astype(BF16), wgt, bg,
                    g_qnorm.reshape(NSA_HD, 1), batch, seq)
    kc, vct, ks, kw, vst, vwt = shared
    att = nsa_attn(qt, gt, kc, vct, ks, kw, vst, vwt, bias_ct, tiles, far, batch, seq)
    return proj_residual(att, w_out.astype(BF16), x2d, ga)


def kernel(x, c, w_ada, b_ada, g_norm_mix, g_norm_ffn, w_ffn_in, w_ffn_out, w_a_in, b_a_if, g_a_out, w_a_out, w_kv_ada, b_kv_ada, g_kv_norm, w_kv, pos_cmp_k, w_cmp_k1, w_cmp_k2, pos_cmp_v, w_cmp_v1, w_cmp_v2, g_knorm, w_b_q, b_b_gate, g_qnorm, w_b_out, rel_table):
    B, S, D = x.shape
    depth = w_ada.shape[0]
    n_a = w_a_in.shape[0]
    x2d = x.reshape(B * S, D)
    mods = ada_mod(c, w_ada, b_ada)
    kv_mod = ada_mod(c, w_kv_ada[None], b_kv_ada[None])[0]
    shared = None
    bias_ct = tiles = far = None
    for layer in range(depth):
        sh1, sc1, ga1, sh2, sc2, ga2 = [mods[layer, :, i * D:(i + 1) * D].reshape(B, 1, D)
                                        for i in range(6)]
        if layer < n_a:
            x2d = _mlstm_layer(x2d, g_norm_mix[layer], sh1, sc1, ga1, w_a_in[layer],
                               b_a_if[layer], g_a_out[layer], w_a_out[layer], B, S)
        else:
            j = layer - n_a
            if shared is None:
                kv_sh = kv_mod[:, :D].reshape(B, 1, D)
                kv_sc = kv_mod[:, D:].reshape(B, 1, D)
                shared = _nsa_shared(x2d, g_kv_norm, kv_sh, kv_sc, w_kv, pos_cmp_k, w_cmp_k1,
                                     w_cmp_k2, pos_cmp_v, w_cmp_v1, w_cmp_v2, g_knorm, B, S)
                bias_ct, tiles, far = bias_prep(rel_table, S)
            x2d = _nsa_layer(x2d, g_norm_mix[layer], sh1, sc1, ga1, shared, w_b_q[j],
                             b_b_gate[j], g_qnorm[j], w_b_out[j], bias_ct, tiles, far, B, S)
        x2d = ffn(x2d, g_norm_ffn[layer], sh2, sc2, ga2, w_ffn_in[layer].astype(BF16),
                  w_ffn_out[layer].astype(BF16))
    return x2d.reshape(B, S, D)
```

```python
import functools
import math

import jax
import jax.numpy as jnp
import numpy as np
from jax import lax
from jax.experimental import pallas as pl
from jax.experimental.pallas import tpu as pltpu

F32 = jnp.float32
BF16 = jnp.bfloat16
NEG_INF = float("-inf")
LOG2E = math.log2(math.e)

RMS_EPS = 1e-6

ML_HEADS = 4
ML_DQK = 128
ML_DV = 256
ML_LC = 256

NSA_HEADS = 16
NSA_GROUPS = 4
NSA_HPG = 4
NSA_HD = 64
CMP_BLOCK = 32
CMP_STRIDE = 16
SLC_BLOCK = 64
SLC_TOPK = 8
WINDOW = 512
FORCE_SCORE = 1e4
REL_BUCKETS = 32
REL_MAX_DIST = 128
ATT_QB = 128
ATT_TK = 128
ATT_GPS = 4
NCMP_PAD = 128
KV_PAD = WINDOW
FAR_CHUNK = 512
VT_ROWS = 80
SEL_ROWS = 2 * KV_PAD // SLC_BLOCK + 2 * 16

VMEM_LIMIT = 48 * 1024 * 1024


def _cparams(sem):
    return pltpu.CompilerParams(dimension_semantics=sem, vmem_limit_bytes=VMEM_LIMIT)


def _dot(a, b):
    return jnp.dot(a, b, preferred_element_type=F32)


def _dot_nt(a, b):
    return lax.dot_general(a, b, (((1,), (1,)), ((), ())), preferred_element_type=F32)


def _dot_tn(a, b):
    return lax.dot_general(a, b, (((0,), (0,)), ((), ())), preferred_element_type=F32)


def _norm_mod(x, g, sh, sc):
    var = jnp.mean(x * x, axis=-1, keepdims=True)
    y = x * lax.rsqrt(var + RMS_EPS) * g
    return y * (1.0 + sc) + sh


def _ada_body(c_ref, w_ref, b_ref, o_ref):
    c = c_ref[...]
    ca = c * jax.nn.sigmoid(c)
    o_ref[...] = jnp.dot(ca, w_ref[...], preferred_element_type=F32,
                         precision=lax.Precision.HIGHEST) + b_ref[...]


def ada_mod(c, w, b, tn=1024):
    L, D, N = w.shape
    B = c.shape[0]
    return pl.pallas_call(
        _ada_body,
        grid=(L, N // tn),
        in_specs=[pl.BlockSpec((B, D), lambda l, j: (0, 0)),
                  pl.BlockSpec((None, D, tn), lambda l, j: (l, 0, j)),
                  pl.BlockSpec((None, 1, tn), lambda l, j: (l, 0, j))],
        out_specs=pl.BlockSpec((None, B, tn), lambda l, j: (l, 0, j)),
        out_shape=jax.ShapeDtypeStruct((L, B, N), F32),
        compiler_params=_cparams(("parallel", "parallel")),
        name="ada_mod",
    )(c, w, b.reshape(L, 1, N))


def _norm_proj_body(n_out, has_bias, tn, x_ref, g_ref, sh_ref, sc_ref, *refs):
    w_refs = refs[:n_out]
    b_refs = refs[n_out:2 * n_out]
    o_refs = refs[2 * n_out:3 * n_out]
    h = _norm_mod(x_ref[...], g_ref[...], sh_ref[...], sc_ref[...]).astype(BF16)
    for w_ref, b_ref, o_ref, hb in zip(w_refs, b_refs, o_refs, has_bias):
        n = w_ref.shape[1]
        step = min(tn, n)
        for n0 in range(0, n, step):
            acc = _dot(h, w_ref[:, n0:n0 + step])
            if hb:
                acc = acc + b_ref[:, n0:n0 + step]
            o_ref[:, n0:n0 + step] = acc.astype(o_ref.dtype)


def norm_proj(x2d, g, sh, sc, ws, biases, out_dtypes, tm=512, tn=512):
    M, D = x2d.shape
    B = sh.shape[0]
    tiles_per_batch = (M // B) // tm
    n_out = len(ws)
    has_bias = tuple(b is not None for b in biases)
    bias_args = [(b if b is not None else jnp.zeros((w.shape[1],), F32)).reshape(1, -1)
                 for b, w in zip(biases, ws)]
    in_specs = [pl.BlockSpec((tm, D), lambda i: (i, 0)),
                pl.BlockSpec((1, D), lambda i: (0, 0)),
                pl.BlockSpec((None, 1, D), lambda i: (i // tiles_per_batch, 0, 0)),
                pl.BlockSpec((None, 1, D), lambda i: (i // tiles_per_batch, 0, 0))]
    in_specs += [pl.BlockSpec(w.shape, lambda i: (0, 0)) for w in ws]
    in_specs += [pl.BlockSpec(b.shape, lambda i: (0, 0)) for b in bias_args]
    out_specs = [pl.BlockSpec((tm, w.shape[1]), lambda i: (i, 0)) for w in ws]
    out_shape = [jax.ShapeDtypeStruct((M, w.shape[1]), dt) for w, dt in zip(ws, out_dtypes)]
    return pl.pallas_call(
        functools.partial(_norm_proj_body, n_out, has_bias, tn),
        grid=(M // tm,),
        in_specs=in_specs, out_specs=out_specs, out_shape=out_shape,
        compiler_params=_cparams(("parallel",)),
        name="norm_proj",
    )(x2d, g.reshape(1, D), sh, sc, *ws, *bias_args)


def _gate_proj_t_body(x_ref, g_ref, sh_ref, sc_ref, wt_ref, b_ref, o_ref):
    h = _norm_mod(x_ref[...], g_ref[...], sh_ref[...], sc_ref[...]).astype(BF16)
    o_ref[...] = _dot_nt(wt_ref[...], h) + b_ref[...]


def gate_proj_t(x2d, g, sh, sc, wt, bias, seq, tm=512):
    M, D = x2d.shape
    B = sh.shape[0]
    NG = wt.shape[0]
    tiles_per_batch = seq // tm
    return pl.pallas_call(
        _gate_proj_t_body,
        grid=(M // tm,),
        in_specs=[pl.BlockSpec((tm, D), lambda i: (i, 0)),
                  pl.BlockSpec((1, D), lambda i: (0, 0)),
                  pl.BlockSpec((None, 1, D), lambda i: (i // tiles_per_batch, 0, 0)),
                  pl.BlockSpec((None, 1, D), lambda i: (i // tiles_per_batch, 0, 0)),
                  pl.BlockSpec((NG, D), lambda i: (0, 0)),
                  pl.BlockSpec((NG, 1), lambda i: (0, 0))],
        out_specs=pl.BlockSpec((None, NG, tm),
                               lambda i: (i // tiles_per_batch, 0, i % tiles_per_batch)),
        out_shape=jax.ShapeDtypeStruct((B, NG, seq), F32),
        compiler_params=_cparams(("parallel",)),
        name="gate_proj_t",
    )(x2d, g.reshape(1, D), sh, sc, wt, bias.reshape(NG, 1))


def _proj_residual_body(a_ref, w_ref, x_ref, ga_ref, o_ref):
    o_ref[...] = x_ref[...] + ga_ref[...] * _dot(a_ref[...], w_ref[...])


def proj_residual(a, w, x2d, ga, tm=512):
    M, K = a.shape
    D = w.shape[1]
    B = ga.shape[0]
    tiles_per_batch = (M // B) // tm
    return pl.pallas_call(
        _proj_residual_body,
        grid=(M // tm,),
        in_specs=[pl.BlockSpec((tm, K), lambda i: (i, 0)),
                  pl.BlockSpec((K, D), lambda i: (0, 0)),
                  pl.BlockSpec((tm, D), lambda i: (i, 0)),
                  pl.BlockSpec((None, 1, D), lambda i: (i // tiles_per_batch, 0, 0))],
        out_specs=pl.BlockSpec((tm, D), lambda i: (i, 0)),
        out_shape=jax.ShapeDtypeStruct((M, D), F32),
        compiler_params=_cparams(("parallel",)),
        name="proj_residual",
    )(a, w, x2d, ga)


def _ffn_body(x_ref, g_ref, sh_ref, sc_ref, ga_ref, wg_ref, wu_ref, wo_ref, o_ref, h_s, acc_s):
    f = pl.program_id(1)

    @pl.when(f == 0)
    def _():
        h_s[...] = _norm_mod(x_ref[...], g_ref[...], sh_ref[...], sc_ref[...]).astype(BF16)

    h = h_s[...]
    gate = _dot(h, wg_ref[...])
    up = _dot(h, wu_ref[...])
    act = (gate * jax.nn.sigmoid(gate) * up).astype(BF16)
    part = _dot(act, wo_ref[...])

    @pl.when(f == 0)
    def _():
        acc_s[...] = part

    @pl.when(f > 0)
    def _():
        acc_s[...] += part

    @pl.when(f == pl.num_programs(1) - 1)
    def _():
        o_ref[...] = x_ref[...] + ga_ref[...] * acc_s[...]


def ffn(x2d, g, sh, sc, ga, w_in, w_out, tm=1024, tf=256):
    M, D = x2d.shape
    F = w_out.shape[0]
    B = sh.shape[0]
    tiles_per_batch = (M // B) // tm
    nf = F // tf
    bvec = pl.BlockSpec((None, 1, D), lambda i, f: (i // tiles_per_batch, 0, 0))
    return pl.pallas_call(
        _ffn_body,
        grid=(M // tm, nf),
        in_specs=[pl.BlockSpec((tm, D), lambda i, f: (i, 0)),
                  pl.BlockSpec((1, D), lambda i, f: (0, 0)),
                  bvec, bvec, bvec,
                  pl.BlockSpec((D, tf), lambda i, f: (0, f)),
                  pl.BlockSpec((D, tf), lambda i, f: (0, nf + f)),
                  pl.BlockSpec((tf, D), lambda i, f: (f, 0))],
        out_specs=pl.BlockSpec((tm, D), lambda i, f: (i, 0)),
        out_shape=jax.ShapeDtypeStruct((M, D), F32),
        scratch_shapes=[pltpu.VMEM((tm, D), BF16), pltpu.VMEM((tm, D), F32)],
        compiler_params=_cparams(("parallel", "arbitrary")),
        name="ffn",
    )(x2d, g.reshape(1, D), sh, sc, ga, w_in, w_in, w_out)


def _lane_cumsum(x):
    n = x.shape[-1]
    lane = lax.broadcasted_iota(jnp.int32, x.shape, 1)
    sh = 1
    while sh < n:
        x = x + jnp.where(lane >= sh, pltpu.roll(x, sh, axis=1), 0.0)
        sh *= 2
    return x


def _mlstm_body(q_ref, k_ref, v_ref, o_ref, gt_ref, gout_ref, out_ref, c_s, n_s, m_s):
    c_idx = pl.program_id(1)
    LC = q_ref.shape[0]
    scale = ML_DQK ** -0.5

    @pl.when(c_idx == 0)
    def _():
        c_s[...] = jnp.zeros_like(c_s)
        n_s[...] = jnp.zeros_like(n_s)
        m_s[...] = jnp.zeros_like(m_s)

    gt = gt_ref[...]
    fpre = gt[ML_HEADS:2 * ML_HEADS, :]
    logf = jnp.minimum(fpre, 0.0) - jnp.log1p(jnp.exp(-jnp.abs(fpre)))
    bcum = _lane_cumsum(logf)

    row = lax.broadcasted_iota(jnp.int32, (LC, LC), 0)
    col = lax.broadcasted_iota(jnp.int32, (LC, LC), 1)
    causal = col <= row
    diag = col == row

    for h in range(ML_HEADS):
        li = gt[h:h + 1, :]
        lf = logf[h:h + 1, :]
        b_r = bcum[h:h + 1, :]
        a_r = li - b_r
        b_last = b_r[:, LC - 1:LC]
        m_prev = m_s[h:h + 1, 0:1]
        m_loc = b_last + jnp.max(a_r, axis=-1, keepdims=True)
        m_new = jnp.maximum(b_last + m_prev, m_loc)
        e_r = jnp.exp(b_last + a_r - m_new)
        decay = jnp.exp(b_last + m_prev - m_new)

        cm_col = jnp.max(jnp.where(causal, a_r, NEG_INF), axis=-1, keepdims=True)
        b_col = jnp.sum(jnp.where(causal, lf, 0.0), axis=-1, keepdims=True)
        e_col = jnp.sum(jnp.where(diag, e_r, 0.0), axis=-1, keepdims=True)
        g_col = jnp.maximum(m_prev, cm_col)
        w_intra = jnp.exp(jnp.where(causal, a_r - g_col, NEG_INF))
        w_inter = jnp.exp(m_prev - g_col)
        floor = jnp.exp(-(b_col + g_col))

        qh = q_ref[:, h * ML_DQK:(h + 1) * ML_DQK]
        kh = k_ref[:, h * ML_DQK:(h + 1) * ML_DQK]
        vh = v_ref[:, h * ML_DV:(h + 1) * ML_DV]
        c_prev = c_s[h]
        n_prev = n_s[h:h + 1, :]

        s = _dot_nt(qh, kh) * scale * w_intra
        inter = _dot(qh, c_prev.astype(BF16)) * scale
        num = _dot(s.astype(BF16), vh) + w_inter * inter
        qn_inter = jnp.sum(qh.astype(F32) * n_prev, axis=-1, keepdims=True) * scale
        qn = jnp.sum(s, axis=-1, keepdims=True) + w_inter * qn_inter
        denom = jnp.maximum(jnp.abs(qn), floor)
        hout = num / denom

        ke = kh.astype(F32) * e_col
        c_s[h] = decay * c_prev + _dot_tn(ke.astype(BF16), vh)
        n_s[h:h + 1, :] = decay * n_prev + jnp.sum(ke, axis=0, keepdims=True)
        m_s[h:h + 1, :] = jnp.broadcast_to(m_new, (1, m_s.shape[1]))

        var = jnp.mean(hout * hout, axis=-1, keepdims=True)
        hn = hout * lax.rsqrt(var + RMS_EPS) * gout_ref[:, h * ML_DV:(h + 1) * ML_DV]
        og = jax.nn.sigmoid(o_ref[:, h * ML_DV:(h + 1) * ML_DV].astype(F32))
        out_ref[:, h * ML_DV:(h + 1) * ML_DV] = (hn * og).astype(out_ref.dtype)


def mlstm_core(proj, gates_t, g_out, batch, seq):
    LC = ML_LC
    nc = seq // LC
    qk = ML_HEADS * ML_DQK
    vd = ML_HEADS * ML_DV
    row = lambda b, c: b * nc + c
    return pl.pallas_call(
        _mlstm_body,
        grid=(batch, nc),
        in_specs=[pl.BlockSpec((LC, qk), lambda b, c: (row(b, c), 0)),
                  pl.BlockSpec((LC, qk), lambda b, c: (row(b, c), 1)),
                  pl.BlockSpec((LC, vd), lambda b, c: (row(b, c), 1)),
                  pl.BlockSpec((LC, vd), lambda b, c: (row(b, c), 2)),
                  pl.BlockSpec((None, 2 * ML_HEADS, LC), lambda b, c: (b, 0, c)),
                  pl.BlockSpec((1, vd), lambda b, c: (0, 0))],
        out_specs=pl.BlockSpec((LC, vd), lambda b, c: (row(b, c), 0)),
        out_shape=jax.ShapeDtypeStruct((batch * seq, vd), BF16),
        scratch_shapes=[pltpu.VMEM((ML_HEADS, ML_DQK, ML_DV), F32),
                        pltpu.VMEM((8, ML_DQK), F32),
                        pltpu.VMEM((8, 128), F32)],
        compiler_params=_cparams(("parallel", "arbitrary")),
        name="mlstm_core",
    )(proj, proj, proj, proj, gates_t, g_out.reshape(1, vd))


def _kv_proj_body(x_ref, g_ref, sh_ref, sc_ref, wk_ref, wvt_ref, wcc_ref, gk_ref,
                  ks_ref, kw_ref, vst_ref, vwt_ref, cc_ref):
    i = pl.program_id(1)
    G, HD = NSA_GROUPS, NSA_HD
    tm = x_ref.shape[0]

    @pl.when(i == 0)
    def _():
        ks_ref[...] = jnp.zeros_like(ks_ref)
        kw_ref[...] = jnp.zeros_like(kw_ref)
        vst_ref[...] = jnp.zeros_like(vst_ref)
        vwt_ref[...] = jnp.zeros_like(vwt_ref)

    @pl.when(i > 0)
    def _():
        h = _norm_mod(x_ref[...], g_ref[...], sh_ref[...], sc_ref[...]).astype(BF16)
        cc_ref[...] = _dot(h, wcc_ref[...])
        kk = _dot(h, wk_ref[...])
        lane = lax.broadcasted_iota(jnp.int32, (tm, 128 - HD), 1)
        tail = jnp.where(lane < 2, 1.0, 0.0).astype(BF16)
        for kind, o_ref in enumerate((ks_ref, kw_ref)):
            for g in range(G):
                seg = kk[:, (kind * G + g) * HD:(kind * G + g + 1) * HD]
                var = jnp.mean(seg * seg, axis=-1, keepdims=True)
                segn = seg * lax.rsqrt(var + RMS_EPS) * gk_ref[kind:kind + 1, :]
                o_ref[:, g * 128:g * 128 + HD] = segn.astype(BF16)
                o_ref[:, g * 128 + HD:(g + 1) * 128] = tail
        vt = _dot_nt(wvt_ref[...], h)
        srow = lax.broadcasted_iota(jnp.int32, (VT_ROWS - HD, tm), 0)
        ones_blk = jnp.where(srow == 0, 1.0, 0.0).astype(BF16)
        for kind, o_ref in enumerate((vst_ref, vwt_ref)):
            for g in range(G):
                r0 = (kind * G + g) * HD
                o_ref[g * VT_ROWS:g * VT_ROWS + HD, :] = vt[r0:r0 + HD, :].astype(BF16)
                o_ref[g * VT_ROWS + HD:(g + 1) * VT_ROWS, :] = ones_blk


def kv_proj(x2d, g, sh, sc, wk, wvt, wcc, gk, batch, seq, tm=KV_PAD):
    assert tm == KV_PAD and seq % tm == 0
    M, D = x2d.shape
    G = NSA_GROUPS
    nt = seq // tm
    sp = seq + KV_PAD
    xrow = lambda b, i: (b * nt + jnp.maximum(i - 1, 0), 0)
    bvec = pl.BlockSpec((None, 1, D), lambda b, i: (b, 0, 0))
    full = lambda a: pl.BlockSpec(a.shape, lambda b, i: (0,) * a.ndim)
    return pl.pallas_call(
        _kv_proj_body,
        grid=(batch, nt + 1),
        in_specs=[pl.BlockSpec((tm, D), xrow),
                  pl.BlockSpec((1, D), lambda b, i: (0, 0)),
                  bvec, bvec, full(wk), full(wvt), full(wcc), full(gk)],
        out_specs=[pl.BlockSpec((None, tm, G * 128), lambda b, i: (b, i, 0)),
                   pl.BlockSpec((None, tm, G * 128), lambda b, i: (b, i, 0)),
                   pl.BlockSpec((None, G * VT_ROWS, tm), lambda b, i: (b, 0, i)),
                   pl.BlockSpec((None, G * VT_ROWS, tm), lambda b, i: (b, 0, i)),
                   pl.BlockSpec((tm, wcc.shape[1]), xrow)],
        out_shape=[jax.ShapeDtypeStruct((batch, sp, G * 128), BF16),
                   jax.ShapeDtypeStruct((batch, sp, G * 128), BF16),
                   jax.ShapeDtypeStruct((batch, G * VT_ROWS, sp), BF16),
                   jax.ShapeDtypeStruct((batch, G * VT_ROWS, sp), BF16),
                   jax.ShapeDtypeStruct((M, wcc.shape[1]), F32)],
        compiler_params=_cparams(("parallel", "arbitrary")),
        name="kv_proj",
    )(x2d, g.reshape(1, D), sh, sc, wk, wvt, wcc, gk)


def _gelu_tanh(x):
    c = math.sqrt(2.0 / math.pi)
    return 0.5 * x * (1.0 + jnp.tanh(c * (x + 0.044715 * (x * x * x))))


def _compress_body(cc0_ref, cc1_ref, cc2_ref, cc3_ref, pos_ref, w1_ref, w2_ref, w2t_ref, g_ref,
                   kc_ref, vct_ref, u_s, v_s):
    G, HD = NSA_GROUPS, NSA_HD
    nwin = u_s.shape[1]
    u_s[...] = jnp.zeros_like(u_s)
    v_s[...] = jnp.zeros_like(v_s)

    def step(i, carry):
        xi = jnp.concatenate([r[pl.ds(i, nwin, stride=CMP_STRIDE), :]
                              for r in (cc0_ref, cc1_ref, cc2_ref, cc3_ref)], axis=1)
        r0 = pl.multiple_of(i * HD, HD)
        r1 = pl.multiple_of((i + CMP_STRIDE) * HD, HD)
        for kind in range(2):
            p_lo = pos_ref[kind, pl.ds(i, 1), :]
            p_hi = pos_ref[kind, pl.ds(i + CMP_STRIDE, 1), :]
            w_lo = w1_ref[kind, pl.ds(r0, HD), :]
            w_hi = w1_ref[kind, pl.ds(r1, HD), :]
            for g in range(G):
                j = kind * G + g
                seg = xi[:, j * HD:(j + 1) * HD]
                u_s[j] += _dot((seg + p_lo).astype(BF16), w_lo)
                v_s[j] += _dot((seg + p_hi).astype(BF16), w_hi)
        return carry

    lax.fori_loop(0, CMP_STRIDE, step, 0)

    for kind in range(2):
        for g in range(G):
            j = kind * G + g
            pre = u_s[j] + pltpu.roll(v_s[j], nwin - 1, axis=0)
            hmid = _gelu_tanh(pre).astype(BF16)
            if kind == 0:
                y = _dot(hmid, w2_ref[...])
                var = jnp.mean(y * y, axis=-1, keepdims=True)
                yn = y * lax.rsqrt(var + RMS_EPS) * g_ref[...]
                kc_ref[:, g * 128:g * 128 + HD] = yn.astype(BF16)
                kc_ref[:, g * 128 + HD:(g + 1) * 128] = jnp.zeros((nwin, 128 - HD), BF16)
            else:
                vct_ref[g * HD:(g + 1) * HD, :] = _dot_nt(w2t_ref[...], hmid).astype(BF16)


def compress(cc, pos, w1, w2k, w2vt, g, batch, seq):
    G, HD = NSA_GROUPS, NSA_HD
    nwin = seq // CMP_STRIDE
    hid = w1.shape[2]
    full = lambda a: pl.BlockSpec(a.shape, lambda b: (0,) * a.ndim)
    return pl.pallas_call(
        _compress_body,
        grid=(batch,),
        in_specs=[pl.BlockSpec((seq, 128), functools.partial(lambda q, b: (b, q), q))
                  for q in range(4)]
                 + [full(pos), full(w1), full(w2k), full(w2vt), full(g)],
        out_specs=[pl.BlockSpec((None, nwin, G * 128), lambda b: (b, 0, 0)),
                   pl.BlockSpec((None, G * HD, nwin), lambda b: (b, 0, 0))],
        out_shape=[jax.ShapeDtypeStruct((batch, nwin, G * 128), BF16),
                   jax.ShapeDtypeStruct((batch, G * HD, nwin), BF16)],
        scratch_shapes=[pltpu.VMEM((2 * G, nwin, hid), F32), pltpu.VMEM((2 * G, nwin, hid), F32)],
        compiler_params=_cparams(("parallel",)),
        name="compress",
    )(cc, cc, cc, cc, pos, w1, w2k, w2vt, g)


def _t5_bucket(dist):
    n = jnp.maximum(dist, 0)
    max_exact = REL_BUCKETS // 2
    nf = jnp.maximum(n, 1).astype(F32)
    large = max_exact + (jnp.log(nf / max_exact) / math.log(REL_MAX_DIST / max_exact)
                         * (REL_BUCKETS - max_exact)).astype(jnp.int32)
    large = jnp.minimum(large, REL_BUCKETS - 1)
    return jnp.where(n < max_exact, n, large)


def _table_lookup(bucket, tab_ref, h):
    out = jnp.zeros(bucket.shape, F32)
    for k in range(REL_BUCKETS):
        out = jnp.where(bucket == k, tab_ref[k, h], out)
    return out


def _bias_prep_body(tab_ref, bc_ref, tp_ref, far_ref):
    h = pl.program_id(0)
    S = bc_ref.shape[1]
    cols = 128
    far = tab_ref[REL_BUCKETS - 1, h]
    far_ref[...] = jnp.full(far_ref.shape, far * LOG2E, F32)

    def chunk(r, carry):
        start = pl.multiple_of(r * cols, cols)
        n = lax.broadcasted_iota(jnp.int32, (NCMP_PAD, cols), 0)
        t = start + lax.broadcasted_iota(jnp.int32, (NCMP_PAD, cols), 1)
        dist_c = t - (n * CMP_STRIDE + CMP_BLOCK - 1)
        bias_c = _table_lookup(_t5_bucket(dist_c), tab_ref, h) * LOG2E
        bc_ref[:, pl.ds(start, cols)] = jnp.where(dist_c >= 0, bias_c, NEG_INF)
        return carry

    lax.fori_loop(0, S // cols, chunk, 0)
    j = lax.broadcasted_iota(jnp.int32, (ATT_TK, ATT_QB), 0)
    i = lax.broadcasted_iota(jnp.int32, (ATT_TK, ATT_QB), 1)
    for d in range(2):
        dist = d * ATT_TK + i - j
        rel = (_table_lookup(_t5_bucket(dist), tab_ref, h) - far) * LOG2E
        tp_ref[d] = jnp.where(dist >= 0, rel, NEG_INF)
    tp_ref[2] = jnp.where(i < j, 0.0, NEG_INF)


def bias_prep(rel_table, seq):
    assert ATT_TK == ATT_QB and ATT_TK + 1 > 113
    return pl.pallas_call(
        _bias_prep_body,
        grid=(NSA_HEADS,),
        in_specs=[pl.BlockSpec(memory_space=pltpu.SMEM)],
        out_specs=[pl.BlockSpec((None, NCMP_PAD, seq), lambda h: (h, 0, 0)),
                   pl.BlockSpec((None, 3, ATT_TK, ATT_QB), lambda h: (h, 0, 0, 0)),
                   pl.BlockSpec((None, 8, 128), lambda h: (h, 0, 0))],
        out_shape=[jax.ShapeDtypeStruct((NSA_HEADS, NCMP_PAD, seq), F32),
                   jax.ShapeDtypeStruct((NSA_HEADS, 3, ATT_TK, ATT_QB), F32),
                   jax.ShapeDtypeStruct((NSA_HEADS, 8, 128), F32)],
        compiler_params=_cparams(("parallel",)),
        name="bias_prep",
    )(rel_table)


def _q_proj_body(x_ref, g_ref, sh_ref, sc_ref, wqt_ref, wgt_ref, bg_ref, gq_ref, qt_ref, gt_ref):
    HD = NSA_HD
    h = _norm_mod(x_ref[...], g_ref[...], sh_ref[...], sc_ref[...]).astype(BF16)
    gt_ref[...] = _dot_nt(wgt_ref[...], h) + bg_ref[...]
    qt = _dot_nt(wqt_ref[...], h)
    scale = gq_ref[...] * (HD ** -0.5 * LOG2E)
    for hh in range(NSA_HEADS):
        seg = qt[hh * HD:(hh + 1) * HD, :]
        var = jnp.mean(seg * seg, axis=0, keepdims=True)
        qt_ref[hh * HD:(hh + 1) * HD, :] = (seg * lax.rsqrt(var + RMS_EPS) * scale).astype(BF16)


def q_proj(x2d, g, sh, sc, wqt, wgt, bg, gq, batch, seq, tm=512):
    M, D = x2d.shape
    nt = seq // tm
    nq = wqt.shape[0]
    ng = wgt.shape[0]
    bvec = pl.BlockSpec((None, 1, D), lambda i: (i // nt, 0, 0))
    full = lambda a: pl.BlockSpec(a.shape, lambda i: (0,) * a.ndim)
    return pl.pallas_call(
        _q_proj_body,
        grid=(M // tm,),
        in_specs=[pl.BlockSpec((tm, D), lambda i: (i, 0)),
                  pl.BlockSpec((1, D), lambda i: (0, 0)),
                  bvec, bvec, full(wqt), full(wgt), full(bg), full(gq)],
        out_specs=[pl.BlockSpec((None, nq, tm), lambda i: (i // nt, 0, i % nt)),
                   pl.BlockSpec((None, ng, tm), lambda i: (i // nt, 0, i % nt))],
        out_shape=[jax.ShapeDtypeStruct((batch, nq, seq), BF16),
                   jax.ShapeDtypeStruct((batch, ng, seq), F32)],
        compiler_params=_cparams(("parallel",)),
        name="q_proj",
    )(x2d, g.reshape(1, D), sh, sc, wqt, wgt, bg, gq)


def _heads_on_lanes(pieces):
    return jnp.concatenate(pieces, axis=1)


def _nsa_body(gps, qt_ref, gt_ref, kc_ref, vct_ref, ks_ref, kw_ref, vst_ref, vwt_ref,
              bct_ref, tp_ref, far_ref, ovt_ref, out_ref, msel_s):
    qb = pl.program_id(2)
    QB, HD, HPG = ATT_QB, NSA_HD, NSA_HPG
    R = HPG * QB
    t0 = pl.multiple_of(qb * QB, QB)
    near0 = pl.multiple_of(t0 + WINDOW - ATT_TK, ATT_TK)
    wlen = WINDOW + QB
    nblk = ovt_ref.shape[0]
    pad_blocks = KV_PAD // SLC_BLOCK
    pens = [jnp.where(qb >= d, 0.0, NEG_INF) for d in range(5)]
    jb = lax.broadcasted_iota(jnp.int32, (nblk, QB), 0)
    jbf = jb.astype(F32)
    qid = jnp.right_shift(t0 + lax.broadcasted_iota(jnp.int32, (nblk, QB), 1), 6)
    forced = (jb == 0) | (jb == qid) | (jb == qid - 1)
    srow = lax.broadcasted_iota(jnp.int32, (16, R), 0)

    def finish(acc):
        return acc[0:HD, :] / acc[HD:HD + 1, :]

    def block_mask(gl, first_row, nblocks):
        rows = msel_s[gl, pl.ds(first_row, nblocks), :]
        one_head = jnp.concatenate(
            [jnp.broadcast_to(rows[r:r + 1, :], (SLC_BLOCK, QB)) for r in range(nblocks)], axis=0)
        return _heads_on_lanes([one_head] * HPG)

    def tile_part(s_tile, v_tile):
        m_t = jnp.max(s_tile, axis=0, keepdims=True)
        m_safe = jnp.where(m_t == NEG_INF, 0.0, m_t)
        p = jnp.exp2(s_tile - m_safe).astype(BF16)
        return m_t, _dot(v_tile, p)[0:HD + 8, :]

    def combine(parts):
        m_fin = parts[0][0]
        for m_t, _ in parts[1:]:
            m_fin = jnp.maximum(m_fin, m_t)
        acc = None
        for m_t, pv in parts:
            term = jnp.exp2(m_t - m_fin) * pv
            acc = term if acc is None else acc + term
        return m_fin, acc

    groups = range(gps)
    hs = [[gl * HPG + h for h in range(HPG)] for gl in groups]
    kcol = [slice(gl * 128, (gl + 1) * 128) for gl in groups]
    vrow = [slice(gl * VT_ROWS, (gl + 1) * VT_ROWS) for gl in groups]

    qms = []
    for gl in groups:
        q_all = _heads_on_lanes([qt_ref[h * HD:(h + 1) * HD, :] for h in hs[gl]])
        far = _heads_on_lanes([far_ref[h, 0:1, :] for h in hs[gl]])
        hi = far.astype(BF16).astype(F32)
        extra = jnp.where(srow == 0, hi, jnp.where(srow == 1, far - hi, 0.0)).astype(BF16)
        qms.append(jnp.concatenate([q_all, extra, jnp.zeros((128 - HD - 16, R), BF16)], axis=0))

    s_cs = [_dot(kc_ref[:, kcol[gl]], qms[gl]) for gl in groups]
    s_ws = [_dot(kw_ref[pl.ds(t0, wlen), kcol[gl]], qms[gl]) for gl in groups]
    s_ns = [_dot(ks_ref[pl.ds(near0, 2 * ATT_TK), kcol[gl]], qms[gl]) for gl in groups]

    o_cmps, scores = [], []
    for gl in groups:
        s_c = s_cs[gl] + _heads_on_lanes([bct_ref[h] for h in hs[gl]])
        m_c = jnp.max(s_c, axis=0, keepdims=True)
        m_c = jnp.where(m_c == NEG_INF, 0.0, m_c)
        e_c = jnp.exp2(s_c - m_c)
        p_c = e_c / jnp.maximum(jnp.sum(e_c, axis=0, keepdims=True), jnp.finfo(F32).tiny)
        o_cmps.append(_dot(vct_ref[gl * HD:(gl + 1) * HD, :], p_c.astype(BF16)))
        p_sum = p_c[:, 0:QB]
        for h in range(1, HPG):
            p_sum = p_sum + p_c[:, h * QB:(h + 1) * QB]
        p_hi = p_sum.astype(BF16)
        p_lo = (p_sum - p_hi.astype(F32)).astype(BF16)
        imp = _dot(ovt_ref[...], jnp.concatenate([p_hi, p_lo], axis=0))
        score = jnp.where(forced, FORCE_SCORE, imp)
        scores.append(jnp.where(jb <= qid, score, NEG_INF))

    def win_bias(h, d):
        if d == 4:
            return tp_ref[h, 2] + pens[4]
        if d in (2, 3):
            return jnp.full((ATT_TK, QB), pens[d], F32)
        return tp_ref[h, 1] + pens[1] if d == 1 else tp_ref[h, 0]

    nwt = wlen // ATT_TK
    o_wins = []
    for gl in groups:
        parts = []
        for i in range(nwt):
            s_t = s_ws[gl][i * ATT_TK:(i + 1) * ATT_TK] + _heads_on_lanes(
                [win_bias(h, nwt - 1 - i) for h in hs[gl]])
            parts.append(tile_part(s_t, vwt_ref[vrow[gl], pl.ds(t0 + i * ATT_TK, ATT_TK)]))
        o_wins.append(finish(combine(parts)[1]))

    msels = [jnp.full((nblk, QB), NEG_INF, F32) for _ in groups]
    for _ in range(SLC_TOPK):
        for gl in groups:
            mx = jnp.max(scores[gl], axis=0, keepdims=True)
            first = jnp.min(jnp.where(scores[gl] == mx, jbf, float(nblk)), axis=0, keepdims=True)
            pick = jbf == first
            msels[gl] = jnp.where(pick & (mx > NEG_INF), 0.0, msels[gl])
            scores[gl] = jnp.where(pick, NEG_INF, scores[gl])
    for gl in groups:
        msel_s[gl, 0:pad_blocks, :] = jnp.full((pad_blocks, QB), NEG_INF, F32)
        msel_s[gl, pad_blocks:pad_blocks + nblk, :] = msels[gl]
        msel_s[gl, pad_blocks + nblk:, :] = jnp.full(
            (msel_s.shape[1] - pad_blocks - nblk, QB), NEG_INF, F32)

    carry0 = []
    for gl in groups:
        prev_mask = jnp.concatenate(
            [jnp.broadcast_to(jnp.max(jnp.where(jb == 2 * qb - 2 + r, msels[gl], NEG_INF), axis=0,
                                      keepdims=True), (SLC_BLOCK, QB)) for r in range(2)], axis=0)
        s_prev = s_ns[gl][0:ATT_TK] + _heads_on_lanes([tp_ref[h, 1] + prev_mask for h in hs[gl]])
        s_diag = s_ns[gl][ATT_TK:] + _heads_on_lanes([tp_ref[h, 0] for h in hs[gl]])
        carry0.append(combine([
            tile_part(s_diag, vst_ref[vrow[gl], pl.ds(near0 + ATT_TK, ATT_TK)]),
            tile_part(s_prev, vst_ref[vrow[gl], pl.ds(near0, ATT_TK)])]))

    def far_step(c, carry):
        tile0 = qb + (WINDOW - ATT_TK) // ATT_TK - (c + 1) * (FAR_CHUNK // ATT_TK)
        row0 = pl.multiple_of(tile0 * ATT_TK, ATT_TK)
        blk0 = tile0 * (ATT_TK // SLC_BLOCK)
        s_fs = [_dot(ks_ref[pl.ds(row0, FAR_CHUNK), kcol[gl]], qms[gl]) for gl in groups]
        out = []
        for gl in groups:
            parts = [carry[gl]]
            for j in range(FAR_CHUNK // ATT_TK):
                s_t = s_fs[gl][j * ATT_TK:(j + 1) * ATT_TK] + block_mask(
                    gl, blk0 + j * (ATT_TK // SLC_BLOCK), ATT_TK // SLC_BLOCK)
                parts.append(tile_part(
                    s_t, vst_ref[vrow[gl], pl.ds(row0 + j * ATT_TK, ATT_TK)]))
            out.append(combine(parts))
        return tuple(out)

    n_far = (qb + 2) // 4
    sel = lax.fori_loop(0, n_far, far_step, tuple(carry0))

    gates = jax.nn.sigmoid(gt_ref[...])
    for gl in groups:
        o_cmp, o_win = o_cmps[gl], o_wins[gl]
        o_sel = finish(sel[gl][1])
        outs = []
        for h in range(HPG):
            lanes = slice(h * QB, (h + 1) * QB)
            r = gl * 16 + 3 * h
            outs.append(gates[r:r + 1, :] * o_cmp[:, lanes]
                        + gates[r + 1:r + 2, :] * o_sel[:, lanes]
                        + gates[r + 2:r + 3, :] * o_win[:, lanes])
        for pair in range(HPG // 2):
            two = jnp.concatenate(outs[2 * pair:2 * pair + 2], axis=0)
            c0 = gl * HPG * HD + pair * 2 * HD
            out_ref[:, c0:c0 + 2 * HD] = two.T.astype(out_ref.dtype)


def _overlap_matrix_t(nblk):
    start = np.arange(NCMP_PAD) * CMP_STRIDE
    sj = np.arange(nblk) * SLC_BLOCK
    ov = (np.minimum(start[None, :] + CMP_BLOCK, sj[:, None] + SLC_BLOCK)
          - np.maximum(start[None, :], sj[:, None]))
    ov = np.clip(ov, 0, None) / CMP_BLOCK
    ov[:, NCMP_PAD - 1] = 0.0
    return np.concatenate([ov, ov], axis=1).astype(np.float32)


def nsa_attn(qt, gt, kc, vct, ks, kw, vst, vwt, bias_ct, tiles, far, batch, seq, gps=ATT_GPS):
    QB, HD, HPG, G = ATT_QB, NSA_HD, NSA_HPG, NSA_GROUPS
    nblk = seq // SLC_BLOCK
    assert nblk % 8 == 0 and seq % QB == 0 and seq // CMP_STRIDE == NCMP_PAD and G % gps == 0
    assert WINDOW % ATT_TK == 0 and FAR_CHUNK == 4 * ATT_TK and KV_PAD >= FAR_CHUNK - ATT_TK
    assert SEL_ROWS >= KV_PAD // SLC_BLOCK + nblk
    nq = seq // QB
    sp = seq + KV_PAD
    ovt = jnp.asarray(_overlap_matrix_t(nblk), dtype=BF16)
    return pl.pallas_call(
        functools.partial(_nsa_body, gps),
        grid=(batch, G // gps, nq),
        in_specs=[pl.BlockSpec((None, gps * HPG * HD, QB), lambda b, g, i: (b, g, i)),
                  pl.BlockSpec((None, gps * 16, QB), lambda b, g, i: (b, g, i)),
                  pl.BlockSpec((None, NCMP_PAD, gps * 128), lambda b, g, i: (b, 0, g)),
                  pl.BlockSpec((None, gps * HD, NCMP_PAD), lambda b, g, i: (b, g, 0)),
                  pl.BlockSpec((None, sp, gps * 128), lambda b, g, i: (b, 0, g)),
                  pl.BlockSpec((None, sp, gps * 128), lambda b, g, i: (b, 0, g)),
                  pl.BlockSpec((None, gps * VT_ROWS, sp), lambda b, g, i: (b, g, 0)),
                  pl.BlockSpec((None, gps * VT_ROWS, sp), lambda b, g, i: (b, g, 0)),
                  pl.BlockSpec((gps * HPG, NCMP_PAD, QB), lambda b, g, i: (g, 0, i)),
                  pl.BlockSpec((gps * HPG, 3, ATT_TK, QB), lambda b, g, i: (g, 0, 0, 0)),
                  pl.BlockSpec((gps * HPG, 8, 128), lambda b, g, i: (g, 0, 0)),
                  pl.BlockSpec((nblk, 2 * NCMP_PAD), lambda b, g, i: (0, 0))],
        out_specs=pl.BlockSpec((QB, gps * HPG * HD), lambda b, g, i: (b * nq + i, g)),
        out_shape=jax.ShapeDtypeStruct((batch * seq, NSA_HEADS * HD), BF16),
        scratch_shapes=[pltpu.VMEM((gps, SEL_ROWS, QB), F32)],
        compiler_params=_cparams(("parallel", "parallel", "arbitrary")),
        name="nsa_attn",
    )(qt, gt, kc, vct, ks, kw, vst, vwt, bias_ct, tiles, far, ovt)


def _mlstm_layer(x2d, g_mix, sh, sc, ga, w_in, b_if, g_out, w_out, batch, seq):
    nbig = 2 * ML_HEADS * ML_DQK + 2 * ML_HEADS * ML_DV
    (proj,) = norm_proj(x2d, g_mix, sh, sc, [w_in[:, :nbig].astype(BF16)], [None], [BF16])
    gates_t = gate_proj_t(x2d, g_mix, sh, sc, w_in[:, nbig:].T.astype(BF16), b_if, seq)
    hg = mlstm_core(proj, gates_t, g_out, batch, seq)
    return proj_residual(hg, w_out.astype(BF16), x2d, ga)


def _nsa_shared(x2d, g_kv, kv_sh, kv_sc, w_kv, pos_k, w_k1, w_k2, pos_v, w_v1, w_v2,
                g_knorm, batch, seq):
    gw = NSA_GROUPS * NSA_HD
    part = lambda i: w_kv[:, i * gw:(i + 1) * gw]
    wk = jnp.concatenate([part(2), part(4)], axis=1).astype(BF16)
    wvt = jnp.concatenate([part(3), part(5)], axis=1).T.astype(BF16)
    wcc = jnp.concatenate([part(0), part(1)], axis=1).astype(BF16)
    ks, kw, vst, vwt, cc = kv_proj(x2d, g_kv, kv_sh, kv_sc, wk, wvt, wcc, g_knorm[1:3],
                                   batch, seq)
    pos = jnp.stack([pos_k, pos_v])
    w1 = jnp.stack([w_k1, w_v1]).astype(BF16)
    kc, vct = compress(cc, pos, w1, w_k2.astype(BF16), w_v2.T.astype(BF16), g_knorm[0:1],
                       batch, seq)
    return kc, vct, ks, kw, vst, vwt


def _gate_weights_t(w_q, b_gate):
    nq = NSA_HEADS * NSA_HD
    per = 3 * NSA_HPG
    wg = w_q[:, nq:].T.reshape(NSA_GROUPS, per, -1)
    wg = jnp.pad(wg, ((0, 0), (0, 16 - per), (0, 0))).reshape(NSA_GROUPS * 16, -1)
    bg = jnp.pad(b_gate.reshape(NSA_GROUPS, per), ((0, 0), (0, 16 - per))).reshape(-1, 1)
    return wg.astype(BF16), bg


def _nsa_layer(x2d, g_mix, sh, sc, ga, shared, w_q, b_gate, g_qnorm, w_out, bias_ct, tiles, far,
               batch, seq):
    nq = NSA_HEADS * NSA_HD
    wgt, bg = _gate_weights_t(w_q, b_gate)
    qt, gt = q_proj(x2d, g_mix, sh, sc, w_q[:, :nq].T.astype(BF16), wgt, bg,
                    g_qnorm.reshape(NSA_HD, 1), batch, seq)
    kc, vct, ks, kw, vst, vwt = shared
    att = nsa_attn(qt, gt, kc, vct, ks, kw, vst, vwt, bias_ct, tiles, far, batch, seq)
    return proj_residual(att, w_out.astype(BF16), x2d, ga)


def kernel(x, c, w_ada, b_ada, g_norm_mix, g_norm_ffn, w_ffn_in, w_ffn_out, w_a_in, b_a_if, g_a_out, w_a_out, w_kv_ada, b_kv_ada, g_kv_norm, w_kv, pos_cmp_k, w_cmp_k1, w_cmp_k2, pos_cmp_v, w_cmp_v1, w_cmp_v2, g_knorm, w_b_q, b_b_gate, g_qnorm, w_b_out, rel_table):
    B, S, D = x.shape
    depth = w_ada.shape[0]
    n_a = w_a_in.shape[0]
    x2d = x.reshape(B * S, D)
    mods = ada_mod(c, w_ada, b_ada)
    kv_mod = ada_mod(c, w_kv_ada[None], b_kv_ada[None])[0]
    shared = None
    bias_ct = tiles = far = None
    for layer in range(depth):
        sh1, sc1, ga1, sh2, sc2, ga2 = [mods[layer, :, i * D:(i + 1) * D].reshape(B, 1, D)
                                        for i in range(6)]
        if layer < n_a:
            x2d = _mlstm_layer(x2d, g_norm_mix[layer], sh1, sc1, ga1, w_a_in[layer],
                               b_a_if[layer], g_a_out[layer], w_a_out[layer], B, S)
        else:
            j = layer - n_a
            if shared is None:
                kv_sh = kv_mod[:, :D].reshape(B, 1, D)
                kv_sc = kv_mod[:, D:].reshape(B, 1, D)
                shared = _nsa_shared(x2d, g_kv_norm, kv_sh, kv_sc, w_kv, pos_cmp_k, w_cmp_k1,
                                     w_cmp_k2, pos_cmp_v, w_cmp_v1, w_cmp_v2, g_knorm, B, S)
                bias_ct, tiles, far = bias_prep(rel_table, S)
            x2d = _nsa_layer(x2d, g_norm_mix[layer], sh1, sc1, ga1, shared, w_b_q[j],
                             b_b_gate[j], g_qnorm[j], w_b_out[j], bias_ct, tiles, far, B, S)
        x2d = ffn(x2d, g_norm_ffn[layer], sh2, sc2, ga2, w_ffn_in[layer].astype(BF16),
                  w_ffn_out[layer].astype(BF16))
    return x2d.reshape(B, S, D)
```

```python
import functools
import math

import jax
import jax.numpy as jnp
import numpy as np
from jax import lax
from jax.experimental import pallas as pl
from jax.experimental.pallas import tpu as pltpu

F32 = jnp.float32
BF16 = jnp.bfloat16
NEG_INF = float("-inf")
LOG2E = math.log2(math.e)

RMS_EPS = 1e-6

ML_HEADS = 4
ML_DQK = 128
ML_DV = 256
ML_LC = 256

NSA_HEADS = 16
NSA_GROUPS = 4
NSA_HPG = 4
NSA_HD = 64
CMP_BLOCK = 32
CMP_STRIDE = 16
SLC_BLOCK = 64
SLC_TOPK = 8
WINDOW = 512
FORCE_SCORE = 1e4
REL_BUCKETS = 32
REL_MAX_DIST = 128
ATT_QB = 128
ATT_TK = 128
ATT_GPS = 4
NCMP_PAD = 128
KV_PAD = WINDOW
FAR_CHUNK = 512
VT_ROWS = 80
SEL_ROWS = 2 * KV_PAD // SLC_BLOCK + 2 * 16

VMEM_LIMIT = 48 * 1024 * 1024


def _cparams(sem):
    return pltpu.CompilerParams(dimension_semantics=sem, vmem_limit_bytes=VMEM_LIMIT)


def _dot(a, b):
    return jnp.dot(a, b, preferred_element_type=F32)


def _dot_nt(a, b):
    return lax.dot_general(a, b, (((1,), (1,)), ((), ())), preferred_element_type=F32)


def _dot_tn(a, b):
    return lax.dot_general(a, b, (((0,), (0,)), ((), ())), preferred_element_type=F32)


def _norm_mod(x, g, sh, sc):
    var = jnp.mean(x * x, axis=-1, keepdims=True)
    y = x * lax.rsqrt(var + RMS_EPS) * g
    return y * (1.0 + sc) + sh


def _ada_body(c_ref, w_ref, b_ref, o_ref):
    c = c_ref[...]
    ca = c * jax.nn.sigmoid(c)
    o_ref[...] = jnp.dot(ca, w_ref[...], preferred_element_type=F32,
                         precision=lax.Precision.HIGHEST) + b_ref[...]


def ada_mod(c, w, b, tn=1024):
    L, D, N = w.shape
    B = c.shape[0]
    return pl.pallas_call(
        _ada_body,
        grid=(L, N // tn),
        in_specs=[pl.BlockSpec((B, D), lambda l, j: (0, 0)),
                  pl.BlockSpec((None, D, tn), lambda l, j: (l, 0, j)),
                  pl.BlockSpec((None, 1, tn), lambda l, j: (l, 0, j))],
        out_specs=pl.BlockSpec((None, B, tn), lambda l, j: (l, 0, j)),
        out_shape=jax.ShapeDtypeStruct((L, B, N), F32),
        compiler_params=_cparams(("parallel", "parallel")),
        name="ada_mod",
    )(c, w, b.reshape(L, 1, N))


def _norm_proj_body(n_out, has_bias, tn, x_ref, g_ref, sh_ref, sc_ref, *refs):
    w_refs = refs[:n_out]
    b_refs = refs[n_out:2 * n_out]
    o_refs = refs[2 * n_out:3 * n_out]
    h = _norm_mod(x_ref[...], g_ref[...], sh_ref[...], sc_ref[...]).astype(BF16)
    for w_ref, b_ref, o_ref, hb in zip(w_refs, b_refs, o_refs, has_bias):
        n = w_ref.shape[1]
        step = min(tn, n)
        for n0 in range(0, n, step):
            acc = _dot(h, w_ref[:, n0:n0 + step])
            if hb:
                acc = acc + b_ref[:, n0:n0 + step]
            o_ref[:, n0:n0 + step] = acc.astype(o_ref.dtype)


def norm_proj(x2d, g, sh, sc, ws, biases, out_dtypes, tm=512, tn=512):
    M, D = x2d.shape
    B = sh.shape[0]
    tiles_per_batch = (M // B) // tm
    n_out = len(ws)
    has_bias = tuple(b is not None for b in biases)
    bias_args = [(b if b is not None else jnp.zeros((w.shape[1],), F32)).reshape(1, -1)
                 for b, w in zip(biases, ws)]
    in_specs = [pl.BlockSpec((tm, D), lambda i: (i, 0)),
                pl.BlockSpec((1, D), lambda i: (0, 0)),
                pl.BlockSpec((None, 1, D), lambda i: (i // tiles_per_batch, 0, 0)),
                pl.BlockSpec((None, 1, D), lambda i: (i // tiles_per_batch, 0, 0))]
    in_specs += [pl.BlockSpec(w.shape, lambda i: (0, 0)) for w in ws]
    in_specs += [pl.BlockSpec(b.shape, lambda i: (0, 0)) for b in bias_args]
    out_specs = [pl.BlockSpec((tm, w.shape[1]), lambda i: (i, 0)) for w in ws]
    out_shape = [jax.ShapeDtypeStruct((M, w.shape[1]), dt) for w, dt in zip(ws, out_dtypes)]
    return pl.pallas_call(
        functools.partial(_norm_proj_body, n_out, has_bias, tn),
        grid=(M // tm,),
        in_specs=in_specs, out_specs=out_specs, out_shape=out_shape,
        compiler_params=_cparams(("parallel",)),
        name="norm_proj",
    )(x2d, g.reshape(1, D), sh, sc, *ws, *bias_args)


def _gate_proj_t_body(x_ref, g_ref, sh_ref, sc_ref, wt_ref, b_ref, o_ref):
    h = _norm_mod(x_ref[...], g_ref[...], sh_ref[...], sc_ref[...]).astype(BF16)
    o_ref[...] = _dot_nt(wt_ref[...], h) + b_ref[...]


def gate_proj_t(x2d, g, sh, sc, wt, bias, seq, tm=512):
    M, D = x2d.shape
    B = sh.shape[0]
    NG = wt.shape[0]
    tiles_per_batch = seq // tm
    return pl.pallas_call(
        _gate_proj_t_body,
        grid=(M // tm,),
        in_specs=[pl.BlockSpec((tm, D), lambda i: (i, 0)),
                  pl.BlockSpec((1, D), lambda i: (0, 0)),
                  pl.BlockSpec((None, 1, D), lambda i: (i // tiles_per_batch, 0, 0)),
                  pl.BlockSpec((None, 1, D), lambda i: (i // tiles_per_batch, 0, 0)),
                  pl.BlockSpec((NG, D), lambda i: (0, 0)),
                  pl.BlockSpec((NG, 1), lambda i: (0, 0))],
        out_specs=pl.BlockSpec((None, NG, tm),
                               lambda i: (i // tiles_per_batch, 0, i % tiles_per_batch)),
        out_shape=jax.ShapeDtypeStruct((B, NG, seq), F32),
        compiler_params=_cparams(("parallel",)),
        name="gate_proj_t",
    )(x2d, g.reshape(1, D), sh, sc, wt, bias.reshape(NG, 1))


def _mix_ffn_body(tf, a_ref, wm_ref, x_ref, ga1_ref, g_ref, sh_ref, sc_ref, ga2_ref,
                  wi_ref, wo_ref, o_ref, act_s):
    F = wo_ref.shape[0]
    x1 = x_ref[...] + ga1_ref[...] * _dot(a_ref[...], wm_ref[...])
    h = _norm_mod(x1, g_ref[...], sh_ref[...], sc_ref[...]).astype(BF16)
    for f0 in range(0, F, tf):
        gate = _dot(h, wi_ref[:, f0:f0 + tf])
        up = _dot(h, wi_ref[:, F + f0:F + f0 + tf])
        act_s[:, f0:f0 + tf] = (gate * jax.nn.sigmoid(gate) * up).astype(BF16)
    o_ref[...] = x1 + ga2_ref[...] * _dot(act_s[...], wo_ref[...])


def mix_ffn(a, w_mix, x2d, ga1, g, sh, sc, ga2, w_in, w_out, tm=512, tf=256):
    M, D = x2d.shape
    K = a.shape[1]
    F = w_out.shape[0]
    B = sh.shape[0]
    tiles_per_batch = (M // B) // tm
    bvec = pl.BlockSpec((None, 1, D), lambda i: (i // tiles_per_batch, 0, 0))
    resident = lambda w: pl.BlockSpec(w.shape, lambda i: (0, 0), pipeline_mode=pl.Buffered(1))
    return pl.pallas_call(
        functools.partial(_mix_ffn_body, tf),
        grid=(M // tm,),
        in_specs=[pl.BlockSpec((tm, K), lambda i: (i, 0)),
                  resident(w_mix),
                  pl.BlockSpec((tm, D), lambda i: (i, 0)),
                  bvec,
                  pl.BlockSpec((1, D), lambda i: (0, 0)),
                  bvec, bvec, bvec,
                  resident(w_in), resident(w_out)],
        out_specs=pl.BlockSpec((tm, D), lambda i: (i, 0)),
        out_shape=jax.ShapeDtypeStruct((M, D), F32),
        scratch_shapes=[pltpu.VMEM((tm, F), BF16)],
        compiler_params=_cparams(("parallel",)),
        name="mix_ffn",
    )(a, w_mix, x2d, ga1, g.reshape(1, D), sh, sc, ga2, w_in, w_out)


def _lane_cumsum(x):
    n = x.shape[-1]
    lane = lax.broadcasted_iota(jnp.int32, x.shape, 1)
    sh = 1
    while sh < n:
        x = x + jnp.where(lane >= sh, pltpu.roll(x, sh, axis=1), 0.0)
        sh *= 2
    return x


def _mlstm_body(q_ref, k_ref, v_ref, o_ref, gt_ref, gout_ref, out_ref, c_s, n_s, m_s):
    c_idx = pl.program_id(1)
    LC = q_ref.shape[0]
    scale = ML_DQK ** -0.5

    @pl.when(c_idx == 0)
    def _():
        c_s[...] = jnp.zeros_like(c_s)
        n_s[...] = jnp.zeros_like(n_s)
        m_s[...] = jnp.zeros_like(m_s)

    gt = gt_ref[...]
    fpre = gt[ML_HEADS:2 * ML_HEADS, :]
    logf = jnp.minimum(fpre, 0.0) - jnp.log1p(jnp.exp(-jnp.abs(fpre)))
    bcum = _lane_cumsum(logf)

    row = lax.broadcasted_iota(jnp.int32, (LC, LC), 0)
    col = lax.broadcasted_iota(jnp.int32, (LC, LC), 1)
    causal = col <= row
    diag = col == row

    for h in range(ML_HEADS):
        li = gt[h:h + 1, :]
        lf = logf[h:h + 1, :]
        b_r = bcum[h:h + 1, :]
        a_r = li - b_r
        b_last = b_r[:, LC - 1:LC]
        m_prev = m_s[h:h + 1, 0:1]
        m_loc = b_last + jnp.max(a_r, axis=-1, keepdims=True)
        m_new = jnp.maximum(b_last + m_prev, m_loc)
        e_r = jnp.exp(b_last + a_r - m_new)
        decay = jnp.exp(b_last + m_prev - m_new)

        cm_col = jnp.max(jnp.where(causal, a_r, NEG_INF), axis=-1, keepdims=True)
        b_col = jnp.sum(jnp.where(causal, lf, 0.0), axis=-1, keepdims=True)
        e_col = jnp.sum(jnp.where(diag, e_r, 0.0), axis=-1, keepdims=True)
        g_col = jnp.maximum(m_prev, cm_col)
        w_intra = jnp.exp(jnp.where(causal, a_r - g_col, NEG_INF))
        w_inter = jnp.exp(m_prev - g_col)
        floor = jnp.exp(-(b_col + g_col))

        qh = q_ref[:, h * ML_DQK:(h + 1) * ML_DQK]
        kh = k_ref[:, h * ML_DQK:(h + 1) * ML_DQK]
        vh = v_ref[:, h * ML_DV:(h + 1) * ML_DV]
        c_prev = c_s[h]
        n_prev = n_s[h:h + 1, :]

        s = _dot_nt(qh, kh) * scale * w_intra
        inter = _dot(qh, c_prev.astype(BF16)) * scale
        num = _dot(s.astype(BF16), vh) + w_inter * inter
        qn_inter = jnp.sum(qh.astype(F32) * n_prev, axis=-1, keepdims=True) * scale
        qn = jnp.sum(s, axis=-1, keepdims=True) + w_inter * qn_inter
        denom = jnp.maximum(jnp.abs(qn), floor)
        hout = num / denom

        ke = kh.astype(F32) * e_col
        c_s[h] = decay * c_prev + _dot_tn(ke.astype(BF16), vh)
        n_s[h:h + 1, :] = decay * n_prev + jnp.sum(ke, axis=0, keepdims=True)
        m_s[h:h + 1, :] = jnp.broadcast_to(m_new, (1, m_s.shape[1]))

        var = jnp.mean(hout * hout, axis=-1, keepdims=True)
        hn = hout * lax.rsqrt(var + RMS_EPS) * gout_ref[:, h * ML_DV:(h + 1) * ML_DV]
        og = jax.nn.sigmoid(o_ref[:, h * ML_DV:(h + 1) * ML_DV].astype(F32))
        out_ref[:, h * ML_DV:(h + 1) * ML_DV] = (hn * og).astype(out_ref.dtype)


def mlstm_core(proj, gates_t, g_out, batch, seq):
    LC = ML_LC
    nc = seq // LC
    qk = ML_HEADS * ML_DQK
    vd = ML_HEADS * ML_DV
    row = lambda b, c: b * nc + c
    return pl.pallas_call(
        _mlstm_body,
        grid=(batch, nc),
        in_specs=[pl.BlockSpec((LC, qk), lambda b, c: (row(b, c), 0)),
                  pl.BlockSpec((LC, qk), lambda b, c: (row(b, c), 1)),
                  pl.BlockSpec((LC, vd), lambda b, c: (row(b, c), 1)),
                  pl.BlockSpec((LC, vd), lambda b, c: (row(b, c), 2)),
                  pl.BlockSpec((None, 2 * ML_HEADS, LC), lambda b, c: (b, 0, c)),
                  pl.BlockSpec((1, vd), lambda b, c: (0, 0))],
        out_specs=pl.BlockSpec((LC, vd), lambda b, c: (row(b, c), 0)),
        out_shape=jax.ShapeDtypeStruct((batch * seq, vd), BF16),
        scratch_shapes=[pltpu.VMEM((ML_HEADS, ML_DQK, ML_DV), F32),
                        pltpu.VMEM((8, ML_DQK), F32),
                        pltpu.VMEM((8, 128), F32)],
        compiler_params=_cparams(("parallel", "arbitrary")),
        name="mlstm_core",
    )(proj, proj, proj, proj, gates_t, g_out.reshape(1, vd))


def _kv_proj_body(x_ref, g_ref, sh_ref, sc_ref, wk_ref, wvt_ref, wcc_ref, gk_ref,
                  ks_ref, kw_ref, vst_ref, vwt_ref, cc_ref):
    i = pl.program_id(1)
    G, HD = NSA_GROUPS, NSA_HD
    tm = x_ref.shape[0]

    @pl.when(i == 0)
    def _():
        ks_ref[...] = jnp.zeros_like(ks_ref)
        kw_ref[...] = jnp.zeros_like(kw_ref)
        vst_ref[...] = jnp.zeros_like(vst_ref)
        vwt_ref[...] = jnp.zeros_like(vwt_ref)

    @pl.when(i > 0)
    def _():
        h = _norm_mod(x_ref[...], g_ref[...], sh_ref[...], sc_ref[...]).astype(BF16)
        cc_ref[...] = _dot(h, wcc_ref[...])
        kk = _dot(h, wk_ref[...])
        lane = lax.broadcasted_iota(jnp.int32, (tm, 128 - HD), 1)
        tail = jnp.where(lane < 2, 1.0, 0.0).astype(BF16)
        for kind, o_ref in enumerate((ks_ref, kw_ref)):
            for g in range(G):
                seg = kk[:, (kind * G + g) * HD:(kind * G + g + 1) * HD]
                var = jnp.mean(seg * seg, axis=-1, keepdims=True)
                segn = seg * lax.rsqrt(var + RMS_EPS) * gk_ref[kind:kind + 1, :]
                o_ref[:, g * 128:g * 128 + HD] = segn.astype(BF16)
                o_ref[:, g * 128 + HD:(g + 1) * 128] = tail
        vt = _dot_nt(wvt_ref[...], h)
        srow = lax.broadcasted_iota(jnp.int32, (VT_ROWS - HD, tm), 0)
        ones_blk = jnp.where(srow == 0, 1.0, 0.0).astype(BF16)
        for kind, o_ref in enumerate((vst_ref, vwt_ref)):
            for g in range(G):
                r0 = (kind * G + g) * HD
                o_ref[g * VT_ROWS:g * VT_ROWS + HD, :] = vt[r0:r0 + HD, :].astype(BF16)
                o_ref[g * VT_ROWS + HD:(g + 1) * VT_ROWS, :] = ones_blk


def kv_proj(x2d, g, sh, sc, wk, wvt, wcc, gk, batch, seq, tm=KV_PAD):
    assert tm == KV_PAD and seq % tm == 0
    M, D = x2d.shape
    G = NSA_GROUPS
    nt = seq // tm
    sp = seq + KV_PAD
    xrow = lambda b, i: (b * nt + jnp.maximum(i - 1, 0), 0)
    bvec = pl.BlockSpec((None, 1, D), lambda b, i: (b, 0, 0))
    full = lambda a: pl.BlockSpec(a.shape, lambda b, i: (0,) * a.ndim)
    return pl.pallas_call(
        _kv_proj_body,
        grid=(batch, nt + 1),
        in_specs=[pl.BlockSpec((tm, D), xrow),
                  pl.BlockSpec((1, D), lambda b, i: (0, 0)),
                  bvec, bvec, full(wk), full(wvt), full(wcc), full(gk)],
        out_specs=[pl.BlockSpec((None, tm, G * 128), lambda b, i: (b, i, 0)),
                   pl.BlockSpec((None, tm, G * 128), lambda b, i: (b, i, 0)),
                   pl.BlockSpec((None, G * VT_ROWS, tm), lambda b, i: (b, 0, i)),
                   pl.BlockSpec((None, G * VT_ROWS, tm), lambda b, i: (b, 0, i)),
                   pl.BlockSpec((tm, wcc.shape[1]), xrow)],
        out_shape=[jax.ShapeDtypeStruct((batch, sp, G * 128), BF16),
                   jax.ShapeDtypeStruct((batch, sp, G * 128), BF16),
                   jax.ShapeDtypeStruct((batch, G * VT_ROWS, sp), BF16),
                   jax.ShapeDtypeStruct((batch, G * VT_ROWS, sp), BF16),
                   jax.ShapeDtypeStruct((M, wcc.shape[1]), F32)],
        compiler_params=_cparams(("parallel", "arbitrary")),
        name="kv_proj",
    )(x2d, g.reshape(1, D), sh, sc, wk, wvt, wcc, gk)


def _gelu_tanh(x):
    c = math.sqrt(2.0 / math.pi)
    return 0.5 * x * (1.0 + jnp.tanh(c * (x + 0.044715 * (x * x * x))))


def _compress_body(cc0_ref, cc1_ref, cc2_ref, cc3_ref, pos_ref, w1_ref, w2_ref, w2t_ref, g_ref,
                   kc_ref, vct_ref, u_s, v_s):
    G, HD = NSA_GROUPS, NSA_HD
    nwin = u_s.shape[1]
    u_s[...] = jnp.zeros_like(u_s)
    v_s[...] = jnp.zeros_like(v_s)

    def step(i, carry):
        xi = jnp.concatenate([r[pl.ds(i, nwin, stride=CMP_STRIDE), :]
                              for r in (cc0_ref, cc1_ref, cc2_ref, cc3_ref)], axis=1)
        r0 = pl.multiple_of(i * HD, HD)
        r1 = pl.multiple_of((i + CMP_STRIDE) * HD, HD)
        for kind in range(2):
            p_lo = pos_ref[kind, pl.ds(i, 1), :]
            p_hi = pos_ref[kind, pl.ds(i + CMP_STRIDE, 1), :]
            w_lo = w1_ref[kind, pl.ds(r0, HD), :]
            w_hi = w1_ref[kind, pl.ds(r1, HD), :]
            for g in range(G):
                j = kind * G + g
                seg = xi[:, j * HD:(j + 1) * HD]
                u_s[j] += _dot((seg + p_lo).astype(BF16), w_lo)
                v_s[j] += _dot((seg + p_hi).astype(BF16), w_hi)
        return carry

    lax.fori_loop(0, CMP_STRIDE, step, 0)

    for kind in range(2):
        for g in range(G):
            j = kind * G + g
            pre = u_s[j] + pltpu.roll(v_s[j], nwin - 1, axis=0)
            hmid = _gelu_tanh(pre).astype(BF16)
            if kind == 0:
                y = _dot(hmid, w2_ref[...])
                var = jnp.mean(y * y, axis=-1, keepdims=True)
                yn = y * lax.rsqrt(var + RMS_EPS) * g_ref[...]
                kc_ref[:, g * 128:g * 128 + HD] = yn.astype(BF16)
                kc_ref[:, g * 128 + HD:(g + 1) * 128] = jnp.zeros((nwin, 128 - HD), BF16)
            else:
                vct_ref[g * HD:(g + 1) * HD, :] = _dot_nt(w2t_ref[...], hmid).astype(BF16)


def compress(cc, pos, w1, w2k, w2vt, g, batch, seq):
    G, HD = NSA_GROUPS, NSA_HD
    nwin = seq // CMP_STRIDE
    hid = w1.shape[2]
    full = lambda a: pl.BlockSpec(a.shape, lambda b: (0,) * a.ndim)
    return pl.pallas_call(
        _compress_body,
        grid=(batch,),
        in_specs=[pl.BlockSpec((seq, 128), functools.partial(lambda q, b: (b, q), q))
                  for q in range(4)]
                 + [full(pos), full(w1), full(w2k), full(w2vt), full(g)],
        out_specs=[pl.BlockSpec((None, nwin, G * 128), lambda b: (b, 0, 0)),
                   pl.BlockSpec((None, G * HD, nwin), lambda b: (b, 0, 0))],
        out_shape=[jax.ShapeDtypeStruct((batch, nwin, G * 128), BF16),
                   jax.ShapeDtypeStruct((batch, G * HD, nwin), BF16)],
        scratch_shapes=[pltpu.VMEM((2 * G, nwin, hid), F32), pltpu.VMEM((2 * G, nwin, hid), F32)],
        compiler_params=_cparams(("parallel",)),
        name="compress",
    )(cc, cc, cc, cc, pos, w1, w2k, w2vt, g)


def _t5_bucket(dist):
    n = jnp.maximum(dist, 0)
    max_exact = REL_BUCKETS // 2
    nf = jnp.maximum(n, 1).astype(F32)
    large = max_exact + (jnp.log(nf / max_exact) / math.log(REL_MAX_DIST / max_exact)
                         * (REL_BUCKETS - max_exact)).astype(jnp.int32)
    large = jnp.minimum(large, REL_BUCKETS - 1)
    return jnp.where(n < max_exact, n, large)


def _table_lookup(bucket, tab_ref, h):
    out = jnp.zeros(bucket.shape, F32)
    for k in range(REL_BUCKETS):
        out = jnp.where(bucket == k, tab_ref[k, h], out)
    return out


def _bias_prep_body(tab_ref, bc_ref, tp_ref, far_ref):
    h = pl.program_id(0)
    S = bc_ref.shape[1]
    cols = 128
    far = tab_ref[REL_BUCKETS - 1, h]
    far_ref[...] = jnp.full(far_ref.shape, far * LOG2E, F32)

    def chunk(r, carry):
        start = pl.multiple_of(r * cols, cols)
        n = lax.broadcasted_iota(jnp.int32, (NCMP_PAD, cols), 0)
        t = start + lax.broadcasted_iota(jnp.int32, (NCMP_PAD, cols), 1)
        dist_c = t - (n * CMP_STRIDE + CMP_BLOCK - 1)
        bias_c = _table_lookup(_t5_bucket(dist_c), tab_ref, h) * LOG2E
        bc_ref[:, pl.ds(start, cols)] = jnp.where(dist_c >= 0, bias_c, NEG_INF)
        return carry

    lax.fori_loop(0, S // cols, chunk, 0)
    j = lax.broadcasted_iota(jnp.int32, (ATT_TK, ATT_QB), 0)
    i = lax.broadcasted_iota(jnp.int32, (ATT_TK, ATT_QB), 1)
    for d in range(2):
        dist = d * ATT_TK + i - j
        rel = (_table_lookup(_t5_bucket(dist), tab_ref, h) - far) * LOG2E
        tp_ref[d] = jnp.where(dist >= 0, rel, NEG_INF)
    tp_ref[2] = jnp.where(i < j, 0.0, NEG_INF)


def bias_prep(rel_table, seq):
    assert ATT_TK == ATT_QB and ATT_TK + 1 > 113
    return pl.pallas_call(
        _bias_prep_body,
        grid=(NSA_HEADS,),
        in_specs=[pl.BlockSpec(memory_space=pltpu.SMEM)],
        out_specs=[pl.BlockSpec((None, NCMP_PAD, seq), lambda h: (h, 0, 0)),
                   pl.BlockSpec((None, 3, ATT_TK, ATT_QB), lambda h: (h, 0, 0, 0)),
                   pl.BlockSpec((None, 8, 128), lambda h: (h, 0, 0))],
        out_shape=[jax.ShapeDtypeStruct((NSA_HEADS, NCMP_PAD, seq), F32),
                   jax.ShapeDtypeStruct((NSA_HEADS, 3, ATT_TK, ATT_QB), F32),
                   jax.ShapeDtypeStruct((NSA_HEADS, 8, 128), F32)],
        compiler_params=_cparams(("parallel",)),
        name="bias_prep",
    )(rel_table)


def _q_proj_body(x_ref, g_ref, sh_ref, sc_ref, wqt_ref, wgt_ref, bg_ref, gq_ref, qt_ref, gt_ref):
    HD = NSA_HD
    h = _norm_mod(x_ref[...], g_ref[...], sh_ref[...], sc_ref[...]).astype(BF16)
    gt_ref[...] = _dot_nt(wgt_ref[...], h) + bg_ref[...]
    qt = _dot_nt(wqt_ref[...], h)
    scale = gq_ref[...] * (HD ** -0.5 * LOG2E)
    for hh in range(NSA_HEADS):
        seg = qt[hh * HD:(hh + 1) * HD, :]
        var = jnp.mean(seg * seg, axis=0, keepdims=True)
        qt_ref[hh * HD:(hh + 1) * HD, :] = (seg * lax.rsqrt(var + RMS_EPS) * scale).astype(BF16)


def q_proj(x2d, g, sh, sc, wqt, wgt, bg, gq, batch, seq, tm=512):
    M, D = x2d.shape
    nt = seq // tm
    nq = wqt.shape[0]
    ng = wgt.shape[0]
    bvec = pl.BlockSpec((None, 1, D), lambda i: (i // nt, 0, 0))
    full = lambda a: pl.BlockSpec(a.shape, lambda i: (0,) * a.ndim)
    return pl.pallas_call(
        _q_proj_body,
        grid=(M // tm,),
        in_specs=[pl.BlockSpec((tm, D), lambda i: (i, 0)),
                  pl.BlockSpec((1, D), lambda i: (0, 0)),
                  bvec, bvec, full(wqt), full(wgt), full(bg), full(gq)],
        out_specs=[pl.BlockSpec((None, nq, tm), lambda i: (i // nt, 0, i % nt)),
                   pl.BlockSpec((None, ng, tm), lambda i: (i // nt, 0, i % nt))],
        out_shape=[jax.ShapeDtypeStruct((batch, nq, seq), BF16),
                   jax.ShapeDtypeStruct((batch, ng, seq), F32)],
        compiler_params=_cparams(("parallel",)),
        name="q_proj",
    )(x2d, g.reshape(1, D), sh, sc, wqt, wgt, bg, gq)


def _heads_on_lanes(pieces):
    return jnp.concatenate(pieces, axis=1)


def _nsa_body(gps, qt_ref, gt_ref, kc_ref, vct_ref, ks_ref, kw_ref, vst_ref, vwt_ref,
              bct_ref, tp_ref, far_ref, ovt_ref, out_ref, msel_s):
    qb = pl.program_id(2)
    QB, HD, HPG = ATT_QB, NSA_HD, NSA_HPG
    R = HPG * QB
    t0 = pl.multiple_of(qb * QB, QB)
    near0 = pl.multiple_of(t0 + WINDOW - ATT_TK, ATT_TK)
    wlen = WINDOW + QB
    nblk = ovt_ref.shape[0]
    pad_blocks = KV_PAD // SLC_BLOCK
    pens = [jnp.where(qb >= d, 0.0, NEG_INF) for d in range(5)]
    jb = lax.broadcasted_iota(jnp.int32, (nblk, QB), 0)
    jbf = jb.astype(F32)
    qid = jnp.right_shift(t0 + lax.broadcasted_iota(jnp.int32, (nblk, QB), 1), 6)
    forced = (jb == 0) | (jb == qid) | (jb == qid - 1)
    srow = lax.broadcasted_iota(jnp.int32, (16, R), 0)

    def finish(acc):
        return acc[0:HD, :] / acc[HD:HD + 1, :]

    def block_mask(gl, first_row, nblocks):
        rows = msel_s[gl, pl.ds(first_row, nblocks), :]
        one_head = jnp.concatenate(
            [jnp.broadcast_to(rows[r:r + 1, :], (SLC_BLOCK, QB)) for r in range(nblocks)], axis=0)
        return _heads_on_lanes([one_head] * HPG)

    def tile_part(s_tile, v_tile):
        m_t = jnp.max(s_tile, axis=0, keepdims=True)
        m_safe = jnp.where(m_t == NEG_INF, 0.0, m_t)
        p = jnp.exp2(s_tile - m_safe).astype(BF16)
        return m_t, _dot(v_tile, p)[0:HD + 8, :]

    def combine(parts):
        m_fin = parts[0][0]
        for m_t, _ in parts[1:]:
            m_fin = jnp.maximum(m_fin, m_t)
        acc = None
        for m_t, pv in parts:
            term = jnp.exp2(m_t - m_fin) * pv
            acc = term if acc is None else acc + term
        return m_fin, acc

    groups = range(gps)
    hs = [[gl * HPG + h for h in range(HPG)] for gl in groups]
    kcol = [slice(gl * 128, (gl + 1) * 128) for gl in groups]
    vrow = [slice(gl * VT_ROWS, (gl + 1) * VT_ROWS) for gl in groups]

    qms = []
    for gl in groups:
        q_all = _heads_on_lanes([qt_ref[h * HD:(h + 1) * HD, :] for h in hs[gl]])
        far = _heads_on_lanes([far_ref[h, 0:1, :] for h in hs[gl]])
        hi = far.astype(BF16).astype(F32)
        extra = jnp.where(srow == 0, hi, jnp.where(srow == 1, far - hi, 0.0)).astype(BF16)
        qms.append(jnp.concatenate([q_all, extra, jnp.zeros((128 - HD - 16, R), BF16)], axis=0))

    s_cs = [_dot(kc_ref[:, kcol[gl]], qms[gl]) for gl in groups]
    s_ws = [_dot(kw_ref[pl.ds(t0, wlen), kcol[gl]], qms[gl]) for gl in groups]
    s_ns = [_dot(ks_ref[pl.ds(near0, 2 * ATT_TK), kcol[gl]], qms[gl]) for gl in groups]

    o_cmps, scores = [], []
    for gl in groups:
        s_c = s_cs[gl] + _heads_on_lanes([bct_ref[h] for h in hs[gl]])
        m_c = jnp.max(s_c, axis=0, keepdims=True)
        m_c = jnp.where(m_c == NEG_INF, 0.0, m_c)
        e_c = jnp.exp2(s_c - m_c)
        p_c = e_c / jnp.maximum(jnp.sum(e_c, axis=0, keepdims=True), jnp.finfo(F32).tiny)
        o_cmps.append(_dot(vct_ref[gl * HD:(gl + 1) * HD, :], p_c.astype(BF16)))
        p_sum = p_c[:, 0:QB]
        for h in range(1, HPG):
            p_sum = p_sum + p_c[:, h * QB:(h + 1) * QB]
        p_hi = p_sum.astype(BF16)
        p_lo = (p_sum - p_hi.astype(F32)).astype(BF16)
        imp = _dot(ovt_ref[...], jnp.concatenate([p_hi, p_lo], axis=0))
        score = jnp.where(forced, FORCE_SCORE, imp)
        scores.append(jnp.where(jb <= qid, score, NEG_INF))

    def win_bias(h, d):
        if d == 4:
            return tp_ref[h, 2] + pens[4]
        if d in (2, 3):
            return jnp.full((ATT_TK, QB), pens[d], F32)
        return tp_ref[h, 1] + pens[1] if d == 1 else tp_ref[h, 0]

    nwt = wlen // ATT_TK
    o_wins = []
    for gl in groups:
        parts = []
        for i in range(nwt):
            s_t = s_ws[gl][i * ATT_TK:(i + 1) * ATT_TK] + _heads_on_lanes(
                [win_bias(h, nwt - 1 - i) for h in hs[gl]])
            parts.append(tile_part(s_t, vwt_ref[vrow[gl], pl.ds(t0 + i * ATT_TK, ATT_TK)]))
        o_wins.append(finish(combine(parts)[1]))

    msels = [jnp.full((nblk, QB), NEG_INF, F32) for _ in groups]
    for _ in range(SLC_TOPK):
        for gl in groups:
            mx = jnp.max(scores[gl], axis=0, keepdims=True)
            first = jnp.min(jnp.where(scores[gl] == mx, jbf, float(nblk)), axis=0, keepdims=True)
            pick = jbf == first
            msels[gl] = jnp.where(pick & (mx > NEG_INF), 0.0, msels[gl])
            scores[gl] = jnp.where(pick, NEG_INF, scores[gl])
    for gl in groups:
        msel_s[gl, 0:pad_blocks, :] = jnp.full((pad_blocks, QB), NEG_INF, F32)
        msel_s[gl, pad_blocks:pad_blocks + nblk, :] = msels[gl]
        msel_s[gl, pad_blocks + nblk:, :] = jnp.full(
            (msel_s.shape[1] - pad_blocks - nblk, QB), NEG_INF, F32)

    carry0 = []
    for gl in groups:
        prev_mask = jnp.concatenate(
            [jnp.broadcast_to(jnp.max(jnp.where(jb == 2 * qb - 2 + r, msels[gl], NEG_INF), axis=0,
                                      keepdims=True), (SLC_BLOCK, QB)) for r in range(2)], axis=0)
        s_prev = s_ns[gl][0:ATT_TK] + _heads_on_lanes([tp_ref[h, 1] + prev_mask for h in hs[gl]])
        s_diag = s_ns[gl][ATT_TK:] + _heads_on_lanes([tp_ref[h, 0] for h in hs[gl]])
        carry0.append(combine([
            tile_part(s_diag, vst_ref[vrow[gl], pl.ds(near0 + ATT_TK, ATT_TK)]),
            tile_part(s_prev, vst_ref[vrow[gl], pl.ds(near0, ATT_TK)])]))

    def far_step(c, carry):
        tile0 = qb + (WINDOW - ATT_TK) // ATT_TK - (c + 1) * (FAR_CHUNK // ATT_TK)
        row0 = pl.multiple_of(tile0 * ATT_TK, ATT_TK)
        blk0 = tile0 * (ATT_TK // SLC_BLOCK)
        s_fs = [_dot(ks_ref[pl.ds(row0, FAR_CHUNK), kcol[gl]], qms[gl]) for gl in groups]
        out = []
        for gl in groups:
            parts = [carry[gl]]
            for j in range(FAR_CHUNK // ATT_TK):
                s_t = s_fs[gl][j * ATT_TK:(j + 1) * ATT_TK] + block_mask(
                    gl, blk0 + j * (ATT_TK // SLC_BLOCK), ATT_TK // SLC_BLOCK)
                parts.append(tile_part(
                    s_t, vst_ref[vrow[gl], pl.ds(row0 + j * ATT_TK, ATT_TK)]))
            out.append(combine(parts))
        return tuple(out)

    n_far = (qb + 2) // 4
    sel = lax.fori_loop(0, n_far, far_step, tuple(carry0))

    gates = jax.nn.sigmoid(gt_ref[...])
    for gl in groups:
        o_cmp, o_win = o_cmps[gl], o_wins[gl]
        o_sel = finish(sel[gl][1])
        outs = []
        for h in range(HPG):
            lanes = slice(h * QB, (h + 1) * QB)
            r = gl * 16 + 3 * h
            outs.append(gates[r:r + 1, :] * o_cmp[:, lanes]
                        + gates[r + 1:r + 2, :] * o_sel[:, lanes]
                        + gates[r + 2:r + 3, :] * o_win[:, lanes])
        for pair in range(HPG // 2):
            two = jnp.concatenate(outs[2 * pair:2 * pair + 2], axis=0)
            c0 = gl * HPG * HD + pair * 2 * HD
            out_ref[:, c0:c0 + 2 * HD] = two.T.astype(out_ref.dtype)


def _overlap_matrix_t(nblk):
    start = np.arange(NCMP_PAD) * CMP_STRIDE
    sj = np.arange(nblk) * SLC_BLOCK
    ov = (np.minimum(start[None, :] + CMP_BLOCK, sj[:, None] + SLC_BLOCK)
          - np.maximum(start[None, :], sj[:, None]))
    ov = np.clip(ov, 0, None) / CMP_BLOCK
    ov[:, NCMP_PAD - 1] = 0.0
    return np.concatenate([ov, ov], axis=1).astype(np.float32)


def nsa_attn(qt, gt, kc, vct, ks, kw, vst, vwt, bias_ct, tiles, far, batch, seq, gps=ATT_GPS):
    QB, HD, HPG, G = ATT_QB, NSA_HD, NSA_HPG, NSA_GROUPS
    nblk = seq // SLC_BLOCK
    assert nblk % 8 == 0 and seq % QB == 0 and seq // CMP_STRIDE == NCMP_PAD and G % gps == 0
    assert WINDOW % ATT_TK == 0 and FAR_CHUNK == 4 * ATT_TK and KV_PAD >= FAR_CHUNK - ATT_TK
    assert SEL_ROWS >= KV_PAD // SLC_BLOCK + nblk
    nq = seq // QB
    sp = seq + KV_PAD
    ovt = jnp.asarray(_overlap_matrix_t(nblk), dtype=BF16)
    return pl.pallas_call(
        functools.partial(_nsa_body, gps),
        grid=(batch, G // gps, nq),
        in_specs=[pl.BlockSpec((None, gps * HPG * HD, QB), lambda b, g, i: (b, g, i)),
                  pl.BlockSpec((None, gps * 16, QB), lambda b, g, i: (b, g, i)),
                  pl.BlockSpec((None, NCMP_PAD, gps * 128), lambda b, g, i: (b, 0, g)),
                  pl.BlockSpec((None, gps * HD, NCMP_PAD), lambda b, g, i: (b, g, 0)),
                  pl.BlockSpec((None, sp, gps * 128), lambda b, g, i: (b, 0, g)),
                  pl.BlockSpec((None, sp, gps * 128), lambda b, g, i: (b, 0, g)),
                  pl.BlockSpec((None, gps * VT_ROWS, sp), lambda b, g, i: (b, g, 0)),
                  pl.BlockSpec((None, gps * VT_ROWS, sp), lambda b, g, i: (b, g, 0)),
                  pl.BlockSpec((gps * HPG, NCMP_PAD, QB), lambda b, g, i: (g, 0, i)),
                  pl.BlockSpec((gps * HPG, 3, ATT_TK, QB), lambda b, g, i: (g, 0, 0, 0)),
                  pl.BlockSpec((gps * HPG, 8, 128), lambda b, g, i: (g, 0, 0)),
                  pl.BlockSpec((nblk, 2 * NCMP_PAD), lambda b, g, i: (0, 0))],
        out_specs=pl.BlockSpec((QB, gps * HPG * HD), lambda b, g, i: (b * nq + i, g)),
        out_shape=jax.ShapeDtypeStruct((batch * seq, NSA_HEADS * HD), BF16),
        scratch_shapes=[pltpu.VMEM((gps, SEL_ROWS, QB), F32)],
        compiler_params=_cparams(("parallel", "parallel", "arbitrary")),
        name="nsa_attn",
    )(qt, gt, kc, vct, ks, kw, vst, vwt, bias_ct, tiles, far, ovt)


def _mlstm_layer(x2d, g_mix, sh, sc, w_in, b_if, g_out, w_out, batch, seq):
    nbig = 2 * ML_HEADS * ML_DQK + 2 * ML_HEADS * ML_DV
    (proj,) = norm_proj(x2d, g_mix, sh, sc, [w_in[:, :nbig].astype(BF16)], [None], [BF16])
    gates_t = gate_proj_t(x2d, g_mix, sh, sc, w_in[:, nbig:].T.astype(BF16), b_if, seq)
    return mlstm_core(proj, gates_t, g_out, batch, seq), w_out.astype(BF16)


def _nsa_shared(x2d, g_kv, kv_sh, kv_sc, w_kv, pos_k, w_k1, w_k2, pos_v, w_v1, w_v2,
                g_knorm, batch, seq):
    gw = NSA_GROUPS * NSA_HD
    part = lambda i: w_kv[:, i * gw:(i + 1) * gw]
    wk = jnp.concatenate([part(2), part(4)], axis=1).astype(BF16)
    wvt = jnp.concatenate([part(3), part(5)], axis=1).T.astype(BF16)
    wcc = jnp.concatenate([part(0), part(1)], axis=1).astype(BF16)
    ks, kw, vst, vwt, cc = kv_proj(x2d, g_kv, kv_sh, kv_sc, wk, wvt, wcc, g_knorm[1:3],
                                   batch, seq)
    pos = jnp.stack([pos_k, pos_v])
    w1 = jnp.stack([w_k1, w_v1]).astype(BF16)
    kc, vct = compress(cc, pos, w1, w_k2.astype(BF16), w_v2.T.astype(BF16), g_knorm[0:1],
                       batch, seq)
    return kc, vct, ks, kw, vst, vwt


def _gate_weights_t(w_q, b_gate):
    nq = NSA_HEADS * NSA_HD
    per = 3 * NSA_HPG
    wg = w_q[:, nq:].T.reshape(NSA_GROUPS, per, -1)
    wg = jnp.pad(wg, ((0, 0), (0, 16 - per), (0, 0))).reshape(NSA_GROUPS * 16, -1)
    bg = jnp.pad(b_gate.reshape(NSA_GROUPS, per), ((0, 0), (0, 16 - per))).reshape(-1, 1)
    return wg.astype(BF16), bg


def _nsa_layer(x2d, g_mix, sh, sc, shared, w_q, b_gate, g_qnorm, w_out, bias_ct, tiles, far,
               batch, seq):
    nq = NSA_HEADS * NSA_HD
    wgt, bg = _gate_weights_t(w_q, b_gate)
    qt, gt = q_proj(x2d, g_mix, sh, sc, w_q[:, :nq].T.astype(BF16), wgt, bg,
                    g_qnorm.reshape(NSA_HD, 1), batch, seq)
    kc, vct, ks, kw, vst, vwt = shared
    att = nsa_attn(qt, gt, kc, vct, ks, kw, vst, vwt, bias_ct, tiles, far, batch, seq)
    return att, w_out.astype(BF16)


def kernel(x, c, w_ada, b_ada, g_norm_mix, g_norm_ffn, w_ffn_in, w_ffn_out, w_a_in, b_a_if, g_a_out, w_a_out, w_kv_ada, b_kv_ada, g_kv_norm, w_kv, pos_cmp_k, w_cmp_k1, w_cmp_k2, pos_cmp_v, w_cmp_v1, w_cmp_v2, g_knorm, w_b_q, b_b_gate, g_qnorm, w_b_out, rel_table):
    B, S, D = x.shape
    depth = w_ada.shape[0]
    n_a = w_a_in.shape[0]
    x2d = x.reshape(B * S, D)
    mods = ada_mod(c, w_ada, b_ada)
    kv_mod = ada_mod(c, w_kv_ada[None], b_kv_ada[None])[0]
    shared = None
    bias_ct = tiles = far = None
    for layer in range(depth):
        sh1, sc1, ga1, sh2, sc2, ga2 = [mods[layer, :, i * D:(i + 1) * D].reshape(B, 1, D)
                                        for i in range(6)]
        if layer < n_a:
            mixed, w_mix = _mlstm_layer(x2d, g_norm_mix[layer], sh1, sc1, w_a_in[layer],
                                        b_a_if[layer], g_a_out[layer], w_a_out[layer], B, S)
        else:
            j = layer - n_a
            if shared is None:
                kv_sh = kv_mod[:, :D].reshape(B, 1, D)
                kv_sc = kv_mod[:, D:].reshape(B, 1, D)
                shared = _nsa_shared(x2d, g_kv_norm, kv_sh, kv_sc, w_kv, pos_cmp_k, w_cmp_k1,
                                     w_cmp_k2, pos_cmp_v, w_cmp_v1, w_cmp_v2, g_knorm, B, S)
                bias_ct, tiles, far = bias_prep(rel_table, S)
            mixed, w_mix = _nsa_layer(x2d, g_norm_mix[layer], sh1, sc1, shared, w_b_q[j],
                                      b_b_gate[j], g_qnorm[j], w_b_out[j], bias_ct, tiles, far, B, S)
        x2d = mix_ffn(mixed, w_mix, x2d, ga1, g_norm_ffn[layer], sh2, sc2, ga2,
                      w_ffn_in[layer].astype(BF16), w_ffn_out[layer].astype(BF16))
    return x2d.reshape(B, S, D)
```

```python
import functools
import math

import jax
import jax.numpy as jnp
import numpy as np
from jax import lax
from jax.experimental import pallas as pl
from jax.experimental.pallas import tpu as pltpu

F32 = jnp.float32
BF16 = jnp.bfloat16
NEG_INF = float("-inf")
LOG2E = math.log2(math.e)

RMS_EPS = 1e-6

ML_HEADS = 4
ML_DQK = 128
ML_DV = 256
ML_LC = 256

NSA_HEADS = 16
NSA_GROUPS = 4
NSA_HPG = 4
NSA_HD = 64
CMP_BLOCK = 32
CMP_STRIDE = 16
SLC_BLOCK = 64
SLC_TOPK = 8
WINDOW = 512
FORCE_SCORE = 1e4
REL_BUCKETS = 32
REL_MAX_DIST = 128
ATT_QB = 128
ATT_TK = 128
ATT_GPS = 4
NCMP_PAD = 128
KV_PAD = WINDOW
FAR_CHUNK = 512
VT_ROWS = 80
SEL_ROWS = 2 * KV_PAD // SLC_BLOCK + 2 * 16

VMEM_LIMIT = 48 * 1024 * 1024


def _cparams(sem):
    return pltpu.CompilerParams(dimension_semantics=sem, vmem_limit_bytes=VMEM_LIMIT)


def _dot(a, b):
    return jnp.dot(a, b, preferred_element_type=F32)


def _dot_nt(a, b):
    return lax.dot_general(a, b, (((1,), (1,)), ((), ())), preferred_element_type=F32)


def _dot_tn(a, b):
    return lax.dot_general(a, b, (((0,), (0,)), ((), ())), preferred_element_type=F32)


def _norm_mod(x, g, sh, sc):
    var = jnp.mean(x * x, axis=-1, keepdims=True)
    y = x * lax.rsqrt(var + RMS_EPS) * g
    return y * (1.0 + sc) + sh


def _ada_body(c_ref, w_ref, b_ref, o_ref):
    c = c_ref[...]
    ca = c * jax.nn.sigmoid(c)
    o_ref[...] = jnp.dot(ca, w_ref[...], preferred_element_type=F32,
                         precision=lax.Precision.HIGHEST) + b_ref[...]


def ada_mod(c, w, b, tn=1024):
    L, D, N = w.shape
    B = c.shape[0]
    return pl.pallas_call(
        _ada_body,
        grid=(L, N // tn),
        in_specs=[pl.BlockSpec((B, D), lambda l, j: (0, 0)),
                  pl.BlockSpec((None, D, tn), lambda l, j: (l, 0, j)),
                  pl.BlockSpec((None, 1, tn), lambda l, j: (l, 0, j))],
        out_specs=pl.BlockSpec((None, B, tn), lambda l, j: (l, 0, j)),
        out_shape=jax.ShapeDtypeStruct((L, B, N), F32),
        compiler_params=_cparams(("parallel", "parallel")),
        name="ada_mod",
    )(c, w, b.reshape(L, 1, N))


def _norm_proj_body(n_out, has_bias, tn, x_ref, g_ref, sh_ref, sc_ref, *refs):
    w_refs = refs[:n_out]
    b_refs = refs[n_out:2 * n_out]
    o_refs = refs[2 * n_out:3 * n_out]
    h = _norm_mod(x_ref[...], g_ref[...], sh_ref[...], sc_ref[...]).astype(BF16)
    for w_ref, b_ref, o_ref, hb in zip(w_refs, b_refs, o_refs, has_bias):
        n = w_ref.shape[1]
        step = min(tn, n)
        for n0 in range(0, n, step):
            acc = _dot(h, w_ref[:, n0:n0 + step])
            if hb:
                acc = acc + b_ref[:, n0:n0 + step]
            o_ref[:, n0:n0 + step] = acc.astype(o_ref.dtype)


def norm_proj(x2d, g, sh, sc, ws, biases, out_dtypes, tm=512, tn=512):
    M, D = x2d.shape
    B = sh.shape[0]
    tiles_per_batch = (M // B) // tm
    n_out = len(ws)
    has_bias = tuple(b is not None for b in biases)
    bias_args = [(b if b is not None else jnp.zeros((w.shape[1],), F32)).reshape(1, -1)
                 for b, w in zip(biases, ws)]
    in_specs = [pl.BlockSpec((tm, D), lambda i: (i, 0)),
                pl.BlockSpec((1, D), lambda i: (0, 0)),
                pl.BlockSpec((None, 1, D), lambda i: (i // tiles_per_batch, 0, 0)),
                pl.BlockSpec((None, 1, D), lambda i: (i // tiles_per_batch, 0, 0))]
    in_specs += [pl.BlockSpec(w.shape, lambda i: (0, 0)) for w in ws]
    in_specs += [pl.BlockSpec(b.shape, lambda i: (0, 0)) for b in bias_args]
    out_specs = [pl.BlockSpec((tm, w.shape[1]), lambda i: (i, 0)) for w in ws]
    out_shape = [jax.ShapeDtypeStruct((M, w.shape[1]), dt) for w, dt in zip(ws, out_dtypes)]
    return pl.pallas_call(
        functools.partial(_norm_proj_body, n_out, has_bias, tn),
        grid=(M // tm,),
        in_specs=in_specs, out_specs=out_specs, out_shape=out_shape,
        compiler_params=_cparams(("parallel",)),
        name="norm_proj",
    )(x2d, g.reshape(1, D), sh, sc, *ws, *bias_args)


def _mix_ffn_body(tf, a_ref, wm_ref, x_ref, ga1_ref, g_ref, sh_ref, sc_ref, ga2_ref,
                  wi_ref, wo_ref, o_ref, act_s):
    F = wo_ref.shape[0]
    x1 = x_ref[...] + ga1_ref[...] * _dot(a_ref[...], wm_ref[...])
    h = _norm_mod(x1, g_ref[...], sh_ref[...], sc_ref[...]).astype(BF16)
    for f0 in range(0, F, tf):
        gate = _dot(h, wi_ref[:, f0:f0 + tf])
        up = _dot(h, wi_ref[:, F + f0:F + f0 + tf])
        act_s[:, f0:f0 + tf] = (gate * jax.nn.sigmoid(gate) * up).astype(BF16)
    o_ref[...] = x1 + ga2_ref[...] * _dot(act_s[...], wo_ref[...])


def mix_ffn(a, w_mix, x2d, ga1, g, sh, sc, ga2, w_in, w_out, tm=512, tf=256):
    M, D = x2d.shape
    K = a.shape[1]
    F = w_out.shape[0]
    B = sh.shape[0]
    tiles_per_batch = (M // B) // tm
    bvec = pl.BlockSpec((None, 1, D), lambda i: (i // tiles_per_batch, 0, 0))
    resident = lambda w: pl.BlockSpec(w.shape, lambda i: (0, 0), pipeline_mode=pl.Buffered(1))
    return pl.pallas_call(
        functools.partial(_mix_ffn_body, tf),
        grid=(M // tm,),
        in_specs=[pl.BlockSpec((tm, K), lambda i: (i, 0)),
                  resident(w_mix),
                  pl.BlockSpec((tm, D), lambda i: (i, 0)),
                  bvec,
                  pl.BlockSpec((1, D), lambda i: (0, 0)),
                  bvec, bvec, bvec,
                  resident(w_in), resident(w_out)],
        out_specs=pl.BlockSpec((tm, D), lambda i: (i, 0)),
        out_shape=jax.ShapeDtypeStruct((M, D), F32),
        scratch_shapes=[pltpu.VMEM((tm, F), BF16)],
        compiler_params=_cparams(("parallel",)),
        name="mix_ffn",
    )(a, w_mix, x2d, ga1, g.reshape(1, D), sh, sc, ga2, w_in, w_out)


def _sublane_scan(x, op, fill):
    n = x.shape[0]
    row = lax.broadcasted_iota(jnp.int32, x.shape, 0)
    sh = 1
    while sh < n:
        x = op(x, jnp.where(row >= sh, pltpu.roll(x, sh, axis=0), fill))
        sh *= 2
    return x


def _mlstm_body(q_ref, k_ref, v_ref, o_ref, gc_ref, gout_ref, out_ref, c_s, n_s, m_s):
    c_idx = pl.program_id(1)
    LC = q_ref.shape[0]
    NH = ML_HEADS
    scale = ML_DQK ** -0.5

    @pl.when(c_idx == 0)
    def _():
        c_s[...] = jnp.zeros_like(c_s)
        n_s[...] = jnp.zeros_like(n_s)
        m_s[...] = jnp.zeros_like(m_s)

    gc = gc_ref[...]
    logf = jnp.minimum(gc, 0.0) - jnp.log1p(jnp.exp(-jnp.abs(gc)))
    bcum = pltpu.roll(_sublane_scan(logf, jnp.add, 0.0), 128 - NH, axis=1)
    a_c = gc - bcum
    cmax = _sublane_scan(a_c, jnp.maximum, NEG_INF)
    a_t = a_c.T

    row = lax.broadcasted_iota(jnp.int32, (LC, LC), 0)
    col = lax.broadcasted_iota(jnp.int32, (LC, LC), 1)
    causal = col <= row
    ones_col = jnp.ones((LC, 128), BF16)

    def lanes2(x):
        return jnp.concatenate([x, x], axis=1)

    for h in range(NH):
        a_rep = jnp.broadcast_to(a_c[:, h:h + 1], (LC, 128))
        b_rep = jnp.broadcast_to(bcum[:, h:h + 1], (LC, 128))
        cm_rep = jnp.broadcast_to(cmax[:, h:h + 1], (LC, 128))
        a_r = a_t[h:h + 1, :]
        b_last = b_rep[LC - 1:LC, :]
        m_prev = m_s[h:h + 1, :]
        m_new = jnp.maximum(b_last + m_prev, b_last + cm_rep[LC - 1:LC, :])
        decay = jnp.exp(b_last + m_prev - m_new)
        e_rep = jnp.exp(b_last + a_rep - m_new)
        g_rep = jnp.maximum(m_prev, cm_rep)
        w_intra = jnp.exp(jnp.where(causal, a_r - lanes2(g_rep), NEG_INF))
        w_inter = jnp.exp(m_prev - g_rep)
        floor = jnp.exp(-(b_rep + g_rep))

        qh = q_ref[:, h * ML_DQK:(h + 1) * ML_DQK]
        kh = k_ref[:, h * ML_DQK:(h + 1) * ML_DQK]
        vh = v_ref[:, h * ML_DV:(h + 1) * ML_DV]
        c_prev = c_s[h]
        n_prev = n_s[h:h + 1, :]

        s = (_dot_nt(qh, kh) * scale * w_intra).astype(BF16)
        inter = _dot(qh, c_prev.astype(BF16)) * scale
        num = _dot(s, vh) + lanes2(w_inter) * inter
        n_rows = jnp.broadcast_to(n_prev, (128, ML_DQK)).astype(BF16)
        qn = _dot(s, ones_col) + w_inter * (_dot_nt(qh, n_rows) * scale)
        denom = jnp.maximum(jnp.abs(qn), floor)
        hout = num / lanes2(denom)

        ke = kh.astype(F32) * e_rep
        c_s[h] = lanes2(decay) * c_prev + _dot_tn(ke.astype(BF16), vh)
        n_s[h:h + 1, :] = decay * n_prev + jnp.sum(ke, axis=0, keepdims=True)
        m_s[h:h + 1, :] = m_new

        ssq = _dot((hout * hout).astype(BF16), jnp.ones((ML_DV, 128), BF16))
        rs = lax.rsqrt(ssq * (1.0 / ML_DV) + RMS_EPS)
        hn = hout * lanes2(rs) * gout_ref[:, h * ML_DV:(h + 1) * ML_DV]
        og = jax.nn.sigmoid(o_ref[:, h * ML_DV:(h + 1) * ML_DV].astype(F32))
        out_ref[:, h * ML_DV:(h + 1) * ML_DV] = (hn * og).astype(out_ref.dtype)


def mlstm_core(proj, gates, g_out, batch, seq):
    LC = ML_LC
    nc = seq // LC
    qk = ML_HEADS * ML_DQK
    vd = ML_HEADS * ML_DV
    row = lambda b, c: b * nc + c
    return pl.pallas_call(
        _mlstm_body,
        grid=(batch, nc),
        in_specs=[pl.BlockSpec((LC, qk), lambda b, c: (row(b, c), 0)),
                  pl.BlockSpec((LC, qk), lambda b, c: (row(b, c), 1)),
                  pl.BlockSpec((LC, vd), lambda b, c: (row(b, c), 1)),
                  pl.BlockSpec((LC, vd), lambda b, c: (row(b, c), 2)),
                  pl.BlockSpec((LC, 128), lambda b, c: (row(b, c), 0)),
                  pl.BlockSpec((1, vd), lambda b, c: (0, 0))],
        out_specs=pl.BlockSpec((LC, vd), lambda b, c: (row(b, c), 0)),
        out_shape=jax.ShapeDtypeStruct((batch * seq, vd), BF16),
        scratch_shapes=[pltpu.VMEM((ML_HEADS, ML_DQK, ML_DV), F32),
                        pltpu.VMEM((8, ML_DQK), F32),
                        pltpu.VMEM((8, 128), F32)],
        compiler_params=_cparams(("parallel", "arbitrary")),
        name="mlstm_core",
    )(proj, proj, proj, proj, gates, g_out.reshape(1, vd))


def _kv_proj_body(x_ref, g_ref, sh_ref, sc_ref, wk_ref, wvt_ref, wcc_ref, gk_ref,
                  ks_ref, kw_ref, vst_ref, vwt_ref, cc_ref):
    i = pl.program_id(1)
    G, HD = NSA_GROUPS, NSA_HD
    tm = x_ref.shape[0]

    @pl.when(i == 0)
    def _():
        ks_ref[...] = jnp.zeros_like(ks_ref)
        kw_ref[...] = jnp.zeros_like(kw_ref)
        vst_ref[...] = jnp.zeros_like(vst_ref)
        vwt_ref[...] = jnp.zeros_like(vwt_ref)

    @pl.when(i > 0)
    def _():
        h = _norm_mod(x_ref[...], g_ref[...], sh_ref[...], sc_ref[...]).astype(BF16)
        cc_ref[...] = _dot(h, wcc_ref[...])
        kk = _dot(h, wk_ref[...])
        lane = lax.broadcasted_iota(jnp.int32, (tm, 128 - HD), 1)
        tail = jnp.where(lane < 2, 1.0, 0.0).astype(BF16)
        for kind, o_ref in enumerate((ks_ref, kw_ref)):
            for g in range(G):
                seg = kk[:, (kind * G + g) * HD:(kind * G + g + 1) * HD]
                var = jnp.mean(seg * seg, axis=-1, keepdims=True)
                segn = seg * lax.rsqrt(var + RMS_EPS) * gk_ref[kind:kind + 1, :]
                o_ref[:, g * 128:g * 128 + HD] = segn.astype(BF16)
                o_ref[:, g * 128 + HD:(g + 1) * 128] = tail
        vt = _dot_nt(wvt_ref[...], h)
        srow = lax.broadcasted_iota(jnp.int32, (VT_ROWS - HD, tm), 0)
        ones_blk = jnp.where(srow == 0, 1.0, 0.0).astype(BF16)
        for kind, o_ref in enumerate((vst_ref, vwt_ref)):
            for g in range(G):
                r0 = (kind * G + g) * HD
                o_ref[g * VT_ROWS:g * VT_ROWS + HD, :] = vt[r0:r0 + HD, :].astype(BF16)
                o_ref[g * VT_ROWS + HD:(g + 1) * VT_ROWS, :] = ones_blk


def kv_proj(x2d, g, sh, sc, wk, wvt, wcc, gk, batch, seq, tm=KV_PAD):
    assert tm == KV_PAD and seq % tm == 0
    M, D = x2d.shape
    G = NSA_GROUPS
    nt = seq // tm
    sp = seq + KV_PAD
    xrow = lambda b, i: (b * nt + jnp.maximum(i - 1, 0), 0)
    bvec = pl.BlockSpec((None, 1, D), lambda b, i: (b, 0, 0))
    full = lambda a: pl.BlockSpec(a.shape, lambda b, i: (0,) * a.ndim)
    return pl.pallas_call(
        _kv_proj_body,
        grid=(batch, nt + 1),
        in_specs=[pl.BlockSpec((tm, D), xrow),
                  pl.BlockSpec((1, D), lambda b, i: (0, 0)),
                  bvec, bvec, full(wk), full(wvt), full(wcc), full(gk)],
        out_specs=[pl.BlockSpec((None, tm, G * 128), lambda b, i: (b, i, 0)),
                   pl.BlockSpec((None, tm, G * 128), lambda b, i: (b, i, 0)),
                   pl.BlockSpec((None, G * VT_ROWS, tm), lambda b, i: (b, 0, i)),
                   pl.BlockSpec((None, G * VT_ROWS, tm), lambda b, i: (b, 0, i)),
                   pl.BlockSpec((tm, wcc.shape[1]), xrow)],
        out_shape=[jax.ShapeDtypeStruct((batch, sp, G * 128), BF16),
                   jax.ShapeDtypeStruct((batch, sp, G * 128), BF16),
                   jax.ShapeDtypeStruct((batch, G * VT_ROWS, sp), BF16),
                   jax.ShapeDtypeStruct((batch, G * VT_ROWS, sp), BF16),
                   jax.ShapeDtypeStruct((M, wcc.shape[1]), F32)],
        compiler_params=_cparams(("parallel", "arbitrary")),
        name="kv_proj",
    )(x2d, g.reshape(1, D), sh, sc, wk, wvt, wcc, gk)


def _gelu_tanh(x):
    c = math.sqrt(2.0 / math.pi)
    return 0.5 * x * (1.0 + jnp.tanh(c * (x + 0.044715 * (x * x * x))))


def _compress_body(cc0_ref, cc1_ref, cc2_ref, cc3_ref, pos_ref, w1_ref, w2_ref, w2t_ref, g_ref,
                   kc_ref, vct_ref, u_s, v_s):
    G, HD = NSA_GROUPS, NSA_HD
    nwin = u_s.shape[1]
    u_s[...] = jnp.zeros_like(u_s)
    v_s[...] = jnp.zeros_like(v_s)

    def step(i, carry):
        xi = jnp.concatenate([r[pl.ds(i, nwin, stride=CMP_STRIDE), :]
                              for r in (cc0_ref, cc1_ref, cc2_ref, cc3_ref)], axis=1)
        r0 = pl.multiple_of(i * HD, HD)
        r1 = pl.multiple_of((i + CMP_STRIDE) * HD, HD)
        for kind in range(2):
            p_lo = pos_ref[kind, pl.ds(i, 1), :]
            p_hi = pos_ref[kind, pl.ds(i + CMP_STRIDE, 1), :]
            w_lo = w1_ref[kind, pl.ds(r0, HD), :]
            w_hi = w1_ref[kind, pl.ds(r1, HD), :]
            for g in range(G):
                j = kind * G + g
                seg = xi[:, j * HD:(j + 1) * HD]
                u_s[j] += _dot((seg + p_lo).astype(BF16), w_lo)
                v_s[j] += _dot((seg + p_hi).astype(BF16), w_hi)
        return carry

    lax.fori_loop(0, CMP_STRIDE, step, 0)

    for kind in range(2):
        for g in range(G):
            j = kind * G + g
            pre = u_s[j] + pltpu.roll(v_s[j], nwin - 1, axis=0)
            hmid = _gelu_tanh(pre).astype(BF16)
            if kind == 0:
                y = _dot(hmid, w2_ref[...])
                var = jnp.mean(y * y, axis=-1, keepdims=True)
                yn = y * lax.rsqrt(var + RMS_EPS) * g_ref[...]
                kc_ref[:, g * 128:g * 128 + HD] = yn.astype(BF16)
                kc_ref[:, g * 128 + HD:(g + 1) * 128] = jnp.zeros((nwin, 128 - HD), BF16)
            else:
                vct_ref[g * HD:(g + 1) * HD, :] = _dot_nt(w2t_ref[...], hmid).astype(BF16)


def compress(cc, pos, w1, w2k, w2vt, g, batch, seq):
    G, HD = NSA_GROUPS, NSA_HD
    nwin = seq // CMP_STRIDE
    hid = w1.shape[2]
    full = lambda a: pl.BlockSpec(a.shape, lambda b: (0,) * a.ndim)
    return pl.pallas_call(
        _compress_body,
        grid=(batch,),
        in_specs=[pl.BlockSpec((seq, 128), functools.partial(lambda q, b: (b, q), q))
                  for q in range(4)]
                 + [full(pos), full(w1), full(w2k), full(w2vt), full(g)],
        out_specs=[pl.BlockSpec((None, nwin, G * 128), lambda b: (b, 0, 0)),
                   pl.BlockSpec((None, G * HD, nwin), lambda b: (b, 0, 0))],
        out_shape=[jax.ShapeDtypeStruct((batch, nwin, G * 128), BF16),
                   jax.ShapeDtypeStruct((batch, G * HD, nwin), BF16)],
        scratch_shapes=[pltpu.VMEM((2 * G, nwin, hid), F32), pltpu.VMEM((2 * G, nwin, hid), F32)],
        compiler_params=_cparams(("parallel",)),
        name="compress",
    )(cc, cc, cc, cc, pos, w1, w2k, w2vt, g)


def _t5_bucket(dist):
    n = jnp.maximum(dist, 0)
    max_exact = REL_BUCKETS // 2
    nf = jnp.maximum(n, 1).astype(F32)
    large = max_exact + (jnp.log(nf / max_exact) / math.log(REL_MAX_DIST / max_exact)
                         * (REL_BUCKETS - max_exact)).astype(jnp.int32)
    large = jnp.minimum(large, REL_BUCKETS - 1)
    return jnp.where(n < max_exact, n, large)


def _table_lookup(bucket, tab_ref, h):
    out = jnp.zeros(bucket.shape, F32)
    for k in range(REL_BUCKETS):
        out = jnp.where(bucket == k, tab_ref[k, h], out)
    return out


def _bias_prep_body(tab_ref, bc_ref, tp_ref, far_ref):
    h = pl.program_id(0)
    S = bc_ref.shape[1]
    cols = 128
    far = tab_ref[REL_BUCKETS - 1, h]
    far_ref[...] = jnp.full(far_ref.shape, far * LOG2E, F32)

    def chunk(r, carry):
        start = pl.multiple_of(r * cols, cols)
        n = lax.broadcasted_iota(jnp.int32, (NCMP_PAD, cols), 0)
        t = start + lax.broadcasted_iota(jnp.int32, (NCMP_PAD, cols), 1)
        dist_c = t - (n * CMP_STRIDE + CMP_BLOCK - 1)
        bias_c = _table_lookup(_t5_bucket(dist_c), tab_ref, h) * LOG2E
        bc_ref[:, pl.ds(start, cols)] = jnp.where(dist_c >= 0, bias_c, NEG_INF)
        return carry

    lax.fori_loop(0, S // cols, chunk, 0)
    j = lax.broadcasted_iota(jnp.int32, (ATT_TK, ATT_QB), 0)
    i = lax.broadcasted_iota(jnp.int32, (ATT_TK, ATT_QB), 1)
    for d in range(2):
        dist = d * ATT_TK + i - j
        rel = (_table_lookup(_t5_bucket(dist), tab_ref, h) - far) * LOG2E
        tp_ref[d] = jnp.where(dist >= 0, rel, NEG_INF)
    tp_ref[2] = jnp.where(i < j, 0.0, NEG_INF)


def bias_prep(rel_table, seq):
    assert ATT_TK == ATT_QB and ATT_TK + 1 > 113
    return pl.pallas_call(
        _bias_prep_body,
        grid=(NSA_HEADS,),
        in_specs=[pl.BlockSpec(memory_space=pltpu.SMEM)],
        out_specs=[pl.BlockSpec((None, NCMP_PAD, seq), lambda h: (h, 0, 0)),
                   pl.BlockSpec((None, 3, ATT_TK, ATT_QB), lambda h: (h, 0, 0, 0)),
                   pl.BlockSpec((None, 8, 128), lambda h: (h, 0, 0))],
        out_shape=[jax.ShapeDtypeStruct((NSA_HEADS, NCMP_PAD, seq), F32),
                   jax.ShapeDtypeStruct((NSA_HEADS, 3, ATT_TK, ATT_QB), F32),
                   jax.ShapeDtypeStruct((NSA_HEADS, 8, 128), F32)],
        compiler_params=_cparams(("parallel",)),
        name="bias_prep",
    )(rel_table)


def _q_proj_body(x_ref, g_ref, sh_ref, sc_ref, wqt_ref, wgt_ref, bg_ref, gq_ref, qt_ref, gt_ref):
    HD = NSA_HD
    h = _norm_mod(x_ref[...], g_ref[...], sh_ref[...], sc_ref[...]).astype(BF16)
    gt_ref[...] = _dot_nt(wgt_ref[...], h) + bg_ref[...]
    qt = _dot_nt(wqt_ref[...], h)
    scale = gq_ref[...] * (HD ** -0.5 * LOG2E)
    for hh in range(NSA_HEADS):
        seg = qt[hh * HD:(hh + 1) * HD, :]
        var = jnp.mean(seg * seg, axis=0, keepdims=True)
        qt_ref[hh * HD:(hh + 1) * HD, :] = (seg * lax.rsqrt(var + RMS_EPS) * scale).astype(BF16)


def q_proj(x2d, g, sh, sc, wqt, wgt, bg, gq, batch, seq, tm=512):
    M, D = x2d.shape
    nt = seq // tm
    nq = wqt.shape[0]
    ng = wgt.shape[0]
    bvec = pl.BlockSpec((None, 1, D), lambda i: (i // nt, 0, 0))
    full = lambda a: pl.BlockSpec(a.shape, lambda i: (0,) * a.ndim)
    return pl.pallas_call(
        _q_proj_body,
        grid=(M // tm,),
        in_specs=[pl.BlockSpec((tm, D), lambda i: (i, 0)),
                  pl.BlockSpec((1, D), lambda i: (0, 0)),
                  bvec, bvec, full(wqt), full(wgt), full(bg), full(gq)],
        out_specs=[pl.BlockSpec((None, nq, tm), lambda i: (i // nt, 0, i % nt)),
                   pl.BlockSpec((None, ng, tm), lambda i: (i // nt, 0, i % nt))],
        out_shape=[jax.ShapeDtypeStruct((batch, nq, seq), BF16),
                   jax.ShapeDtypeStruct((batch, ng, seq), F32)],
        compiler_params=_cparams(("parallel",)),
        name="q_proj",
    )(x2d, g.reshape(1, D), sh, sc, wqt, wgt, bg, gq)


def _heads_on_lanes(pieces):
    return jnp.concatenate(pieces, axis=1)


def _nsa_body(gps, qt_ref, gt_ref, kc_ref, vct_ref, ks_ref, kw_ref, vst_ref, vwt_ref,
              bct_ref, tp_ref, far_ref, ovt_ref, out_ref, msel_s):
    qb = pl.program_id(2)
    QB, HD, HPG = ATT_QB, NSA_HD, NSA_HPG
    R = HPG * QB
    t0 = pl.multiple_of(qb * QB, QB)
    near0 = pl.multiple_of(t0 + WINDOW - ATT_TK, ATT_TK)
    wlen = WINDOW + QB
    nblk = ovt_ref.shape[0]
    pad_blocks = KV_PAD // SLC_BLOCK
    pens = [jnp.where(qb >= d, 0.0, NEG_INF) for d in range(5)]
    jb = lax.broadcasted_iota(jnp.int32, (nblk, QB), 0)
    jbf = jb.astype(F32)
    qid = jnp.right_shift(t0 + lax.broadcasted_iota(jnp.int32, (nblk, QB), 1), 6)
    forced = (jb == 0) | (jb == qid) | (jb == qid - 1)
    srow = lax.broadcasted_iota(jnp.int32, (16, R), 0)

    def finish(acc):
        return acc[0:HD, :] / acc[HD:HD + 1, :]

    def block_mask(gl, first_row, nblocks):
        rows = msel_s[gl, pl.ds(first_row, nblocks), :]
        one_head = jnp.concatenate(
            [jnp.broadcast_to(rows[r:r + 1, :], (SLC_BLOCK, QB)) for r in range(nblocks)], axis=0)
        return _heads_on_lanes([one_head] * HPG)

    def tile_part(s_tile, v_tile):
        m_t = jnp.max(s_tile, axis=0, keepdims=True)
        m_safe = jnp.where(m_t == NEG_INF, 0.0, m_t)
        p = jnp.exp2(s_tile - m_safe).astype(BF16)
        return m_t, _dot(v_tile, p)[0:HD + 8, :]

    def combine(parts):
        m_fin = parts[0][0]
        for m_t, _ in parts[1:]:
            m_fin = jnp.maximum(m_fin, m_t)
        acc = None
        for m_t, pv in parts:
            term = jnp.exp2(m_t - m_fin) * pv
            acc = term if acc is None else acc + term
        return m_fin, acc

    groups = range(gps)
    hs = [[gl * HPG + h for h in range(HPG)] for gl in groups]
    kcol = [slice(gl * 128, (gl + 1) * 128) for gl in groups]
    vrow = [slice(gl * VT_ROWS, (gl + 1) * VT_ROWS) for gl in groups]

    qms = []
    for gl in groups:
        q_all = _heads_on_lanes([qt_ref[h * HD:(h + 1) * HD, :] for h in hs[gl]])
        far = _heads_on_lanes([far_ref[h, 0:1, :] for h in hs[gl]])
        hi = far.astype(BF16).astype(F32)
        extra = jnp.where(srow == 0, hi, jnp.where(srow == 1, far - hi, 0.0)).astype(BF16)
        qms.append(jnp.concatenate([q_all, extra, jnp.zeros((128 - HD - 16, R), BF16)], axis=0))

    s_cs = [_dot(kc_ref[:, kcol[gl]], qms[gl]) for gl in groups]
    s_ws = [_dot(kw_ref[pl.ds(t0, wlen), kcol[gl]], qms[gl]) for gl in groups]
    s_ns = [_dot(ks_ref[pl.ds(near0, 2 * ATT_TK), kcol[gl]], qms[gl]) for gl in groups]

    o_cmps, scores = [], []
    for gl in groups:
        s_c = s_cs[gl] + _heads_on_lanes([bct_ref[h] for h in hs[gl]])
        m_c = jnp.max(s_c, axis=0, keepdims=True)
        m_c = jnp.where(m_c == NEG_INF, 0.0, m_c)
        e_c = jnp.exp2(s_c - m_c)
        p_c = e_c / jnp.maximum(jnp.sum(e_c, axis=0, keepdims=True), jnp.finfo(F32).tiny)
        o_cmps.append(_dot(vct_ref[gl * HD:(gl + 1) * HD, :], p_c.astype(BF16)))
        p_sum = p_c[:, 0:QB]
        for h in range(1, HPG):
            p_sum = p_sum + p_c[:, h * QB:(h + 1) * QB]
        p_hi = p_sum.astype(BF16)
        p_lo = (p_sum - p_hi.astype(F32)).astype(BF16)
        imp = _dot(ovt_ref[...], jnp.concatenate([p_hi, p_lo], axis=0))
        score = jnp.where(forced, FORCE_SCORE, imp)
        scores.append(jnp.where(jb <= qid, score, NEG_INF))

    def win_bias(h, d):
        if d == 4:
            return tp_ref[h, 2] + pens[4]
        if d in (2, 3):
            return jnp.full((ATT_TK, QB), pens[d], F32)
        return tp_ref[h, 1] + pens[1] if d == 1 else tp_ref[h, 0]

    nwt = wlen // ATT_TK
    o_wins = []
    for gl in groups:
        parts = []
        for i in range(nwt):
            s_t = s_ws[gl][i * ATT_TK:(i + 1) * ATT_TK] + _heads_on_lanes(
                [win_bias(h, nwt - 1 - i) for h in hs[gl]])
            parts.append(tile_part(s_t, vwt_ref[vrow[gl], pl.ds(t0 + i * ATT_TK, ATT_TK)]))
        o_wins.append(finish(combine(parts)[1]))

    msels = [jnp.full((nblk, QB), NEG_INF, F32) for _ in groups]
    for _ in range(SLC_TOPK):
        for gl in groups:
            mx = jnp.max(scores[gl], axis=0, keepdims=True)
            first = jnp.min(jnp.where(scores[gl] == mx, jbf, float(nblk)), axis=0, keepdims=True)
            pick = jbf == first
            msels[gl] = jnp.where(pick & (mx > NEG_INF), 0.0, msels[gl])
            scores[gl] = jnp.where(pick, NEG_INF, scores[gl])
    for gl in groups:
        msel_s[gl, 0:pad_blocks, :] = jnp.full((pad_blocks, QB), NEG_INF, F32)
        msel_s[gl, pad_blocks:pad_blocks + nblk, :] = msels[gl]
        msel_s[gl, pad_blocks + nblk:, :] = jnp.full(
            (msel_s.shape[1] - pad_blocks - nblk, QB), NEG_INF, F32)

    carry0 = []
    for gl in groups:
        prev_mask = jnp.concatenate(
            [jnp.broadcast_to(jnp.max(jnp.where(jb == 2 * qb - 2 + r, msels[gl], NEG_INF), axis=0,
                                      keepdims=True), (SLC_BLOCK, QB)) for r in range(2)], axis=0)
        s_prev = s_ns[gl][0:ATT_TK] + _heads_on_lanes([tp_ref[h, 1] + prev_mask for h in hs[gl]])
        s_diag = s_ns[gl][ATT_TK:] + _heads_on_lanes([tp_ref[h, 0] for h in hs[gl]])
        carry0.append(combine([
            tile_part(s_diag, vst_ref[vrow[gl], pl.ds(near0 + ATT_TK, ATT_TK)]),
            tile_part(s_prev, vst_ref[vrow[gl], pl.ds(near0, ATT_TK)])]))

    def far_step(c, carry):
        tile0 = qb + (WINDOW - ATT_TK) // ATT_TK - (c + 1) * (FAR_CHUNK // ATT_TK)
        row0 = pl.multiple_of(tile0 * ATT_TK, ATT_TK)
        blk0 = tile0 * (ATT_TK // SLC_BLOCK)
        s_fs = [_dot(ks_ref[pl.ds(row0, FAR_CHUNK), kcol[gl]], qms[gl]) for gl in groups]
        out = []
        for gl in groups:
            parts = [carry[gl]]
            for j in range(FAR_CHUNK // ATT_TK):
                s_t = s_fs[gl][j * ATT_TK:(j + 1) * ATT_TK] + block_mask(
                    gl, blk0 + j * (ATT_TK // SLC_BLOCK), ATT_TK // SLC_BLOCK)
                parts.append(tile_part(
                    s_t, vst_ref[vrow[gl], pl.ds(row0 + j * ATT_TK, ATT_TK)]))
            out.append(combine(parts))
        return tuple(out)

    n_far = (qb + 2) // 4
    sel = lax.fori_loop(0, n_far, far_step, tuple(carry0))

    gates = jax.nn.sigmoid(gt_ref[...])
    for gl in groups:
        o_cmp, o_win = o_cmps[gl], o_wins[gl]
        o_sel = finish(sel[gl][1])
        outs = []
        for h in range(HPG):
            lanes = slice(h * QB, (h + 1) * QB)
            r = gl * 16 + 3 * h
            outs.append(gates[r:r + 1, :] * o_cmp[:, lanes]
                        + gates[r + 1:r + 2, :] * o_sel[:, lanes]
                        + gates[r + 2:r + 3, :] * o_win[:, lanes])
        for pair in range(HPG // 2):
            two = jnp.concatenate(outs[2 * pair:2 * pair + 2], axis=0)
            c0 = gl * HPG * HD + pair * 2 * HD
            out_ref[:, c0:c0 + 2 * HD] = two.T.astype(out_ref.dtype)


def _overlap_matrix_t(nblk):
    start = np.arange(NCMP_PAD) * CMP_STRIDE
    sj = np.arange(nblk) * SLC_BLOCK
    ov = (np.minimum(start[None, :] + CMP_BLOCK, sj[:, None] + SLC_BLOCK)
          - np.maximum(start[None, :], sj[:, None]))
    ov = np.clip(ov, 0, None) / CMP_BLOCK
    ov[:, NCMP_PAD - 1] = 0.0
    return np.concatenate([ov, ov], axis=1).astype(np.float32)


def nsa_attn(qt, gt, kc, vct, ks, kw, vst, vwt, bias_ct, tiles, far, batch, seq, gps=ATT_GPS):
    QB, HD, HPG, G = ATT_QB, NSA_HD, NSA_HPG, NSA_GROUPS
    nblk = seq // SLC_BLOCK
    assert nblk % 8 == 0 and seq % QB == 0 and seq // CMP_STRIDE == NCMP_PAD and G % gps == 0
    assert WINDOW % ATT_TK == 0 and FAR_CHUNK == 4 * ATT_TK and KV_PAD >= FAR_CHUNK - ATT_TK
    assert SEL_ROWS >= KV_PAD // SLC_BLOCK + nblk
    nq = seq // QB
    sp = seq + KV_PAD
    ovt = jnp.asarray(_overlap_matrix_t(nblk), dtype=BF16)
    return pl.pallas_call(
        functools.partial(_nsa_body, gps),
        grid=(batch, G // gps, nq),
        in_specs=[pl.BlockSpec((None, gps * HPG * HD, QB), lambda b, g, i: (b, g, i)),
                  pl.BlockSpec((None, gps * 16, QB), lambda b, g, i: (b, g, i)),
                  pl.BlockSpec((None, NCMP_PAD, gps * 128), lambda b, g, i: (b, 0, g)),
                  pl.BlockSpec((None, gps * HD, NCMP_PAD), lambda b, g, i: (b, g, 0)),
                  pl.BlockSpec((None, sp, gps * 128), lambda b, g, i: (b, 0, g)),
                  pl.BlockSpec((None, sp, gps * 128), lambda b, g, i: (b, 0, g)),
                  pl.BlockSpec((None, gps * VT_ROWS, sp), lambda b, g, i: (b, g, 0)),
                  pl.BlockSpec((None, gps * VT_ROWS, sp), lambda b, g, i: (b, g, 0)),
                  pl.BlockSpec((gps * HPG, NCMP_PAD, QB), lambda b, g, i: (g, 0, i)),
                  pl.BlockSpec((gps * HPG, 3, ATT_TK, QB), lambda b, g, i: (g, 0, 0, 0)),
                  pl.BlockSpec((gps * HPG, 8, 128), lambda b, g, i: (g, 0, 0)),
                  pl.BlockSpec((nblk, 2 * NCMP_PAD), lambda b, g, i: (0, 0))],
        out_specs=pl.BlockSpec((QB, gps * HPG * HD), lambda b, g, i: (b * nq + i, g)),
        out_shape=jax.ShapeDtypeStruct((batch * seq, NSA_HEADS * HD), BF16),
        scratch_shapes=[pltpu.VMEM((gps, SEL_ROWS, QB), F32)],
        compiler_params=_cparams(("parallel", "parallel", "arbitrary")),
        name="nsa_attn",
    )(qt, gt, kc, vct, ks, kw, vst, vwt, bias_ct, tiles, far, ovt)


def _mlstm_layer(x2d, g_mix, sh, sc, w_in, b_if, g_out, w_out, batch, seq):
    nbig = 2 * ML_HEADS * ML_DQK + 2 * ML_HEADS * ML_DV
    ng = 2 * ML_HEADS
    wg = jnp.pad(w_in[:, nbig:], ((0, 0), (0, 128 - ng))).astype(BF16)
    bg = jnp.pad(b_if, (0, 128 - ng))
    proj, gates = norm_proj(x2d, g_mix, sh, sc, [w_in[:, :nbig].astype(BF16), wg], [None, bg],
                            [BF16, F32])
    return mlstm_core(proj, gates, g_out, batch, seq), w_out.astype(BF16)


def _nsa_shared(x2d, g_kv, kv_sh, kv_sc, w_kv, pos_k, w_k1, w_k2, pos_v, w_v1, w_v2,
                g_knorm, batch, seq):
    gw = NSA_GROUPS * NSA_HD
    part = lambda i: w_kv[:, i * gw:(i + 1) * gw]
    wk = jnp.concatenate([part(2), part(4)], axis=1).astype(BF16)
    wvt = jnp.concatenate([part(3), part(5)], axis=1).T.astype(BF16)
    wcc = jnp.concatenate([part(0), part(1)], axis=1).astype(BF16)
    ks, kw, vst, vwt, cc = kv_proj(x2d, g_kv, kv_sh, kv_sc, wk, wvt, wcc, g_knorm[1:3],
                                   batch, seq)
    pos = jnp.stack([pos_k, pos_v])
    w1 = jnp.stack([w_k1, w_v1]).astype(BF16)
    kc, vct = compress(cc, pos, w1, w_k2.astype(BF16), w_v2.T.astype(BF16), g_knorm[0:1],
                       batch, seq)
    return kc, vct, ks, kw, vst, vwt


def _gate_weights_t(w_q, b_gate):
    nq = NSA_HEADS * NSA_HD
    per = 3 * NSA_HPG
    wg = w_q[:, nq:].T.reshape(NSA_GROUPS, per, -1)
    wg = jnp.pad(wg, ((0, 0), (0, 16 - per), (0, 0))).reshape(NSA_GROUPS * 16, -1)
    bg = jnp.pad(b_gate.reshape(NSA_GROUPS, per), ((0, 0), (0, 16 - per))).reshape(-1, 1)
    return wg.astype(BF16), bg


def _nsa_layer(x2d, g_mix, sh, sc, shared, w_q, b_gate, g_qnorm, w_out, bias_ct, tiles, far,
               batch, seq):
    nq = NSA_HEADS * NSA_HD
    wgt, bg = _gate_weights_t(w_q, b_gate)
    qt, gt = q_proj(x2d, g_mix, sh, sc, w_q[:, :nq].T.astype(BF16), wgt, bg,
                    g_qnorm.reshape(NSA_HD, 1), batch, seq)
    kc, vct, ks, kw, vst, vwt = shared
    att = nsa_attn(qt, gt, kc, vct, ks, kw, vst, vwt, bias_ct, tiles, far, batch, seq)
    return att, w_out.astype(BF16)


def kernel(x, c, w_ada, b_ada, g_norm_mix, g_norm_ffn, w_ffn_in, w_ffn_out, w_a_in, b_a_if, g_a_out, w_a_out, w_kv_ada, b_kv_ada, g_kv_norm, w_kv, pos_cmp_k, w_cmp_k1, w_cmp_k2, pos_cmp_v, w_cmp_v1, w_cmp_v2, g_knorm, w_b_q, b_b_gate, g_qnorm, w_b_out, rel_table):
    B, S, D = x.shape
    depth = w_ada.shape[0]
    n_a = w_a_in.shape[0]
    x2d = x.reshape(B * S, D)
    mods = ada_mod(c, w_ada, b_ada)
    kv_mod = ada_mod(c, w_kv_ada[None], b_kv_ada[None])[0]
    shared = None
    bias_ct = tiles = far = None
    for layer in range(depth):
        sh1, sc1, ga1, sh2, sc2, ga2 = [mods[layer, :, i * D:(i + 1) * D].reshape(B, 1, D)
                                        for i in range(6)]
        if layer < n_a:
            mixed, w_mix = _mlstm_layer(x2d, g_norm_mix[layer], sh1, sc1, w_a_in[layer],
                                        b_a_if[layer], g_a_out[layer], w_a_out[layer], B, S)
        else:
            j = layer - n_a
            if shared is None:
                kv_sh = kv_mod[:, :D].reshape(B, 1, D)
                kv_sc = kv_mod[:, D:].reshape(B, 1, D)
                shared = _nsa_shared(x2d, g_kv_norm, kv_sh, kv_sc, w_kv, pos_cmp_k, w_cmp_k1,
                                     w_cmp_k2, pos_cmp_v, w_cmp_v1, w_cmp_v2, g_knorm, B, S)
                bias_ct, tiles, far = bias_prep(rel_table, S)
            mixed, w_mix = _nsa_layer(x2d, g_norm_mix[layer], sh1, sc1, shared, w_b_q[j],
                                      b_b_gate[j], g_qnorm[j], w_b_out[j], bias_ct, tiles, far, B, S)
        x2d = mix_ffn(mixed, w_mix, x2d, ga1, g_norm_ffn[layer], sh2, sc2, ga2,
                      w_ffn_in[layer].astype(BF16), w_ffn_out[layer].astype(BF16))
    return x2d.reshape(B, S, D)
```

```python
import functools
import math

import jax
import jax.numpy as jnp
import numpy as np
from jax import lax
from jax.experimental import pallas as pl
from jax.experimental.pallas import tpu as pltpu

F32 = jnp.float32
BF16 = jnp.bfloat16
NEG_INF = float("-inf")
LOG2E = math.log2(math.e)

RMS_EPS = 1e-6

ML_HEADS = 4
ML_DQK = 128
ML_DV = 256
ML_LC = 256

NSA_HEADS = 16
NSA_GROUPS = 4
NSA_HPG = 4
NSA_HD = 64
CMP_BLOCK = 32
CMP_STRIDE = 16
SLC_BLOCK = 64
SLC_TOPK = 8
WINDOW = 512
FORCE_SCORE = 1e4
REL_BUCKETS = 32
REL_MAX_DIST = 128
ATT_QB = 128
ATT_TK = 128
ATT_GPS = 4
NCMP_PAD = 128
KV_PAD = WINDOW
FAR_CHUNK = 512
VT_ROWS = 80
KEY_BLK = 8
KEY_PAD = 40
MASK_BIG = -32768.0

VMEM_LIMIT = 48 * 1024 * 1024


def _cparams(sem):
    return pltpu.CompilerParams(dimension_semantics=sem, vmem_limit_bytes=VMEM_LIMIT)


def _dot(a, b):
    return jnp.dot(a, b, preferred_element_type=F32)


def _dot_nt(a, b):
    return lax.dot_general(a, b, (((1,), (1,)), ((), ())), preferred_element_type=F32)


def _dot_tn(a, b):
    return lax.dot_general(a, b, (((0,), (0,)), ((), ())), preferred_element_type=F32)


def _norm_mod(x, g, sh, sc):
    var = jnp.mean(x * x, axis=-1, keepdims=True)
    y = x * lax.rsqrt(var + RMS_EPS) * g
    return y * (1.0 + sc) + sh


def _ada_body(c_ref, w_ref, b_ref, o_ref):
    c = c_ref[...]
    ca = c * jax.nn.sigmoid(c)
    nb = ca.shape[0]
    a_hi = ca.astype(BF16).astype(F32)
    a_mid = (ca - a_hi).astype(BF16).astype(F32)
    a_lo = ca - a_hi - a_mid
    a3 = jnp.concatenate([a_hi, a_mid, a_lo, jnp.zeros_like(ca)], axis=0).astype(BF16)
    w = w_ref[...]
    w_hi = w.astype(BF16)
    w_lo = (w - w_hi.astype(F32)).astype(BF16)
    p = _dot(a3, w_hi)
    q = _dot(a3[0:2 * nb], w_lo)
    o_ref[...] = (p[0:nb] + p[nb:2 * nb] + p[2 * nb:3 * nb] + q[0:nb] + q[nb:2 * nb]) + b_ref[...]


def ada_mod(c, w, b, tn=1024):
    L, D, N = w.shape
    B = c.shape[0]
    return pl.pallas_call(
        _ada_body,
        grid=(L, N // tn),
        in_specs=[pl.BlockSpec((B, D), lambda l, j: (0, 0)),
                  pl.BlockSpec((None, D, tn), lambda l, j: (l, 0, j)),
                  pl.BlockSpec((None, 1, tn), lambda l, j: (l, 0, j))],
        out_specs=pl.BlockSpec((None, B, tn), lambda l, j: (l, 0, j)),
        out_shape=jax.ShapeDtypeStruct((L, B, N), F32),
        compiler_params=_cparams(("parallel", "parallel")),
        name="ada_mod",
    )(c, w, b.reshape(L, 1, N))


def _norm_proj_body(n_out, has_bias, tn, x_ref, g_ref, sh_ref, sc_ref, *refs):
    w_refs = refs[:n_out]
    b_refs = refs[n_out:2 * n_out]
    o_refs = refs[2 * n_out:3 * n_out]
    h = _norm_mod(x_ref[...], g_ref[...], sh_ref[...], sc_ref[...]).astype(BF16)
    for w_ref, b_ref, o_ref, hb in zip(w_refs, b_refs, o_refs, has_bias):
        n = w_ref.shape[1]
        step = min(tn, n)
        for n0 in range(0, n, step):
            acc = _dot(h, w_ref[:, n0:n0 + step])
            if hb:
                acc = acc + b_ref[:, n0:n0 + step]
            o_ref[:, n0:n0 + step] = acc.astype(o_ref.dtype)


def norm_proj(x2d, g, sh, sc, ws, biases, out_dtypes, tm=512, tn=512):
    M, D = x2d.shape
    B = sh.shape[0]
    tiles_per_batch = (M // B) // tm
    n_out = len(ws)
    has_bias = tuple(b is not None for b in biases)
    bias_args = [(b if b is not None else jnp.zeros((w.shape[1],), F32)).reshape(1, -1)
                 for b, w in zip(biases, ws)]
    in_specs = [pl.BlockSpec((tm, D), lambda i: (i, 0)),
                pl.BlockSpec((1, D), lambda i: (0, 0)),
                pl.BlockSpec((None, 1, D), lambda i: (i // tiles_per_batch, 0, 0)),
                pl.BlockSpec((None, 1, D), lambda i: (i // tiles_per_batch, 0, 0))]
    in_specs += [pl.BlockSpec(w.shape, lambda i: (0, 0)) for w in ws]
    in_specs += [pl.BlockSpec(b.shape, lambda i: (0, 0)) for b in bias_args]
    out_specs = [pl.BlockSpec((tm, w.shape[1]), lambda i: (i, 0)) for w in ws]
    out_shape = [jax.ShapeDtypeStruct((M, w.shape[1]), dt) for w, dt in zip(ws, out_dtypes)]
    return pl.pallas_call(
        functools.partial(_norm_proj_body, n_out, has_bias, tn),
        grid=(M // tm,),
        in_specs=in_specs, out_specs=out_specs, out_shape=out_shape,
        compiler_params=_cparams(("parallel",)),
        name="norm_proj",
    )(x2d, g.reshape(1, D), sh, sc, *ws, *bias_args)


def _mix_ffn_body(tf, a_ref, wm_ref, x_ref, ga1_ref, g_ref, sh_ref, sc_ref, ga2_ref,
                  wi_ref, wo_ref, o_ref, act_s):
    F = wo_ref.shape[0]
    x1 = x_ref[...] + ga1_ref[...] * _dot(a_ref[...], wm_ref[...])
    h = _norm_mod(x1, g_ref[...], sh_ref[...], sc_ref[...]).astype(BF16)
    for f0 in range(0, F, tf):
        gate = _dot(h, wi_ref[:, f0:f0 + tf])
        up = _dot(h, wi_ref[:, F + f0:F + f0 + tf])
        act_s[:, f0:f0 + tf] = (gate * jax.nn.sigmoid(gate) * up).astype(BF16)
    o_ref[...] = x1 + ga2_ref[...] * _dot(act_s[...], wo_ref[...])


def mix_ffn(a, w_mix, x2d, ga1, g, sh, sc, ga2, w_in, w_out, tm=512, tf=256):
    M, D = x2d.shape
    K = a.shape[1]
    F = w_out.shape[0]
    B = sh.shape[0]
    tiles_per_batch = (M // B) // tm
    bvec = pl.BlockSpec((None, 1, D), lambda i: (i // tiles_per_batch, 0, 0))
    resident = lambda w: pl.BlockSpec(w.shape, lambda i: (0, 0), pipeline_mode=pl.Buffered(1))
    return pl.pallas_call(
        functools.partial(_mix_ffn_body, tf),
        grid=(M // tm,),
        in_specs=[pl.BlockSpec((tm, K), lambda i: (i, 0)),
                  resident(w_mix),
                  pl.BlockSpec((tm, D), lambda i: (i, 0)),
                  bvec,
                  pl.BlockSpec((1, D), lambda i: (0, 0)),
                  bvec, bvec, bvec,
                  resident(w_in), resident(w_out)],
        out_specs=pl.BlockSpec((tm, D), lambda i: (i, 0)),
        out_shape=jax.ShapeDtypeStruct((M, D), F32),
        scratch_shapes=[pltpu.VMEM((tm, F), BF16)],
        compiler_params=_cparams(("parallel",)),
        name="mix_ffn",
    )(a, w_mix, x2d, ga1, g.reshape(1, D), sh, sc, ga2, w_in, w_out)


def _sublane_scan(x, op, fill):
    n = x.shape[0]
    row = lax.broadcasted_iota(jnp.int32, x.shape, 0)
    sh = 1
    while sh < n:
        x = op(x, jnp.where(row >= sh, pltpu.roll(x, sh, axis=0), fill))
        sh *= 2
    return x


def _mlstm_body(q_ref, k_ref, v_ref, o_ref, gc_ref, gout_ref, out_ref, c_s, n_s, m_s):
    c_idx = pl.program_id(1)
    LC = q_ref.shape[0]
    NH = ML_HEADS
    scale = ML_DQK ** -0.5

    @pl.when(c_idx == 0)
    def _():
        c_s[...] = jnp.zeros_like(c_s)
        n_s[...] = jnp.zeros_like(n_s)
        m_s[...] = jnp.zeros_like(m_s)

    gc = gc_ref[...]
    logf = jnp.minimum(gc, 0.0) - jnp.log1p(jnp.exp(-jnp.abs(gc)))
    bcum = pltpu.roll(_sublane_scan(logf, jnp.add, 0.0), 128 - NH, axis=1)
    a_c = gc - bcum
    cmax = _sublane_scan(a_c, jnp.maximum, NEG_INF)
    a_t = a_c.T

    row = lax.broadcasted_iota(jnp.int32, (LC, LC), 0)
    col = lax.broadcasted_iota(jnp.int32, (LC, LC), 1)
    causal = col <= row
    ones_col = jnp.ones((LC, 128), BF16)

    def lanes2(x):
        return jnp.concatenate([x, x], axis=1)

    for h in range(NH):
        a_rep = jnp.broadcast_to(a_c[:, h:h + 1], (LC, 128))
        b_rep = jnp.broadcast_to(bcum[:, h:h + 1], (LC, 128))
        cm_rep = jnp.broadcast_to(cmax[:, h:h + 1], (LC, 128))
        a_r = a_t[h:h + 1, :]
        b_last = b_rep[LC - 1:LC, :]
        m_prev = m_s[h:h + 1, :]
        m_new = jnp.maximum(b_last + m_prev, b_last + cm_rep[LC - 1:LC, :])
        decay = jnp.exp(b_last + m_prev - m_new)
        e_rep = jnp.exp(b_last + a_rep - m_new)
        g_rep = jnp.maximum(m_prev, cm_rep)
        w_intra = jnp.exp(jnp.where(causal, a_r - lanes2(g_rep), NEG_INF))
        w_inter = jnp.exp(m_prev - g_rep)
        floor = jnp.exp(-(b_rep + g_rep))

        qh = q_ref[:, h * ML_DQK:(h + 1) * ML_DQK]
        kh = k_ref[:, h * ML_DQK:(h + 1) * ML_DQK]
        vh = v_ref[:, h * ML_DV:(h + 1) * ML_DV]
        c_prev = c_s[h]
        n_prev = n_s[h:h + 1, :]

        s = (_dot_nt(qh, kh) * scale * w_intra).astype(BF16)
        inter = _dot(qh, c_prev.astype(BF16)) * scale
        num = _dot(s, vh) + lanes2(w_inter) * inter
        n_rows = jnp.broadcast_to(n_prev, (128, ML_DQK)).astype(BF16)
        qn = _dot(s, ones_col) + w_inter * (_dot_nt(qh, n_rows) * scale)
        denom = jnp.maximum(jnp.abs(qn), floor)
        hout = num / lanes2(denom)

        ke = kh.astype(F32) * e_rep
        c_s[h] = lanes2(decay) * c_prev + _dot_tn(ke.astype(BF16), vh)
        n_s[h:h + 1, :] = decay * n_prev + jnp.sum(ke, axis=0, keepdims=True)
        m_s[h:h + 1, :] = m_new

        ssq = _dot((hout * hout).astype(BF16), jnp.ones((ML_DV, 128), BF16))
        rs = lax.rsqrt(ssq * (1.0 / ML_DV) + RMS_EPS)
        hn = hout * lanes2(rs) * gout_ref[:, h * ML_DV:(h + 1) * ML_DV]
        og = jax.nn.sigmoid(o_ref[:, h * ML_DV:(h + 1) * ML_DV].astype(F32))
        out_ref[:, h * ML_DV:(h + 1) * ML_DV] = (hn * og).astype(out_ref.dtype)


def mlstm_core(proj, gates, g_out, batch, seq):
    LC = ML_LC
    nc = seq // LC
    qk = ML_HEADS * ML_DQK
    vd = ML_HEADS * ML_DV
    row = lambda b, c: b * nc + c
    return pl.pallas_call(
        _mlstm_body,
        grid=(batch, nc),
        in_specs=[pl.BlockSpec((LC, qk), lambda b, c: (row(b, c), 0)),
                  pl.BlockSpec((LC, qk), lambda b, c: (row(b, c), 1)),
                  pl.BlockSpec((LC, vd), lambda b, c: (row(b, c), 1)),
                  pl.BlockSpec((LC, vd), lambda b, c: (row(b, c), 2)),
                  pl.BlockSpec((LC, 128), lambda b, c: (row(b, c), 0)),
                  pl.BlockSpec((1, vd), lambda b, c: (0, 0))],
        out_specs=pl.BlockSpec((LC, vd), lambda b, c: (row(b, c), 0)),
        out_shape=jax.ShapeDtypeStruct((batch * seq, vd), BF16),
        scratch_shapes=[pltpu.VMEM((ML_HEADS, ML_DQK, ML_DV), F32),
                        pltpu.VMEM((8, ML_DQK), F32),
                        pltpu.VMEM((8, 128), F32)],
        compiler_params=_cparams(("parallel", "arbitrary")),
        name="mlstm_core",
    )(proj, proj, proj, proj, gates, g_out.reshape(1, vd))


def _kv_proj_body(x_ref, g_ref, sh_ref, sc_ref, wk_ref, wvt_ref, wcc_ref, gk_ref, bd_ref,
                  ks_ref, kw_ref, vst_ref, vwt_ref, cc_ref):
    i = pl.program_id(1)
    G, HD = NSA_GROUPS, NSA_HD
    tm = x_ref.shape[0]
    lane = lax.broadcasted_iota(jnp.int32, (tm, G * 128), 1) % 128

    @pl.when(i == 0)
    def _():
        pad_rows = jnp.where(lane == HD + KEY_PAD, 1.0, 0.0).astype(BF16)
        ks_ref[...] = pad_rows
        kw_ref[...] = pad_rows
        vst_ref[...] = jnp.zeros_like(vst_ref)
        vwt_ref[...] = jnp.zeros_like(vwt_ref)

    @pl.when(i > 0)
    def _():
        h = _norm_mod(x_ref[...], g_ref[...], sh_ref[...], sc_ref[...]).astype(BF16)
        cc = _dot(h, wcc_ref[...])
        for q in range(cc_ref.shape[0]):
            cc_ref[q] = cc[:, q * 128:(q + 1) * 128]
        kk = _dot(h, wk_ref[...])
        ssq = _dot((kk * kk).astype(BF16), bd_ref[...])
        kn = kk * lax.rsqrt(ssq * (1.0 / HD) + RMS_EPS) * gk_ref[...]
        blk = (i - 1) * (tm // SLC_BLOCK) + lax.broadcasted_iota(
            jnp.int32, (tm, G * 128), 0) // SLC_BLOCK
        tail = jnp.where((lane == HD) | (lane == HD + 1) | (lane == HD + KEY_BLK + blk), 1.0, 0.0)
        ks_ref[...] = jnp.where(lane < HD, kn, tail).astype(BF16)
        kn_sw = jnp.concatenate(
            [pltpu.roll(kn[:, g * 128:(g + 1) * 128], HD, axis=1) for g in range(G)], axis=1)
        kw_ref[...] = jnp.where(lane < HD, kn_sw, tail).astype(BF16)
        vt = _dot_nt(wvt_ref[...], h)
        srow = lax.broadcasted_iota(jnp.int32, (VT_ROWS - HD, tm), 0)
        ones_blk = jnp.where(srow == 0, 1.0, 0.0).astype(BF16)
        for kind, o_ref in enumerate((vst_ref, vwt_ref)):
            for g in range(G):
                r0 = (kind * G + g) * HD
                o_ref[g * VT_ROWS:g * VT_ROWS + HD, :] = vt[r0:r0 + HD, :].astype(BF16)
                o_ref[g * VT_ROWS + HD:(g + 1) * VT_ROWS, :] = ones_blk


def _segment_ones(width, seg):
    idx = np.arange(width) // seg
    return (idx[:, None] == idx[None, :]).astype(np.float32)


def kv_proj(x2d, g, sh, sc, wk, wvt, wcc, gk, batch, seq, tm=KV_PAD):
    assert tm == KV_PAD and seq % tm == 0
    M, D = x2d.shape
    G = NSA_GROUPS
    nt = seq // tm
    sp = seq + KV_PAD
    ncs = wcc.shape[1] // 128
    bd = jnp.asarray(_segment_ones(G * 128, NSA_HD), dtype=BF16)
    xrow = lambda b, i: (b * nt + jnp.maximum(i - 1, 0), 0)
    bvec = pl.BlockSpec((None, 1, D), lambda b, i: (b, 0, 0))
    full = lambda a: pl.BlockSpec(a.shape, lambda b, i: (0,) * a.ndim)
    return pl.pallas_call(
        _kv_proj_body,
        grid=(batch, nt + 1),
        in_specs=[pl.BlockSpec((tm, D), xrow),
                  pl.BlockSpec((1, D), lambda b, i: (0, 0)),
                  bvec, bvec, full(wk), full(wvt), full(wcc), full(gk), full(bd)],
        out_specs=[pl.BlockSpec((None, tm, G * 128), lambda b, i: (b, i, 0)),
                   pl.BlockSpec((None, tm, G * 128), lambda b, i: (b, i, 0)),
                   pl.BlockSpec((None, G * VT_ROWS, tm), lambda b, i: (b, 0, i)),
                   pl.BlockSpec((None, G * VT_ROWS, tm), lambda b, i: (b, 0, i)),
                   pl.BlockSpec((ncs, tm, 128),
                                lambda b, i: (0, b * nt + jnp.maximum(i - 1, 0), 0))],
        out_shape=[jax.ShapeDtypeStruct((batch, sp, G * 128), BF16),
                   jax.ShapeDtypeStruct((batch, sp, G * 128), BF16),
                   jax.ShapeDtypeStruct((batch, G * VT_ROWS, sp), BF16),
                   jax.ShapeDtypeStruct((batch, G * VT_ROWS, sp), BF16),
                   jax.ShapeDtypeStruct((ncs, M, 128), F32)],
        compiler_params=_cparams(("parallel", "arbitrary")),
        name="kv_proj",
    )(x2d, g.reshape(1, D), sh, sc, wk, wvt, wcc, gk, bd)


def _gelu_tanh(x):
    c = math.sqrt(2.0 / math.pi)
    return 0.5 * x * (1.0 + jnp.tanh(c * (x + 0.044715 * (x * x * x))))


def _compress_body(cc0_ref, cc1_ref, cc2_ref, cc3_ref, pos_ref, w1_ref, w2_ref, w2t_ref, g_ref,
                   kc_ref, vct_ref, u_s, v_s):
    G, HD = NSA_GROUPS, NSA_HD
    nwin = u_s.shape[1]
    u_s[...] = jnp.zeros_like(u_s)
    v_s[...] = jnp.zeros_like(v_s)

    def step(i, carry):
        xi = jnp.concatenate([r[pl.ds(i, nwin, stride=CMP_STRIDE), :]
                              for r in (cc0_ref, cc1_ref, cc2_ref, cc3_ref)], axis=1)
        r0 = pl.multiple_of(i * HD, HD)
        r1 = pl.multiple_of((i + CMP_STRIDE) * HD, HD)
        for kind in range(2):
            p_lo = pos_ref[kind, pl.ds(i, 1), :]
            p_hi = pos_ref[kind, pl.ds(i + CMP_STRIDE, 1), :]
            w_lo = w1_ref[kind, pl.ds(r0, HD), :]
            w_hi = w1_ref[kind, pl.ds(r1, HD), :]
            for g in range(G):
                j = kind * G + g
                seg = xi[:, j * HD:(j + 1) * HD]
                u_s[j] += _dot((seg + p_lo).astype(BF16), w_lo)
                v_s[j] += _dot((seg + p_hi).astype(BF16), w_hi)
        return carry

    lax.fori_loop(0, CMP_STRIDE, step, 0)

    for kind in range(2):
        for g in range(G):
            j = kind * G + g
            pre = u_s[j] + pltpu.roll(v_s[j], nwin - 1, axis=0)
            hmid = _gelu_tanh(pre).astype(BF16)
            if kind == 0:
                y = _dot(hmid, w2_ref[...])
                var = jnp.mean(y * y, axis=-1, keepdims=True)
                yn = y * lax.rsqrt(var + RMS_EPS) * g_ref[...]
                kc_ref[:, g * 128:g * 128 + HD] = yn.astype(BF16)
                kc_ref[:, g * 128 + HD:(g + 1) * 128] = jnp.zeros((nwin, 128 - HD), BF16)
            else:
                vct_ref[g * HD:(g + 1) * HD, :] = _dot_nt(w2t_ref[...], hmid).astype(BF16)


def compress(cc, pos, w1, w2k, w2vt, g, batch, seq):
    G, HD = NSA_GROUPS, NSA_HD
    nwin = seq // CMP_STRIDE
    hid = w1.shape[2]
    full = lambda a: pl.BlockSpec(a.shape, lambda b: (0,) * a.ndim)
    return pl.pallas_call(
        _compress_body,
        grid=(batch,),
        in_specs=[pl.BlockSpec((None, seq, 128), functools.partial(lambda q, b: (q, b, 0), q))
                  for q in range(4)]
                 + [full(pos), full(w1), full(w2k), full(w2vt), full(g)],
        out_specs=[pl.BlockSpec((None, nwin, G * 128), lambda b: (b, 0, 0)),
                   pl.BlockSpec((None, G * HD, nwin), lambda b: (b, 0, 0))],
        out_shape=[jax.ShapeDtypeStruct((batch, nwin, G * 128), BF16),
                   jax.ShapeDtypeStruct((batch, G * HD, nwin), BF16)],
        scratch_shapes=[pltpu.VMEM((2 * G, nwin, hid), F32), pltpu.VMEM((2 * G, nwin, hid), F32)],
        compiler_params=_cparams(("parallel",)),
        name="compress",
    )(cc, cc, cc, cc, pos, w1, w2k, w2vt, g)


def _t5_bucket(dist):
    n = jnp.maximum(dist, 0)
    max_exact = REL_BUCKETS // 2
    nf = jnp.maximum(n, 1).astype(F32)
    large = max_exact + (jnp.log(nf / max_exact) / math.log(REL_MAX_DIST / max_exact)
                         * (REL_BUCKETS - max_exact)).astype(jnp.int32)
    large = jnp.minimum(large, REL_BUCKETS - 1)
    return jnp.where(n < max_exact, n, large)


def _table_lookup(bucket, tab_ref, h):
    out = jnp.zeros(bucket.shape, F32)
    for k in range(REL_BUCKETS):
        out = jnp.where(bucket == k, tab_ref[k, h], out)
    return out


def _bias_prep_body(tab_ref, bc_ref, tp_ref, far_ref):
    h = pl.program_id(0)
    S = bc_ref.shape[1]
    far = tab_ref[REL_BUCKETS - 1, h]
    far_ref[...] = jnp.full(far_ref.shape, far * LOG2E, F32)

    dist = lax.broadcasted_iota(jnp.int32, (8, S), 1)
    by_dist = _table_lookup(_t5_bucket(dist), tab_ref, h) * LOG2E
    shifted = pltpu.roll(jnp.broadcast_to(by_dist[0:1, :], (NCMP_PAD, S)), 0, axis=1,
                         stride=CMP_STRIDE, stride_axis=0)
    shifted = pltpu.roll(shifted, CMP_BLOCK - 1, axis=1)
    n = lax.broadcasted_iota(jnp.int32, (NCMP_PAD, S), 0)
    t = lax.broadcasted_iota(jnp.int32, (NCMP_PAD, S), 1)
    bc_ref[...] = jnp.where(t >= n * CMP_STRIDE + CMP_BLOCK - 1, shifted, NEG_INF)
    j = lax.broadcasted_iota(jnp.int32, (ATT_TK, ATT_QB), 0)
    i = lax.broadcasted_iota(jnp.int32, (ATT_TK, ATT_QB), 1)
    for d in range(2):
        dist = d * ATT_TK + i - j
        rel = (_table_lookup(_t5_bucket(dist), tab_ref, h) - far) * LOG2E
        tp_ref[d] = jnp.where(dist >= 0, rel, NEG_INF)
    tp_ref[2] = jnp.where(i < j, 0.0, NEG_INF)


def bias_prep(rel_table, seq):
    assert ATT_TK == ATT_QB and ATT_TK + 1 > 113
    return pl.pallas_call(
        _bias_prep_body,
        grid=(NSA_HEADS,),
        in_specs=[pl.BlockSpec(memory_space=pltpu.SMEM)],
        out_specs=[pl.BlockSpec((None, NCMP_PAD, seq), lambda h: (h, 0, 0)),
                   pl.BlockSpec((None, 3, ATT_TK, ATT_QB), lambda h: (h, 0, 0, 0)),
                   pl.BlockSpec((None, 8, 128), lambda h: (h, 0, 0))],
        out_shape=[jax.ShapeDtypeStruct((NSA_HEADS, NCMP_PAD, seq), F32),
                   jax.ShapeDtypeStruct((NSA_HEADS, 3, ATT_TK, ATT_QB), F32),
                   jax.ShapeDtypeStruct((NSA_HEADS, 8, 128), F32)],
        compiler_params=_cparams(("parallel",)),
        name="bias_prep",
    )(rel_table)


def _q_proj_body(x_ref, g_ref, sh_ref, sc_ref, wqt_ref, wgt_ref, bg_ref, gq_ref, qt_ref, gt_ref):
    HD = NSA_HD
    h = _norm_mod(x_ref[...], g_ref[...], sh_ref[...], sc_ref[...]).astype(BF16)
    gt_ref[...] = _dot_nt(wgt_ref[...], h) + bg_ref[...]
    qt = _dot_nt(wqt_ref[...], h)
    scale = gq_ref[...] * (HD ** -0.5 * LOG2E)
    for hh in range(NSA_HEADS):
        seg = qt[hh * HD:(hh + 1) * HD, :]
        var = jnp.mean(seg * seg, axis=0, keepdims=True)
        qt_ref[hh * HD:(hh + 1) * HD, :] = (seg * lax.rsqrt(var + RMS_EPS) * scale).astype(BF16)


def q_proj(x2d, g, sh, sc, wqt, wgt, bg, gq, batch, seq, tm=512):
    M, D = x2d.shape
    nt = seq // tm
    nq = wqt.shape[0]
    ng = wgt.shape[0]
    bvec = pl.BlockSpec((None, 1, D), lambda i: (i // nt, 0, 0))
    full = lambda a: pl.BlockSpec(a.shape, lambda i: (0,) * a.ndim)
    return pl.pallas_call(
        _q_proj_body,
        grid=(M // tm,),
        in_specs=[pl.BlockSpec((tm, D), lambda i: (i, 0)),
                  pl.BlockSpec((1, D), lambda i: (0, 0)),
                  bvec, bvec, full(wqt), full(wgt), full(bg), full(gq)],
        out_specs=[pl.BlockSpec((None, nq, tm), lambda i: (i // nt, 0, i % nt)),
                   pl.BlockSpec((None, ng, tm), lambda i: (i // nt, 0, i % nt))],
        out_shape=[jax.ShapeDtypeStruct((batch, nq, seq), BF16),
                   jax.ShapeDtypeStruct((batch, ng, seq), F32)],
        compiler_params=_cparams(("parallel",)),
        name="q_proj",
    )(x2d, g.reshape(1, D), sh, sc, wqt, wgt, bg, gq)


def _heads_on_lanes(pieces):
    return jnp.concatenate(pieces, axis=1)


def _nsa_body(gps, qt_ref, gt_ref, kc_ref, vct_ref, ks_ref, kw_ref, vst_ref, vwt_ref,
              bct_ref, tp_ref, far_ref, ovt_ref, out_ref):
    qb = pl.program_id(2)
    QB, HD, HPG = ATT_QB, NSA_HD, NSA_HPG
    R = HPG * QB
    t0 = pl.multiple_of(qb * QB, QB)
    near0 = pl.multiple_of(t0 + WINDOW - ATT_TK, ATT_TK)
    wlen = WINDOW + QB
    nblk = ovt_ref.shape[0]
    jb = lax.broadcasted_iota(jnp.int32, (nblk, QB), 0)
    jbf = jb.astype(F32)
    qid = jnp.right_shift(t0 + lax.broadcasted_iota(jnp.int32, (nblk, QB), 1), 6)
    forced = (jb == 0) | (jb == qid) | (jb == qid - 1)
    srow = lax.broadcasted_iota(jnp.int32, (8, R), 0)

    def finish(acc):
        return acc[0:HD, :] / acc[HD:HD + 1, :]

    def tile_part(s_tile, v_tile):
        m_t = jnp.max(s_tile, axis=0, keepdims=True)
        m_safe = jnp.where(m_t == NEG_INF, 0.0, m_t)
        p = jnp.exp2(s_tile - m_safe).astype(BF16)
        return m_t, _dot(v_tile, p)[0:HD + 8, :]

    def combine(parts):
        m_fin = parts[0][0]
        for m_t, _ in parts[1:]:
            m_fin = jnp.maximum(m_fin, m_t)
        acc = None
        for m_t, pv in parts:
            term = jnp.exp2(m_t - m_fin) * pv
            acc = term if acc is None else acc + term
        return m_fin, acc

    groups = range(gps)
    hs = [[gl * HPG + h for h in range(HPG)] for gl in groups]
    kcol = [slice(gl * 128, (gl + 1) * 128) for gl in groups]
    vrow = [slice(gl * VT_ROWS, (gl + 1) * VT_ROWS) for gl in groups]

    def stationary(q_all, hi_lo, block_rows):
        pad = jnp.where(srow == 0, MASK_BIG, 0.0)
        rest = jnp.zeros((128 - HD - KEY_PAD - 8, R), F32)
        tail = jnp.concatenate([hi_lo, block_rows, pad, rest], axis=0).astype(BF16)
        return jnp.concatenate([q_all, tail], axis=0)

    q_alls, hi_los, qms = [], [], []
    for gl in groups:
        q_alls.append(_heads_on_lanes([qt_ref[h * HD:(h + 1) * HD, :] for h in hs[gl]]))
        far = _heads_on_lanes([far_ref[h, 0:1, :] for h in hs[gl]])
        hi = far.astype(BF16).astype(F32)
        hi_los.append(jnp.where(srow == 0, hi, jnp.where(srow == 1, far - hi, 0.0)))
        qms.append(stationary(q_alls[gl], hi_los[gl], jnp.zeros((nblk, R), F32)))

    s_cs = [_dot(kc_ref[:, kcol[gl]], qms[gl]) for gl in groups]
    s_ws = [_dot(kw_ref[pl.ds(t0, wlen), kcol[gl]], qms[gl]) for gl in groups]
    s_ns = [_dot(ks_ref[pl.ds(near0, 2 * ATT_TK), kcol[gl]], qms[gl]) for gl in groups]

    o_cmps, scores = [], []
    for gl in groups:
        s_c = s_cs[gl] + _heads_on_lanes([bct_ref[h] for h in hs[gl]])
        m_c = jnp.max(s_c, axis=0, keepdims=True)
        m_c = jnp.where(m_c == NEG_INF, 0.0, m_c)
        e_c = jnp.exp2(s_c - m_c)
        p_c = e_c / jnp.maximum(jnp.sum(e_c, axis=0, keepdims=True), jnp.finfo(F32).tiny)
        o_cmps.append(_dot(vct_ref[gl * HD:(gl + 1) * HD, :], p_c.astype(BF16)))
        p_sum = p_c[:, 0:QB]
        for h in range(1, HPG):
            p_sum = p_sum + p_c[:, h * QB:(h + 1) * QB]
        p_hi = p_sum.astype(BF16)
        p_lo = (p_sum - p_hi.astype(F32)).astype(BF16)
        imp = _dot(ovt_ref[...], jnp.concatenate([p_hi, p_lo], axis=0))
        score = jnp.where(forced, FORCE_SCORE, imp)
        scores.append(jnp.where(jb <= qid, score, NEG_INF))

    nwt = wlen // ATT_TK
    o_wins = []
    for gl in groups:
        parts = []
        for i in range(nwt):
            d = nwt - 1 - i
            s_t = s_ws[gl][i * ATT_TK:(i + 1) * ATT_TK]
            if d in (0, 1):
                s_t = s_t + _heads_on_lanes([tp_ref[h, d] for h in hs[gl]])
            elif d == nwt - 1:
                s_t = s_t + _heads_on_lanes([tp_ref[h, 2] for h in hs[gl]])
            parts.append(tile_part(s_t, vwt_ref[vrow[gl], pl.ds(t0 + i * ATT_TK, ATT_TK)]))
        o_wins.append(finish(combine(parts)[1]))

    msels = [jnp.full((nblk, QB), NEG_INF, F32) for _ in groups]
    for _ in range(SLC_TOPK):
        for gl in groups:
            mx = jnp.max(scores[gl], axis=0, keepdims=True)
            first = jnp.min(jnp.where(scores[gl] == mx, jbf, float(nblk)), axis=0, keepdims=True)
            pick = jbf == first
            msels[gl] = jnp.where(pick & (mx > NEG_INF), 0.0, msels[gl])
            scores[gl] = jnp.where(pick, NEG_INF, scores[gl])
    carry0 = []
    for gl in groups:
        prev_mask = jnp.concatenate(
            [jnp.broadcast_to(jnp.max(jnp.where(jb == 2 * qb - 2 + r, msels[gl], NEG_INF), axis=0,
                                      keepdims=True), (SLC_BLOCK, QB)) for r in range(2)], axis=0)
        s_prev = s_ns[gl][0:ATT_TK] + _heads_on_lanes([tp_ref[h, 1] + prev_mask for h in hs[gl]])
        s_diag = s_ns[gl][ATT_TK:] + _heads_on_lanes([tp_ref[h, 0] for h in hs[gl]])
        carry0.append(combine([
            tile_part(s_diag, vst_ref[vrow[gl], pl.ds(near0 + ATT_TK, ATT_TK)]),
            tile_part(s_prev, vst_ref[vrow[gl], pl.ds(near0, ATT_TK)])]))

    qss = [stationary(q_alls[gl], hi_los[gl], _heads_on_lanes(
        [jnp.where(msels[gl] == 0.0, 0.0, MASK_BIG)] * HPG)) for gl in groups]

    def far_step(c, carry):
        tile0 = qb + (WINDOW - ATT_TK) // ATT_TK - (c + 1) * (FAR_CHUNK // ATT_TK)
        row0 = pl.multiple_of(tile0 * ATT_TK, ATT_TK)
        s_fs = [_dot(ks_ref[pl.ds(row0, FAR_CHUNK), kcol[gl]], qss[gl]) for gl in groups]
        out = []
        for gl in groups:
            parts = [carry[gl]]
            for j in range(FAR_CHUNK // ATT_TK):
                parts.append(tile_part(s_fs[gl][j * ATT_TK:(j + 1) * ATT_TK],
                                       vst_ref[vrow[gl], pl.ds(row0 + j * ATT_TK, ATT_TK)]))
            out.append(combine(parts))
        return tuple(out)

    n_far = (qb + 2) // 4
    sel = lax.fori_loop(0, n_far, far_step, tuple(carry0))

    gates = jax.nn.sigmoid(gt_ref[...])
    for gl in groups:
        o_cmp, o_win = o_cmps[gl], o_wins[gl]
        o_sel = finish(sel[gl][1])
        outs = []
        for h in range(HPG):
            lanes = slice(h * QB, (h + 1) * QB)
            r = gl * 16 + 3 * h
            outs.append(gates[r:r + 1, :] * o_cmp[:, lanes]
                        + gates[r + 1:r + 2, :] * o_sel[:, lanes]
                        + gates[r + 2:r + 3, :] * o_win[:, lanes])
        for pair in range(HPG // 2):
            two = jnp.concatenate(outs[2 * pair:2 * pair + 2], axis=0)
            c0 = gl * HPG * HD + pair * 2 * HD
            out_ref[:, c0:c0 + 2 * HD] = two.T.astype(out_ref.dtype)


def _overlap_matrix_t(nblk):
    start = np.arange(NCMP_PAD) * CMP_STRIDE
    sj = np.arange(nblk) * SLC_BLOCK
    ov = (np.minimum(start[None, :] + CMP_BLOCK, sj[:, None] + SLC_BLOCK)
          - np.maximum(start[None, :], sj[:, None]))
    ov = np.clip(ov, 0, None) / CMP_BLOCK
    ov[:, NCMP_PAD - 1] = 0.0
    return np.concatenate([ov, ov], axis=1).astype(np.float32)


def nsa_attn(qt, gt, kc, vct, ks, kw, vst, vwt, bias_ct, tiles, far, batch, seq, gps=ATT_GPS):
    QB, HD, HPG, G = ATT_QB, NSA_HD, NSA_HPG, NSA_GROUPS
    nblk = seq // SLC_BLOCK
    assert nblk % 8 == 0 and seq % QB == 0 and seq // CMP_STRIDE == NCMP_PAD and G % gps == 0
    assert WINDOW % ATT_TK == 0 and FAR_CHUNK == 4 * ATT_TK and KV_PAD >= FAR_CHUNK - ATT_TK
    assert KEY_BLK % 8 == 0 and KEY_BLK + nblk <= KEY_PAD and KEY_PAD + 8 <= 128 - HD
    nq = seq // QB
    sp = seq + KV_PAD
    ovt = jnp.asarray(_overlap_matrix_t(nblk), dtype=BF16)
    return pl.pallas_call(
        functools.partial(_nsa_body, gps),
        grid=(batch, G // gps, nq),
        in_specs=[pl.BlockSpec((None, gps * HPG * HD, QB), lambda b, g, i: (b, g, i)),
                  pl.BlockSpec((None, gps * 16, QB), lambda b, g, i: (b, g, i)),
                  pl.BlockSpec((None, NCMP_PAD, gps * 128), lambda b, g, i: (b, 0, g)),
                  pl.BlockSpec((None, gps * HD, NCMP_PAD), lambda b, g, i: (b, g, 0)),
                  pl.BlockSpec((None, sp, gps * 128), lambda b, g, i: (b, 0, g)),
                  pl.BlockSpec((None, sp, gps * 128), lambda b, g, i: (b, 0, g)),
                  pl.BlockSpec((None, gps * VT_ROWS, sp), lambda b, g, i: (b, g, 0)),
                  pl.BlockSpec((None, gps * VT_ROWS, sp), lambda b, g, i: (b, g, 0)),
                  pl.BlockSpec((gps * HPG, NCMP_PAD, QB), lambda b, g, i: (g, 0, i)),
                  pl.BlockSpec((gps * HPG, 3, ATT_TK, QB), lambda b, g, i: (g, 0, 0, 0)),
                  pl.BlockSpec((gps * HPG, 8, 128), lambda b, g, i: (g, 0, 0)),
                  pl.BlockSpec((nblk, 2 * NCMP_PAD), lambda b, g, i: (0, 0))],
        out_specs=pl.BlockSpec((QB, gps * HPG * HD), lambda b, g, i: (b * nq + i, g)),
        out_shape=jax.ShapeDtypeStruct((batch * seq, NSA_HEADS * HD), BF16),
        compiler_params=_cparams(("parallel", "parallel", "arbitrary")),
        name="nsa_attn",
    )(qt, gt, kc, vct, ks, kw, vst, vwt, bias_ct, tiles, far, ovt)


def _mlstm_layer(x2d, g_mix, sh, sc, w_in, b_if, g_out, w_out, batch, seq):
    nbig = 2 * ML_HEADS * ML_DQK + 2 * ML_HEADS * ML_DV
    ng = 2 * ML_HEADS
    wg = jnp.pad(w_in[:, nbig:], ((0, 0), (0, 128 - ng))).astype(BF16)
    bg = jnp.pad(b_if, (0, 128 - ng))
    proj, gates = norm_proj(x2d, g_mix, sh, sc, [w_in[:, :nbig].astype(BF16), wg], [None, bg],
                            [BF16, F32])
    return mlstm_core(proj, gates, g_out, batch, seq), w_out.astype(BF16)


def _nsa_shared(x2d, g_kv, kv_sh, kv_sc, w_kv, pos_k, w_k1, w_k2, pos_v, w_v1, w_v2,
                g_knorm, batch, seq):
    gw = NSA_GROUPS * NSA_HD
    part = lambda i: w_kv[:, i * gw:(i + 1) * gw]
    hd = NSA_HD
    wk = jnp.concatenate([p[:, g * hd:(g + 1) * hd] for g in range(NSA_GROUPS)
                          for p in (part(2), part(4))], axis=1).astype(BF16)
    gk = jnp.tile(jnp.concatenate([g_knorm[1], g_knorm[2]]), NSA_GROUPS).reshape(1, -1)
    wvt = jnp.concatenate([part(3), part(5)], axis=1).T.astype(BF16)
    wcc = jnp.concatenate([part(0), part(1)], axis=1).astype(BF16)
    ks, kw, vst, vwt, cc = kv_proj(x2d, g_kv, kv_sh, kv_sc, wk, wvt, wcc, gk, batch, seq)
    pos = jnp.stack([pos_k, pos_v])
    w1 = jnp.stack([w_k1, w_v1]).astype(BF16)
    kc, vct = compress(cc, pos, w1, w_k2.astype(BF16), w_v2.T.astype(BF16), g_knorm[0:1],
                       batch, seq)
    return kc, vct, ks, kw, vst, vwt


def _gate_weights_t(w_q, b_gate):
    nq = NSA_HEADS * NSA_HD
    per = 3 * NSA_HPG
    wg = w_q[:, nq:].T.reshape(NSA_GROUPS, per, -1)
    wg = jnp.pad(wg, ((0, 0), (0, 16 - per), (0, 0))).reshape(NSA_GROUPS * 16, -1)
    bg = jnp.pad(b_gate.reshape(NSA_GROUPS, per), ((0, 0), (0, 16 - per))).reshape(-1, 1)
    return wg.astype(BF16), bg


def _nsa_layer(x2d, g_mix, sh, sc, shared, w_q, b_gate, g_qnorm, w_out, bias_ct, tiles, far,
               batch, seq):
    nq = NSA_HEADS * NSA_HD
    wgt, bg = _gate_weights_t(w_q, b_gate)
    qt, gt = q_proj(x2d, g_mix, sh, sc, w_q[:, :nq].T.astype(BF16), wgt, bg,
                    g_qnorm.reshape(NSA_HD, 1), batch, seq)
    kc, vct, ks, kw, vst, vwt = shared
    att = nsa_attn(qt, gt, kc, vct, ks, kw, vst, vwt, bias_ct, tiles, far, batch, seq)
    return att, w_out.astype(BF16)


def kernel(x, c, w_ada, b_ada, g_norm_mix, g_norm_ffn, w_ffn_in, w_ffn_out, w_a_in, b_a_if, g_a_out, w_a_out, w_kv_ada, b_kv_ada, g_kv_norm, w_kv, pos_cmp_k, w_cmp_k1, w_cmp_k2, pos_cmp_v, w_cmp_v1, w_cmp_v2, g_knorm, w_b_q, b_b_gate, g_qnorm, w_b_out, rel_table):
    B, S, D = x.shape
    depth = w_ada.shape[0]
    n_a = w_a_in.shape[0]
    x2d = x.reshape(B * S, D)
    mods = ada_mod(c, w_ada, b_ada)
    kv_mod = ada_mod(c, w_kv_ada[None], b_kv_ada[None])[0]
    shared = None
    bias_ct = tiles = far = None
    for layer in range(depth):
        sh1, sc1, ga1, sh2, sc2, ga2 = [mods[layer, :, i * D:(i + 1) * D].reshape(B, 1, D)
                                        for i in range(6)]
        if layer < n_a:
            mixed, w_mix = _mlstm_layer(x2d, g_norm_mix[layer], sh1, sc1, w_a_in[layer],
                                        b_a_if[layer], g_a_out[layer], w_a_out[layer], B, S)
        else:
            j = layer - n_a
            if shared is None:
                kv_sh = kv_mod[:, :D].reshape(B, 1, D)
                kv_sc = kv_mod[:, D:].reshape(B, 1, D)
                shared = _nsa_shared(x2d, g_kv_norm, kv_sh, kv_sc, w_kv, pos_cmp_k, w_cmp_k1,
                                     w_cmp_k2, pos_cmp_v, w_cmp_v1, w_cmp_v2, g_knorm, B, S)
                bias_ct, tiles, far = bias_prep(rel_table, S)
            mixed, w_mix = _nsa_layer(x2d, g_norm_mix[layer], sh1, sc1, shared, w_b_q[j],
                                      b_b_gate[j], g_qnorm[j], w_b_out[j], bias_ct, tiles, far, B, S)
        x2d = mix_ffn(mixed, w_mix, x2d, ga1, g_norm_ffn[layer], sh2, sc2, ga2,
                      w_ffn_in[layer].astype(BF16), w_ffn_out[layer].astype(BF16))
    return x2d.reshape(B, S, D)
```

```python
import functools
import math

import jax
import jax.numpy as jnp
import numpy as np
from jax import lax
from jax.experimental import pallas as pl
from jax.experimental.pallas import tpu as pltpu

F32 = jnp.float32
BF16 = jnp.bfloat16
NEG_INF = float("-inf")
LOG2E = math.log2(math.e)

RMS_EPS = 1e-6

ML_HEADS = 4
ML_DQK = 128
ML_DV = 256
ML_LC = 256

NSA_HEADS = 16
NSA_GROUPS = 4
NSA_HPG = 4
NSA_HD = 64
CMP_BLOCK = 32
CMP_STRIDE = 16
SLC_BLOCK = 64
SLC_TOPK = 8
WINDOW = 512
FORCE_SCORE = 1e4
REL_BUCKETS = 32
REL_MAX_DIST = 128
ATT_QB = 128
ATT_TK = 128
ATT_GPS = 4
NCMP_PAD = 128
KV_PAD = WINDOW
FAR_CHUNK = 512
VT_ROWS = 80
KEY_BLK = 8
KEY_PAD = 40
MASK_BIG = -32768.0

VMEM_LIMIT = 48 * 1024 * 1024


def _cparams(sem):
    return pltpu.CompilerParams(dimension_semantics=sem, vmem_limit_bytes=VMEM_LIMIT)


def _dot(a, b):
    return jnp.dot(a, b, preferred_element_type=F32)


def _dot_nt(a, b):
    return lax.dot_general(a, b, (((1,), (1,)), ((), ())), preferred_element_type=F32)


def _dot_tn(a, b):
    return lax.dot_general(a, b, (((0,), (0,)), ((), ())), preferred_element_type=F32)


def _norm_mod(x, g, sh, sc):
    var = jnp.mean(x * x, axis=-1, keepdims=True)
    y = x * lax.rsqrt(var + RMS_EPS) * g
    return y * (1.0 + sc) + sh


def _ada_body(c_ref, w_ref, b_ref, o_ref):
    c = c_ref[...]
    ca = c * jax.nn.sigmoid(c)
    nb = ca.shape[0]
    a_hi = ca.astype(BF16).astype(F32)
    a_mid = (ca - a_hi).astype(BF16).astype(F32)
    a_lo = ca - a_hi - a_mid
    a3 = jnp.concatenate([a_hi, a_mid, a_lo, jnp.zeros_like(ca)], axis=0).astype(BF16)
    w = w_ref[...]
    w_hi = w.astype(BF16)
    w_lo = (w - w_hi.astype(F32)).astype(BF16)
    p = _dot(a3, w_hi)
    q = _dot(a3[0:2 * nb], w_lo)
    o_ref[...] = (p[0:nb] + p[nb:2 * nb] + p[2 * nb:3 * nb] + q[0:nb] + q[nb:2 * nb]) + b_ref[...]


def ada_mod(c, w, b, tn=1024):
    L, D, N = w.shape
    B = c.shape[0]
    return pl.pallas_call(
        _ada_body,
        grid=(L, N // tn),
        in_specs=[pl.BlockSpec((B, D), lambda l, j: (0, 0)),
                  pl.BlockSpec((None, D, tn), lambda l, j: (l, 0, j)),
                  pl.BlockSpec((None, 1, tn), lambda l, j: (l, 0, j))],
        out_specs=pl.BlockSpec((None, B, tn), lambda l, j: (l, 0, j)),
        out_shape=jax.ShapeDtypeStruct((L, B, N), F32),
        compiler_params=_cparams(("parallel", "parallel")),
        name="ada_mod",
    )(c, w, b.reshape(L, 1, N))


def _norm_proj_body(n_out, has_bias, tn, x_ref, g_ref, sh_ref, sc_ref, *refs):
    w_refs = refs[:n_out]
    b_refs = refs[n_out:2 * n_out]
    o_refs = refs[2 * n_out:3 * n_out]
    h = _norm_mod(x_ref[...], g_ref[...], sh_ref[...], sc_ref[...]).astype(BF16)
    for w_ref, b_ref, o_ref, hb in zip(w_refs, b_refs, o_refs, has_bias):
        n = w_ref.shape[1]
        step = min(tn, n)
        for n0 in range(0, n, step):
            acc = _dot(h, w_ref[:, n0:n0 + step])
            if hb:
                acc = acc + b_ref[:, n0:n0 + step]
            o_ref[:, n0:n0 + step] = acc.astype(o_ref.dtype)


def norm_proj(x2d, g, sh, sc, ws, biases, out_dtypes, tm=512, tn=512):
    M, D = x2d.shape
    B = sh.shape[0]
    tiles_per_batch = (M // B) // tm
    n_out = len(ws)
    has_bias = tuple(b is not None for b in biases)
    bias_args = [(b if b is not None else jnp.zeros((w.shape[1],), F32)).reshape(1, -1)
                 for b, w in zip(biases, ws)]
    in_specs = [pl.BlockSpec((tm, D), lambda i: (i, 0)),
                pl.BlockSpec((1, D), lambda i: (0, 0)),
                pl.BlockSpec((None, 1, D), lambda i: (i // tiles_per_batch, 0, 0)),
                pl.BlockSpec((None, 1, D), lambda i: (i // tiles_per_batch, 0, 0))]
    in_specs += [pl.BlockSpec(w.shape, lambda i: (0, 0)) for w in ws]
    in_specs += [pl.BlockSpec(b.shape, lambda i: (0, 0)) for b in bias_args]
    out_specs = [pl.BlockSpec((tm, w.shape[1]), lambda i: (i, 0)) for w in ws]
    out_shape = [jax.ShapeDtypeStruct((M, w.shape[1]), dt) for w, dt in zip(ws, out_dtypes)]
    return pl.pallas_call(
        functools.partial(_norm_proj_body, n_out, has_bias, tn),
        grid=(M // tm,),
        in_specs=in_specs, out_specs=out_specs, out_shape=out_shape,
        compiler_params=_cparams(("parallel",)),
        name="norm_proj",
    )(x2d, g.reshape(1, D), sh, sc, *ws, *bias_args)


def _mix_ffn_body(tf, a_ref, wm_ref, x_ref, ga1_ref, g_ref, sh_ref, sc_ref, ga2_ref,
                  wi_ref, wo_ref, o_ref, act_s):
    F = wo_ref.shape[0]
    x1 = x_ref[...] + ga1_ref[...] * _dot(a_ref[...], wm_ref[...])
    h = _norm_mod(x1, g_ref[...], sh_ref[...], sc_ref[...]).astype(BF16)
    for f0 in range(0, F, tf):
        gate = _dot(h, wi_ref[:, f0:f0 + tf])
        up = _dot(h, wi_ref[:, F + f0:F + f0 + tf])
        act_s[:, f0:f0 + tf] = (gate * jax.nn.sigmoid(gate) * up).astype(BF16)
    o_ref[...] = x1 + ga2_ref[...] * _dot(act_s[...], wo_ref[...])


def mix_ffn(a, w_mix, x2d, ga1, g, sh, sc, ga2, w_in, w_out, tm=512, tf=256):
    M, D = x2d.shape
    K = a.shape[1]
    F = w_out.shape[0]
    B = sh.shape[0]
    tiles_per_batch = (M // B) // tm
    bvec = pl.BlockSpec((None, 1, D), lambda i: (i // tiles_per_batch, 0, 0))
    resident = lambda w: pl.BlockSpec(w.shape, lambda i: (0, 0), pipeline_mode=pl.Buffered(1))
    return pl.pallas_call(
        functools.partial(_mix_ffn_body, tf),
        grid=(M // tm,),
        in_specs=[pl.BlockSpec((tm, K), lambda i: (i, 0)),
                  resident(w_mix),
                  pl.BlockSpec((tm, D), lambda i: (i, 0)),
                  bvec,
                  pl.BlockSpec((1, D), lambda i: (0, 0)),
                  bvec, bvec, bvec,
                  resident(w_in), resident(w_out)],
        out_specs=pl.BlockSpec((tm, D), lambda i: (i, 0)),
        out_shape=jax.ShapeDtypeStruct((M, D), F32),
        scratch_shapes=[pltpu.VMEM((tm, F), BF16)],
        compiler_params=_cparams(("parallel",)),
        name="mix_ffn",
    )(a, w_mix, x2d, ga1, g.reshape(1, D), sh, sc, ga2, w_in, w_out)


def _sublane_scan(x, op, fill):
    n = x.shape[0]
    row = lax.broadcasted_iota(jnp.int32, x.shape, 0)
    sh = 1
    while sh < n:
        x = op(x, jnp.where(row >= sh, pltpu.roll(x, sh, axis=0), fill))
        sh *= 2
    return x


def _mlstm_body(q_ref, k_ref, v_ref, o_ref, gc_ref, gout_ref, out_ref, c_s, n_s, m_s):
    c_idx = pl.program_id(1)
    LC = q_ref.shape[0]
    NH = ML_HEADS
    scale = ML_DQK ** -0.5
    log_scale = math.log(scale)

    @pl.when(c_idx == 0)
    def _():
        c_s[...] = jnp.zeros_like(c_s)
        n_s[...] = jnp.zeros_like(n_s)
        m_s[...] = jnp.zeros_like(m_s)

    gc = gc_ref[...]
    logf = jnp.minimum(gc, 0.0) - jnp.log1p(jnp.exp(-jnp.abs(gc)))
    bcum = pltpu.roll(_sublane_scan(logf, jnp.add, 0.0), 128 - NH, axis=1)
    a_c = gc - bcum
    cmax = _sublane_scan(a_c, jnp.maximum, NEG_INF)
    a_t = a_c.T

    row = lax.broadcasted_iota(jnp.int32, (LC, LC), 0)
    col = lax.broadcasted_iota(jnp.int32, (LC, LC), 1)
    causal = col <= row
    ones_col = jnp.ones((LC, 128), BF16)

    def lanes2(x):
        return jnp.concatenate([x, x], axis=1)

    for h in range(NH):
        a_rep = jnp.broadcast_to(a_c[:, h:h + 1], (LC, 128))
        b_rep = jnp.broadcast_to(bcum[:, h:h + 1], (LC, 128))
        cm_rep = jnp.broadcast_to(cmax[:, h:h + 1], (LC, 128))
        a_r = a_t[h:h + 1, :]
        b_last = b_rep[LC - 1:LC, :]
        m_prev = m_s[h:h + 1, :]
        m_new = jnp.maximum(b_last + m_prev, b_last + cm_rep[LC - 1:LC, :])
        decay = jnp.exp(b_last + m_prev - m_new)
        e_rep = jnp.exp(b_last + a_rep - m_new)
        g_rep = jnp.maximum(m_prev, cm_rep)
        w_intra = jnp.exp(jnp.where(causal, a_r - lanes2(g_rep) + log_scale, NEG_INF))
        w_inter = jnp.exp(m_prev - g_rep) * scale
        floor = jnp.exp(-(b_rep + g_rep))

        qh = q_ref[:, h * ML_DQK:(h + 1) * ML_DQK]
        kh = k_ref[:, h * ML_DQK:(h + 1) * ML_DQK]
        vh = v_ref[:, h * ML_DV:(h + 1) * ML_DV]
        c_prev = c_s[h]
        n_prev = n_s[h:h + 1, :]

        s = (_dot_nt(qh, kh) * w_intra).astype(BF16)
        inter = _dot(qh, c_prev.astype(BF16))
        num = _dot(s, vh) + lanes2(w_inter) * inter
        n_rows = jnp.broadcast_to(n_prev, (128, ML_DQK)).astype(BF16)
        qn = _dot(s, ones_col) + w_inter * _dot_nt(qh, n_rows)
        inv = 1.0 / jnp.maximum(jnp.abs(qn), floor)

        ke = kh.astype(F32) * e_rep
        c_s[h] = lanes2(decay) * c_prev + _dot_tn(ke.astype(BF16), vh)
        n_s[h:h + 1, :] = decay * n_prev + jnp.sum(ke, axis=0, keepdims=True)
        m_s[h:h + 1, :] = m_new

        ssq = _dot((num * num).astype(BF16), jnp.ones((ML_DV, 128), BF16))
        rs = lax.rsqrt(ssq * (inv * inv) * (1.0 / ML_DV) + RMS_EPS) * inv
        hn = num * lanes2(rs) * gout_ref[:, h * ML_DV:(h + 1) * ML_DV]
        og = jax.nn.sigmoid(o_ref[:, h * ML_DV:(h + 1) * ML_DV].astype(F32))
        out_ref[:, h * ML_DV:(h + 1) * ML_DV] = (hn * og).astype(out_ref.dtype)


def mlstm_core(proj, gates, g_out, batch, seq):
    LC = ML_LC
    nc = seq // LC
    qk = ML_HEADS * ML_DQK
    vd = ML_HEADS * ML_DV
    row = lambda b, c: b * nc + c
    return pl.pallas_call(
        _mlstm_body,
        grid=(batch, nc),
        in_specs=[pl.BlockSpec((LC, qk), lambda b, c: (row(b, c), 0)),
                  pl.BlockSpec((LC, qk), lambda b, c: (row(b, c), 1)),
                  pl.BlockSpec((LC, vd), lambda b, c: (row(b, c), 1)),
                  pl.BlockSpec((LC, vd), lambda b, c: (row(b, c), 2)),
                  pl.BlockSpec((LC, 128), lambda b, c: (row(b, c), 0)),
                  pl.BlockSpec((1, vd), lambda b, c: (0, 0))],
        out_specs=pl.BlockSpec((LC, vd), lambda b, c: (row(b, c), 0)),
        out_shape=jax.ShapeDtypeStruct((batch * seq, vd), BF16),
        scratch_shapes=[pltpu.VMEM((ML_HEADS, ML_DQK, ML_DV), F32),
                        pltpu.VMEM((8, ML_DQK), F32),
                        pltpu.VMEM((8, 128), F32)],
        compiler_params=_cparams(("parallel", "arbitrary")),
        name="mlstm_core",
    )(proj, proj, proj, proj, gates, g_out.reshape(1, vd))


def _kv_proj_body(x_ref, g_ref, sh_ref, sc_ref, wk_ref, wvt_ref, wcc_ref, gk_ref, bd_ref,
                  ks_ref, kw_ref, vst_ref, vwt_ref, cc_ref):
    i = pl.program_id(1)
    G, HD = NSA_GROUPS, NSA_HD
    tm = x_ref.shape[0]
    lane = lax.broadcasted_iota(jnp.int32, (tm, G * 128), 1) % 128

    @pl.when(i == 0)
    def _():
        pad_rows = jnp.where(lane == HD + KEY_PAD, 1.0, 0.0).astype(BF16)
        ks_ref[...] = pad_rows
        kw_ref[...] = pad_rows
        vst_ref[...] = jnp.zeros_like(vst_ref)
        vwt_ref[...] = jnp.zeros_like(vwt_ref)

    @pl.when(i > 0)
    def _():
        h = _norm_mod(x_ref[...], g_ref[...], sh_ref[...], sc_ref[...]).astype(BF16)
        cc = _dot(h, wcc_ref[...])
        for q in range(cc_ref.shape[0]):
            cc_ref[q] = cc[:, q * 128:(q + 1) * 128]
        kk = _dot(h, wk_ref[...])
        ssq = _dot((kk * kk).astype(BF16), bd_ref[...])
        kn = kk * lax.rsqrt(ssq * (1.0 / HD) + RMS_EPS) * gk_ref[...]
        blk = (i - 1) * (tm // SLC_BLOCK) + lax.broadcasted_iota(
            jnp.int32, (tm, G * 128), 0) // SLC_BLOCK
        tail = jnp.where((lane == HD) | (lane == HD + 1) | (lane == HD + KEY_BLK + blk), 1.0, 0.0)
        ks_ref[...] = jnp.where(lane < HD, kn, tail).astype(BF16)
        kn_sw = jnp.concatenate(
            [pltpu.roll(kn[:, g * 128:(g + 1) * 128], HD, axis=1) for g in range(G)], axis=1)
        kw_ref[...] = jnp.where(lane < HD, kn_sw, tail).astype(BF16)
        vt = _dot_nt(wvt_ref[...], h)
        srow = lax.broadcasted_iota(jnp.int32, (VT_ROWS - HD, tm), 0)
        ones_blk = jnp.where(srow == 0, 1.0, 0.0).astype(BF16)
        for kind, o_ref in enumerate((vst_ref, vwt_ref)):
            for g in range(G):
                r0 = (kind * G + g) * HD
                o_ref[g * VT_ROWS:g * VT_ROWS + HD, :] = vt[r0:r0 + HD, :].astype(BF16)
                o_ref[g * VT_ROWS + HD:(g + 1) * VT_ROWS, :] = ones_blk


def _segment_ones(width, seg):
    idx = np.arange(width) // seg
    return (idx[:, None] == idx[None, :]).astype(np.float32)


def kv_proj(x2d, g, sh, sc, wk, wvt, wcc, gk, batch, seq, tm=KV_PAD):
    assert tm == KV_PAD and seq % tm == 0
    M, D = x2d.shape
    G = NSA_GROUPS
    nt = seq // tm
    sp = seq + KV_PAD
    ncs = wcc.shape[1] // 128
    bd = jnp.asarray(_segment_ones(G * 128, NSA_HD), dtype=BF16)
    xrow = lambda b, i: (b * nt + jnp.maximum(i - 1, 0), 0)
    bvec = pl.BlockSpec((None, 1, D), lambda b, i: (b, 0, 0))
    full = lambda a: pl.BlockSpec(a.shape, lambda b, i: (0,) * a.ndim)
    return pl.pallas_call(
        _kv_proj_body,
        grid=(batch, nt + 1),
        in_specs=[pl.BlockSpec((tm, D), xrow),
                  pl.BlockSpec((1, D), lambda b, i: (0, 0)),
                  bvec, bvec, full(wk), full(wvt), full(wcc), full(gk), full(bd)],
        out_specs=[pl.BlockSpec((None, tm, G * 128), lambda b, i: (b, i, 0)),
                   pl.BlockSpec((None, tm, G * 128), lambda b, i: (b, i, 0)),
                   pl.BlockSpec((None, G * VT_ROWS, tm), lambda b, i: (b, 0, i)),
                   pl.BlockSpec((None, G * VT_ROWS, tm), lambda b, i: (b, 0, i)),
                   pl.BlockSpec((ncs, tm, 128),
                                lambda b, i: (0, b * nt + jnp.maximum(i - 1, 0), 0))],
        out_shape=[jax.ShapeDtypeStruct((batch, sp, G * 128), BF16),
                   jax.ShapeDtypeStruct((batch, sp, G * 128), BF16),
                   jax.ShapeDtypeStruct((batch, G * VT_ROWS, sp), BF16),
                   jax.ShapeDtypeStruct((batch, G * VT_ROWS, sp), BF16),
                   jax.ShapeDtypeStruct((ncs, M, 128), F32)],
        compiler_params=_cparams(("parallel", "arbitrary")),
        name="kv_proj",
    )(x2d, g.reshape(1, D), sh, sc, wk, wvt, wcc, gk, bd)


def _gelu_tanh(x):
    c = math.sqrt(2.0 / math.pi)
    return 0.5 * x * (1.0 + jnp.tanh(c * (x + 0.044715 * (x * x * x))))


def _compress_body(cc0_ref, cc1_ref, cc2_ref, cc3_ref, pos_ref, w1_ref, w2_ref, w2t_ref, g_ref,
                   kc_ref, vct_ref):
    HD = NSA_HD
    nwin = kc_ref.shape[0]
    hid = w2_ref.shape[0]
    for p, cc_ref in enumerate((cc0_ref, cc1_ref, cc2_ref, cc3_ref)):
        kind = p // 2
        x = jnp.concatenate([cc_ref[pl.ds(i, nwin, stride=CMP_STRIDE), :]
                             for i in range(CMP_STRIDE)], axis=1)
        u = _dot((x + pos_ref[kind, 0:1, :]).astype(BF16), w1_ref[kind, 0])
        v = _dot((x + pos_ref[kind, 1:2, :]).astype(BF16), w1_ref[kind, 1])
        pre = u + pltpu.roll(v, nwin - 1, axis=0)
        for s in range(2):
            g = 2 * (p % 2) + s
            hmid = _gelu_tanh(pre[:, s * hid:(s + 1) * hid]).astype(BF16)
            if kind == 0:
                y = _dot(hmid, w2_ref[...])
                var = jnp.mean(y * y, axis=-1, keepdims=True)
                yn = y * lax.rsqrt(var + RMS_EPS) * g_ref[...]
                kc_ref[:, g * 128:g * 128 + HD] = yn.astype(BF16)
                kc_ref[:, g * 128 + HD:(g + 1) * 128] = jnp.zeros((nwin, 128 - HD), BF16)
            else:
                vct_ref[g * HD:(g + 1) * HD, :] = _dot_nt(w2t_ref[...], hmid).astype(BF16)


def _pair_expand(w):
    k, t, d, c = w.shape
    eye = jnp.eye(2, dtype=w.dtype)
    return jnp.einsum("ktdc,su->ktsduc", w, eye).reshape(k, t * 2 * d, 2 * c)


def compress(cc, pos, w1, w2k, w2vt, g, batch, seq):
    G, HD = NSA_GROUPS, NSA_HD
    nwin = seq // CMP_STRIDE
    hid = w1.shape[2]
    half = CMP_BLOCK // 2
    assert half == CMP_STRIDE
    w1r = w1.reshape(2, CMP_BLOCK, HD, hid)
    w1x = jnp.stack([_pair_expand(w1r[:, :half]), _pair_expand(w1r[:, half:])], axis=1).astype(BF16)
    posr = jnp.tile(pos.reshape(2, 2, half, 1, HD), (1, 1, 1, 2, 1)).reshape(2, 2, half * 2 * HD)
    resident = lambda a: pl.BlockSpec(a.shape, lambda b: (0,) * a.ndim, pipeline_mode=pl.Buffered(1))
    return pl.pallas_call(
        _compress_body,
        grid=(batch,),
        in_specs=[pl.BlockSpec((None, seq, 128), functools.partial(lambda q, b: (q, b, 0), q))
                  for q in range(4)]
                 + [resident(posr), resident(w1x), resident(w2k), resident(w2vt), resident(g)],
        out_specs=[pl.BlockSpec((None, nwin, G * 128), lambda b: (b, 0, 0)),
                   pl.BlockSpec((None, G * HD, nwin), lambda b: (b, 0, 0))],
        out_shape=[jax.ShapeDtypeStruct((batch, nwin, G * 128), BF16),
                   jax.ShapeDtypeStruct((batch, G * HD, nwin), BF16)],
        compiler_params=_cparams(("parallel",)),
        name="compress",
    )(cc, cc, cc, cc, posr, w1x, w2k, w2vt, g)


def _t5_bucket(dist):
    n = jnp.maximum(dist, 0)
    max_exact = REL_BUCKETS // 2
    nf = jnp.maximum(n, 1).astype(F32)
    large = max_exact + (jnp.log(nf / max_exact) / math.log(REL_MAX_DIST / max_exact)
                         * (REL_BUCKETS - max_exact)).astype(jnp.int32)
    large = jnp.minimum(large, REL_BUCKETS - 1)
    return jnp.where(n < max_exact, n, large)


def _table_lookup(bucket, tab_ref, h):
    out = jnp.zeros(bucket.shape, F32)
    for k in range(REL_BUCKETS):
        out = jnp.where(bucket == k, tab_ref[k, h], out)
    return out


def _bias_prep_body(tab_ref, bc_ref, tp_ref, far_ref):
    h = pl.program_id(0)
    S = bc_ref.shape[1]
    far = tab_ref[REL_BUCKETS - 1, h]
    far_ref[...] = jnp.full(far_ref.shape, far * LOG2E, F32)

    dist = lax.broadcasted_iota(jnp.int32, (8, S), 1)
    by_dist = _table_lookup(_t5_bucket(dist), tab_ref, h) * LOG2E
    shifted = pltpu.roll(jnp.broadcast_to(by_dist[0:1, :], (NCMP_PAD, S)), 0, axis=1,
                         stride=CMP_STRIDE, stride_axis=0)
    shifted = pltpu.roll(shifted, CMP_BLOCK - 1, axis=1)
    n = lax.broadcasted_iota(jnp.int32, (NCMP_PAD, S), 0)
    t = lax.broadcasted_iota(jnp.int32, (NCMP_PAD, S), 1)
    bc_ref[...] = jnp.where(t >= n * CMP_STRIDE + CMP_BLOCK - 1, shifted, NEG_INF)
    j = lax.broadcasted_iota(jnp.int32, (ATT_TK, ATT_QB), 0)
    i = lax.broadcasted_iota(jnp.int32, (ATT_TK, ATT_QB), 1)
    for d in range(2):
        dist = d * ATT_TK + i - j
        rel = (_table_lookup(_t5_bucket(dist), tab_ref, h) - far) * LOG2E
        tp_ref[d] = jnp.where(dist >= 0, rel, NEG_INF)
    tp_ref[2] = jnp.where(i < j, 0.0, NEG_INF)


def bias_prep(rel_table, seq):
    assert ATT_TK == ATT_QB and ATT_TK + 1 > 113
    return pl.pallas_call(
        _bias_prep_body,
        grid=(NSA_HEADS,),
        in_specs=[pl.BlockSpec(memory_space=pltpu.SMEM)],
        out_specs=[pl.BlockSpec((None, NCMP_PAD, seq), lambda h: (h, 0, 0)),
                   pl.BlockSpec((None, 3, ATT_TK, ATT_QB), lambda h: (h, 0, 0, 0)),
                   pl.BlockSpec((None, 8, 128), lambda h: (h, 0, 0))],
        out_shape=[jax.ShapeDtypeStruct((NSA_HEADS, NCMP_PAD, seq), F32),
                   jax.ShapeDtypeStruct((NSA_HEADS, 3, ATT_TK, ATT_QB), F32),
                   jax.ShapeDtypeStruct((NSA_HEADS, 8, 128), F32)],
        compiler_params=_cparams(("parallel",)),
        name="bias_prep",
    )(rel_table)


def _q_proj_body(x_ref, g_ref, sh_ref, sc_ref, wqt_ref, wgt_ref, bg_ref, gq_ref, qt_ref, gt_ref):
    HD = NSA_HD
    h = _norm_mod(x_ref[...], g_ref[...], sh_ref[...], sc_ref[...]).astype(BF16)
    gt_ref[...] = _dot_nt(wgt_ref[...], h) + bg_ref[...]
    qt = _dot_nt(wqt_ref[...], h)
    scale = gq_ref[...] * (HD ** -0.5 * LOG2E)
    for hh in range(NSA_HEADS):
        seg = qt[hh * HD:(hh + 1) * HD, :]
        var = jnp.mean(seg * seg, axis=0, keepdims=True)
        qt_ref[hh * HD:(hh + 1) * HD, :] = (seg * lax.rsqrt(var + RMS_EPS) * scale).astype(BF16)


def q_proj(x2d, g, sh, sc, wqt, wgt, bg, gq, batch, seq, tm=512):
    M, D = x2d.shape
    nt = seq // tm
    nq = wqt.shape[0]
    ng = wgt.shape[0]
    bvec = pl.BlockSpec((None, 1, D), lambda i: (i // nt, 0, 0))
    full = lambda a: pl.BlockSpec(a.shape, lambda i: (0,) * a.ndim)
    return pl.pallas_call(
        _q_proj_body,
        grid=(M // tm,),
        in_specs=[pl.BlockSpec((tm, D), lambda i: (i, 0)),
                  pl.BlockSpec((1, D), lambda i: (0, 0)),
                  bvec, bvec, full(wqt), full(wgt), full(bg), full(gq)],
        out_specs=[pl.BlockSpec((None, nq, tm), lambda i: (i // nt, 0, i % nt)),
                   pl.BlockSpec((None, ng, tm), lambda i: (i // nt, 0, i % nt))],
        out_shape=[jax.ShapeDtypeStruct((batch, nq, seq), BF16),
                   jax.ShapeDtypeStruct((batch, ng, seq), F32)],
        compiler_params=_cparams(("parallel",)),
        name="q_proj",
    )(x2d, g.reshape(1, D), sh, sc, wqt, wgt, bg, gq)


def _heads_on_lanes(pieces):
    return jnp.concatenate(pieces, axis=1)


def _nsa_body(gps, qt_ref, gt_ref, kc_ref, vct_ref, ks_ref, kw_ref, vst_ref, vwt_ref,
              bct_ref, tp_ref, far_ref, ovt_ref, out_ref):
    qb = pl.program_id(2)
    QB, HD, HPG = ATT_QB, NSA_HD, NSA_HPG
    R = HPG * QB
    t0 = pl.multiple_of(qb * QB, QB)
    near0 = pl.multiple_of(t0 + WINDOW - ATT_TK, ATT_TK)
    wlen = WINDOW + QB
    nblk = ovt_ref.shape[0]
    jb = lax.broadcasted_iota(jnp.int32, (nblk, QB), 0)
    jbf = jb.astype(F32)
    qid = jnp.right_shift(t0 + lax.broadcasted_iota(jnp.int32, (nblk, QB), 1), 6)
    forced = (jb == 0) | (jb == qid) | (jb == qid - 1)
    srow = lax.broadcasted_iota(jnp.int32, (8, R), 0)

    def finish(acc):
        return acc[0:HD, :] / acc[HD:HD + 1, :]

    def tile_part(s_tile, v_tile):
        m_t = jnp.max(s_tile, axis=0, keepdims=True)
        m_safe = jnp.where(m_t == NEG_INF, 0.0, m_t)
        p = jnp.exp2(s_tile - m_safe).astype(BF16)
        return m_t, _dot(v_tile, p)[0:HD + 8, :]

    def combine(parts):
        m_fin = parts[0][0]
        for m_t, _ in parts[1:]:
            m_fin = jnp.maximum(m_fin, m_t)
        acc = None
        for m_t, pv in parts:
            term = jnp.exp2(m_t - m_fin) * pv
            acc = term if acc is None else acc + term
        return m_fin, acc

    groups = range(gps)
    hs = [[gl * HPG + h for h in range(HPG)] for gl in groups]
    kcol = [slice(gl * 128, (gl + 1) * 128) for gl in groups]
    vrow = [slice(gl * VT_ROWS, (gl + 1) * VT_ROWS) for gl in groups]

    def stationary(q_all, hi_lo, block_rows):
        pad = jnp.where(srow == 0, MASK_BIG, 0.0)
        rest = jnp.zeros((128 - HD - KEY_PAD - 8, R), F32)
        tail = jnp.concatenate([hi_lo, block_rows, pad, rest], axis=0).astype(BF16)
        return jnp.concatenate([q_all, tail], axis=0)

    q_alls, hi_los, qms = [], [], []
    for gl in groups:
        q_alls.append(_heads_on_lanes([qt_ref[h * HD:(h + 1) * HD, :] for h in hs[gl]]))
        far = _heads_on_lanes([far_ref[h, 0:1, :] for h in hs[gl]])
        hi = far.astype(BF16).astype(F32)
        hi_los.append(jnp.where(srow == 0, hi, jnp.where(srow == 1, far - hi, 0.0)))
        qms.append(stationary(q_alls[gl], hi_los[gl], jnp.zeros((nblk, R), F32)))

    s_cs = [_dot(kc_ref[:, kcol[gl]], qms[gl]) for gl in groups]
    s_ws = [_dot(kw_ref[pl.ds(t0, wlen), kcol[gl]], qms[gl]) for gl in groups]
    s_ns = [_dot(ks_ref[pl.ds(near0, 2 * ATT_TK), kcol[gl]], qms[gl]) for gl in groups]

    o_cmps, scores = [], []
    for gl in groups:
        s_c = s_cs[gl] + _heads_on_lanes([bct_ref[h] for h in hs[gl]])
        m_c = jnp.max(s_c, axis=0, keepdims=True)
        m_c = jnp.where(m_c == NEG_INF, 0.0, m_c)
        e_c = jnp.exp2(s_c - m_c)
        p_c = e_c / jnp.maximum(jnp.sum(e_c, axis=0, keepdims=True), jnp.finfo(F32).tiny)
        o_cmps.append(_dot(vct_ref[gl * HD:(gl + 1) * HD, :], p_c.astype(BF16)))
        p_sum = p_c[:, 0:QB]
        for h in range(1, HPG):
            p_sum = p_sum + p_c[:, h * QB:(h + 1) * QB]
        p_hi = p_sum.astype(BF16)
        p_lo = (p_sum - p_hi.astype(F32)).astype(BF16)
        imp = _dot(ovt_ref[...], jnp.concatenate([p_hi, p_lo], axis=0))
        score = jnp.where(forced, FORCE_SCORE, imp)
        scores.append(jnp.where(jb <= qid, score, NEG_INF))

    nwt = wlen // ATT_TK
    o_wins = []
    for gl in groups:
        parts = []
        for i in range(nwt):
            d = nwt - 1 - i
            s_t = s_ws[gl][i * ATT_TK:(i + 1) * ATT_TK]
            if d in (0, 1):
                s_t = s_t + _heads_on_lanes([tp_ref[h, d] for h in hs[gl]])
            elif d == nwt - 1:
                s_t = s_t + _heads_on_lanes([tp_ref[h, 2] for h in hs[gl]])
            parts.append(tile_part(s_t, vwt_ref[vrow[gl], pl.ds(t0 + i * ATT_TK, ATT_TK)]))
        o_wins.append(finish(combine(parts)[1]))

    msels = [jnp.full((nblk, QB), NEG_INF, F32) for _ in groups]
    for _ in range(SLC_TOPK):
        for gl in groups:
            mx = jnp.max(scores[gl], axis=0, keepdims=True)
            first = jnp.min(jnp.where(scores[gl] == mx, jbf, float(nblk)), axis=0, keepdims=True)
            pick = jbf == first
            msels[gl] = jnp.where(pick & (mx > NEG_INF), 0.0, msels[gl])
            scores[gl] = jnp.where(pick, NEG_INF, scores[gl])
    carry0 = []
    for gl in groups:
        prev_mask = jnp.concatenate(
            [jnp.broadcast_to(jnp.max(jnp.where(jb == 2 * qb - 2 + r, msels[gl], NEG_INF), axis=0,
                                      keepdims=True), (SLC_BLOCK, QB)) for r in range(2)], axis=0)
        s_prev = s_ns[gl][0:ATT_TK] + _heads_on_lanes([tp_ref[h, 1] + prev_mask for h in hs[gl]])
        s_diag = s_ns[gl][ATT_TK:] + _heads_on_lanes([tp_ref[h, 0] for h in hs[gl]])
        carry0.append(combine([
            tile_part(s_diag, vst_ref[vrow[gl], pl.ds(near0 + ATT_TK, ATT_TK)]),
            tile_part(s_prev, vst_ref[vrow[gl], pl.ds(near0, ATT_TK)])]))

    qss = [stationary(q_alls[gl], hi_los[gl], _heads_on_lanes(
        [jnp.where(msels[gl] == 0.0, 0.0, MASK_BIG)] * HPG)) for gl in groups]

    def far_step(c, carry):
        tile0 = qb + (WINDOW - ATT_TK) // ATT_TK - (c + 1) * (FAR_CHUNK // ATT_TK)
        row0 = pl.multiple_of(tile0 * ATT_TK, ATT_TK)
        s_fs = [_dot(ks_ref[pl.ds(row0, FAR_CHUNK), kcol[gl]], qss[gl]) for gl in groups]
        out = []
        for gl in groups:
            parts = [carry[gl]]
            for j in range(FAR_CHUNK // ATT_TK):
                parts.append(tile_part(s_fs[gl][j * ATT_TK:(j + 1) * ATT_TK],
                                       vst_ref[vrow[gl], pl.ds(row0 + j * ATT_TK, ATT_TK)]))
            out.append(combine(parts))
        return tuple(out)

    n_far = (qb + 2) // 4
    sel = lax.fori_loop(0, n_far, far_step, tuple(carry0))

    gates = jax.nn.sigmoid(gt_ref[...])
    for gl in groups:
        o_cmp, o_win = o_cmps[gl], o_wins[gl]
        o_sel = finish(sel[gl][1])
        outs = []
        for h in range(HPG):
            lanes = slice(h * QB, (h + 1) * QB)
            r = gl * 16 + 3 * h
            outs.append(gates[r:r + 1, :] * o_cmp[:, lanes]
                        + gates[r + 1:r + 2, :] * o_sel[:, lanes]
                        + gates[r + 2:r + 3, :] * o_win[:, lanes])
        for pair in range(HPG // 2):
            two = jnp.concatenate(outs[2 * pair:2 * pair + 2], axis=0)
            c0 = gl * HPG * HD + pair * 2 * HD
            out_ref[:, c0:c0 + 2 * HD] = two.T.astype(out_ref.dtype)


def _overlap_matrix_t(nblk):
    start = np.arange(NCMP_PAD) * CMP_STRIDE
    sj = np.arange(nblk) * SLC_BLOCK
    ov = (np.minimum(start[None, :] + CMP_BLOCK, sj[:, None] + SLC_BLOCK)
          - np.maximum(start[None, :], sj[:, None]))
    ov = np.clip(ov, 0, None) / CMP_BLOCK
    ov[:, NCMP_PAD - 1] = 0.0
    return np.concatenate([ov, ov], axis=1).astype(np.float32)


def nsa_attn(qt, gt, kc, vct, ks, kw, vst, vwt, bias_ct, tiles, far, batch, seq, gps=ATT_GPS):
    QB, HD, HPG, G = ATT_QB, NSA_HD, NSA_HPG, NSA_GROUPS
    nblk = seq // SLC_BLOCK
    assert nblk % 8 == 0 and seq % QB == 0 and seq // CMP_STRIDE == NCMP_PAD and G % gps == 0
    assert WINDOW % ATT_TK == 0 and FAR_CHUNK == 4 * ATT_TK and KV_PAD >= FAR_CHUNK - ATT_TK
    assert KEY_BLK % 8 == 0 and KEY_BLK + nblk <= KEY_PAD and KEY_PAD + 8 <= 128 - HD
    nq = seq // QB
    sp = seq + KV_PAD
    ovt = jnp.asarray(_overlap_matrix_t(nblk), dtype=BF16)
    return pl.pallas_call(
        functools.partial(_nsa_body, gps),
        grid=(batch, G // gps, nq),
        in_specs=[pl.BlockSpec((None, gps * HPG * HD, QB), lambda b, g, i: (b, g, i)),
                  pl.BlockSpec((None, gps * 16, QB), lambda b, g, i: (b, g, i)),
                  pl.BlockSpec((None, NCMP_PAD, gps * 128), lambda b, g, i: (b, 0, g)),
                  pl.BlockSpec((None, gps * HD, NCMP_PAD), lambda b, g, i: (b, g, 0)),
                  pl.BlockSpec((None, sp, gps * 128), lambda b, g, i: (b, 0, g)),
                  pl.BlockSpec((None, sp, gps * 128), lambda b, g, i: (b, 0, g)),
                  pl.BlockSpec((None, gps * VT_ROWS, sp), lambda b, g, i: (b, g, 0)),
                  pl.BlockSpec((None, gps * VT_ROWS, sp), lambda b, g, i: (b, g, 0)),
                  pl.BlockSpec((gps * HPG, NCMP_PAD, QB), lambda b, g, i: (g, 0, i)),
                  pl.BlockSpec((gps * HPG, 3, ATT_TK, QB), lambda b, g, i: (g, 0, 0, 0)),
                  pl.BlockSpec((gps * HPG, 8, 128), lambda b, g, i: (g, 0, 0)),
                  pl.BlockSpec((nblk, 2 * NCMP_PAD), lambda b, g, i: (0, 0))],
        out_specs=pl.BlockSpec((QB, gps * HPG * HD), lambda b, g, i: (b * nq + i, g)),
        out_shape=jax.ShapeDtypeStruct((batch * seq, NSA_HEADS * HD), BF16),
        compiler_params=_cparams(("parallel", "parallel", "arbitrary")),
        name="nsa_attn",
    )(qt, gt, kc, vct, ks, kw, vst, vwt, bias_ct, tiles, far, ovt)


def _mlstm_layer(x2d, g_mix, sh, sc, w_in, b_if, g_out, w_out, batch, seq):
    nbig = 2 * ML_HEADS * ML_DQK + 2 * ML_HEADS * ML_DV
    ng = 2 * ML_HEADS
    wg = jnp.pad(w_in[:, nbig:], ((0, 0), (0, 128 - ng))).astype(BF16)
    bg = jnp.pad(b_if, (0, 128 - ng))
    proj, gates = norm_proj(x2d, g_mix, sh, sc, [w_in[:, :nbig].astype(BF16), wg], [None, bg],
                            [BF16, F32])
    return mlstm_core(proj, gates, g_out, batch, seq), w_out.astype(BF16)


def _nsa_shared(x2d, g_kv, kv_sh, kv_sc, w_kv, pos_k, w_k1, w_k2, pos_v, w_v1, w_v2,
                g_knorm, batch, seq):
    gw = NSA_GROUPS * NSA_HD
    part = lambda i: w_kv[:, i * gw:(i + 1) * gw]
    hd = NSA_HD
    wk = jnp.concatenate([p[:, g * hd:(g + 1) * hd] for g in range(NSA_GROUPS)
                          for p in (part(2), part(4))], axis=1).astype(BF16)
    gk = jnp.tile(jnp.concatenate([g_knorm[1], g_knorm[2]]), NSA_GROUPS).reshape(1, -1)
    wvt = jnp.concatenate([part(3), part(5)], axis=1).T.astype(BF16)
    wcc = jnp.concatenate([part(0), part(1)], axis=1).astype(BF16)
    ks, kw, vst, vwt, cc = kv_proj(x2d, g_kv, kv_sh, kv_sc, wk, wvt, wcc, gk, batch, seq)
    pos = jnp.stack([pos_k, pos_v])
    w1 = jnp.stack([w_k1, w_v1])
    kc, vct = compress(cc, pos, w1, w_k2.astype(BF16), w_v2.T.astype(BF16), g_knorm[0:1],
                       batch, seq)
    return kc, vct, ks, kw, vst, vwt


def _gate_weights_t(w_q, b_gate):
    nq = NSA_HEADS * NSA_HD
    per = 3 * NSA_HPG
    wg = w_q[:, nq:].T.reshape(NSA_GROUPS, per, -1)
    wg = jnp.pad(wg, ((0, 0), (0, 16 - per), (0, 0))).reshape(NSA_GROUPS * 16, -1)
    bg = jnp.pad(b_gate.reshape(NSA_GROUPS, per), ((0, 0), (0, 16 - per))).reshape(-1, 1)
    return wg.astype(BF16), bg


def _nsa_layer(x2d, g_mix, sh, sc, shared, w_q, b_gate, g_qnorm, w_out, bias_ct, tiles, far,
               batch, seq):
    nq = NSA_HEADS * NSA_HD
    wgt, bg = _gate_weights_t(w_q, b_gate)
    qt, gt = q_proj(x2d, g_mix, sh, sc, w_q[:, :nq].T.astype(BF16), wgt, bg,
                    g_qnorm.reshape(NSA_HD, 1), batch, seq)
    kc, vct, ks, kw, vst, vwt = shared
    att = nsa_attn(qt, gt, kc, vct, ks, kw, vst, vwt, bias_ct, tiles, far, batch, seq)
    return att, w_out.astype(BF16)


def kernel(x, c, w_ada, b_ada, g_norm_mix, g_norm_ffn, w_ffn_in, w_ffn_out, w_a_in, b_a_if, g_a_out, w_a_out, w_kv_ada, b_kv_ada, g_kv_norm, w_kv, pos_cmp_k, w_cmp_k1, w_cmp_k2, pos_cmp_v, w_cmp_v1, w_cmp_v2, g_knorm, w_b_q, b_b_gate, g_qnorm, w_b_out, rel_table):
    B, S, D = x.shape
    depth = w_ada.shape[0]
    n_a = w_a_in.shape[0]
    x2d = x.reshape(B * S, D)
    mods = ada_mod(c, w_ada, b_ada)
    kv_mod = ada_mod(c, w_kv_ada[None], b_kv_ada[None])[0]
    shared = None
    bias_ct = tiles = far = None
    for layer in range(depth):
        sh1, sc1, ga1, sh2, sc2, ga2 = [mods[layer, :, i * D:(i + 1) * D].reshape(B, 1, D)
                                        for i in range(6)]
        if layer < n_a:
            mixed, w_mix = _mlstm_layer(x2d, g_norm_mix[layer], sh1, sc1, w_a_in[layer],
                                        b_a_if[layer], g_a_out[layer], w_a_out[layer], B, S)
        else:
            j = layer - n_a
            if shared is None:
                kv_sh = kv_mod[:, :D].reshape(B, 1, D)
                kv_sc = kv_mod[:, D:].reshape(B, 1, D)
                shared = _nsa_shared(x2d, g_kv_norm, kv_sh, kv_sc, w_kv, pos_cmp_k, w_cmp_k1,
                                     w_cmp_k2, pos_cmp_v, w_cmp_v1, w_cmp_v2, g_knorm, B, S)
                bias_ct, tiles, far = bias_prep(rel_table, S)
            mixed, w_mix = _nsa_layer(x2d, g_norm_mix[layer], sh1, sc1, shared, w_b_q[j],
                                      b_b_gate[j], g_qnorm[j], w_b_out[j], bias_ct, tiles, far, B, S)
        x2d = mix_ffn(mixed, w_mix, x2d, ga1, g_norm_ffn[layer], sh2, sc2, ga2,
                      w_ffn_in[layer].astype(BF16), w_ffn_out[layer].astype(BF16))
    return x2d.reshape(B, S, D)
```

```python
import functools
import math

import jax
import jax.numpy as jnp
import numpy as np
from jax import lax
from jax.experimental import pallas as pl
from jax.experimental.pallas import tpu as pltpu

F32 = jnp.float32
BF16 = jnp.bfloat16
NEG_INF = float("-inf")
LOG2E = math.log2(math.e)

RMS_EPS = 1e-6

ML_HEADS = 4
ML_DQK = 128
ML_DV = 256
ML_LC = 256

NSA_HEADS = 16
NSA_GROUPS = 4
NSA_HPG = 4
NSA_HD = 64
CMP_BLOCK = 32
CMP_STRIDE = 16
SLC_BLOCK = 64
SLC_TOPK = 8
WINDOW = 512
FORCE_SCORE = 1e4
REL_BUCKETS = 32
REL_MAX_DIST = 128
ATT_QB = 128
ATT_TK = 128
ATT_GPS = 4
NCMP_PAD = 128
KV_PAD = WINDOW
FAR_CHUNK = 512
VT_ROWS = 80
KEY_BLK = 8
KEY_PAD = 40
MASK_BIG = -32768.0

VMEM_LIMIT = 56 * 1024 * 1024


def _cparams(sem):
    return pltpu.CompilerParams(dimension_semantics=sem, vmem_limit_bytes=VMEM_LIMIT)


def _dot(a, b):
    return jnp.dot(a, b, preferred_element_type=F32)


def _dot_nt(a, b):
    return lax.dot_general(a, b, (((1,), (1,)), ((), ())), preferred_element_type=F32)


def _dot_tn(a, b):
    return lax.dot_general(a, b, (((0,), (0,)), ((), ())), preferred_element_type=F32)


def _norm_mod(x, g, sh, sc):
    var = jnp.mean(x * x, axis=-1, keepdims=True)
    y = x * lax.rsqrt(var + RMS_EPS) * g
    return y * (1.0 + sc) + sh


def _ada_body(c_ref, w_ref, b_ref, o_ref):
    c = c_ref[...]
    ca = c * jax.nn.sigmoid(c)
    nb = ca.shape[0]
    a_hi = ca.astype(BF16).astype(F32)
    a_mid = (ca - a_hi).astype(BF16).astype(F32)
    a_lo = ca - a_hi - a_mid
    a3 = jnp.concatenate([a_hi, a_mid, a_lo, jnp.zeros_like(ca)], axis=0).astype(BF16)
    w = w_ref[...]
    w_hi = w.astype(BF16)
    w_lo = (w - w_hi.astype(F32)).astype(BF16)
    p = _dot(a3, w_hi)
    q = _dot(a3[0:2 * nb], w_lo)
    o_ref[...] = (p[0:nb] + p[nb:2 * nb] + p[2 * nb:3 * nb] + q[0:nb] + q[nb:2 * nb]) + b_ref[...]


def ada_mod(c, w, b, tn=1024):
    L, D, N = w.shape
    B = c.shape[0]
    return pl.pallas_call(
        _ada_body,
        grid=(L, N // tn),
        in_specs=[pl.BlockSpec((B, D), lambda l, j: (0, 0)),
                  pl.BlockSpec((None, D, tn), lambda l, j: (l, 0, j)),
                  pl.BlockSpec((None, 1, tn), lambda l, j: (l, 0, j))],
        out_specs=pl.BlockSpec((None, B, tn), lambda l, j: (l, 0, j)),
        out_shape=jax.ShapeDtypeStruct((L, B, N), F32),
        compiler_params=_cparams(("parallel", "parallel")),
        name="ada_mod",
    )(c, w, b.reshape(L, 1, N))


def _norm_proj_body(n_out, has_bias, tn, x_ref, g_ref, sh_ref, sc_ref, *refs):
    w_refs = refs[:n_out]
    b_refs = refs[n_out:2 * n_out]
    o_refs = refs[2 * n_out:3 * n_out]
    h = _norm_mod(x_ref[...], g_ref[...], sh_ref[...], sc_ref[...]).astype(BF16)
    for w_ref, b_ref, o_ref, hb in zip(w_refs, b_refs, o_refs, has_bias):
        n = w_ref.shape[1]
        step = min(tn, n)
        for n0 in range(0, n, step):
            acc = _dot(h, w_ref[:, n0:n0 + step].astype(BF16))
            if hb:
                acc = acc + b_ref[:, n0:n0 + step]
            o_ref[:, n0:n0 + step] = acc.astype(o_ref.dtype)


def norm_proj(x2d, g, sh, sc, ws, biases, out_dtypes, tm=512, tn=512):
    M, D = x2d.shape
    B = sh.shape[0]
    tiles_per_batch = (M // B) // tm
    n_out = len(ws)
    has_bias = tuple(b is not None for b in biases)
    bias_args = [(b if b is not None else jnp.zeros((w.shape[1],), F32)).reshape(1, -1)
                 for b, w in zip(biases, ws)]
    in_specs = [pl.BlockSpec((tm, D), lambda i: (i, 0)),
                pl.BlockSpec((1, D), lambda i: (0, 0)),
                pl.BlockSpec((None, 1, D), lambda i: (i // tiles_per_batch, 0, 0)),
                pl.BlockSpec((None, 1, D), lambda i: (i // tiles_per_batch, 0, 0))]
    resident = lambda a: pl.BlockSpec(a.shape, lambda i: (0, 0), pipeline_mode=pl.Buffered(1))
    in_specs += [resident(w) for w in ws] + [resident(b) for b in bias_args]
    out_specs = [pl.BlockSpec((tm, w.shape[1]), lambda i: (i, 0)) for w in ws]
    out_shape = [jax.ShapeDtypeStruct((M, w.shape[1]), dt) for w, dt in zip(ws, out_dtypes)]
    return pl.pallas_call(
        functools.partial(_norm_proj_body, n_out, has_bias, tn),
        grid=(M // tm,),
        in_specs=in_specs, out_specs=out_specs, out_shape=out_shape,
        compiler_params=_cparams(("parallel",)),
        name="norm_proj",
    )(x2d, g.reshape(1, D), sh, sc, *ws, *bias_args)


def _mix_ffn_body(tf, a_ref, wm_ref, x_ref, ga1_ref, g_ref, sh_ref, sc_ref, ga2_ref,
                  wi_ref, wo_ref, o_ref, act_s):
    F = wo_ref.shape[0]
    x1 = x_ref[...] + ga1_ref[...] * _dot(a_ref[...], wm_ref[...].astype(BF16))
    h = _norm_mod(x1, g_ref[...], sh_ref[...], sc_ref[...]).astype(BF16)
    for f0 in range(0, F, tf):
        gate = _dot(h, wi_ref[:, f0:f0 + tf].astype(BF16))
        up = _dot(h, wi_ref[:, F + f0:F + f0 + tf].astype(BF16))
        act_s[:, f0:f0 + tf] = (gate * jax.nn.sigmoid(gate) * up).astype(BF16)
    y = _dot(act_s[:, 0:tf], wo_ref[0:tf, :].astype(BF16))
    for f0 in range(tf, F, tf):
        y = y + _dot(act_s[:, f0:f0 + tf], wo_ref[f0:f0 + tf, :].astype(BF16))
    o_ref[...] = x1 + ga2_ref[...] * y


def mix_ffn(a, w_mix, x2d, ga1, g, sh, sc, ga2, w_in, w_out, tm=512, tf=256):
    M, D = x2d.shape
    K = a.shape[1]
    F = w_out.shape[0]
    B = sh.shape[0]
    tiles_per_batch = (M // B) // tm
    bvec = pl.BlockSpec((None, 1, D), lambda i: (i // tiles_per_batch, 0, 0))
    resident = lambda w: pl.BlockSpec(w.shape, lambda i: (0, 0), pipeline_mode=pl.Buffered(1))
    return pl.pallas_call(
        functools.partial(_mix_ffn_body, tf),
        grid=(M // tm,),
        in_specs=[pl.BlockSpec((tm, K), lambda i: (i, 0)),
                  resident(w_mix),
                  pl.BlockSpec((tm, D), lambda i: (i, 0)),
                  bvec,
                  pl.BlockSpec((1, D), lambda i: (0, 0)),
                  bvec, bvec, bvec,
                  resident(w_in), resident(w_out)],
        out_specs=pl.BlockSpec((tm, D), lambda i: (i, 0)),
        out_shape=jax.ShapeDtypeStruct((M, D), F32),
        scratch_shapes=[pltpu.VMEM((tm, F), BF16)],
        compiler_params=_cparams(("parallel",)),
        name="mix_ffn",
    )(a, w_mix, x2d, ga1, g.reshape(1, D), sh, sc, ga2, w_in, w_out)


def _sublane_scan(x, op, fill):
    n = x.shape[0]
    row = lax.broadcasted_iota(jnp.int32, x.shape, 0)
    sh = 1
    while sh < n:
        x = op(x, jnp.where(row >= sh, pltpu.roll(x, sh, axis=0), fill))
        sh *= 2
    return x


def _mlstm_body(q_ref, k_ref, v_ref, o_ref, gc_ref, gout_ref, out_ref, c_s, n_s, m_s):
    c_idx = pl.program_id(1)
    LC = q_ref.shape[0]
    NH = ML_HEADS
    scale = ML_DQK ** -0.5
    log_scale = math.log(scale)

    @pl.when(c_idx == 0)
    def _():
        c_s[...] = jnp.zeros_like(c_s)
        n_s[...] = jnp.zeros_like(n_s)
        m_s[...] = jnp.zeros_like(m_s)

    gc = gc_ref[...]
    logf = jnp.minimum(gc, 0.0) - jnp.log1p(jnp.exp(-jnp.abs(gc)))
    bcum = pltpu.roll(_sublane_scan(logf, jnp.add, 0.0), 128 - NH, axis=1)
    a_c = gc - bcum
    cmax = _sublane_scan(a_c, jnp.maximum, NEG_INF)
    a_t = a_c.T

    row = lax.broadcasted_iota(jnp.int32, (LC, LC), 0)
    col = lax.broadcasted_iota(jnp.int32, (LC, LC), 1)
    causal = col <= row
    ones_col = jnp.ones((LC, 128), BF16)

    def lanes2(x):
        return jnp.concatenate([x, x], axis=1)

    for h in range(NH):
        a_rep = jnp.broadcast_to(a_c[:, h:h + 1], (LC, 128))
        b_rep = jnp.broadcast_to(bcum[:, h:h + 1], (LC, 128))
        cm_rep = jnp.broadcast_to(cmax[:, h:h + 1], (LC, 128))
        a_r = a_t[h:h + 1, :]
        b_last = b_rep[LC - 1:LC, :]
        m_prev = m_s[h:h + 1, :]
        m_new = jnp.maximum(b_last + m_prev, b_last + cm_rep[LC - 1:LC, :])
        decay = jnp.exp(b_last + m_prev - m_new)
        e_rep = jnp.exp(b_last + a_rep - m_new)
        g_rep = jnp.maximum(m_prev, cm_rep)
        w_intra = jnp.exp(jnp.where(causal, a_r - lanes2(g_rep) + log_scale, NEG_INF))
        w_inter = jnp.exp(m_prev - g_rep) * scale
        floor = jnp.exp(-(b_rep + g_rep))

        qh = q_ref[:, h * ML_DQK:(h + 1) * ML_DQK]
        kh = k_ref[:, h * ML_DQK:(h + 1) * ML_DQK]
        vh = v_ref[:, h * ML_DV:(h + 1) * ML_DV]
        c_prev = c_s[h]
        n_prev = n_s[h:h + 1, :]

        s = (_dot_nt(qh, kh) * w_intra).astype(BF16)
        inter = _dot(qh, c_prev.astype(BF16))
        num = _dot(s, vh) + lanes2(w_inter) * inter
        n_rows = jnp.broadcast_to(n_prev, (128, ML_DQK)).astype(BF16)
        qn = _dot(s, ones_col) + w_inter * _dot_nt(qh, n_rows)
        inv = 1.0 / jnp.maximum(jnp.abs(qn), floor)

        ke = kh.astype(F32) * e_rep
        c_s[h] = lanes2(decay) * c_prev + _dot_tn(ke.astype(BF16), vh)
        n_s[h:h + 1, :] = decay * n_prev + jnp.sum(ke, axis=0, keepdims=True)
        m_s[h:h + 1, :] = m_new

        ssq = _dot((num * num).astype(BF16), jnp.ones((ML_DV, 128), BF16))
        rs = lax.rsqrt(ssq * (inv * inv) * (1.0 / ML_DV) + RMS_EPS) * inv
        hn = num * lanes2(rs) * gout_ref[:, h * ML_DV:(h + 1) * ML_DV]
        og = jax.nn.sigmoid(o_ref[:, h * ML_DV:(h + 1) * ML_DV].astype(F32))
        out_ref[:, h * ML_DV:(h + 1) * ML_DV] = (hn * og).astype(out_ref.dtype)


def mlstm_core(proj, gates, g_out, batch, seq):
    LC = ML_LC
    nc = seq // LC
    qk = ML_HEADS * ML_DQK
    vd = ML_HEADS * ML_DV
    row = lambda b, c: b * nc + c
    return pl.pallas_call(
        _mlstm_body,
        grid=(batch, nc),
        in_specs=[pl.BlockSpec((LC, qk), lambda b, c: (row(b, c), 0)),
                  pl.BlockSpec((LC, qk), lambda b, c: (row(b, c), 1)),
                  pl.BlockSpec((LC, vd), lambda b, c: (row(b, c), 1)),
                  pl.BlockSpec((LC, vd), lambda b, c: (row(b, c), 2)),
                  pl.BlockSpec((LC, 128), lambda b, c: (row(b, c), 0)),
                  pl.BlockSpec((1, vd), lambda b, c: (0, 0))],
        out_specs=pl.BlockSpec((LC, vd), lambda b, c: (row(b, c), 0)),
        out_shape=jax.ShapeDtypeStruct((batch * seq, vd), BF16),
        scratch_shapes=[pltpu.VMEM((ML_HEADS, ML_DQK, ML_DV), F32),
                        pltpu.VMEM((8, ML_DQK), F32),
                        pltpu.VMEM((8, 128), F32)],
        compiler_params=_cparams(("parallel", "arbitrary")),
        name="mlstm_core",
    )(proj, proj, proj, proj, gates, g_out.reshape(1, vd))


def _kv_proj_body(x_ref, g_ref, sh_ref, sc_ref, wk_ref, wvt_ref, wcc_ref, gk_ref, bd_ref,
                  ks_ref, kw_ref, vst_ref, vwt_ref, cc_ref):
    i = pl.program_id(1)
    G, HD = NSA_GROUPS, NSA_HD
    tm = x_ref.shape[0]
    lane = lax.broadcasted_iota(jnp.int32, (tm, G * 128), 1) % 128

    @pl.when(i == 0)
    def _():
        pad_rows = jnp.where(lane == HD + KEY_PAD, 1.0, 0.0).astype(BF16)
        ks_ref[...] = pad_rows
        kw_ref[...] = pad_rows
        vst_ref[...] = jnp.zeros_like(vst_ref)
        vwt_ref[...] = jnp.zeros_like(vwt_ref)

    @pl.when(i > 0)
    def _():
        h = _norm_mod(x_ref[...], g_ref[...], sh_ref[...], sc_ref[...]).astype(BF16)
        cc = _dot(h, wcc_ref[...])
        for q in range(cc_ref.shape[0]):
            cc_ref[q] = cc[:, q * 128:(q + 1) * 128]
        kk = _dot(h, wk_ref[...])
        ssq = _dot((kk * kk).astype(BF16), bd_ref[...])
        kn = kk * lax.rsqrt(ssq * (1.0 / HD) + RMS_EPS) * gk_ref[...]
        blk = (i - 1) * (tm // SLC_BLOCK) + lax.broadcasted_iota(
            jnp.int32, (tm, G * 128), 0) // SLC_BLOCK
        tail = jnp.where((lane == HD) | (lane == HD + 1) | (lane == HD + KEY_BLK + blk), 1.0, 0.0)
        ks_ref[...] = jnp.where(lane < HD, kn, tail).astype(BF16)
        kn_sw = jnp.concatenate(
            [pltpu.roll(kn[:, g * 128:(g + 1) * 128], HD, axis=1) for g in range(G)], axis=1)
        kw_ref[...] = jnp.where(lane < HD, kn_sw, tail).astype(BF16)
        vt = _dot_nt(wvt_ref[...], h)
        srow = lax.broadcasted_iota(jnp.int32, (VT_ROWS - HD, tm), 0)
        ones_blk = jnp.where(srow == 0, 1.0, 0.0).astype(BF16)
        for kind, o_ref in enumerate((vst_ref, vwt_ref)):
            for g in range(G):
                r0 = (kind * G + g) * HD
                o_ref[g * VT_ROWS:g * VT_ROWS + HD, :] = vt[r0:r0 + HD, :].astype(BF16)
                o_ref[g * VT_ROWS + HD:(g + 1) * VT_ROWS, :] = ones_blk


def _segment_ones(width, seg):
    idx = np.arange(width) // seg
    return (idx[:, None] == idx[None, :]).astype(np.float32)


def kv_proj(x2d, g, sh, sc, wk, wvt, wcc, gk, batch, seq, tm=KV_PAD):
    assert tm == KV_PAD and seq % tm == 0
    M, D = x2d.shape
    G = NSA_GROUPS
    nt = seq // tm
    sp = seq + KV_PAD
    ncs = wcc.shape[1] // 128
    bd = jnp.asarray(_segment_ones(G * 128, NSA_HD), dtype=BF16)
    xrow = lambda b, i: (b * nt + jnp.maximum(i - 1, 0), 0)
    bvec = pl.BlockSpec((None, 1, D), lambda b, i: (b, 0, 0))
    full = lambda a: pl.BlockSpec(a.shape, lambda b, i: (0,) * a.ndim)
    return pl.pallas_call(
        _kv_proj_body,
        grid=(batch, nt + 1),
        in_specs=[pl.BlockSpec((tm, D), xrow),
                  pl.BlockSpec((1, D), lambda b, i: (0, 0)),
                  bvec, bvec, full(wk), full(wvt), full(wcc), full(gk), full(bd)],
        out_specs=[pl.BlockSpec((None, tm, G * 128), lambda b, i: (b, i, 0)),
                   pl.BlockSpec((None, tm, G * 128), lambda b, i: (b, i, 0)),
                   pl.BlockSpec((None, G * VT_ROWS, tm), lambda b, i: (b, 0, i)),
                   pl.BlockSpec((None, G * VT_ROWS, tm), lambda b, i: (b, 0, i)),
                   pl.BlockSpec((ncs, tm, 128),
                                lambda b, i: (0, b * nt + jnp.maximum(i - 1, 0), 0))],
        out_shape=[jax.ShapeDtypeStruct((batch, sp, G * 128), BF16),
                   jax.ShapeDtypeStruct((batch, sp, G * 128), BF16),
                   jax.ShapeDtypeStruct((batch, G * VT_ROWS, sp), BF16),
                   jax.ShapeDtypeStruct((batch, G * VT_ROWS, sp), BF16),
                   jax.ShapeDtypeStruct((ncs, M, 128), F32)],
        compiler_params=_cparams(("parallel", "arbitrary")),
        name="kv_proj",
    )(x2d, g.reshape(1, D), sh, sc, wk, wvt, wcc, gk, bd)


def _gelu_tanh(x):
    c = math.sqrt(2.0 / math.pi)
    return 0.5 * x * (1.0 + jnp.tanh(c * (x + 0.044715 * (x * x * x))))


def _compress_body(cc0_ref, cc1_ref, cc2_ref, cc3_ref, pos_ref, w1_ref, w2_ref, w2t_ref, g_ref,
                   kc_ref, vct_ref):
    HD = NSA_HD
    nwin = kc_ref.shape[0]
    hid = w2_ref.shape[0]
    for p, cc_ref in enumerate((cc0_ref, cc1_ref, cc2_ref, cc3_ref)):
        kind = p // 2
        x = jnp.concatenate([cc_ref[pl.ds(i, nwin, stride=CMP_STRIDE), :]
                             for i in range(CMP_STRIDE)], axis=1)
        u = _dot((x + pos_ref[kind, 0:1, :]).astype(BF16), w1_ref[kind, 0])
        v = _dot((x + pos_ref[kind, 1:2, :]).astype(BF16), w1_ref[kind, 1])
        pre = u + pltpu.roll(v, nwin - 1, axis=0)
        for s in range(2):
            g = 2 * (p % 2) + s
            hmid = _gelu_tanh(pre[:, s * hid:(s + 1) * hid]).astype(BF16)
            if kind == 0:
                y = _dot(hmid, w2_ref[...])
                var = jnp.mean(y * y, axis=-1, keepdims=True)
                yn = y * lax.rsqrt(var + RMS_EPS) * g_ref[...]
                kc_ref[:, g * 128:g * 128 + HD] = yn.astype(BF16)
                kc_ref[:, g * 128 + HD:(g + 1) * 128] = jnp.zeros((nwin, 128 - HD), BF16)
            else:
                vct_ref[g * HD:(g + 1) * HD, :] = _dot_nt(w2t_ref[...], hmid).astype(BF16)


def _pair_expand(w):
    k, t, d, c = w.shape
    eye = jnp.eye(2, dtype=w.dtype)
    return jnp.einsum("ktdc,su->ktsduc", w, eye).reshape(k, t * 2 * d, 2 * c)


def compress(cc, pos, w1, w2k, w2vt, g, batch, seq):
    G, HD = NSA_GROUPS, NSA_HD
    nwin = seq // CMP_STRIDE
    hid = w1.shape[2]
    half = CMP_BLOCK // 2
    assert half == CMP_STRIDE
    w1r = w1.reshape(2, CMP_BLOCK, HD, hid)
    w1x = jnp.stack([_pair_expand(w1r[:, :half]), _pair_expand(w1r[:, half:])], axis=1).astype(BF16)
    posr = jnp.tile(pos.reshape(2, 2, half, 1, HD), (1, 1, 1, 2, 1)).reshape(2, 2, half * 2 * HD)
    resident = lambda a: pl.BlockSpec(a.shape, lambda b: (0,) * a.ndim, pipeline_mode=pl.Buffered(1))
    return pl.pallas_call(
        _compress_body,
        grid=(batch,),
        in_specs=[pl.BlockSpec((None, seq, 128), functools.partial(lambda q, b: (q, b, 0), q))
                  for q in range(4)]
                 + [resident(posr), resident(w1x), resident(w2k), resident(w2vt), resident(g)],
        out_specs=[pl.BlockSpec((None, nwin, G * 128), lambda b: (b, 0, 0)),
                   pl.BlockSpec((None, G * HD, nwin), lambda b: (b, 0, 0))],
        out_shape=[jax.ShapeDtypeStruct((batch, nwin, G * 128), BF16),
                   jax.ShapeDtypeStruct((batch, G * HD, nwin), BF16)],
        compiler_params=_cparams(("parallel",)),
        name="compress",
    )(cc, cc, cc, cc, posr, w1x, w2k, w2vt, g)


def _t5_bucket(dist):
    n = jnp.maximum(dist, 0)
    max_exact = REL_BUCKETS // 2
    nf = jnp.maximum(n, 1).astype(F32)
    large = max_exact + (jnp.log(nf / max_exact) / math.log(REL_MAX_DIST / max_exact)
                         * (REL_BUCKETS - max_exact)).astype(jnp.int32)
    large = jnp.minimum(large, REL_BUCKETS - 1)
    return jnp.where(n < max_exact, n, large)


def _table_lookup(bucket, tab_ref, h):
    out = jnp.zeros(bucket.shape, F32)
    for k in range(REL_BUCKETS):
        out = jnp.where(bucket == k, tab_ref[k, h], out)
    return out


def _bias_prep_body(tab_ref, bc_ref, tp_ref, far_ref):
    h = pl.program_id(0)
    S = bc_ref.shape[1]
    far = tab_ref[REL_BUCKETS - 1, h]
    far_ref[...] = jnp.full(far_ref.shape, far * LOG2E, F32)

    dist = lax.broadcasted_iota(jnp.int32, (8, S), 1)
    by_dist = _table_lookup(_t5_bucket(dist), tab_ref, h) * LOG2E
    shifted = pltpu.roll(jnp.broadcast_to(by_dist[0:1, :], (NCMP_PAD, S)), 0, axis=1,
                         stride=CMP_STRIDE, stride_axis=0)
    shifted = pltpu.roll(shifted, CMP_BLOCK - 1, axis=1)
    n = lax.broadcasted_iota(jnp.int32, (NCMP_PAD, S), 0)
    t = lax.broadcasted_iota(jnp.int32, (NCMP_PAD, S), 1)
    bc_ref[...] = jnp.where(t >= n * CMP_STRIDE + CMP_BLOCK - 1, shifted, NEG_INF)
    j = lax.broadcasted_iota(jnp.int32, (ATT_TK, ATT_QB), 0)
    i = lax.broadcasted_iota(jnp.int32, (ATT_TK, ATT_QB), 1)
    for d in range(2):
        dist = d * ATT_TK + i - j
        rel = (_table_lookup(_t5_bucket(dist), tab_ref, h) - far) * LOG2E
        tp_ref[d] = jnp.where(dist >= 0, rel, NEG_INF)
    tp_ref[2] = jnp.where(i < j, 0.0, NEG_INF)


def bias_prep(rel_table, seq):
    assert ATT_TK == ATT_QB and ATT_TK + 1 > 113
    return pl.pallas_call(
        _bias_prep_body,
        grid=(NSA_HEADS,),
        in_specs=[pl.BlockSpec(memory_space=pltpu.SMEM)],
        out_specs=[pl.BlockSpec((None, NCMP_PAD, seq), lambda h: (h, 0, 0)),
                   pl.BlockSpec((None, 3, ATT_TK, ATT_QB), lambda h: (h, 0, 0, 0)),
                   pl.BlockSpec((None, 8, 128), lambda h: (h, 0, 0))],
        out_shape=[jax.ShapeDtypeStruct((NSA_HEADS, NCMP_PAD, seq), F32),
                   jax.ShapeDtypeStruct((NSA_HEADS, 3, ATT_TK, ATT_QB), F32),
                   jax.ShapeDtypeStruct((NSA_HEADS, 8, 128), F32)],
        compiler_params=_cparams(("parallel",)),
        name="bias_prep",
    )(rel_table)


def _q_proj_body(x_ref, g_ref, sh_ref, sc_ref, wqt_ref, wgt_ref, bg_ref, gq_ref, qt_ref, gt_ref):
    HD = NSA_HD
    h = _norm_mod(x_ref[...], g_ref[...], sh_ref[...], sc_ref[...]).astype(BF16)
    gt_ref[...] = _dot_nt(wgt_ref[...], h) + bg_ref[...]
    qt = _dot_nt(wqt_ref[...], h)
    scale = gq_ref[...] * (HD ** -0.5 * LOG2E)
    for hh in range(NSA_HEADS):
        seg = qt[hh * HD:(hh + 1) * HD, :]
        var = jnp.mean(seg * seg, axis=0, keepdims=True)
        qt_ref[hh * HD:(hh + 1) * HD, :] = (seg * lax.rsqrt(var + RMS_EPS) * scale).astype(BF16)


def q_proj(x2d, g, sh, sc, wqt, wgt, bg, gq, batch, seq, tm=512):
    M, D = x2d.shape
    nt = seq // tm
    nq = wqt.shape[0]
    ng = wgt.shape[0]
    bvec = pl.BlockSpec((None, 1, D), lambda i: (i // nt, 0, 0))
    full = lambda a: pl.BlockSpec(a.shape, lambda i: (0,) * a.ndim)
    return pl.pallas_call(
        _q_proj_body,
        grid=(M // tm,),
        in_specs=[pl.BlockSpec((tm, D), lambda i: (i, 0)),
                  pl.BlockSpec((1, D), lambda i: (0, 0)),
                  bvec, bvec, full(wqt), full(wgt), full(bg), full(gq)],
        out_specs=[pl.BlockSpec((None, nq, tm), lambda i: (i // nt, 0, i % nt)),
                   pl.BlockSpec((None, ng, tm), lambda i: (i // nt, 0, i % nt))],
        out_shape=[jax.ShapeDtypeStruct((batch, nq, seq), BF16),
                   jax.ShapeDtypeStruct((batch, ng, seq), F32)],
        compiler_params=_cparams(("parallel",)),
        name="q_proj",
    )(x2d, g.reshape(1, D), sh, sc, wqt, wgt, bg, gq)


def _heads_on_lanes(pieces):
    return jnp.concatenate(pieces, axis=1)


def _nsa_body(gps, qt_ref, gt_ref, kc_ref, vct_ref, ks_ref, kw_ref, vst_ref, vwt_ref,
              bct_ref, tp_ref, far_ref, ovt_ref, out_ref):
    qb = pl.program_id(2)
    QB, HD, HPG = ATT_QB, NSA_HD, NSA_HPG
    R = HPG * QB
    t0 = pl.multiple_of(qb * QB, QB)
    near0 = pl.multiple_of(t0 + WINDOW - ATT_TK, ATT_TK)
    wlen = WINDOW + QB
    nblk = ovt_ref.shape[0]
    jb = lax.broadcasted_iota(jnp.int32, (nblk, QB), 0)
    jbf = jb.astype(F32)
    qid = jnp.right_shift(t0 + lax.broadcasted_iota(jnp.int32, (nblk, QB), 1), 6)
    forced = (jb == 0) | (jb == qid) | (jb == qid - 1)
    srow = lax.broadcasted_iota(jnp.int32, (8, R), 0)

    def finish(acc):
        return acc[0:HD, :] / acc[HD:HD + 1, :]

    def tile_part(s_tile, v_tile):
        m_t = jnp.max(s_tile, axis=0, keepdims=True)
        m_safe = jnp.where(m_t == NEG_INF, 0.0, m_t)
        p = jnp.exp2(s_tile - m_safe).astype(BF16)
        return m_t, _dot(v_tile, p)[0:HD + 8, :]

    def combine(parts):
        m_fin = parts[0][0]
        for m_t, _ in parts[1:]:
            m_fin = jnp.maximum(m_fin, m_t)
        acc = None
        for m_t, pv in parts:
            term = jnp.exp2(m_t - m_fin) * pv
            acc = term if acc is None else acc + term
        return m_fin, acc

    groups = range(gps)
    hs = [[gl * HPG + h for h in range(HPG)] for gl in groups]
    kcol = [slice(gl * 128, (gl + 1) * 128) for gl in groups]
    vrow = [slice(gl * VT_ROWS, (gl + 1) * VT_ROWS) for gl in groups]

    def stationary(q_all, hi_lo, block_rows):
        pad = jnp.where(srow == 0, MASK_BIG, 0.0)
        rest = jnp.zeros((128 - HD - KEY_PAD - 8, R), F32)
        tail = jnp.concatenate([hi_lo, block_rows, pad, rest], axis=0).astype(BF16)
        return jnp.concatenate([q_all, tail], axis=0)

    q_alls, hi_los, qms = [], [], []
    for gl in groups:
        q_alls.append(_heads_on_lanes([qt_ref[h * HD:(h + 1) * HD, :] for h in hs[gl]]))
        far = _heads_on_lanes([far_ref[h, 0:1, :] for h in hs[gl]])
        hi = far.astype(BF16).astype(F32)
        hi_los.append(jnp.where(srow == 0, hi, jnp.where(srow == 1, far - hi, 0.0)))
        qms.append(stationary(q_alls[gl], hi_los[gl], jnp.zeros((nblk, R), F32)))

    s_cs = [_dot(kc_ref[:, kcol[gl]], qms[gl]) for gl in groups]
    s_ws = [_dot(kw_ref[pl.ds(t0, wlen), kcol[gl]], qms[gl]) for gl in groups]
    s_ns = [_dot(ks_ref[pl.ds(near0, 2 * ATT_TK), kcol[gl]], qms[gl]) for gl in groups]

    o_cmps, scores = [], []
    for gl in groups:
        s_c = s_cs[gl] + _heads_on_lanes([bct_ref[h] for h in hs[gl]])
        m_c = jnp.max(s_c, axis=0, keepdims=True)
        m_c = jnp.where(m_c == NEG_INF, 0.0, m_c)
        e_c = jnp.exp2(s_c - m_c)
        p_c = e_c / jnp.maximum(jnp.sum(e_c, axis=0, keepdims=True), jnp.finfo(F32).tiny)
        o_cmps.append(_dot(vct_ref[gl * HD:(gl + 1) * HD, :], p_c.astype(BF16)))
        p_sum = p_c[:, 0:QB]
        for h in range(1, HPG):
            p_sum = p_sum + p_c[:, h * QB:(h + 1) * QB]
        p_hi = p_sum.astype(BF16)
        p_lo = (p_sum - p_hi.astype(F32)).astype(BF16)
        imp = _dot(ovt_ref[...], jnp.concatenate([p_hi, p_lo], axis=0))
        score = jnp.where(forced, FORCE_SCORE, imp)
        scores.append(jnp.where(jb <= qid, score, NEG_INF))

    nwt = wlen // ATT_TK
    o_wins = []
    for gl in groups:
        parts = []
        for i in range(nwt):
            d = nwt - 1 - i
            s_t = s_ws[gl][i * ATT_TK:(i + 1) * ATT_TK]
            if d in (0, 1):
                s_t = s_t + _heads_on_lanes([tp_ref[h, d] for h in hs[gl]])
            elif d == nwt - 1:
                s_t = s_t + _heads_on_lanes([tp_ref[h, 2] for h in hs[gl]])
            parts.append(tile_part(s_t, vwt_ref[vrow[gl], pl.ds(t0 + i * ATT_TK, ATT_TK)]))
        o_wins.append(finish(combine(parts)[1]))

    msels = [jnp.full((nblk, QB), NEG_INF, F32) for _ in groups]
    for _ in range(SLC_TOPK):
        for gl in groups:
            mx = jnp.max(scores[gl], axis=0, keepdims=True)
            first = jnp.min(jnp.where(scores[gl] == mx, jbf, float(nblk)), axis=0, keepdims=True)
            pick = jbf == first
            msels[gl] = jnp.where(pick & (mx > NEG_INF), 0.0, msels[gl])
            scores[gl] = jnp.where(pick, NEG_INF, scores[gl])
    carry0 = []
    for gl in groups:
        prev_mask = jnp.concatenate(
            [jnp.broadcast_to(jnp.max(jnp.where(jb == 2 * qb - 2 + r, msels[gl], NEG_INF), axis=0,
                                      keepdims=True), (SLC_BLOCK, QB)) for r in range(2)], axis=0)
        s_prev = s_ns[gl][0:ATT_TK] + _heads_on_lanes([tp_ref[h, 1] + prev_mask for h in hs[gl]])
        s_diag = s_ns[gl][ATT_TK:] + _heads_on_lanes([tp_ref[h, 0] for h in hs[gl]])
        carry0.append(combine([
            tile_part(s_diag, vst_ref[vrow[gl], pl.ds(near0 + ATT_TK, ATT_TK)]),
            tile_part(s_prev, vst_ref[vrow[gl], pl.ds(near0, ATT_TK)])]))

    qss = [stationary(q_alls[gl], hi_los[gl], _heads_on_lanes(
        [jnp.where(msels[gl] == 0.0, 0.0, MASK_BIG)] * HPG)) for gl in groups]

    def far_step(c, carry):
        tile0 = qb + (WINDOW - ATT_TK) // ATT_TK - (c + 1) * (FAR_CHUNK // ATT_TK)
        row0 = pl.multiple_of(tile0 * ATT_TK, ATT_TK)
        s_fs = [_dot(ks_ref[pl.ds(row0, FAR_CHUNK), kcol[gl]], qss[gl]) for gl in groups]
        out = []
        for gl in groups:
            parts = [carry[gl]]
            for j in range(FAR_CHUNK // ATT_TK):
                parts.append(tile_part(s_fs[gl][j * ATT_TK:(j + 1) * ATT_TK],
                                       vst_ref[vrow[gl], pl.ds(row0 + j * ATT_TK, ATT_TK)]))
            out.append(combine(parts))
        return tuple(out)

    n_far = (qb + 2) // 4
    sel = lax.fori_loop(0, n_far, far_step, tuple(carry0))

    gates = jax.nn.sigmoid(gt_ref[...])
    for gl in groups:
        o_cmp, o_win = o_cmps[gl], o_wins[gl]
        o_sel = finish(sel[gl][1])
        outs = []
        for h in range(HPG):
            lanes = slice(h * QB, (h + 1) * QB)
            r = gl * 16 + 3 * h
            outs.append(gates[r:r + 1, :] * o_cmp[:, lanes]
                        + gates[r + 1:r + 2, :] * o_sel[:, lanes]
                        + gates[r + 2:r + 3, :] * o_win[:, lanes])
        for pair in range(HPG // 2):
            two = jnp.concatenate(outs[2 * pair:2 * pair + 2], axis=0)
            c0 = gl * HPG * HD + pair * 2 * HD
            out_ref[:, c0:c0 + 2 * HD] = two.T.astype(out_ref.dtype)


def _overlap_matrix_t(nblk):
    start = np.arange(NCMP_PAD) * CMP_STRIDE
    sj = np.arange(nblk) * SLC_BLOCK
    ov = (np.minimum(start[None, :] + CMP_BLOCK, sj[:, None] + SLC_BLOCK)
          - np.maximum(start[None, :], sj[:, None]))
    ov = np.clip(ov, 0, None) / CMP_BLOCK
    ov[:, NCMP_PAD - 1] = 0.0
    return np.concatenate([ov, ov], axis=1).astype(np.float32)


def nsa_attn(qt, gt, kc, vct, ks, kw, vst, vwt, bias_ct, tiles, far, batch, seq, gps=ATT_GPS):
    QB, HD, HPG, G = ATT_QB, NSA_HD, NSA_HPG, NSA_GROUPS
    nblk = seq // SLC_BLOCK
    assert nblk % 8 == 0 and seq % QB == 0 and seq // CMP_STRIDE == NCMP_PAD and G % gps == 0
    assert WINDOW % ATT_TK == 0 and FAR_CHUNK == 4 * ATT_TK and KV_PAD >= FAR_CHUNK - ATT_TK
    assert KEY_BLK % 8 == 0 and KEY_BLK + nblk <= KEY_PAD and KEY_PAD + 8 <= 128 - HD
    nq = seq // QB
    sp = seq + KV_PAD
    ovt = jnp.asarray(_overlap_matrix_t(nblk), dtype=BF16)
    return pl.pallas_call(
        functools.partial(_nsa_body, gps),
        grid=(batch, G // gps, nq),
        in_specs=[pl.BlockSpec((None, gps * HPG * HD, QB), lambda b, g, i: (b, g, i)),
                  pl.BlockSpec((None, gps * 16, QB), lambda b, g, i: (b, g, i)),
                  pl.BlockSpec((None, NCMP_PAD, gps * 128), lambda b, g, i: (b, 0, g)),
                  pl.BlockSpec((None, gps * HD, NCMP_PAD), lambda b, g, i: (b, g, 0)),
                  pl.BlockSpec((None, sp, gps * 128), lambda b, g, i: (b, 0, g)),
                  pl.BlockSpec((None, sp, gps * 128), lambda b, g, i: (b, 0, g)),
                  pl.BlockSpec((None, gps * VT_ROWS, sp), lambda b, g, i: (b, g, 0)),
                  pl.BlockSpec((None, gps * VT_ROWS, sp), lambda b, g, i: (b, g, 0)),
                  pl.BlockSpec((gps * HPG, NCMP_PAD, QB), lambda b, g, i: (g, 0, i)),
                  pl.BlockSpec((gps * HPG, 3, ATT_TK, QB), lambda b, g, i: (g, 0, 0, 0)),
                  pl.BlockSpec((gps * HPG, 8, 128), lambda b, g, i: (g, 0, 0)),
                  pl.BlockSpec((nblk, 2 * NCMP_PAD), lambda b, g, i: (0, 0))],
        out_specs=pl.BlockSpec((QB, gps * HPG * HD), lambda b, g, i: (b * nq + i, g)),
        out_shape=jax.ShapeDtypeStruct((batch * seq, NSA_HEADS * HD), BF16),
        compiler_params=_cparams(("parallel", "parallel", "arbitrary")),
        name="nsa_attn",
    )(qt, gt, kc, vct, ks, kw, vst, vwt, bias_ct, tiles, far, ovt)


def _mlstm_layer(x2d, g_mix, sh, sc, w_in, b_if, g_out, w_out, batch, seq):
    nbig = 2 * ML_HEADS * ML_DQK + 2 * ML_HEADS * ML_DV
    ng = 2 * ML_HEADS
    wg = jnp.pad(w_in[:, nbig:], ((0, 0), (0, 128 - ng)))
    bg = jnp.pad(b_if, (0, 128 - ng))
    proj, gates = norm_proj(x2d, g_mix, sh, sc, [w_in[:, :nbig], wg], [None, bg], [BF16, F32])
    return mlstm_core(proj, gates, g_out, batch, seq), w_out


def _nsa_shared(x2d, g_kv, kv_sh, kv_sc, w_kv, pos_k, w_k1, w_k2, pos_v, w_v1, w_v2,
                g_knorm, batch, seq):
    gw = NSA_GROUPS * NSA_HD
    part = lambda i: w_kv[:, i * gw:(i + 1) * gw]
    hd = NSA_HD
    wk = jnp.concatenate([p[:, g * hd:(g + 1) * hd] for g in range(NSA_GROUPS)
                          for p in (part(2), part(4))], axis=1).astype(BF16)
    gk = jnp.tile(jnp.concatenate([g_knorm[1], g_knorm[2]]), NSA_GROUPS).reshape(1, -1)
    wvt = jnp.concatenate([part(3), part(5)], axis=1).T.astype(BF16)
    wcc = jnp.concatenate([part(0), part(1)], axis=1).astype(BF16)
    ks, kw, vst, vwt, cc = kv_proj(x2d, g_kv, kv_sh, kv_sc, wk, wvt, wcc, gk, batch, seq)
    pos = jnp.stack([pos_k, pos_v])
    w1 = jnp.stack([w_k1, w_v1])
    kc, vct = compress(cc, pos, w1, w_k2.astype(BF16), w_v2.T.astype(BF16), g_knorm[0:1],
                       batch, seq)
    return kc, vct, ks, kw, vst, vwt


def _gate_weights_t(w_q, b_gate):
    nq = NSA_HEADS * NSA_HD
    per = 3 * NSA_HPG
    wg = w_q[:, nq:].T.reshape(NSA_GROUPS, per, -1)
    wg = jnp.pad(wg, ((0, 0), (0, 16 - per), (0, 0))).reshape(NSA_GROUPS * 16, -1)
    bg = jnp.pad(b_gate.reshape(NSA_GROUPS, per), ((0, 0), (0, 16 - per))).reshape(-1, 1)
    return wg.astype(BF16), bg


def _nsa_layer(x2d, g_mix, sh, sc, shared, w_q, b_gate, g_qnorm, w_out, bias_ct, tiles, far,
               batch, seq):
    nq = NSA_HEADS * NSA_HD
    wgt, bg = _gate_weights_t(w_q, b_gate)
    qt, gt = q_proj(x2d, g_mix, sh, sc, w_q[:, :nq].T.astype(BF16), wgt, bg,
                    g_qnorm.reshape(NSA_HD, 1), batch, seq)
    kc, vct, ks, kw, vst, vwt = shared
    att = nsa_attn(qt, gt, kc, vct, ks, kw, vst, vwt, bias_ct, tiles, far, batch, seq)
    return att, w_out


def kernel(x, c, w_ada, b_ada, g_norm_mix, g_norm_ffn, w_ffn_in, w_ffn_out, w_a_in, b_a_if, g_a_out, w_a_out, w_kv_ada, b_kv_ada, g_kv_norm, w_kv, pos_cmp_k, w_cmp_k1, w_cmp_k2, pos_cmp_v, w_cmp_v1, w_cmp_v2, g_knorm, w_b_q, b_b_gate, g_qnorm, w_b_out, rel_table):
    B, S, D = x.shape
    depth = w_ada.shape[0]
    n_a = w_a_in.shape[0]
    x2d = x.reshape(B * S, D)
    mods = ada_mod(c, w_ada, b_ada)
    kv_mod = ada_mod(c, w_kv_ada[None], b_kv_ada[None])[0]
    shared = None
    bias_ct = tiles = far = None
    for layer in range(depth):
        sh1, sc1, ga1, sh2, sc2, ga2 = [mods[layer, :, i * D:(i + 1) * D].reshape(B, 1, D)
                                        for i in range(6)]
        if layer < n_a:
            mixed, w_mix = _mlstm_layer(x2d, g_norm_mix[layer], sh1, sc1, w_a_in[layer],
                                        b_a_if[layer], g_a_out[layer], w_a_out[layer], B, S)
        else:
            j = layer - n_a
            if shared is None:
                kv_sh = kv_mod[:, :D].reshape(B, 1, D)
                kv_sc = kv_mod[:, D:].reshape(B, 1, D)
                shared = _nsa_shared(x2d, g_kv_norm, kv_sh, kv_sc, w_kv, pos_cmp_k, w_cmp_k1,
                                     w_cmp_k2, pos_cmp_v, w_cmp_v1, w_cmp_v2, g_knorm, B, S)
                bias_ct, tiles, far = bias_prep(rel_table, S)
            mixed, w_mix = _nsa_layer(x2d, g_norm_mix[layer], sh1, sc1, shared, w_b_q[j],
                                      b_b_gate[j], g_qnorm[j], w_b_out[j], bias_ct, tiles, far, B, S)
        x2d = mix_ffn(mixed, w_mix, x2d, ga1, g_norm_ffn[layer], sh2, sc2, ga2,
                      w_ffn_in[layer], w_ffn_out[layer])
    return x2d.reshape(B, S, D)
```

```python
import functools
import math

import jax
import jax.numpy as jnp
import numpy as np
from jax import lax
from jax.experimental import pallas as pl
from jax.experimental.pallas import tpu as pltpu

F32 = jnp.float32
BF16 = jnp.bfloat16
NEG_INF = float("-inf")
LOG2E = math.log2(math.e)

RMS_EPS = 1e-6

ML_HEADS = 4
ML_DQK = 128
ML_DV = 256
ML_LC = 256

NSA_HEADS = 16
NSA_GROUPS = 4
NSA_HPG = 4
NSA_HD = 64
CMP_BLOCK = 32
CMP_STRIDE = 16
SLC_BLOCK = 64
SLC_TOPK = 8
WINDOW = 512
FORCE_SCORE = 1e4
REL_BUCKETS = 32
REL_MAX_DIST = 128
ATT_QB = 128
ATT_TK = 128
ATT_GPS = 4
NCMP_PAD = 128
KV_PAD = WINDOW
FAR_CHUNK = 512
VT_ROWS = 80
KEY_BLK = 8
KEY_PAD = 40
MASK_BIG = -32768.0

VMEM_LIMIT = 56 * 1024 * 1024


def _cparams(sem):
    return pltpu.CompilerParams(dimension_semantics=sem, vmem_limit_bytes=VMEM_LIMIT)


def _dot(a, b):
    return jnp.dot(a, b, preferred_element_type=F32)


def _dot_nt(a, b):
    return lax.dot_general(a, b, (((1,), (1,)), ((), ())), preferred_element_type=F32)


def _dot_tn(a, b):
    return lax.dot_general(a, b, (((0,), (0,)), ((), ())), preferred_element_type=F32)


def _norm_mod(x, g, sh, sc):
    var = jnp.mean(x * x, axis=-1, keepdims=True)
    y = x * lax.rsqrt(var + RMS_EPS) * g
    return y * (1.0 + sc) + sh


def _ada_body(c_ref, w_ref, b_ref, o_ref):
    c = c_ref[...]
    ca = c * jax.nn.sigmoid(c)
    nb = ca.shape[0]
    a_hi = ca.astype(BF16).astype(F32)
    a_mid = (ca - a_hi).astype(BF16).astype(F32)
    a_lo = ca - a_hi - a_mid
    a3 = jnp.concatenate([a_hi, a_mid, a_lo, jnp.zeros_like(ca)], axis=0).astype(BF16)
    w = w_ref[...]
    w_hi = w.astype(BF16)
    w_lo = (w - w_hi.astype(F32)).astype(BF16)
    p = _dot(a3, w_hi)
    q = _dot(a3[0:2 * nb], w_lo)
    o_ref[...] = (p[0:nb] + p[nb:2 * nb] + p[2 * nb:3 * nb] + q[0:nb] + q[nb:2 * nb]) + b_ref[...]


def ada_mod(c, w, b, tn=1024):
    L, D, N = w.shape
    B = c.shape[0]
    return pl.pallas_call(
        _ada_body,
        grid=(L, N // tn),
        in_specs=[pl.BlockSpec((B, D), lambda l, j: (0, 0)),
                  pl.BlockSpec((None, D, tn), lambda l, j: (l, 0, j)),
                  pl.BlockSpec((None, 1, tn), lambda l, j: (l, 0, j))],
        out_specs=pl.BlockSpec((None, B, tn), lambda l, j: (l, 0, j)),
        out_shape=jax.ShapeDtypeStruct((L, B, N), F32),
        compiler_params=_cparams(("parallel", "parallel")),
        name="ada_mod",
    )(c, w, b.reshape(L, 1, N))


def _norm_proj_body(n_out, has_bias, tn, x_ref, g_ref, sh_ref, sc_ref, *refs):
    w_refs = refs[:n_out]
    b_refs = refs[n_out:2 * n_out]
    o_refs = refs[2 * n_out:3 * n_out]
    h = _norm_mod(x_ref[...], g_ref[...], sh_ref[...], sc_ref[...]).astype(BF16)
    for w_ref, b_ref, o_ref, hb in zip(w_refs, b_refs, o_refs, has_bias):
        n = w_ref.shape[1]
        step = min(tn, n)
        for n0 in range(0, n, step):
            acc = _dot(h, w_ref[:, n0:n0 + step].astype(BF16))
            if hb:
                acc = acc + b_ref[:, n0:n0 + step]
            o_ref[:, n0:n0 + step] = acc.astype(o_ref.dtype)


def norm_proj(x2d, g, sh, sc, ws, biases, out_dtypes, tm=512, tn=512):
    M, D = x2d.shape
    B = sh.shape[0]
    tiles_per_batch = (M // B) // tm
    n_out = len(ws)
    has_bias = tuple(b is not None for b in biases)
    bias_args = [(b if b is not None else jnp.zeros((w.shape[1],), F32)).reshape(1, -1)
                 for b, w in zip(biases, ws)]
    in_specs = [pl.BlockSpec((tm, D), lambda i: (i, 0)),
                pl.BlockSpec((1, D), lambda i: (0, 0)),
                pl.BlockSpec((None, 1, D), lambda i: (i // tiles_per_batch, 0, 0)),
                pl.BlockSpec((None, 1, D), lambda i: (i // tiles_per_batch, 0, 0))]
    resident = lambda a: pl.BlockSpec(a.shape, lambda i: (0, 0), pipeline_mode=pl.Buffered(1))
    in_specs += [resident(w) for w in ws] + [resident(b) for b in bias_args]
    out_specs = [pl.BlockSpec((tm, w.shape[1]), lambda i: (i, 0)) for w in ws]
    out_shape = [jax.ShapeDtypeStruct((M, w.shape[1]), dt) for w, dt in zip(ws, out_dtypes)]
    return pl.pallas_call(
        functools.partial(_norm_proj_body, n_out, has_bias, tn),
        grid=(M // tm,),
        in_specs=in_specs, out_specs=out_specs, out_shape=out_shape,
        compiler_params=_cparams(("parallel",)),
        name="norm_proj",
    )(x2d, g.reshape(1, D), sh, sc, *ws, *bias_args)


def _mix_ffn_body(tf, a_ref, wm_ref, x_ref, ga1_ref, g_ref, sh_ref, sc_ref, ga2_ref,
                  wi_ref, wo_ref, o_ref, act_s):
    F = wo_ref.shape[0]
    x1 = x_ref[...] + ga1_ref[...] * _dot(a_ref[...], wm_ref[...].astype(BF16))
    h = _norm_mod(x1, g_ref[...], sh_ref[...], sc_ref[...]).astype(BF16)
    for f0 in range(0, F, tf):
        gate = _dot(h, wi_ref[:, f0:f0 + tf].astype(BF16))
        up = _dot(h, wi_ref[:, F + f0:F + f0 + tf].astype(BF16))
        act_s[:, f0:f0 + tf] = (gate * jax.nn.sigmoid(gate) * up).astype(BF16)
    y = _dot(act_s[:, 0:tf], wo_ref[0:tf, :].astype(BF16))
    for f0 in range(tf, F, tf):
        y = y + _dot(act_s[:, f0:f0 + tf], wo_ref[f0:f0 + tf, :].astype(BF16))
    o_ref[...] = x1 + ga2_ref[...] * y


def mix_ffn(a, w_mix, x2d, ga1, g, sh, sc, ga2, w_in, w_out, tm=512, tf=256):
    M, D = x2d.shape
    K = a.shape[1]
    F = w_out[0].shape[1]
    B = sh.shape[0]
    tiles_per_batch = (M // B) // tm
    bvec = pl.BlockSpec((None, 1, D), lambda i: (i // tiles_per_batch, 0, 0))

    def resident(stacked_layer):
        w, layer = stacked_layer
        return pl.BlockSpec((None,) + w.shape[1:], lambda i: (layer, 0, 0),
                            pipeline_mode=pl.Buffered(1))

    return pl.pallas_call(
        functools.partial(_mix_ffn_body, tf),
        grid=(M // tm,),
        in_specs=[pl.BlockSpec((tm, K), lambda i: (i, 0)),
                  resident(w_mix),
                  pl.BlockSpec((tm, D), lambda i: (i, 0)),
                  bvec,
                  pl.BlockSpec((1, D), lambda i: (0, 0)),
                  bvec, bvec, bvec,
                  resident(w_in), resident(w_out)],
        out_specs=pl.BlockSpec((tm, D), lambda i: (i, 0)),
        out_shape=jax.ShapeDtypeStruct((M, D), F32),
        scratch_shapes=[pltpu.VMEM((tm, F), BF16)],
        compiler_params=_cparams(("parallel",)),
        name="mix_ffn",
    )(a, w_mix[0], x2d, ga1, g.reshape(1, D), sh, sc, ga2, w_in[0], w_out[0])


def _sublane_scan(x, op, fill):
    n = x.shape[0]
    row = lax.broadcasted_iota(jnp.int32, x.shape, 0)
    sh = 1
    while sh < n:
        x = op(x, jnp.where(row >= sh, pltpu.roll(x, sh, axis=0), fill))
        sh *= 2
    return x


def _mlstm_body(q_ref, k_ref, v_ref, o_ref, gc_ref, gout_ref, out_ref, c_s, n_s, m_s):
    c_idx = pl.program_id(1)
    LC = q_ref.shape[0]
    NH = ML_HEADS
    scale = ML_DQK ** -0.5
    log_scale = math.log(scale)

    @pl.when(c_idx == 0)
    def _():
        c_s[...] = jnp.zeros_like(c_s)
        n_s[...] = jnp.zeros_like(n_s)
        m_s[...] = jnp.zeros_like(m_s)

    gc = gc_ref[...]
    logf = jnp.minimum(gc, 0.0) - jnp.log1p(jnp.exp(-jnp.abs(gc)))
    bcum = pltpu.roll(_sublane_scan(logf, jnp.add, 0.0), 128 - NH, axis=1)
    a_c = gc - bcum
    cmax = _sublane_scan(a_c, jnp.maximum, NEG_INF)
    a_t = a_c.T

    row = lax.broadcasted_iota(jnp.int32, (LC, LC), 0)
    col = lax.broadcasted_iota(jnp.int32, (LC, LC), 1)
    causal = col <= row
    ones_col = jnp.ones((LC, 128), BF16)

    def lanes2(x):
        return jnp.concatenate([x, x], axis=1)

    for h in range(NH):
        a_rep = jnp.broadcast_to(a_c[:, h:h + 1], (LC, 128))
        b_rep = jnp.broadcast_to(bcum[:, h:h + 1], (LC, 128))
        cm_rep = jnp.broadcast_to(cmax[:, h:h + 1], (LC, 128))
        a_r = a_t[h:h + 1, :]
        b_last = b_rep[LC - 1:LC, :]
        m_prev = m_s[h:h + 1, :]
        m_new = jnp.maximum(b_last + m_prev, b_last + cm_rep[LC - 1:LC, :])
        decay = jnp.exp(b_last + m_prev - m_new)
        e_rep = jnp.exp(b_last + a_rep - m_new)
        g_rep = jnp.maximum(m_prev, cm_rep)
        w_intra = jnp.exp(jnp.where(causal, a_r - lanes2(g_rep) + log_scale, NEG_INF))
        w_inter = jnp.exp(m_prev - g_rep) * scale
        floor = jnp.exp(-(b_rep + g_rep))

        qh = q_ref[:, h * ML_DQK:(h + 1) * ML_DQK]
        kh = k_ref[:, h * ML_DQK:(h + 1) * ML_DQK]
        vh = v_ref[:, h * ML_DV:(h + 1) * ML_DV]
        c_prev = c_s[h]
        n_prev = n_s[h:h + 1, :]

        s = (_dot_nt(qh, kh) * w_intra).astype(BF16)
        inter = _dot(qh, c_prev.astype(BF16))
        num = _dot(s, vh) + lanes2(w_inter) * inter
        n_rows = jnp.broadcast_to(n_prev, (128, ML_DQK)).astype(BF16)
        qn = _dot(s, ones_col) + w_inter * _dot_nt(qh, n_rows)
        inv = 1.0 / jnp.maximum(jnp.abs(qn), floor)

        ke = kh.astype(F32) * e_rep
        c_s[h] = lanes2(decay) * c_prev + _dot_tn(ke.astype(BF16), vh)
        n_s[h:h + 1, :] = decay * n_prev + jnp.sum(ke, axis=0, keepdims=True)
        m_s[h:h + 1, :] = m_new

        ssq = _dot((num * num).astype(BF16), jnp.ones((ML_DV, 128), BF16))
        rs = lax.rsqrt(ssq * (inv * inv) * (1.0 / ML_DV) + RMS_EPS) * inv
        hn = num * lanes2(rs) * gout_ref[:, h * ML_DV:(h + 1) * ML_DV]
        og = jax.nn.sigmoid(o_ref[:, h * ML_DV:(h + 1) * ML_DV].astype(F32))
        out_ref[:, h * ML_DV:(h + 1) * ML_DV] = (hn * og).astype(out_ref.dtype)


def mlstm_core(proj, gates, g_out, batch, seq):
    LC = ML_LC
    nc = seq // LC
    qk = ML_HEADS * ML_DQK
    vd = ML_HEADS * ML_DV
    row = lambda b, c: b * nc + c
    return pl.pallas_call(
        _mlstm_body,
        grid=(batch, nc),
        in_specs=[pl.BlockSpec((LC, qk), lambda b, c: (row(b, c), 0)),
                  pl.BlockSpec((LC, qk), lambda b, c: (row(b, c), 1)),
                  pl.BlockSpec((LC, vd), lambda b, c: (row(b, c), 1)),
                  pl.BlockSpec((LC, vd), lambda b, c: (row(b, c), 2)),
                  pl.BlockSpec((LC, 128), lambda b, c: (row(b, c), 0)),
                  pl.BlockSpec((1, vd), lambda b, c: (0, 0))],
        out_specs=pl.BlockSpec((LC, vd), lambda b, c: (row(b, c), 0)),
        out_shape=jax.ShapeDtypeStruct((batch * seq, vd), BF16),
        scratch_shapes=[pltpu.VMEM((ML_HEADS, ML_DQK, ML_DV), F32),
                        pltpu.VMEM((8, ML_DQK), F32),
                        pltpu.VMEM((8, 128), F32)],
        compiler_params=_cparams(("parallel", "arbitrary")),
        name="mlstm_core",
    )(proj, proj, proj, proj, gates, g_out.reshape(1, vd))


def _kv_proj_body(x_ref, g_ref, sh_ref, sc_ref, wk_ref, wvt_ref, wcc_ref, gk_ref, bd_ref,
                  ks_ref, kw_ref, vst_ref, vwt_ref, cc_ref):
    i = pl.program_id(1)
    G, HD = NSA_GROUPS, NSA_HD
    tm = x_ref.shape[0]
    lane = lax.broadcasted_iota(jnp.int32, (tm, G * 128), 1) % 128

    @pl.when(i == 0)
    def _():
        pad_rows = jnp.where(lane == HD + KEY_PAD, 1.0, 0.0).astype(BF16)
        ks_ref[...] = pad_rows
        kw_ref[...] = pad_rows
        vst_ref[...] = jnp.zeros_like(vst_ref)
        vwt_ref[...] = jnp.zeros_like(vwt_ref)

    @pl.when(i > 0)
    def _():
        h = _norm_mod(x_ref[...], g_ref[...], sh_ref[...], sc_ref[...]).astype(BF16)
        cc = _dot(h, wcc_ref[...])
        for q in range(cc_ref.shape[0]):
            cc_ref[q] = cc[:, q * 128:(q + 1) * 128]
        kk = _dot(h, wk_ref[...])
        ssq = _dot((kk * kk).astype(BF16), bd_ref[...])
        kn = kk * lax.rsqrt(ssq * (1.0 / HD) + RMS_EPS) * gk_ref[...]
        blk = (i - 1) * (tm // SLC_BLOCK) + lax.broadcasted_iota(
            jnp.int32, (tm, G * 128), 0) // SLC_BLOCK
        tail = jnp.where((lane == HD) | (lane == HD + 1) | (lane == HD + KEY_BLK + blk), 1.0, 0.0)
        ks_ref[...] = jnp.where(lane < HD, kn, tail).astype(BF16)
        kn_sw = jnp.concatenate(
            [pltpu.roll(kn[:, g * 128:(g + 1) * 128], HD, axis=1) for g in range(G)], axis=1)
        kw_ref[...] = jnp.where(lane < HD, kn_sw, tail).astype(BF16)
        vt = _dot_nt(wvt_ref[...], h)
        srow = lax.broadcasted_iota(jnp.int32, (VT_ROWS - HD, tm), 0)
        ones_blk = jnp.where(srow == 0, 1.0, 0.0).astype(BF16)
        for kind, o_ref in enumerate((vst_ref, vwt_ref)):
            for g in range(G):
                r0 = (kind * G + g) * HD
                o_ref[g * VT_ROWS:g * VT_ROWS + HD, :] = vt[r0:r0 + HD, :].astype(BF16)
                o_ref[g * VT_ROWS + HD:(g + 1) * VT_ROWS, :] = ones_blk


def _segment_ones(width, seg):
    idx = np.arange(width) // seg
    return (idx[:, None] == idx[None, :]).astype(np.float32)


def kv_proj(x2d, g, sh, sc, wk, wvt, wcc, gk, batch, seq, tm=KV_PAD):
    assert tm == KV_PAD and seq % tm == 0
    M, D = x2d.shape
    G = NSA_GROUPS
    nt = seq // tm
    sp = seq + KV_PAD
    ncs = wcc.shape[1] // 128
    bd = jnp.asarray(_segment_ones(G * 128, NSA_HD), dtype=BF16)
    xrow = lambda b, i: (b * nt + jnp.maximum(i - 1, 0), 0)
    bvec = pl.BlockSpec((None, 1, D), lambda b, i: (b, 0, 0))
    full = lambda a: pl.BlockSpec(a.shape, lambda b, i: (0,) * a.ndim)
    return pl.pallas_call(
        _kv_proj_body,
        grid=(batch, nt + 1),
        in_specs=[pl.BlockSpec((tm, D), xrow),
                  pl.BlockSpec((1, D), lambda b, i: (0, 0)),
                  bvec, bvec, full(wk), full(wvt), full(wcc), full(gk), full(bd)],
        out_specs=[pl.BlockSpec((None, tm, G * 128), lambda b, i: (b, i, 0)),
                   pl.BlockSpec((None, tm, G * 128), lambda b, i: (b, i, 0)),
                   pl.BlockSpec((None, G * VT_ROWS, tm), lambda b, i: (b, 0, i)),
                   pl.BlockSpec((None, G * VT_ROWS, tm), lambda b, i: (b, 0, i)),
                   pl.BlockSpec((ncs, tm, 128),
                                lambda b, i: (0, b * nt + jnp.maximum(i - 1, 0), 0))],
        out_shape=[jax.ShapeDtypeStruct((batch, sp, G * 128), BF16),
                   jax.ShapeDtypeStruct((batch, sp, G * 128), BF16),
                   jax.ShapeDtypeStruct((batch, G * VT_ROWS, sp), BF16),
                   jax.ShapeDtypeStruct((batch, G * VT_ROWS, sp), BF16),
                   jax.ShapeDtypeStruct((ncs, M, 128), F32)],
        compiler_params=_cparams(("parallel", "arbitrary")),
        name="kv_proj",
    )(x2d, g.reshape(1, D), sh, sc, wk, wvt, wcc, gk, bd)


def _gelu_tanh(x):
    c = math.sqrt(2.0 / math.pi)
    return 0.5 * x * (1.0 + jnp.tanh(c * (x + 0.044715 * (x * x * x))))


def _compress_body(cc0_ref, cc1_ref, cc2_ref, cc3_ref, pos_ref, w1_ref, w2_ref, w2t_ref, g_ref,
                   kc_ref, vct_ref):
    HD = NSA_HD
    nwin = kc_ref.shape[0]
    hid = w2_ref.shape[0]
    for p, cc_ref in enumerate((cc0_ref, cc1_ref, cc2_ref, cc3_ref)):
        kind = p // 2
        x = jnp.concatenate([cc_ref[pl.ds(i, nwin, stride=CMP_STRIDE), :]
                             for i in range(CMP_STRIDE)], axis=1)
        u = _dot((x + pos_ref[kind, 0:1, :]).astype(BF16), w1_ref[kind, 0])
        v = _dot((x + pos_ref[kind, 1:2, :]).astype(BF16), w1_ref[kind, 1])
        pre = u + pltpu.roll(v, nwin - 1, axis=0)
        for s in range(2):
            g = 2 * (p % 2) + s
            hmid = _gelu_tanh(pre[:, s * hid:(s + 1) * hid]).astype(BF16)
            if kind == 0:
                y = _dot(hmid, w2_ref[...])
                var = jnp.mean(y * y, axis=-1, keepdims=True)
                yn = y * lax.rsqrt(var + RMS_EPS) * g_ref[...]
                kc_ref[:, g * 128:g * 128 + HD] = yn.astype(BF16)
                kc_ref[:, g * 128 + HD:(g + 1) * 128] = jnp.zeros((nwin, 128 - HD), BF16)
            else:
                vct_ref[g * HD:(g + 1) * HD, :] = _dot_nt(w2t_ref[...], hmid).astype(BF16)


def _pair_expand(w):
    k, t, d, c = w.shape
    eye = jnp.eye(2, dtype=w.dtype)
    return jnp.einsum("ktdc,su->ktsduc", w, eye).reshape(k, t * 2 * d, 2 * c)


def compress(cc, pos, w1, w2k, w2vt, g, batch, seq):
    G, HD = NSA_GROUPS, NSA_HD
    nwin = seq // CMP_STRIDE
    hid = w1.shape[2]
    half = CMP_BLOCK // 2
    assert half == CMP_STRIDE
    w1r = w1.reshape(2, CMP_BLOCK, HD, hid)
    w1x = jnp.stack([_pair_expand(w1r[:, :half]), _pair_expand(w1r[:, half:])], axis=1).astype(BF16)
    posr = jnp.tile(pos.reshape(2, 2, half, 1, HD), (1, 1, 1, 2, 1)).reshape(2, 2, half * 2 * HD)
    resident = lambda a: pl.BlockSpec(a.shape, lambda b: (0,) * a.ndim, pipeline_mode=pl.Buffered(1))
    return pl.pallas_call(
        _compress_body,
        grid=(batch,),
        in_specs=[pl.BlockSpec((None, seq, 128), functools.partial(lambda q, b: (q, b, 0), q))
                  for q in range(4)]
                 + [resident(posr), resident(w1x), resident(w2k), resident(w2vt), resident(g)],
        out_specs=[pl.BlockSpec((None, nwin, G * 128), lambda b: (b, 0, 0)),
                   pl.BlockSpec((None, G * HD, nwin), lambda b: (b, 0, 0))],
        out_shape=[jax.ShapeDtypeStruct((batch, nwin, G * 128), BF16),
                   jax.ShapeDtypeStruct((batch, G * HD, nwin), BF16)],
        compiler_params=_cparams(("parallel",)),
        name="compress",
    )(cc, cc, cc, cc, posr, w1x, w2k, w2vt, g)


def _t5_bucket(dist):
    n = jnp.maximum(dist, 0)
    max_exact = REL_BUCKETS // 2
    nf = jnp.maximum(n, 1).astype(F32)
    large = max_exact + (jnp.log(nf / max_exact) / math.log(REL_MAX_DIST / max_exact)
                         * (REL_BUCKETS - max_exact)).astype(jnp.int32)
    large = jnp.minimum(large, REL_BUCKETS - 1)
    return jnp.where(n < max_exact, n, large)


def _table_lookup(bucket, tab_ref, h):
    out = jnp.zeros(bucket.shape, F32)
    for k in range(REL_BUCKETS):
        out = jnp.where(bucket == k, tab_ref[k, h], out)
    return out


def _bias_prep_body(tab_ref, bc_ref, tp_ref, far_ref):
    h = pl.program_id(0)
    S = bc_ref.shape[1]
    far = tab_ref[REL_BUCKETS - 1, h]
    far_ref[...] = jnp.full(far_ref.shape, far * LOG2E, F32)

    dist = lax.broadcasted_iota(jnp.int32, (8, S), 1)
    by_dist = _table_lookup(_t5_bucket(dist), tab_ref, h) * LOG2E
    shifted = pltpu.roll(jnp.broadcast_to(by_dist[0:1, :], (NCMP_PAD, S)), 0, axis=1,
                         stride=CMP_STRIDE, stride_axis=0)
    shifted = pltpu.roll(shifted, CMP_BLOCK - 1, axis=1)
    n = lax.broadcasted_iota(jnp.int32, (NCMP_PAD, S), 0)
    t = lax.broadcasted_iota(jnp.int32, (NCMP_PAD, S), 1)
    bc_ref[...] = jnp.where(t >= n * CMP_STRIDE + CMP_BLOCK - 1, shifted, NEG_INF)
    j = lax.broadcasted_iota(jnp.int32, (ATT_TK, ATT_QB), 0)
    i = lax.broadcasted_iota(jnp.int32, (ATT_TK, ATT_QB), 1)
    for d in range(2):
        dist = d * ATT_TK + i - j
        rel = (_table_lookup(_t5_bucket(dist), tab_ref, h) - far) * LOG2E
        tp_ref[d] = jnp.where(dist >= 0, rel, NEG_INF)
    tp_ref[2] = jnp.where(i < j, 0.0, NEG_INF)


def bias_prep(rel_table, seq):
    assert ATT_TK == ATT_QB and ATT_TK + 1 > 113
    return pl.pallas_call(
        _bias_prep_body,
        grid=(NSA_HEADS,),
        in_specs=[pl.BlockSpec(memory_space=pltpu.SMEM)],
        out_specs=[pl.BlockSpec((None, NCMP_PAD, seq), lambda h: (h, 0, 0)),
                   pl.BlockSpec((None, 3, ATT_TK, ATT_QB), lambda h: (h, 0, 0, 0)),
                   pl.BlockSpec((None, 8, 128), lambda h: (h, 0, 0))],
        out_shape=[jax.ShapeDtypeStruct((NSA_HEADS, NCMP_PAD, seq), F32),
                   jax.ShapeDtypeStruct((NSA_HEADS, 3, ATT_TK, ATT_QB), F32),
                   jax.ShapeDtypeStruct((NSA_HEADS, 8, 128), F32)],
        compiler_params=_cparams(("parallel",)),
        name="bias_prep",
    )(rel_table)


def _q_proj_body(x_ref, g_ref, sh_ref, sc_ref, wqt_ref, wgt_ref, bg_ref, gq_ref, qt_ref, gt_ref):
    HD = NSA_HD
    h = _norm_mod(x_ref[...], g_ref[...], sh_ref[...], sc_ref[...]).astype(BF16)
    gt_ref[...] = _dot_nt(wgt_ref[...], h) + bg_ref[...]
    qt = _dot_nt(wqt_ref[...], h)
    scale = gq_ref[...] * (HD ** -0.5 * LOG2E)
    for hh in range(NSA_HEADS):
        seg = qt[hh * HD:(hh + 1) * HD, :]
        var = jnp.mean(seg * seg, axis=0, keepdims=True)
        qt_ref[hh * HD:(hh + 1) * HD, :] = (seg * lax.rsqrt(var + RMS_EPS) * scale).astype(BF16)


def q_proj(x2d, g, sh, sc, wqt, wgt, bg, gq, batch, seq, tm=512):
    M, D = x2d.shape
    nt = seq // tm
    nq = wqt.shape[0]
    ng = wgt.shape[0]
    bvec = pl.BlockSpec((None, 1, D), lambda i: (i // nt, 0, 0))
    full = lambda a: pl.BlockSpec(a.shape, lambda i: (0,) * a.ndim)
    return pl.pallas_call(
        _q_proj_body,
        grid=(M // tm,),
        in_specs=[pl.BlockSpec((tm, D), lambda i: (i, 0)),
                  pl.BlockSpec((1, D), lambda i: (0, 0)),
                  bvec, bvec, full(wqt), full(wgt), full(bg), full(gq)],
        out_specs=[pl.BlockSpec((None, nq, tm), lambda i: (i // nt, 0, i % nt)),
                   pl.BlockSpec((None, ng, tm), lambda i: (i // nt, 0, i % nt))],
        out_shape=[jax.ShapeDtypeStruct((batch, nq, seq), BF16),
                   jax.ShapeDtypeStruct((batch, ng, seq), F32)],
        compiler_params=_cparams(("parallel",)),
        name="q_proj",
    )(x2d, g.reshape(1, D), sh, sc, wqt, wgt, bg, gq)


def _heads_on_lanes(pieces):
    return jnp.concatenate(pieces, axis=1)


def _nsa_body(gps, qt_ref, gt_ref, kc_ref, vct_ref, ks_ref, kw_ref, vst_ref, vwt_ref,
              bct_ref, tp_ref, far_ref, ovt_ref, out_ref):
    qb = pl.program_id(2)
    QB, HD, HPG = ATT_QB, NSA_HD, NSA_HPG
    R = HPG * QB
    t0 = pl.multiple_of(qb * QB, QB)
    near0 = pl.multiple_of(t0 + WINDOW - ATT_TK, ATT_TK)
    wlen = WINDOW + QB
    nblk = ovt_ref.shape[0]
    jb = lax.broadcasted_iota(jnp.int32, (nblk, QB), 0)
    jbf = jb.astype(F32)
    qid = jnp.right_shift(t0 + lax.broadcasted_iota(jnp.int32, (nblk, QB), 1), 6)
    forced = (jb == 0) | (jb == qid) | (jb == qid - 1)
    srow = lax.broadcasted_iota(jnp.int32, (8, R), 0)

    def finish(acc):
        return acc[0:HD, :] / acc[HD:HD + 1, :]

    def tile_part(s_tile, v_tile):
        m_t = jnp.max(s_tile, axis=0, keepdims=True)
        m_safe = jnp.where(m_t == NEG_INF, 0.0, m_t)
        p = jnp.exp2(s_tile - m_safe).astype(BF16)
        return m_t, _dot(v_tile, p)[0:HD + 8, :]

    def combine(parts):
        m_fin = parts[0][0]
        for m_t, _ in parts[1:]:
            m_fin = jnp.maximum(m_fin, m_t)
        acc = None
        for m_t, pv in parts:
            term = jnp.exp2(m_t - m_fin) * pv
            acc = term if acc is None else acc + term
        return m_fin, acc

    groups = range(gps)
    hs = [[gl * HPG + h for h in range(HPG)] for gl in groups]
    kcol = [slice(gl * 128, (gl + 1) * 128) for gl in groups]
    vrow = [slice(gl * VT_ROWS, (gl + 1) * VT_ROWS) for gl in groups]

    def stationary(q_all, hi_lo, block_rows):
        pad = jnp.where(srow == 0, MASK_BIG, 0.0)
        rest = jnp.zeros((128 - HD - KEY_PAD - 8, R), F32)
        tail = jnp.concatenate([hi_lo, block_rows, pad, rest], axis=0).astype(BF16)
        return jnp.concatenate([q_all, tail], axis=0)

    q_alls, hi_los, qms = [], [], []
    for gl in groups:
        q_alls.append(_heads_on_lanes([qt_ref[h * HD:(h + 1) * HD, :] for h in hs[gl]]))
        far = _heads_on_lanes([far_ref[h, 0:1, :] for h in hs[gl]])
        hi = far.astype(BF16).astype(F32)
        hi_los.append(jnp.where(srow == 0, hi, jnp.where(srow == 1, far - hi, 0.0)))
        qms.append(stationary(q_alls[gl], hi_los[gl], jnp.zeros((nblk, R), F32)))

    s_cs = [_dot(kc_ref[:, kcol[gl]], qms[gl]) for gl in groups]
    s_ws = [_dot(kw_ref[pl.ds(t0, wlen), kcol[gl]], qms[gl]) for gl in groups]
    s_ns = [_dot(ks_ref[pl.ds(near0, 2 * ATT_TK), kcol[gl]], qms[gl]) for gl in groups]

    o_cmps, scores = [], []
    for gl in groups:
        s_c = s_cs[gl] + _heads_on_lanes([bct_ref[h] for h in hs[gl]])
        m_c = jnp.max(s_c, axis=0, keepdims=True)
        m_c = jnp.where(m_c == NEG_INF, 0.0, m_c)
        e_c = jnp.exp2(s_c - m_c)
        p_c = e_c / jnp.maximum(jnp.sum(e_c, axis=0, keepdims=True), jnp.finfo(F32).tiny)
        o_cmps.append(_dot(vct_ref[gl * HD:(gl + 1) * HD, :], p_c.astype(BF16)))
        p_sum = p_c[:, 0:QB]
        for h in range(1, HPG):
            p_sum = p_sum + p_c[:, h * QB:(h + 1) * QB]
        p_hi = p_sum.astype(BF16)
        p_lo = (p_sum - p_hi.astype(F32)).astype(BF16)
        imp = _dot(ovt_ref[...], jnp.concatenate([p_hi, p_lo], axis=0))
        score = jnp.where(forced, FORCE_SCORE, imp)
        scores.append(jnp.where(jb <= qid, score, NEG_INF))

    nwt = wlen // ATT_TK
    o_wins = []
    for gl in groups:
        parts = []
        for i in range(nwt):
            d = nwt - 1 - i
            s_t = s_ws[gl][i * ATT_TK:(i + 1) * ATT_TK]
            if d in (0, 1):
                s_t = s_t + _heads_on_lanes([tp_ref[h, d] for h in hs[gl]])
            elif d == nwt - 1:
                s_t = s_t + _heads_on_lanes([tp_ref[h, 2] for h in hs[gl]])
            parts.append(tile_part(s_t, vwt_ref[vrow[gl], pl.ds(t0 + i * ATT_TK, ATT_TK)]))
        o_wins.append(finish(combine(parts)[1]))

    msels = [jnp.full((nblk, QB), NEG_INF, F32) for _ in groups]
    for _ in range(SLC_TOPK):
        for gl in groups:
            mx = jnp.max(scores[gl], axis=0, keepdims=True)
            first = jnp.min(jnp.where(scores[gl] == mx, jbf, float(nblk)), axis=0, keepdims=True)
            pick = jbf == first
            msels[gl] = jnp.where(pick & (mx > NEG_INF), 0.0, msels[gl])
            scores[gl] = jnp.where(pick, NEG_INF, scores[gl])
    carry0 = []
    for gl in groups:
        prev_mask = jnp.concatenate(
            [jnp.broadcast_to(jnp.max(jnp.where(jb == 2 * qb - 2 + r, msels[gl], NEG_INF), axis=0,
                                      keepdims=True), (SLC_BLOCK, QB)) for r in range(2)], axis=0)
        s_prev = s_ns[gl][0:ATT_TK] + _heads_on_lanes([tp_ref[h, 1] + prev_mask for h in hs[gl]])
        s_diag = s_ns[gl][ATT_TK:] + _heads_on_lanes([tp_ref[h, 0] for h in hs[gl]])
        carry0.append(combine([
            tile_part(s_diag, vst_ref[vrow[gl], pl.ds(near0 + ATT_TK, ATT_TK)]),
            tile_part(s_prev, vst_ref[vrow[gl], pl.ds(near0, ATT_TK)])]))

    qss = [stationary(q_alls[gl], hi_los[gl], _heads_on_lanes(
        [jnp.where(msels[gl] == 0.0, 0.0, MASK_BIG)] * HPG)) for gl in groups]

    def far_step(c, carry):
        tile0 = qb + (WINDOW - ATT_TK) // ATT_TK - (c + 1) * (FAR_CHUNK // ATT_TK)
        row0 = pl.multiple_of(tile0 * ATT_TK, ATT_TK)
        s_fs = [_dot(ks_ref[pl.ds(row0, FAR_CHUNK), kcol[gl]], qss[gl]) for gl in groups]
        out = []
        for gl in groups:
            parts = [carry[gl]]
            for j in range(FAR_CHUNK // ATT_TK):
                parts.append(tile_part(s_fs[gl][j * ATT_TK:(j + 1) * ATT_TK],
                                       vst_ref[vrow[gl], pl.ds(row0 + j * ATT_TK, ATT_TK)]))
            out.append(combine(parts))
        return tuple(out)

    n_far = (qb + 2) // 4
    sel = lax.fori_loop(0, n_far, far_step, tuple(carry0))

    gates = jax.nn.sigmoid(gt_ref[...])
    for gl in groups:
        o_cmp, o_win = o_cmps[gl], o_wins[gl]
        o_sel = finish(sel[gl][1])
        outs = []
        for h in range(HPG):
            lanes = slice(h * QB, (h + 1) * QB)
            r = gl * 16 + 3 * h
            outs.append(gates[r:r + 1, :] * o_cmp[:, lanes]
                        + gates[r + 1:r + 2, :] * o_sel[:, lanes]
                        + gates[r + 2:r + 3, :] * o_win[:, lanes])
        for pair in range(HPG // 2):
            two = jnp.concatenate(outs[2 * pair:2 * pair + 2], axis=0)
            c0 = gl * HPG * HD + pair * 2 * HD
            out_ref[:, c0:c0 + 2 * HD] = two.T.astype(out_ref.dtype)


def _overlap_matrix_t(nblk):
    start = np.arange(NCMP_PAD) * CMP_STRIDE
    sj = np.arange(nblk) * SLC_BLOCK
    ov = (np.minimum(start[None, :] + CMP_BLOCK, sj[:, None] + SLC_BLOCK)
          - np.maximum(start[None, :], sj[:, None]))
    ov = np.clip(ov, 0, None) / CMP_BLOCK
    ov[:, NCMP_PAD - 1] = 0.0
    return np.concatenate([ov, ov], axis=1).astype(np.float32)


def nsa_attn(qt, gt, kc, vct, ks, kw, vst, vwt, bias_ct, tiles, far, batch, seq, gps=ATT_GPS):
    QB, HD, HPG, G = ATT_QB, NSA_HD, NSA_HPG, NSA_GROUPS
    nblk = seq // SLC_BLOCK
    assert nblk % 8 == 0 and seq % QB == 0 and seq // CMP_STRIDE == NCMP_PAD and G % gps == 0
    assert WINDOW % ATT_TK == 0 and FAR_CHUNK == 4 * ATT_TK and KV_PAD >= FAR_CHUNK - ATT_TK
    assert KEY_BLK % 8 == 0 and KEY_BLK + nblk <= KEY_PAD and KEY_PAD + 8 <= 128 - HD
    nq = seq // QB
    sp = seq + KV_PAD
    ovt = jnp.asarray(_overlap_matrix_t(nblk), dtype=BF16)
    return pl.pallas_call(
        functools.partial(_nsa_body, gps),
        grid=(batch, G // gps, nq),
        in_specs=[pl.BlockSpec((None, gps * HPG * HD, QB), lambda b, g, i: (b, g, i)),
                  pl.BlockSpec((None, gps * 16, QB), lambda b, g, i: (b, g, i)),
                  pl.BlockSpec((None, NCMP_PAD, gps * 128), lambda b, g, i: (b, 0, g)),
                  pl.BlockSpec((None, gps * HD, NCMP_PAD), lambda b, g, i: (b, g, 0)),
                  pl.BlockSpec((None, sp, gps * 128), lambda b, g, i: (b, 0, g)),
                  pl.BlockSpec((None, sp, gps * 128), lambda b, g, i: (b, 0, g)),
                  pl.BlockSpec((None, gps * VT_ROWS, sp), lambda b, g, i: (b, g, 0)),
                  pl.BlockSpec((None, gps * VT_ROWS, sp), lambda b, g, i: (b, g, 0)),
                  pl.BlockSpec((gps * HPG, NCMP_PAD, QB), lambda b, g, i: (g, 0, i)),
                  pl.BlockSpec((gps * HPG, 3, ATT_TK, QB), lambda b, g, i: (g, 0, 0, 0)),
                  pl.BlockSpec((gps * HPG, 8, 128), lambda b, g, i: (g, 0, 0)),
                  pl.BlockSpec((nblk, 2 * NCMP_PAD), lambda b, g, i: (0, 0))],
        out_specs=pl.BlockSpec((QB, gps * HPG * HD), lambda b, g, i: (b * nq + i, g)),
        out_shape=jax.ShapeDtypeStruct((batch * seq, NSA_HEADS * HD), BF16),
        compiler_params=_cparams(("parallel", "parallel", "arbitrary")),
        name="nsa_attn",
    )(qt, gt, kc, vct, ks, kw, vst, vwt, bias_ct, tiles, far, ovt)


def _mlstm_layer(x2d, g_mix, sh, sc, w_in, b_if, g_out, batch, seq):
    nbig = 2 * ML_HEADS * ML_DQK + 2 * ML_HEADS * ML_DV
    ng = 2 * ML_HEADS
    wg = jnp.pad(w_in[:, nbig:], ((0, 0), (0, 128 - ng)))
    bg = jnp.pad(b_if, (0, 128 - ng))
    proj, gates = norm_proj(x2d, g_mix, sh, sc, [w_in[:, :nbig], wg], [None, bg], [BF16, F32])
    return mlstm_core(proj, gates, g_out, batch, seq)


def _nsa_shared(x2d, g_kv, kv_sh, kv_sc, w_kv, pos_k, w_k1, w_k2, pos_v, w_v1, w_v2,
                g_knorm, batch, seq):
    gw = NSA_GROUPS * NSA_HD
    part = lambda i: w_kv[:, i * gw:(i + 1) * gw]
    hd = NSA_HD
    wk = jnp.concatenate([p[:, g * hd:(g + 1) * hd] for g in range(NSA_GROUPS)
                          for p in (part(2), part(4))], axis=1).astype(BF16)
    gk = jnp.tile(jnp.concatenate([g_knorm[1], g_knorm[2]]), NSA_GROUPS).reshape(1, -1)
    wvt = jnp.concatenate([part(3), part(5)], axis=1).T.astype(BF16)
    wcc = jnp.concatenate([part(0), part(1)], axis=1).astype(BF16)
    ks, kw, vst, vwt, cc = kv_proj(x2d, g_kv, kv_sh, kv_sc, wk, wvt, wcc, gk, batch, seq)
    pos = jnp.stack([pos_k, pos_v])
    w1 = jnp.stack([w_k1, w_v1])
    kc, vct = compress(cc, pos, w1, w_k2.astype(BF16), w_v2.T.astype(BF16), g_knorm[0:1],
                       batch, seq)
    return kc, vct, ks, kw, vst, vwt


def _gate_weights_t(w_q, b_gate):
    nq = NSA_HEADS * NSA_HD
    per = 3 * NSA_HPG
    wg = w_q[:, nq:].T.reshape(NSA_GROUPS, per, -1)
    wg = jnp.pad(wg, ((0, 0), (0, 16 - per), (0, 0))).reshape(NSA_GROUPS * 16, -1)
    bg = jnp.pad(b_gate.reshape(NSA_GROUPS, per), ((0, 0), (0, 16 - per))).reshape(-1, 1)
    return wg.astype(BF16), bg


def _nsa_layer(x2d, g_mix, sh, sc, shared, w_q, b_gate, g_qnorm, bias_ct, tiles, far, batch, seq):
    nq = NSA_HEADS * NSA_HD
    wgt, bg = _gate_weights_t(w_q, b_gate)
    qt, gt = q_proj(x2d, g_mix, sh, sc, w_q[:, :nq].T.astype(BF16), wgt, bg,
                    g_qnorm.reshape(NSA_HD, 1), batch, seq)
    kc, vct, ks, kw, vst, vwt = shared
    return nsa_attn(qt, gt, kc, vct, ks, kw, vst, vwt, bias_ct, tiles, far, batch, seq)


def kernel(x, c, w_ada, b_ada, g_norm_mix, g_norm_ffn, w_ffn_in, w_ffn_out, w_a_in, b_a_if, g_a_out, w_a_out, w_kv_ada, b_kv_ada, g_kv_norm, w_kv, pos_cmp_k, w_cmp_k1, w_cmp_k2, pos_cmp_v, w_cmp_v1, w_cmp_v2, g_knorm, w_b_q, b_b_gate, g_qnorm, w_b_out, rel_table):
    B, S, D = x.shape
    depth = w_ada.shape[0]
    n_a = w_a_in.shape[0]
    x2d = x.reshape(B * S, D)
    mods = ada_mod(c, w_ada, b_ada)
    kv_mod = ada_mod(c, w_kv_ada[None], b_kv_ada[None])[0]
    shared = None
    bias_ct = tiles = far = None
    for layer in range(depth):
        sh1, sc1, ga1, sh2, sc2, ga2 = [mods[layer, :, i * D:(i + 1) * D].reshape(B, 1, D)
                                        for i in range(6)]
        if layer < n_a:
            mixed = _mlstm_layer(x2d, g_norm_mix[layer], sh1, sc1, w_a_in[layer], b_a_if[layer],
                                 g_a_out[layer], B, S)
            w_mix = (w_a_out, layer)
        else:
            j = layer - n_a
            if shared is None:
                kv_sh = kv_mod[:, :D].reshape(B, 1, D)
                kv_sc = kv_mod[:, D:].reshape(B, 1, D)
                shared = _nsa_shared(x2d, g_kv_norm, kv_sh, kv_sc, w_kv, pos_cmp_k, w_cmp_k1,
                                     w_cmp_k2, pos_cmp_v, w_cmp_v1, w_cmp_v2, g_knorm, B, S)
                bias_ct, tiles, far = bias_prep(rel_table, S)
            mixed = _nsa_layer(x2d, g_norm_mix[layer], sh1, sc1, shared, w_b_q[j], b_b_gate[j],
                               g_qnorm[j], bias_ct, tiles, far, B, S)
            w_mix = (w_b_out, j)
        x2d = mix_ffn(mixed, w_mix, x2d, ga1, g_norm_ffn[layer], sh2, sc2, ga2,
                      (w_ffn_in, layer), (w_ffn_out, layer))
    return x2d.reshape(B, S, D)
```

```python
import functools
import math

import jax
import jax.numpy as jnp
import numpy as np
from jax import lax
from jax.experimental import pallas as pl
from jax.experimental.pallas import tpu as pltpu

F32 = jnp.float32
BF16 = jnp.bfloat16
NEG_INF = float("-inf")
LOG2E = math.log2(math.e)

RMS_EPS = 1e-6

ML_HEADS = 4
ML_DQK = 128
ML_DV = 256
ML_LC = 256

NSA_HEADS = 16
NSA_GROUPS = 4
NSA_HPG = 4
NSA_HD = 64
CMP_BLOCK = 32
CMP_STRIDE = 16
SLC_BLOCK = 64
SLC_TOPK = 8
WINDOW = 512
FORCE_SCORE = 1e4
REL_BUCKETS = 32
REL_MAX_DIST = 128
ATT_QB = 128
ATT_TK = 128
ATT_GPS = 4
NCMP_PAD = 128
KV_PAD = WINDOW
FAR_CHUNK = 512
VT_ROWS = 80
KEY_BLK = 8
KEY_PAD = 40
MASK_BIG = -32768.0

VMEM_LIMIT = 56 * 1024 * 1024


def _cparams(sem):
    return pltpu.CompilerParams(dimension_semantics=sem, vmem_limit_bytes=VMEM_LIMIT)


def _dot(a, b):
    return jnp.dot(a, b, preferred_element_type=F32)


def _dot_nt(a, b):
    return lax.dot_general(a, b, (((1,), (1,)), ((), ())), preferred_element_type=F32)


def _dot_tn(a, b):
    return lax.dot_general(a, b, (((0,), (0,)), ((), ())), preferred_element_type=F32)


def _norm_mod(x, g, sh, sc):
    var = jnp.mean(x * x, axis=-1, keepdims=True)
    y = x * lax.rsqrt(var + RMS_EPS) * g
    return y * (1.0 + sc) + sh


def _ada_body(c_ref, w_ref, b_ref, o_ref):
    c = c_ref[...]
    ca = c * jax.nn.sigmoid(c)
    nb = ca.shape[0]
    a_hi = ca.astype(BF16).astype(F32)
    a_mid = (ca - a_hi).astype(BF16).astype(F32)
    a_lo = ca - a_hi - a_mid
    a3 = jnp.concatenate([a_hi, a_mid, a_lo, jnp.zeros_like(ca)], axis=0).astype(BF16)
    w = w_ref[...]
    w_hi = w.astype(BF16)
    w_lo = (w - w_hi.astype(F32)).astype(BF16)
    p = _dot(a3, w_hi)
    q = _dot(a3[0:2 * nb], w_lo)
    o_ref[...] = (p[0:nb] + p[nb:2 * nb] + p[2 * nb:3 * nb] + q[0:nb] + q[nb:2 * nb]) + b_ref[...]


def ada_mod(c, w, b, tn=1024):
    L, D, N = w.shape
    B = c.shape[0]
    return pl.pallas_call(
        _ada_body,
        grid=(L, N // tn),
        in_specs=[pl.BlockSpec((B, D), lambda l, j: (0, 0)),
                  pl.BlockSpec((None, D, tn), lambda l, j: (l, 0, j)),
                  pl.BlockSpec((None, 1, tn), lambda l, j: (l, 0, j))],
        out_specs=pl.BlockSpec((None, B, tn), lambda l, j: (l, 0, j)),
        out_shape=jax.ShapeDtypeStruct((L, B, N), F32),
        compiler_params=_cparams(("parallel", "parallel")),
        name="ada_mod",
    )(c, w, b.reshape(L, 1, N))


def _norm_proj_body(n_out, has_bias, tn, x_ref, g_ref, sh_ref, sc_ref, *refs):
    w_refs = refs[:n_out]
    b_refs = refs[n_out:2 * n_out]
    o_refs = refs[2 * n_out:3 * n_out]
    tm = x_ref.shape[0]
    half = tm // 2
    for r0 in (0, half):
        h = _norm_mod(x_ref[r0:r0 + half, :], g_ref[...], sh_ref[...], sc_ref[...]).astype(BF16)
        for w_ref, b_ref, o_ref, hb in zip(w_refs, b_refs, o_refs, has_bias):
            n = w_ref.shape[1]
            step = min(tn, n)
            for n0 in range(0, n, step):
                acc = _dot(h, w_ref[:, n0:n0 + step].astype(BF16))
                if hb:
                    acc = acc + b_ref[:, n0:n0 + step]
                o_ref[r0:r0 + half, n0:n0 + step] = acc.astype(o_ref.dtype)


def norm_proj(x2d, g, sh, sc, ws, biases, out_dtypes, tm=1024, tn=512):
    M, D = x2d.shape
    B = sh.shape[0]
    tiles_per_batch = (M // B) // tm
    n_out = len(ws)
    has_bias = tuple(b is not None for b in biases)
    bias_args = [(b if b is not None else jnp.zeros((w.shape[1],), F32)).reshape(1, -1)
                 for b, w in zip(biases, ws)]
    in_specs = [pl.BlockSpec((tm, D), lambda i: (i, 0)),
                pl.BlockSpec((1, D), lambda i: (0, 0)),
                pl.BlockSpec((None, 1, D), lambda i: (i // tiles_per_batch, 0, 0)),
                pl.BlockSpec((None, 1, D), lambda i: (i // tiles_per_batch, 0, 0))]
    resident = lambda a: pl.BlockSpec(a.shape, lambda i: (0, 0), pipeline_mode=pl.Buffered(1))
    in_specs += [resident(w) for w in ws] + [resident(b) for b in bias_args]
    out_specs = [pl.BlockSpec((tm, w.shape[1]), lambda i: (i, 0)) for w in ws]
    out_shape = [jax.ShapeDtypeStruct((M, w.shape[1]), dt) for w, dt in zip(ws, out_dtypes)]
    return pl.pallas_call(
        functools.partial(_norm_proj_body, n_out, has_bias, tn),
        grid=(M // tm,),
        in_specs=in_specs, out_specs=out_specs, out_shape=out_shape,
        compiler_params=_cparams(("parallel",)),
        name="norm_proj",
    )(x2d, g.reshape(1, D), sh, sc, *ws, *bias_args)


def _mix_ffn_body(tf, a_ref, wm_ref, x_ref, ga1_ref, g_ref, sh_ref, sc_ref, ga2_ref,
                  wi_ref, wo_ref, o_ref, act_s):
    F = wo_ref.shape[0]
    x1 = x_ref[...] + ga1_ref[...] * _dot(a_ref[...], wm_ref[...].astype(BF16))
    h = _norm_mod(x1, g_ref[...], sh_ref[...], sc_ref[...]).astype(BF16)
    for f0 in range(0, F, tf):
        gate = _dot(h, wi_ref[:, f0:f0 + tf].astype(BF16))
        up = _dot(h, wi_ref[:, F + f0:F + f0 + tf].astype(BF16))
        act_s[:, f0:f0 + tf] = (gate * jax.nn.sigmoid(gate) * up).astype(BF16)
    y = _dot(act_s[:, 0:tf], wo_ref[0:tf, :].astype(BF16))
    for f0 in range(tf, F, tf):
        y = y + _dot(act_s[:, f0:f0 + tf], wo_ref[f0:f0 + tf, :].astype(BF16))
    o_ref[...] = x1 + ga2_ref[...] * y


def mix_ffn(a, w_mix, x2d, ga1, g, sh, sc, ga2, w_in, w_out, tm=512, tf=256):
    M, D = x2d.shape
    K = a.shape[1]
    F = w_out[0].shape[1]
    B = sh.shape[0]
    tiles_per_batch = (M // B) // tm
    bvec = pl.BlockSpec((None, 1, D), lambda i: (i // tiles_per_batch, 0, 0))

    def resident(stacked_layer):
        w, layer = stacked_layer
        return pl.BlockSpec((None,) + w.shape[1:], lambda i: (layer, 0, 0),
                            pipeline_mode=pl.Buffered(1))

    return pl.pallas_call(
        functools.partial(_mix_ffn_body, tf),
        grid=(M // tm,),
        in_specs=[pl.BlockSpec((tm, K), lambda i: (i, 0)),
                  resident(w_mix),
                  pl.BlockSpec((tm, D), lambda i: (i, 0)),
                  bvec,
                  pl.BlockSpec((1, D), lambda i: (0, 0)),
                  bvec, bvec, bvec,
                  resident(w_in), resident(w_out)],
        out_specs=pl.BlockSpec((tm, D), lambda i: (i, 0)),
        out_shape=jax.ShapeDtypeStruct((M, D), F32),
        scratch_shapes=[pltpu.VMEM((tm, F), BF16)],
        compiler_params=_cparams(("parallel",)),
        name="mix_ffn",
    )(a, w_mix[0], x2d, ga1, g.reshape(1, D), sh, sc, ga2, w_in[0], w_out[0])


def _sublane_scan(x, op, fill):
    n = x.shape[0]
    row = lax.broadcasted_iota(jnp.int32, x.shape, 0)
    sh = 1
    while sh < n:
        x = op(x, jnp.where(row >= sh, pltpu.roll(x, sh, axis=0), fill))
        sh *= 2
    return x


def _mlstm_body(q_ref, k_ref, v_ref, o_ref, gc_ref, gout_ref, out_ref, c_s, n_s, m_s):
    c_idx = pl.program_id(1)
    LC = q_ref.shape[0]
    NH = ML_HEADS
    scale = ML_DQK ** -0.5
    log_scale = math.log(scale)

    @pl.when(c_idx == 0)
    def _():
        c_s[...] = jnp.zeros_like(c_s)
        n_s[...] = jnp.zeros_like(n_s)
        m_s[...] = jnp.zeros_like(m_s)

    gc = gc_ref[...]
    logf = jnp.minimum(gc, 0.0) - jnp.log1p(jnp.exp(-jnp.abs(gc)))
    bcum = pltpu.roll(_sublane_scan(logf, jnp.add, 0.0), 128 - NH, axis=1)
    a_c = gc - bcum
    cmax = _sublane_scan(a_c, jnp.maximum, NEG_INF)
    a_t = a_c.T

    row = lax.broadcasted_iota(jnp.int32, (LC, LC), 0)
    col = lax.broadcasted_iota(jnp.int32, (LC, LC), 1)
    causal = col <= row
    ones_col = jnp.ones((LC, 128), BF16)

    def lanes2(x):
        return jnp.concatenate([x, x], axis=1)

    for h in range(NH):
        a_rep = jnp.broadcast_to(a_c[:, h:h + 1], (LC, 128))
        b_rep = jnp.broadcast_to(bcum[:, h:h + 1], (LC, 128))
        cm_rep = jnp.broadcast_to(cmax[:, h:h + 1], (LC, 128))
        a_r = a_t[h:h + 1, :]
        b_last = b_rep[LC - 1:LC, :]
        m_prev = m_s[h:h + 1, :]
        m_new = jnp.maximum(b_last + m_prev, b_last + cm_rep[LC - 1:LC, :])
        decay = jnp.exp(b_last + m_prev - m_new)
        e_rep = jnp.exp(b_last + a_rep - m_new)
        g_rep = jnp.maximum(m_prev, cm_rep)
        w_intra = jnp.exp(jnp.where(causal, a_r - lanes2(g_rep) + log_scale, NEG_INF))
        w_inter = jnp.exp(m_prev - g_rep) * scale
        floor = jnp.exp(-(b_rep + g_rep))

        qh = q_ref[:, h * ML_DQK:(h + 1) * ML_DQK]
        kh = k_ref[:, h * ML_DQK:(h + 1) * ML_DQK]
        vh = v_ref[:, h * ML_DV:(h + 1) * ML_DV]
        c_prev = c_s[h]
        n_prev = n_s[h:h + 1, :]

        s = (_dot_nt(qh, kh) * w_intra).astype(BF16)
        inter = _dot(qh, c_prev.astype(BF16))
        num = _dot(s, vh) + lanes2(w_inter) * inter
        n_rows = jnp.broadcast_to(n_prev, (128, ML_DQK)).astype(BF16)
        qn = _dot(s, ones_col) + w_inter * _dot_nt(qh, n_rows)
        inv = 1.0 / jnp.maximum(jnp.abs(qn), floor)

        ke = kh.astype(F32) * e_rep
        c_s[h] = lanes2(decay) * c_prev + _dot_tn(ke.astype(BF16), vh)
        n_s[h:h + 1, :] = decay * n_prev + jnp.sum(ke, axis=0, keepdims=True)
        m_s[h:h + 1, :] = m_new

        ssq = _dot((num * num).astype(BF16), jnp.ones((ML_DV, 128), BF16))
        rs = lax.rsqrt(ssq * (inv * inv) * (1.0 / ML_DV) + RMS_EPS) * inv
        hn = num * lanes2(rs) * gout_ref[:, h * ML_DV:(h + 1) * ML_DV]
        og = jax.nn.sigmoid(o_ref[:, h * ML_DV:(h + 1) * ML_DV].astype(F32))
        out_ref[:, h * ML_DV:(h + 1) * ML_DV] = (hn * og).astype(out_ref.dtype)


def mlstm_core(proj, gates, g_out, batch, seq):
    LC = ML_LC
    nc = seq // LC
    qk = ML_HEADS * ML_DQK
    vd = ML_HEADS * ML_DV
    row = lambda b, c: b * nc + c
    return pl.pallas_call(
        _mlstm_body,
        grid=(batch, nc),
        in_specs=[pl.BlockSpec((LC, qk), lambda b, c: (row(b, c), 0)),
                  pl.BlockSpec((LC, qk), lambda b, c: (row(b, c), 1)),
                  pl.BlockSpec((LC, vd), lambda b, c: (row(b, c), 1)),
                  pl.BlockSpec((LC, vd), lambda b, c: (row(b, c), 2)),
                  pl.BlockSpec((LC, 128), lambda b, c: (row(b, c), 0)),
                  pl.BlockSpec((1, vd), lambda b, c: (0, 0))],
        out_specs=pl.BlockSpec((LC, vd), lambda b, c: (row(b, c), 0)),
        out_shape=jax.ShapeDtypeStruct((batch * seq, vd), BF16),
        scratch_shapes=[pltpu.VMEM((ML_HEADS, ML_DQK, ML_DV), F32),
                        pltpu.VMEM((8, ML_DQK), F32),
                        pltpu.VMEM((8, 128), F32)],
        compiler_params=_cparams(("parallel", "arbitrary")),
        name="mlstm_core",
    )(proj, proj, proj, proj, gates, g_out.reshape(1, vd))


def _kv_proj_body(x_ref, g_ref, sh_ref, sc_ref, wk_ref, wvt_ref, wcc_ref, gk_ref, bd_ref,
                  ks_ref, kw_ref, vst_ref, vwt_ref, cc_ref):
    i = pl.program_id(1)
    G, HD = NSA_GROUPS, NSA_HD
    tm = x_ref.shape[0]
    lane = lax.broadcasted_iota(jnp.int32, (tm, G * 128), 1) % 128

    @pl.when(i == 0)
    def _():
        pad_rows = jnp.where(lane == HD + KEY_PAD, 1.0, 0.0).astype(BF16)
        ks_ref[...] = pad_rows
        kw_ref[...] = pad_rows
        vst_ref[...] = jnp.zeros_like(vst_ref)
        vwt_ref[...] = jnp.zeros_like(vwt_ref)

    @pl.when(i > 0)
    def _():
        h = _norm_mod(x_ref[...], g_ref[...], sh_ref[...], sc_ref[...]).astype(BF16)
        cc = _dot(h, wcc_ref[...])
        for q in range(cc_ref.shape[0]):
            cc_ref[q] = cc[:, q * 128:(q + 1) * 128]
        kk = _dot(h, wk_ref[...])
        ssq = _dot((kk * kk).astype(BF16), bd_ref[...])
        kn = kk * lax.rsqrt(ssq * (1.0 / HD) + RMS_EPS) * gk_ref[...]
        blk = (i - 1) * (tm // SLC_BLOCK) + lax.broadcasted_iota(
            jnp.int32, (tm, G * 128), 0) // SLC_BLOCK
        tail = jnp.where((lane == HD) | (lane == HD + 1) | (lane == HD + KEY_BLK + blk), 1.0, 0.0)
        ks_ref[...] = jnp.where(lane < HD, kn, tail).astype(BF16)
        kn_sw = jnp.concatenate(
            [pltpu.roll(kn[:, g * 128:(g + 1) * 128], HD, axis=1) for g in range(G)], axis=1)
        kw_ref[...] = jnp.where(lane < HD, kn_sw, tail).astype(BF16)
        vt = _dot_nt(wvt_ref[...], h)
        srow = lax.broadcasted_iota(jnp.int32, (VT_ROWS - HD, tm), 0)
        ones_blk = jnp.where(srow == 0, 1.0, 0.0).astype(BF16)
        for kind, o_ref in enumerate((vst_ref, vwt_ref)):
            for g in range(G):
                r0 = (kind * G + g) * HD
                o_ref[g * VT_ROWS:g * VT_ROWS + HD, :] = vt[r0:r0 + HD, :].astype(BF16)
                o_ref[g * VT_ROWS + HD:(g + 1) * VT_ROWS, :] = ones_blk


def _segment_ones(width, seg):
    idx = np.arange(width) // seg
    return (idx[:, None] == idx[None, :]).astype(np.float32)


def kv_proj(x2d, g, sh, sc, wk, wvt, wcc, gk, batch, seq, tm=KV_PAD):
    assert tm == KV_PAD and seq % tm == 0
    M, D = x2d.shape
    G = NSA_GROUPS
    nt = seq // tm
    sp = seq + KV_PAD
    ncs = wcc.shape[1] // 128
    bd = jnp.asarray(_segment_ones(G * 128, NSA_HD), dtype=BF16)
    xrow = lambda b, i: (b * nt + jnp.maximum(i - 1, 0), 0)
    bvec = pl.BlockSpec((None, 1, D), lambda b, i: (b, 0, 0))
    full = lambda a: pl.BlockSpec(a.shape, lambda b, i: (0,) * a.ndim)
    return pl.pallas_call(
        _kv_proj_body,
        grid=(batch, nt + 1),
        in_specs=[pl.BlockSpec((tm, D), xrow),
                  pl.BlockSpec((1, D), lambda b, i: (0, 0)),
                  bvec, bvec, full(wk), full(wvt), full(wcc), full(gk), full(bd)],
        out_specs=[pl.BlockSpec((None, tm, G * 128), lambda b, i: (b, i, 0)),
                   pl.BlockSpec((None, tm, G * 128), lambda b, i: (b, i, 0)),
                   pl.BlockSpec((None, G * VT_ROWS, tm), lambda b, i: (b, 0, i)),
                   pl.BlockSpec((None, G * VT_ROWS, tm), lambda b, i: (b, 0, i)),
                   pl.BlockSpec((ncs, tm, 128),
                                lambda b, i: (0, b * nt + jnp.maximum(i - 1, 0), 0))],
        out_shape=[jax.ShapeDtypeStruct((batch, sp, G * 128), BF16),
                   jax.ShapeDtypeStruct((batch, sp, G * 128), BF16),
                   jax.ShapeDtypeStruct((batch, G * VT_ROWS, sp), BF16),
                   jax.ShapeDtypeStruct((batch, G * VT_ROWS, sp), BF16),
                   jax.ShapeDtypeStruct((ncs, M, 128), F32)],
        compiler_params=_cparams(("parallel", "arbitrary")),
        name="kv_proj",
    )(x2d, g.reshape(1, D), sh, sc, wk, wvt, wcc, gk, bd)


def _gelu_tanh(x):
    c = math.sqrt(2.0 / math.pi)
    return 0.5 * x * (1.0 + jnp.tanh(c * (x + 0.044715 * (x * x * x))))


def _compress_body(cc0_ref, cc1_ref, cc2_ref, cc3_ref, pos_ref, w1_ref, w2_ref, w2t_ref, g_ref,
                   kc_ref, vct_ref):
    HD = NSA_HD
    nwin = kc_ref.shape[0]
    hid = w2_ref.shape[0]
    for p, cc_ref in enumerate((cc0_ref, cc1_ref, cc2_ref, cc3_ref)):
        kind = p // 2
        x = jnp.concatenate([cc_ref[pl.ds(i, nwin, stride=CMP_STRIDE), :]
                             for i in range(CMP_STRIDE)], axis=1)
        u = _dot((x + pos_ref[kind, 0:1, :]).astype(BF16), w1_ref[kind, 0])
        v = _dot((x + pos_ref[kind, 1:2, :]).astype(BF16), w1_ref[kind, 1])
        pre = u + pltpu.roll(v, nwin - 1, axis=0)
        for s in range(2):
            g = 2 * (p % 2) + s
            hmid = _gelu_tanh(pre[:, s * hid:(s + 1) * hid]).astype(BF16)
            if kind == 0:
                y = _dot(hmid, w2_ref[...])
                var = jnp.mean(y * y, axis=-1, keepdims=True)
                yn = y * lax.rsqrt(var + RMS_EPS) * g_ref[...]
                kc_ref[:, g * 128:g * 128 + HD] = yn.astype(BF16)
                kc_ref[:, g * 128 + HD:(g + 1) * 128] = jnp.zeros((nwin, 128 - HD), BF16)
            else:
                vct_ref[g * HD:(g + 1) * HD, :] = _dot_nt(w2t_ref[...], hmid).astype(BF16)


def _pair_expand(w):
    k, t, d, c = w.shape
    eye = jnp.eye(2, dtype=w.dtype)
    return jnp.einsum("ktdc,su->ktsduc", w, eye).reshape(k, t * 2 * d, 2 * c)


def compress(cc, pos, w1, w2k, w2vt, g, batch, seq):
    G, HD = NSA_GROUPS, NSA_HD
    nwin = seq // CMP_STRIDE
    hid = w1.shape[2]
    half = CMP_BLOCK // 2
    assert half == CMP_STRIDE
    w1r = w1.reshape(2, CMP_BLOCK, HD, hid)
    w1x = jnp.stack([_pair_expand(w1r[:, :half]), _pair_expand(w1r[:, half:])], axis=1).astype(BF16)
    posr = jnp.tile(pos.reshape(2, 2, half, 1, HD), (1, 1, 1, 2, 1)).reshape(2, 2, half * 2 * HD)
    resident = lambda a: pl.BlockSpec(a.shape, lambda b: (0,) * a.ndim, pipeline_mode=pl.Buffered(1))
    return pl.pallas_call(
        _compress_body,
        grid=(batch,),
        in_specs=[pl.BlockSpec((None, seq, 128), functools.partial(lambda q, b: (q, b, 0), q))
                  for q in range(4)]
                 + [resident(posr), resident(w1x), resident(w2k), resident(w2vt), resident(g)],
        out_specs=[pl.BlockSpec((None, nwin, G * 128), lambda b: (b, 0, 0)),
                   pl.BlockSpec((None, G * HD, nwin), lambda b: (b, 0, 0))],
        out_shape=[jax.ShapeDtypeStruct((batch, nwin, G * 128), BF16),
                   jax.ShapeDtypeStruct((batch, G * HD, nwin), BF16)],
        compiler_params=_cparams(("parallel",)),
        name="compress",
    )(cc, cc, cc, cc, posr, w1x, w2k, w2vt, g)


def _t5_bucket(dist):
    n = jnp.maximum(dist, 0)
    max_exact = REL_BUCKETS // 2
    nf = jnp.maximum(n, 1).astype(F32)
    large = max_exact + (jnp.log(nf / max_exact) / math.log(REL_MAX_DIST / max_exact)
                         * (REL_BUCKETS - max_exact)).astype(jnp.int32)
    large = jnp.minimum(large, REL_BUCKETS - 1)
    return jnp.where(n < max_exact, n, large)


def _table_lookup(bucket, tab_ref, h):
    out = jnp.zeros(bucket.shape, F32)
    for k in range(REL_BUCKETS):
        out = jnp.where(bucket == k, tab_ref[k, h], out)
    return out


def _bias_prep_body(tab_ref, bc_ref, tp_ref, far_ref):
    h = pl.program_id(0)
    S = bc_ref.shape[1]
    far = tab_ref[REL_BUCKETS - 1, h]
    far_ref[...] = jnp.full(far_ref.shape, far * LOG2E, F32)

    dist = lax.broadcasted_iota(jnp.int32, (8, S), 1)
    by_dist = _table_lookup(_t5_bucket(dist), tab_ref, h) * LOG2E
    shifted = pltpu.roll(jnp.broadcast_to(by_dist[0:1, :], (NCMP_PAD, S)), 0, axis=1,
                         stride=CMP_STRIDE, stride_axis=0)
    shifted = pltpu.roll(shifted, CMP_BLOCK - 1, axis=1)
    n = lax.broadcasted_iota(jnp.int32, (NCMP_PAD, S), 0)
    t = lax.broadcasted_iota(jnp.int32, (NCMP_PAD, S), 1)
    bc_ref[...] = jnp.where(t >= n * CMP_STRIDE + CMP_BLOCK - 1, shifted, NEG_INF)
    j = lax.broadcasted_iota(jnp.int32, (ATT_TK, ATT_QB), 0)
    i = lax.broadcasted_iota(jnp.int32, (ATT_TK, ATT_QB), 1)
    for d in range(2):
        dist = d * ATT_TK + i - j
        rel = (_table_lookup(_t5_bucket(dist), tab_ref, h) - far) * LOG2E
        tp_ref[d] = jnp.where(dist >= 0, rel, NEG_INF)
    tp_ref[2] = jnp.where(i < j, 0.0, NEG_INF)


def bias_prep(rel_table, seq):
    assert ATT_TK == ATT_QB and ATT_TK + 1 > 113
    return pl.pallas_call(
        _bias_prep_body,
        grid=(NSA_HEADS,),
        in_specs=[pl.BlockSpec(memory_space=pltpu.SMEM)],
        out_specs=[pl.BlockSpec((None, NCMP_PAD, seq), lambda h: (h, 0, 0)),
                   pl.BlockSpec((None, 3, ATT_TK, ATT_QB), lambda h: (h, 0, 0, 0)),
                   pl.BlockSpec((None, 8, 128), lambda h: (h, 0, 0))],
        out_shape=[jax.ShapeDtypeStruct((NSA_HEADS, NCMP_PAD, seq), F32),
                   jax.ShapeDtypeStruct((NSA_HEADS, 3, ATT_TK, ATT_QB), F32),
                   jax.ShapeDtypeStruct((NSA_HEADS, 8, 128), F32)],
        compiler_params=_cparams(("parallel",)),
        name="bias_prep",
    )(rel_table)


def _q_proj_body(x_ref, g_ref, sh_ref, sc_ref, wqt_ref, wgt_ref, bg_ref, gq_ref, qt_ref, gt_ref):
    HD = NSA_HD
    tm = x_ref.shape[0]
    half = tm // 2
    scale = gq_ref[...] * (HD ** -0.5 * LOG2E)
    for r0 in (0, half):
        h = _norm_mod(x_ref[r0:r0 + half, :], g_ref[...], sh_ref[...], sc_ref[...]).astype(BF16)
        gt_ref[:, r0:r0 + half] = _dot_nt(wgt_ref[...], h) + bg_ref[...]
        qt = _dot_nt(wqt_ref[...], h)
        for hh in range(NSA_HEADS):
            seg = qt[hh * HD:(hh + 1) * HD, :]
            var = jnp.mean(seg * seg, axis=0, keepdims=True)
            qt_ref[hh * HD:(hh + 1) * HD, r0:r0 + half] = (
                seg * lax.rsqrt(var + RMS_EPS) * scale).astype(BF16)


def q_proj(x2d, g, sh, sc, wqt, wgt, bg, gq, batch, seq, tm=1024):
    M, D = x2d.shape
    nt = seq // tm
    nq = wqt.shape[0]
    ng = wgt.shape[0]
    bvec = pl.BlockSpec((None, 1, D), lambda i: (i // nt, 0, 0))
    full = lambda a: pl.BlockSpec(a.shape, lambda i: (0,) * a.ndim)
    return pl.pallas_call(
        _q_proj_body,
        grid=(M // tm,),
        in_specs=[pl.BlockSpec((tm, D), lambda i: (i, 0)),
                  pl.BlockSpec((1, D), lambda i: (0, 0)),
                  bvec, bvec, full(wqt), full(wgt), full(bg), full(gq)],
        out_specs=[pl.BlockSpec((None, nq, tm), lambda i: (i // nt, 0, i % nt)),
                   pl.BlockSpec((None, ng, tm), lambda i: (i // nt, 0, i % nt))],
        out_shape=[jax.ShapeDtypeStruct((batch, nq, seq), BF16),
                   jax.ShapeDtypeStruct((batch, ng, seq), F32)],
        compiler_params=_cparams(("parallel",)),
        name="q_proj",
    )(x2d, g.reshape(1, D), sh, sc, wqt, wgt, bg, gq)


def _heads_on_lanes(pieces):
    return jnp.concatenate(pieces, axis=1)


def _nsa_body(gps, qt_ref, gt_ref, kc_ref, vct_ref, ks_ref, kw_ref, vst_ref, vwt_ref,
              bct_ref, tp_ref, far_ref, ovt_ref, out_ref):
    qb = pl.program_id(2)
    QB, HD, HPG = ATT_QB, NSA_HD, NSA_HPG
    R = HPG * QB
    t0 = pl.multiple_of(qb * QB, QB)
    near0 = pl.multiple_of(t0 + WINDOW - ATT_TK, ATT_TK)
    wlen = WINDOW + QB
    nblk = ovt_ref.shape[0]
    jb = lax.broadcasted_iota(jnp.int32, (nblk, QB), 0)
    jbf = jb.astype(F32)
    qid = jnp.right_shift(t0 + lax.broadcasted_iota(jnp.int32, (nblk, QB), 1), 6)
    forced = (jb == 0) | (jb == qid) | (jb == qid - 1)
    srow = lax.broadcasted_iota(jnp.int32, (8, R), 0)

    def finish(acc):
        return acc[0:HD, :] / acc[HD:HD + 1, :]

    def tile_part(s_tile, v_tile):
        m_t = jnp.max(s_tile, axis=0, keepdims=True)
        m_safe = jnp.where(m_t == NEG_INF, 0.0, m_t)
        p = jnp.exp2(s_tile - m_safe).astype(BF16)
        return m_t, _dot(v_tile, p)[0:HD + 8, :]

    def combine(parts):
        m_fin = parts[0][0]
        for m_t, _ in parts[1:]:
            m_fin = jnp.maximum(m_fin, m_t)
        acc = None
        for m_t, pv in parts:
            term = jnp.exp2(m_t - m_fin) * pv
            acc = term if acc is None else acc + term
        return m_fin, acc

    groups = range(gps)
    hs = [[gl * HPG + h for h in range(HPG)] for gl in groups]
    kcol = [slice(gl * 128, (gl + 1) * 128) for gl in groups]
    vrow = [slice(gl * VT_ROWS, (gl + 1) * VT_ROWS) for gl in groups]

    def stationary(q_all, hi_lo, block_rows):
        pad = jnp.where(srow == 0, MASK_BIG, 0.0)
        rest = jnp.zeros((128 - HD - KEY_PAD - 8, R), F32)
        tail = jnp.concatenate([hi_lo, block_rows, pad, rest], axis=0).astype(BF16)
        return jnp.concatenate([q_all, tail], axis=0)

    q_alls, hi_los, qms = [], [], []
    for gl in groups:
        q_alls.append(_heads_on_lanes([qt_ref[h * HD:(h + 1) * HD, :] for h in hs[gl]]))
        far = _heads_on_lanes([far_ref[h, 0:1, :] for h in hs[gl]])
        hi = far.astype(BF16).astype(F32)
        hi_los.append(jnp.where(srow == 0, hi, jnp.where(srow == 1, far - hi, 0.0)))
        qms.append(stationary(q_alls[gl], hi_los[gl], jnp.zeros((nblk, R), F32)))

    s_cs = [_dot(kc_ref[:, kcol[gl]], qms[gl]) for gl in groups]
    s_ws = [_dot(kw_ref[pl.ds(t0, wlen), kcol[gl]], qms[gl]) for gl in groups]
    s_ns = [_dot(ks_ref[pl.ds(near0, 2 * ATT_TK), kcol[gl]], qms[gl]) for gl in groups]

    o_cmps, scores = [], []
    for gl in groups:
        s_c = s_cs[gl] + _heads_on_lanes([bct_ref[h] for h in hs[gl]])
        m_c = jnp.max(s_c, axis=0, keepdims=True)
        m_c = jnp.where(m_c == NEG_INF, 0.0, m_c)
        e_c = jnp.exp2(s_c - m_c)
        p_c = e_c / jnp.maximum(jnp.sum(e_c, axis=0, keepdims=True), jnp.finfo(F32).tiny)
        o_cmps.append(_dot(vct_ref[gl * HD:(gl + 1) * HD, :], p_c.astype(BF16)))
        p_sum = p_c[:, 0:QB]
        for h in range(1, HPG):
            p_sum = p_sum + p_c[:, h * QB:(h + 1) * QB]
        p_hi = p_sum.astype(BF16)
        p_lo = (p_sum - p_hi.astype(F32)).astype(BF16)
        imp = _dot(ovt_ref[...], jnp.concatenate([p_hi, p_lo], axis=0))
        score = jnp.where(forced, FORCE_SCORE, imp)
        scores.append(jnp.where(jb <= qid, score, NEG_INF))

    nwt = wlen // ATT_TK
    o_wins = []
    for gl in groups:
        parts = []
        for i in range(nwt):
            d = nwt - 1 - i
            s_t = s_ws[gl][i * ATT_TK:(i + 1) * ATT_TK]
            if d in (0, 1):
                s_t = s_t + _heads_on_lanes([tp_ref[h, d] for h in hs[gl]])
            elif d == nwt - 1:
                s_t = s_t + _heads_on_lanes([tp_ref[h, 2] for h in hs[gl]])
            parts.append(tile_part(s_t, vwt_ref[vrow[gl], pl.ds(t0 + i * ATT_TK, ATT_TK)]))
        o_wins.append(finish(combine(parts)[1]))

    msels = [jnp.full((nblk, QB), NEG_INF, F32) for _ in groups]
    for _ in range(SLC_TOPK):
        for gl in groups:
            mx = jnp.max(scores[gl], axis=0, keepdims=True)
            first = jnp.min(jnp.where(scores[gl] == mx, jbf, float(nblk)), axis=0, keepdims=True)
            pick = jbf == first
            msels[gl] = jnp.where(pick & (mx > NEG_INF), 0.0, msels[gl])
            scores[gl] = jnp.where(pick, NEG_INF, scores[gl])
    carry0 = []
    for gl in groups:
        prev_mask = jnp.concatenate(
            [jnp.broadcast_to(jnp.max(jnp.where(jb == 2 * qb - 2 + r, msels[gl], NEG_INF), axis=0,
                                      keepdims=True), (SLC_BLOCK, QB)) for r in range(2)], axis=0)
        s_prev = s_ns[gl][0:ATT_TK] + _heads_on_lanes([tp_ref[h, 1] + prev_mask for h in hs[gl]])
        s_diag = s_ns[gl][ATT_TK:] + _heads_on_lanes([tp_ref[h, 0] for h in hs[gl]])
        carry0.append(combine([
            tile_part(s_diag, vst_ref[vrow[gl], pl.ds(near0 + ATT_TK, ATT_TK)]),
            tile_part(s_prev, vst_ref[vrow[gl], pl.ds(near0, ATT_TK)])]))

    qss = [stationary(q_alls[gl], hi_los[gl], _heads_on_lanes(
        [jnp.where(msels[gl] == 0.0, 0.0, MASK_BIG)] * HPG)) for gl in groups]

    def far_step(c, carry):
        tile0 = qb + (WINDOW - ATT_TK) // ATT_TK - (c + 1) * (FAR_CHUNK // ATT_TK)
        row0 = pl.multiple_of(tile0 * ATT_TK, ATT_TK)
        s_fs = [_dot(ks_ref[pl.ds(row0, FAR_CHUNK), kcol[gl]], qss[gl]) for gl in groups]
        out = []
        for gl in groups:
            parts = [carry[gl]]
            for j in range(FAR_CHUNK // ATT_TK):
                parts.append(tile_part(s_fs[gl][j * ATT_TK:(j + 1) * ATT_TK],
                                       vst_ref[vrow[gl], pl.ds(row0 + j * ATT_TK, ATT_TK)]))
            out.append(combine(parts))
        return tuple(out)

    n_far = (qb + 2) // 4
    sel = lax.fori_loop(0, n_far, far_step, tuple(carry0))

    gates = jax.nn.sigmoid(gt_ref[...])
    for gl in groups:
        o_cmp, o_win = o_cmps[gl], o_wins[gl]
        o_sel = finish(sel[gl][1])
        outs = []
        for h in range(HPG):
            lanes = slice(h * QB, (h + 1) * QB)
            r = gl * 16 + 3 * h
            outs.append(gates[r:r + 1, :] * o_cmp[:, lanes]
                        + gates[r + 1:r + 2, :] * o_sel[:, lanes]
                        + gates[r + 2:r + 3, :] * o_win[:, lanes])
        for pair in range(HPG // 2):
            two = jnp.concatenate(outs[2 * pair:2 * pair + 2], axis=0)
            c0 = gl * HPG * HD + pair * 2 * HD
            out_ref[:, c0:c0 + 2 * HD] = two.T.astype(out_ref.dtype)


def _overlap_matrix_t(nblk):
    start = np.arange(NCMP_PAD) * CMP_STRIDE
    sj = np.arange(nblk) * SLC_BLOCK
    ov = (np.minimum(start[None, :] + CMP_BLOCK, sj[:, None] + SLC_BLOCK)
          - np.maximum(start[None, :], sj[:, None]))
    ov = np.clip(ov, 0, None) / CMP_BLOCK
    ov[:, NCMP_PAD - 1] = 0.0
    return np.concatenate([ov, ov], axis=1).astype(np.float32)


def nsa_attn(qt, gt, kc, vct, ks, kw, vst, vwt, bias_ct, tiles, far, batch, seq, gps=ATT_GPS):
    QB, HD, HPG, G = ATT_QB, NSA_HD, NSA_HPG, NSA_GROUPS
    nblk = seq // SLC_BLOCK
    assert nblk % 8 == 0 and seq % QB == 0 and seq // CMP_STRIDE == NCMP_PAD and G % gps == 0
    assert WINDOW % ATT_TK == 0 and FAR_CHUNK == 4 * ATT_TK and KV_PAD >= FAR_CHUNK - ATT_TK
    assert KEY_BLK % 8 == 0 and KEY_BLK + nblk <= KEY_PAD and KEY_PAD + 8 <= 128 - HD
    nq = seq // QB
    sp = seq + KV_PAD
    ovt = jnp.asarray(_overlap_matrix_t(nblk), dtype=BF16)
    return pl.pallas_call(
        functools.partial(_nsa_body, gps),
        grid=(batch, G // gps, nq),
        in_specs=[pl.BlockSpec((None, gps * HPG * HD, QB), lambda b, g, i: (b, g, i)),
                  pl.BlockSpec((None, gps * 16, QB), lambda b, g, i: (b, g, i)),
                  pl.BlockSpec((None, NCMP_PAD, gps * 128), lambda b, g, i: (b, 0, g)),
                  pl.BlockSpec((None, gps * HD, NCMP_PAD), lambda b, g, i: (b, g, 0)),
                  pl.BlockSpec((None, sp, gps * 128), lambda b, g, i: (b, 0, g)),
                  pl.BlockSpec((None, sp, gps * 128), lambda b, g, i: (b, 0, g)),
                  pl.BlockSpec((None, gps * VT_ROWS, sp), lambda b, g, i: (b, g, 0)),
                  pl.BlockSpec((None, gps * VT_ROWS, sp), lambda b, g, i: (b, g, 0)),
                  pl.BlockSpec((gps * HPG, NCMP_PAD, QB), lambda b, g, i: (g, 0, i)),
                  pl.BlockSpec((gps * HPG, 3, ATT_TK, QB), lambda b, g, i: (g, 0, 0, 0)),
                  pl.BlockSpec((gps * HPG, 8, 128), lambda b, g, i: (g, 0, 0)),
                  pl.BlockSpec((nblk, 2 * NCMP_PAD), lambda b, g, i: (0, 0))],
        out_specs=pl.BlockSpec((QB, gps * HPG * HD), lambda b, g, i: (b * nq + i, g)),
        out_shape=jax.ShapeDtypeStruct((batch * seq, NSA_HEADS * HD), BF16),
        compiler_params=_cparams(("parallel", "parallel", "arbitrary")),
        name="nsa_attn",
    )(qt, gt, kc, vct, ks, kw, vst, vwt, bias_ct, tiles, far, ovt)


def _mlstm_layer(x2d, g_mix, sh, sc, w_in, b_if, g_out, batch, seq):
    nbig = 2 * ML_HEADS * ML_DQK + 2 * ML_HEADS * ML_DV
    ng = 2 * ML_HEADS
    wg = jnp.pad(w_in[:, nbig:], ((0, 0), (0, 128 - ng)))
    bg = jnp.pad(b_if, (0, 128 - ng))
    proj, gates = norm_proj(x2d, g_mix, sh, sc, [w_in[:, :nbig], wg], [None, bg], [BF16, F32])
    return mlstm_core(proj, gates, g_out, batch, seq)


def _nsa_shared(x2d, g_kv, kv_sh, kv_sc, w_kv, pos_k, w_k1, w_k2, pos_v, w_v1, w_v2,
                g_knorm, batch, seq):
    gw = NSA_GROUPS * NSA_HD
    part = lambda i: w_kv[:, i * gw:(i + 1) * gw]
    hd = NSA_HD
    wk = jnp.concatenate([p[:, g * hd:(g + 1) * hd] for g in range(NSA_GROUPS)
                          for p in (part(2), part(4))], axis=1).astype(BF16)
    gk = jnp.tile(jnp.concatenate([g_knorm[1], g_knorm[2]]), NSA_GROUPS).reshape(1, -1)
    wvt = jnp.concatenate([part(3), part(5)], axis=1).T.astype(BF16)
    wcc = jnp.concatenate([part(0), part(1)], axis=1).astype(BF16)
    ks, kw, vst, vwt, cc = kv_proj(x2d, g_kv, kv_sh, kv_sc, wk, wvt, wcc, gk, batch, seq)
    pos = jnp.stack([pos_k, pos_v])
    w1 = jnp.stack([w_k1, w_v1])
    kc, vct = compress(cc, pos, w1, w_k2.astype(BF16), w_v2.T.astype(BF16), g_knorm[0:1],
                       batch, seq)
    return kc, vct, ks, kw, vst, vwt


def _gate_weights_t(w_q, b_gate):
    nq = NSA_HEADS * NSA_HD
    per = 3 * NSA_HPG
    wg = w_q[:, nq:].T.reshape(NSA_GROUPS, per, -1)
    wg = jnp.pad(wg, ((0, 0), (0, 16 - per), (0, 0))).reshape(NSA_GROUPS * 16, -1)
    bg = jnp.pad(b_gate.reshape(NSA_GROUPS, per), ((0, 0), (0, 16 - per))).reshape(-1, 1)
    return wg.astype(BF16), bg


def _nsa_layer(x2d, g_mix, sh, sc, shared, w_q, b_gate, g_qnorm, bias_ct, tiles, far, batch, seq):
    nq = NSA_HEADS * NSA_HD
    wgt, bg = _gate_weights_t(w_q, b_gate)
    qt, gt = q_proj(x2d, g_mix, sh, sc, w_q[:, :nq].T.astype(BF16), wgt, bg,
                    g_qnorm.reshape(NSA_HD, 1), batch, seq)
    kc, vct, ks, kw, vst, vwt = shared
    return nsa_attn(qt, gt, kc, vct, ks, kw, vst, vwt, bias_ct, tiles, far, batch, seq)


def kernel(x, c, w_ada, b_ada, g_norm_mix, g_norm_ffn, w_ffn_in, w_ffn_out, w_a_in, b_a_if, g_a_out, w_a_out, w_kv_ada, b_kv_ada, g_kv_norm, w_kv, pos_cmp_k, w_cmp_k1, w_cmp_k2, pos_cmp_v, w_cmp_v1, w_cmp_v2, g_knorm, w_b_q, b_b_gate, g_qnorm, w_b_out, rel_table):
    B, S, D = x.shape
    depth = w_ada.shape[0]
    n_a = w_a_in.shape[0]
    x2d = x.reshape(B * S, D)
    mods = ada_mod(c, w_ada, b_ada)
    kv_mod = ada_mod(c, w_kv_ada[None], b_kv_ada[None])[0]
    shared = None
    bias_ct = tiles = far = None
    for layer in range(depth):
        sh1, sc1, ga1, sh2, sc2, ga2 = [mods[layer, :, i * D:(i + 1) * D].reshape(B, 1, D)
                                        for i in range(6)]
        if layer < n_a:
            mixed = _mlstm_layer(x2d, g_norm_mix[layer], sh1, sc1, w_a_in[layer], b_a_if[layer],
                                 g_a_out[layer], B, S)
            w_mix = (w_a_out, layer)
        else:
            j = layer - n_a
            if shared is None:
                kv_sh = kv_mod[:, :D].reshape(B, 1, D)
                kv_sc = kv_mod[:, D:].reshape(B, 1, D)
                shared = _nsa_shared(x2d, g_kv_norm, kv_sh, kv_sc, w_kv, pos_cmp_k, w_cmp_k1,
                                     w_cmp_k2, pos_cmp_v, w_cmp_v1, w_cmp_v2, g_knorm, B, S)
                bias_ct, tiles, far = bias_prep(rel_table, S)
            mixed = _nsa_layer(x2d, g_norm_mix[layer], sh1, sc1, shared, w_b_q[j], b_b_gate[j],
                               g_qnorm[j], bias_ct, tiles, far, B, S)
            w_mix = (w_b_out, j)
        x2d = mix_ffn(mixed, w_mix, x2d, ga1, g_norm_ffn[layer], sh2, sc2, ga2,
                      (w_ffn_in, layer), (w_ffn_out, layer))
    return x2d.reshape(B, S, D)
```

```python
import functools
import math

import jax
import jax.numpy as jnp
import numpy as np
from jax import lax
from jax.experimental import pallas as pl
from jax.experimental.pallas import tpu as pltpu

F32 = jnp.float32
BF16 = jnp.bfloat16
NEG_INF = float("-inf")
LOG2E = math.log2(math.e)

RMS_EPS = 1e-6

ML_HEADS = 4
ML_DQK = 128
ML_DV = 256
ML_LC = 256

NSA_HEADS = 16
NSA_GROUPS = 4
NSA_HPG = 4
NSA_HD = 64
CMP_BLOCK = 32
CMP_STRIDE = 16
SLC_BLOCK = 64
SLC_TOPK = 8
WINDOW = 512
FORCE_SCORE = 1e4
REL_BUCKETS = 32
REL_MAX_DIST = 128
ATT_QB = 128
ATT_TK = 128
ATT_GPS = 4
NCMP_PAD = 128
KV_PAD = WINDOW
FAR_CHUNK = 512
VT_ROWS = 80
KEY_BLK = 8
KEY_PAD = 40
MASK_BIG = -32768.0

VMEM_LIMIT = 56 * 1024 * 1024


def _cparams(sem):
    return pltpu.CompilerParams(dimension_semantics=sem, vmem_limit_bytes=VMEM_LIMIT)


def _dot(a, b):
    return jnp.dot(a, b, preferred_element_type=F32)


def _dot_nt(a, b):
    return lax.dot_general(a, b, (((1,), (1,)), ((), ())), preferred_element_type=F32)


def _dot_tn(a, b):
    return lax.dot_general(a, b, (((0,), (0,)), ((), ())), preferred_element_type=F32)


def _norm_mod(x, g, sh, sc):
    var = jnp.mean(x * x, axis=-1, keepdims=True)
    y = x * lax.rsqrt(var + RMS_EPS) * g
    return y * (1.0 + sc) + sh


def _ada_body(c_ref, w_ref, b_ref, o_ref):
    c = c_ref[...]
    ca = c * jax.nn.sigmoid(c)
    nb = ca.shape[0]
    a_hi = ca.astype(BF16).astype(F32)
    a_mid = (ca - a_hi).astype(BF16).astype(F32)
    a_lo = ca - a_hi - a_mid
    a3 = jnp.concatenate([a_hi, a_mid, a_lo, jnp.zeros_like(ca)], axis=0).astype(BF16)
    w = w_ref[...]
    w_hi = w.astype(BF16)
    w_lo = (w - w_hi.astype(F32)).astype(BF16)
    p = _dot(a3, w_hi)
    q = _dot(a3[0:2 * nb], w_lo)
    o_ref[...] = (p[0:nb] + p[nb:2 * nb] + p[2 * nb:3 * nb] + q[0:nb] + q[nb:2 * nb]) + b_ref[...]


def ada_mod(c, w, b, tn=2048):
    L, D, N = w.shape
    B = c.shape[0]
    return pl.pallas_call(
        _ada_body,
        grid=(L, N // tn),
        in_specs=[pl.BlockSpec((B, D), lambda l, j: (0, 0)),
                  pl.BlockSpec((None, D, tn), lambda l, j: (l, 0, j)),
                  pl.BlockSpec((None, 1, tn), lambda l, j: (l, 0, j))],
        out_specs=pl.BlockSpec((None, B, tn), lambda l, j: (l, 0, j)),
        out_shape=jax.ShapeDtypeStruct((L, B, N), F32),
        compiler_params=_cparams(("parallel", "parallel")),
        name="ada_mod",
    )(c, w, b.reshape(L, 1, N))


def _norm_proj_body(n_out, has_bias, tn, x_ref, g_ref, sh_ref, sc_ref, *refs):
    w_refs = refs[:n_out]
    b_refs = refs[n_out:2 * n_out]
    o_refs = refs[2 * n_out:3 * n_out]
    tm = x_ref.shape[0]
    half = tm // 2
    for r0 in (0, half):
        h = _norm_mod(x_ref[r0:r0 + half, :], g_ref[...], sh_ref[...], sc_ref[...]).astype(BF16)
        for w_ref, b_ref, o_ref, hb in zip(w_refs, b_refs, o_refs, has_bias):
            n = w_ref.shape[1]
            step = min(tn, n)
            for n0 in range(0, n, step):
                acc = _dot(h, w_ref[:, n0:n0 + step].astype(BF16))
                if hb:
                    acc = acc + b_ref[:, n0:n0 + step]
                o_ref[r0:r0 + half, n0:n0 + step] = acc.astype(o_ref.dtype)


def norm_proj(x2d, g, sh, sc, ws, biases, out_dtypes, tm=1024, tn=512):
    M, D = x2d.shape
    B = sh.shape[0]
    tiles_per_batch = (M // B) // tm
    n_out = len(ws)
    has_bias = tuple(b is not None for b in biases)
    bias_args = [(b if b is not None else jnp.zeros((w.shape[1],), F32)).reshape(1, -1)
                 for b, w in zip(biases, ws)]
    in_specs = [pl.BlockSpec((tm, D), lambda i: (i, 0)),
                pl.BlockSpec((1, D), lambda i: (0, 0)),
                pl.BlockSpec((None, 1, D), lambda i: (i // tiles_per_batch, 0, 0)),
                pl.BlockSpec((None, 1, D), lambda i: (i // tiles_per_batch, 0, 0))]
    resident = lambda a: pl.BlockSpec(a.shape, lambda i: (0, 0), pipeline_mode=pl.Buffered(1))
    in_specs += [resident(w) for w in ws] + [resident(b) for b in bias_args]
    out_specs = [pl.BlockSpec((tm, w.shape[1]), lambda i: (i, 0)) for w in ws]
    out_shape = [jax.ShapeDtypeStruct((M, w.shape[1]), dt) for w, dt in zip(ws, out_dtypes)]
    return pl.pallas_call(
        functools.partial(_norm_proj_body, n_out, has_bias, tn),
        grid=(M // tm,),
        in_specs=in_specs, out_specs=out_specs, out_shape=out_shape,
        compiler_params=_cparams(("parallel",)),
        name="norm_proj",
    )(x2d, g.reshape(1, D), sh, sc, *ws, *bias_args)


def _mix_ffn_body(tf, a_ref, wm_ref, x_ref, ga1_ref, g_ref, sh_ref, sc_ref, ga2_ref,
                  wi_ref, wo_ref, o_ref, act_s):
    F = wo_ref.shape[0]
    x1 = x_ref[...] + ga1_ref[...] * _dot(a_ref[...], wm_ref[...].astype(BF16))
    h = _norm_mod(x1, g_ref[...], sh_ref[...], sc_ref[...]).astype(BF16)
    for f0 in range(0, F, tf):
        gate = _dot(h, wi_ref[:, f0:f0 + tf].astype(BF16))
        up = _dot(h, wi_ref[:, F + f0:F + f0 + tf].astype(BF16))
        act_s[:, f0:f0 + tf] = (gate * jax.nn.sigmoid(gate) * up).astype(BF16)
    y = _dot(act_s[:, 0:tf], wo_ref[0:tf, :].astype(BF16))
    for f0 in range(tf, F, tf):
        y = y + _dot(act_s[:, f0:f0 + tf], wo_ref[f0:f0 + tf, :].astype(BF16))
    o_ref[...] = x1 + ga2_ref[...] * y


def mix_ffn(a, w_mix, x2d, ga1, g, sh, sc, ga2, w_in, w_out, tm=512, tf=256):
    M, D = x2d.shape
    K = a.shape[1]
    F = w_out[0].shape[1]
    B = sh.shape[0]
    tiles_per_batch = (M // B) // tm
    bvec = pl.BlockSpec((None, 1, D), lambda i: (i // tiles_per_batch, 0, 0))

    def resident(stacked_layer):
        w, layer = stacked_layer
        return pl.BlockSpec((None,) + w.shape[1:], lambda i: (layer, 0, 0),
                            pipeline_mode=pl.Buffered(1))

    return pl.pallas_call(
        functools.partial(_mix_ffn_body, tf),
        grid=(M // tm,),
        in_specs=[pl.BlockSpec((tm, K), lambda i: (i, 0)),
                  resident(w_mix),
                  pl.BlockSpec((tm, D), lambda i: (i, 0)),
                  bvec,
                  pl.BlockSpec((1, D), lambda i: (0, 0)),
                  bvec, bvec, bvec,
                  resident(w_in), resident(w_out)],
        out_specs=pl.BlockSpec((tm, D), lambda i: (i, 0)),
        out_shape=jax.ShapeDtypeStruct((M, D), F32),
        scratch_shapes=[pltpu.VMEM((tm, F), BF16)],
        compiler_params=_cparams(("parallel",)),
        name="mix_ffn",
    )(a, w_mix[0], x2d, ga1, g.reshape(1, D), sh, sc, ga2, w_in[0], w_out[0])


def _sublane_scan(x, op, fill):
    n = x.shape[0]
    row = lax.broadcasted_iota(jnp.int32, x.shape, 0)
    sh = 1
    while sh < n:
        x = op(x, jnp.where(row >= sh, pltpu.roll(x, sh, axis=0), fill))
        sh *= 2
    return x


def _mlstm_body(q_ref, k_ref, v_ref, o_ref, gc_ref, gout_ref, out_ref, c_s, n_s, m_s):
    c_idx = pl.program_id(1)
    LC = q_ref.shape[0]
    NH = ML_HEADS
    scale = ML_DQK ** -0.5
    log_scale = math.log(scale)

    @pl.when(c_idx == 0)
    def _():
        c_s[...] = jnp.zeros_like(c_s)
        n_s[...] = jnp.zeros_like(n_s)
        m_s[...] = jnp.zeros_like(m_s)

    gc = gc_ref[...]
    logf = jnp.minimum(gc, 0.0) - jnp.log1p(jnp.exp(-jnp.abs(gc)))
    bcum = pltpu.roll(_sublane_scan(logf, jnp.add, 0.0), 128 - NH, axis=1)
    a_c = gc - bcum
    cmax = _sublane_scan(a_c, jnp.maximum, NEG_INF)
    a_t = a_c.T

    row = lax.broadcasted_iota(jnp.int32, (LC, LC), 0)
    col = lax.broadcasted_iota(jnp.int32, (LC, LC), 1)
    causal = col <= row
    ones_col = jnp.ones((LC, 128), BF16)

    def lanes2(x):
        return jnp.concatenate([x, x], axis=1)

    for h in range(NH):
        a_rep = jnp.broadcast_to(a_c[:, h:h + 1], (LC, 128))
        b_rep = jnp.broadcast_to(bcum[:, h:h + 1], (LC, 128))
        cm_rep = jnp.broadcast_to(cmax[:, h:h + 1], (LC, 128))
        a_r = a_t[h:h + 1, :] + log_scale
        b_last = b_rep[LC - 1:LC, :]
        m_prev = m_s[h:h + 1, :]
        m_new = jnp.maximum(b_last + m_prev, b_last + cm_rep[LC - 1:LC, :])
        decay = jnp.exp(b_last + m_prev - m_new)
        e_rep = jnp.exp(b_last + a_rep - m_new)
        g_rep = jnp.maximum(m_prev, cm_rep)
        w_intra = jnp.exp(jnp.where(causal, a_r - lanes2(g_rep), NEG_INF))
        w_inter = jnp.exp(m_prev - g_rep) * scale
        floor = jnp.exp(-(b_rep + g_rep))

        qh = q_ref[:, h * ML_DQK:(h + 1) * ML_DQK]
        kh = k_ref[:, h * ML_DQK:(h + 1) * ML_DQK]
        vh = v_ref[:, h * ML_DV:(h + 1) * ML_DV]
        c_prev = c_s[h]
        n_prev = n_s[h:h + 1, :]

        s = (_dot_nt(qh, kh) * w_intra).astype(BF16)
        inter = _dot(qh, c_prev.astype(BF16))
        num = _dot(s, vh) + lanes2(w_inter) * inter
        n_rows = jnp.broadcast_to(n_prev, (128, ML_DQK)).astype(BF16)
        qn = _dot(s, ones_col) + w_inter * _dot_nt(qh, n_rows)
        inv = 1.0 / jnp.maximum(jnp.abs(qn), floor)

        ke = kh.astype(F32) * e_rep
        c_s[h] = lanes2(decay) * c_prev + _dot_tn(ke.astype(BF16), vh)
        n_s[h:h + 1, :] = decay * n_prev + jnp.sum(ke, axis=0, keepdims=True)
        m_s[h:h + 1, :] = m_new

        ssq = _dot((num * num).astype(BF16), jnp.ones((ML_DV, 128), BF16))
        rs = lax.rsqrt(ssq * (inv * inv) * (1.0 / ML_DV) + RMS_EPS) * inv
        hn = num * lanes2(rs) * gout_ref[:, h * ML_DV:(h + 1) * ML_DV]
        og = jax.nn.sigmoid(o_ref[:, h * ML_DV:(h + 1) * ML_DV])
        out_ref[:, h * ML_DV:(h + 1) * ML_DV] = hn.astype(BF16) * og


def mlstm_core(proj, gates, g_out, batch, seq):
    LC = ML_LC
    nc = seq // LC
    qk = ML_HEADS * ML_DQK
    vd = ML_HEADS * ML_DV
    row = lambda b, c: b * nc + c
    return pl.pallas_call(
        _mlstm_body,
        grid=(batch, nc),
        in_specs=[pl.BlockSpec((LC, qk), lambda b, c: (row(b, c), 0)),
                  pl.BlockSpec((LC, qk), lambda b, c: (row(b, c), 1)),
                  pl.BlockSpec((LC, vd), lambda b, c: (row(b, c), 1)),
                  pl.BlockSpec((LC, vd), lambda b, c: (row(b, c), 2)),
                  pl.BlockSpec((LC, 128), lambda b, c: (row(b, c), 0)),
                  pl.BlockSpec((1, vd), lambda b, c: (0, 0))],
        out_specs=pl.BlockSpec((LC, vd), lambda b, c: (row(b, c), 0)),
        out_shape=jax.ShapeDtypeStruct((batch * seq, vd), BF16),
        scratch_shapes=[pltpu.VMEM((ML_HEADS, ML_DQK, ML_DV), F32),
                        pltpu.VMEM((8, ML_DQK), F32),
                        pltpu.VMEM((8, 128), F32)],
        compiler_params=_cparams(("parallel", "arbitrary")),
        name="mlstm_core",
    )(proj, proj, proj, proj, gates, g_out.reshape(1, vd))


def _kv_proj_body(x_ref, g_ref, sh_ref, sc_ref, wk_ref, wvt_ref, wcc_ref, gk_ref, bd_ref,
                  ks_ref, kw_ref, vst_ref, vwt_ref, cc_ref):
    i = pl.program_id(1)
    G, HD = NSA_GROUPS, NSA_HD
    tm = x_ref.shape[0]
    lane = lax.broadcasted_iota(jnp.int32, (tm, G * 128), 1) % 128

    @pl.when(i == 0)
    def _():
        pad_rows = jnp.where(lane == HD + KEY_PAD, 1.0, 0.0).astype(BF16)
        ks_ref[...] = pad_rows
        kw_ref[...] = pad_rows
        vst_ref[...] = jnp.zeros_like(vst_ref)
        vwt_ref[...] = jnp.zeros_like(vwt_ref)

    @pl.when(i > 0)
    def _():
        h = _norm_mod(x_ref[...], g_ref[...], sh_ref[...], sc_ref[...]).astype(BF16)
        cc = _dot(h, wcc_ref[...])
        for q in range(cc_ref.shape[0]):
            cc_ref[q] = cc[:, q * 128:(q + 1) * 128]
        kk = _dot(h, wk_ref[...])
        ssq = _dot((kk * kk).astype(BF16), bd_ref[...])
        kn = kk * lax.rsqrt(ssq * (1.0 / HD) + RMS_EPS) * gk_ref[...]
        blk = (i - 1) * (tm // SLC_BLOCK) + lax.broadcasted_iota(
            jnp.int32, (tm, G * 128), 0) // SLC_BLOCK
        tail = jnp.where((lane == HD) | (lane == HD + 1) | (lane == HD + KEY_BLK + blk), 1.0, 0.0)
        ks_ref[...] = jnp.where(lane < HD, kn, tail).astype(BF16)
        kn_sw = jnp.concatenate(
            [pltpu.roll(kn[:, g * 128:(g + 1) * 128], HD, axis=1) for g in range(G)], axis=1)
        kw_ref[...] = jnp.where(lane < HD, kn_sw, tail).astype(BF16)
        vt = _dot_nt(wvt_ref[...], h)
        srow = lax.broadcasted_iota(jnp.int32, (VT_ROWS - HD, tm), 0)
        ones_blk = jnp.where(srow == 0, 1.0, 0.0).astype(BF16)
        for kind, o_ref in enumerate((vst_ref, vwt_ref)):
            for g in range(G):
                r0 = (kind * G + g) * HD
                o_ref[g * VT_ROWS:g * VT_ROWS + HD, :] = vt[r0:r0 + HD, :].astype(BF16)
                o_ref[g * VT_ROWS + HD:(g + 1) * VT_ROWS, :] = ones_blk


def _segment_ones(width, seg):
    idx = np.arange(width) // seg
    return (idx[:, None] == idx[None, :]).astype(np.float32)


def kv_proj(x2d, g, sh, sc, wk, wvt, wcc, gk, batch, seq, tm=KV_PAD):
    assert tm == KV_PAD and seq % tm == 0
    M, D = x2d.shape
    G = NSA_GROUPS
    nt = seq // tm
    sp = seq + KV_PAD
    ncs = wcc.shape[1] // 128
    bd = jnp.asarray(_segment_ones(G * 128, NSA_HD), dtype=BF16)
    xrow = lambda b, i: (b * nt + jnp.maximum(i - 1, 0), 0)
    bvec = pl.BlockSpec((None, 1, D), lambda b, i: (b, 0, 0))
    full = lambda a: pl.BlockSpec(a.shape, lambda b, i: (0,) * a.ndim)
    return pl.pallas_call(
        _kv_proj_body,
        grid=(batch, nt + 1),
        in_specs=[pl.BlockSpec((tm, D), xrow),
                  pl.BlockSpec((1, D), lambda b, i: (0, 0)),
                  bvec, bvec, full(wk), full(wvt), full(wcc), full(gk), full(bd)],
        out_specs=[pl.BlockSpec((None, tm, G * 128), lambda b, i: (b, i, 0)),
                   pl.BlockSpec((None, tm, G * 128), lambda b, i: (b, i, 0)),
                   pl.BlockSpec((None, G * VT_ROWS, tm), lambda b, i: (b, 0, i)),
                   pl.BlockSpec((None, G * VT_ROWS, tm), lambda b, i: (b, 0, i)),
                   pl.BlockSpec((ncs, tm, 128),
                                lambda b, i: (0, b * nt + jnp.maximum(i - 1, 0), 0))],
        out_shape=[jax.ShapeDtypeStruct((batch, sp, G * 128), BF16),
                   jax.ShapeDtypeStruct((batch, sp, G * 128), BF16),
                   jax.ShapeDtypeStruct((batch, G * VT_ROWS, sp), BF16),
                   jax.ShapeDtypeStruct((batch, G * VT_ROWS, sp), BF16),
                   jax.ShapeDtypeStruct((ncs, M, 128), F32)],
        compiler_params=_cparams(("parallel", "arbitrary")),
        name="kv_proj",
    )(x2d, g.reshape(1, D), sh, sc, wk, wvt, wcc, gk, bd)


def _gelu_tanh(x):
    c = math.sqrt(2.0 / math.pi)
    return 0.5 * x * (1.0 + jnp.tanh(c * (x + 0.044715 * (x * x * x))))


def _compress_body(cc0_ref, cc1_ref, cc2_ref, cc3_ref, pos_ref, w1_ref, w2_ref, w2t_ref, g_ref,
                   kc_ref, vct_ref):
    HD = NSA_HD
    nwin = kc_ref.shape[0]
    hid = w2_ref.shape[0]
    for p, cc_ref in enumerate((cc0_ref, cc1_ref, cc2_ref, cc3_ref)):
        kind = p // 2
        x = jnp.concatenate([cc_ref[pl.ds(i, nwin, stride=CMP_STRIDE), :]
                             for i in range(CMP_STRIDE)], axis=1)
        u = _dot((x + pos_ref[kind, 0:1, :]).astype(BF16), w1_ref[kind, 0])
        v = _dot((x + pos_ref[kind, 1:2, :]).astype(BF16), w1_ref[kind, 1])
        pre = u + pltpu.roll(v, nwin - 1, axis=0)
        for s in range(2):
            g = 2 * (p % 2) + s
            hmid = _gelu_tanh(pre[:, s * hid:(s + 1) * hid]).astype(BF16)
            if kind == 0:
                y = _dot(hmid, w2_ref[...])
                var = jnp.mean(y * y, axis=-1, keepdims=True)
                yn = y * lax.rsqrt(var + RMS_EPS) * g_ref[...]
                kc_ref[:, g * 128:g * 128 + HD] = yn.astype(BF16)
                kc_ref[:, g * 128 + HD:(g + 1) * 128] = jnp.zeros((nwin, 128 - HD), BF16)
            else:
                vct_ref[g * HD:(g + 1) * HD, :] = _dot_nt(w2t_ref[...], hmid).astype(BF16)


def _pair_expand(w):
    k, t, d, c = w.shape
    eye = jnp.eye(2, dtype=w.dtype)
    return jnp.einsum("ktdc,su->ktsduc", w, eye).reshape(k, t * 2 * d, 2 * c)


def compress(cc, pos, w1, w2k, w2vt, g, batch, seq):
    G, HD = NSA_GROUPS, NSA_HD
    nwin = seq // CMP_STRIDE
    hid = w1.shape[2]
    half = CMP_BLOCK // 2
    assert half == CMP_STRIDE
    w1r = w1.reshape(2, CMP_BLOCK, HD, hid)
    w1x = jnp.stack([_pair_expand(w1r[:, :half]), _pair_expand(w1r[:, half:])], axis=1).astype(BF16)
    posr = jnp.tile(pos.reshape(2, 2, half, 1, HD), (1, 1, 1, 2, 1)).reshape(2, 2, half * 2 * HD)
    resident = lambda a: pl.BlockSpec(a.shape, lambda b: (0,) * a.ndim, pipeline_mode=pl.Buffered(1))
    return pl.pallas_call(
        _compress_body,
        grid=(batch,),
        in_specs=[pl.BlockSpec((None, seq, 128), functools.partial(lambda q, b: (q, b, 0), q))
                  for q in range(4)]
                 + [resident(posr), resident(w1x), resident(w2k), resident(w2vt), resident(g)],
        out_specs=[pl.BlockSpec((None, nwin, G * 128), lambda b: (b, 0, 0)),
                   pl.BlockSpec((None, G * HD, nwin), lambda b: (b, 0, 0))],
        out_shape=[jax.ShapeDtypeStruct((batch, nwin, G * 128), BF16),
                   jax.ShapeDtypeStruct((batch, G * HD, nwin), BF16)],
        compiler_params=_cparams(("parallel",)),
        name="compress",
    )(cc, cc, cc, cc, posr, w1x, w2k, w2vt, g)


def _t5_bucket(dist):
    n = jnp.maximum(dist, 0)
    max_exact = REL_BUCKETS // 2
    nf = jnp.maximum(n, 1).astype(F32)
    large = max_exact + (jnp.log(nf / max_exact) / math.log(REL_MAX_DIST / max_exact)
                         * (REL_BUCKETS - max_exact)).astype(jnp.int32)
    large = jnp.minimum(large, REL_BUCKETS - 1)
    return jnp.where(n < max_exact, n, large)


def _table_lookup(bucket, tab_ref, h):
    out = jnp.zeros(bucket.shape, F32)
    for k in range(REL_BUCKETS):
        out = jnp.where(bucket == k, tab_ref[k, h], out)
    return out


def _bias_prep_body(tab_ref, bc_ref, tp_ref, far_ref):
    h = pl.program_id(0)
    S = bc_ref.shape[1]
    far = tab_ref[REL_BUCKETS - 1, h]
    far_ref[...] = jnp.full(far_ref.shape, far * LOG2E, F32)

    dist = lax.broadcasted_iota(jnp.int32, (8, S), 1)
    by_dist = _table_lookup(_t5_bucket(dist), tab_ref, h) * LOG2E
    shifted = pltpu.roll(jnp.broadcast_to(by_dist[0:1, :], (NCMP_PAD, S)), 0, axis=1,
                         stride=CMP_STRIDE, stride_axis=0)
    shifted = pltpu.roll(shifted, CMP_BLOCK - 1, axis=1)
    n = lax.broadcasted_iota(jnp.int32, (NCMP_PAD, S), 0)
    t = lax.broadcasted_iota(jnp.int32, (NCMP_PAD, S), 1)
    bc_ref[...] = jnp.where(t >= n * CMP_STRIDE + CMP_BLOCK - 1, shifted, NEG_INF)
    j = lax.broadcasted_iota(jnp.int32, (ATT_TK, ATT_QB), 0)
    i = lax.broadcasted_iota(jnp.int32, (ATT_TK, ATT_QB), 1)
    for d in range(2):
        dist = d * ATT_TK + i - j
        rel = (_table_lookup(_t5_bucket(dist), tab_ref, h) - far) * LOG2E
        tp_ref[d] = jnp.where(dist >= 0, rel, NEG_INF)
    tp_ref[2] = jnp.where(i < j, 0.0, NEG_INF)


def bias_prep(rel_table, seq):
    assert ATT_TK == ATT_QB and ATT_TK + 1 > 113
    return pl.pallas_call(
        _bias_prep_body,
        grid=(NSA_HEADS,),
        in_specs=[pl.BlockSpec(memory_space=pltpu.SMEM)],
        out_specs=[pl.BlockSpec((None, NCMP_PAD, seq), lambda h: (h, 0, 0)),
                   pl.BlockSpec((None, 3, ATT_TK, ATT_QB), lambda h: (h, 0, 0, 0)),
                   pl.BlockSpec((None, 8, 128), lambda h: (h, 0, 0))],
        out_shape=[jax.ShapeDtypeStruct((NSA_HEADS, NCMP_PAD, seq), F32),
                   jax.ShapeDtypeStruct((NSA_HEADS, 3, ATT_TK, ATT_QB), F32),
                   jax.ShapeDtypeStruct((NSA_HEADS, 8, 128), F32)],
        compiler_params=_cparams(("parallel",)),
        name="bias_prep",
    )(rel_table)


def _q_proj_body(x_ref, g_ref, sh_ref, sc_ref, wqt_ref, wgt_ref, bg_ref, gq_ref, qt_ref, gt_ref):
    HD = NSA_HD
    tm = x_ref.shape[0]
    half = tm // 2
    scale = gq_ref[...] * (HD ** -0.5 * LOG2E)
    for r0 in (0, half):
        h = _norm_mod(x_ref[r0:r0 + half, :], g_ref[...], sh_ref[...], sc_ref[...]).astype(BF16)
        gt_ref[:, r0:r0 + half] = _dot_nt(wgt_ref[...], h) + bg_ref[...]
        qt = _dot_nt(wqt_ref[...], h)
        for hh in range(NSA_HEADS):
            seg = qt[hh * HD:(hh + 1) * HD, :]
            var = jnp.mean(seg * seg, axis=0, keepdims=True)
            qt_ref[hh * HD:(hh + 1) * HD, r0:r0 + half] = (
                seg * lax.rsqrt(var + RMS_EPS) * scale).astype(BF16)


def q_proj(x2d, g, sh, sc, wqt, wgt, bg, gq, batch, seq, tm=1024):
    M, D = x2d.shape
    nt = seq // tm
    nq = wqt.shape[0]
    ng = wgt.shape[0]
    bvec = pl.BlockSpec((None, 1, D), lambda i: (i // nt, 0, 0))
    full = lambda a: pl.BlockSpec(a.shape, lambda i: (0,) * a.ndim)
    return pl.pallas_call(
        _q_proj_body,
        grid=(M // tm,),
        in_specs=[pl.BlockSpec((tm, D), lambda i: (i, 0)),
                  pl.BlockSpec((1, D), lambda i: (0, 0)),
                  bvec, bvec, full(wqt), full(wgt), full(bg), full(gq)],
        out_specs=[pl.BlockSpec((None, nq, tm), lambda i: (i // nt, 0, i % nt)),
                   pl.BlockSpec((None, ng, tm), lambda i: (i // nt, 0, i % nt))],
        out_shape=[jax.ShapeDtypeStruct((batch, nq, seq), BF16),
                   jax.ShapeDtypeStruct((batch, ng, seq), F32)],
        compiler_params=_cparams(("parallel",)),
        name="q_proj",
    )(x2d, g.reshape(1, D), sh, sc, wqt, wgt, bg, gq)


def _heads_on_lanes(pieces):
    return jnp.concatenate(pieces, axis=1)


def _nsa_body(gps, qt_ref, gt_ref, kc_ref, vct_ref, ks_ref, kw_ref, vst_ref, vwt_ref,
              bct_ref, tp_ref, far_ref, ovt_ref, out_ref):
    qb = pl.program_id(2)
    QB, HD, HPG = ATT_QB, NSA_HD, NSA_HPG
    R = HPG * QB
    t0 = pl.multiple_of(qb * QB, QB)
    near0 = pl.multiple_of(t0 + WINDOW - ATT_TK, ATT_TK)
    wlen = WINDOW + QB
    nblk = ovt_ref.shape[0]
    jb = lax.broadcasted_iota(jnp.int32, (nblk, QB), 0)
    jbf = jb.astype(F32)
    qid = jnp.right_shift(t0 + lax.broadcasted_iota(jnp.int32, (nblk, QB), 1), 6)
    forced = (jb == 0) | (jb == qid) | (jb == qid - 1)
    srow = lax.broadcasted_iota(jnp.int32, (8, R), 0)

    def finish(acc):
        return acc[0:HD, :] / acc[HD:HD + 1, :]

    def tile_part(s_tile, v_tile):
        m_t = jnp.max(s_tile, axis=0, keepdims=True)
        m_safe = jnp.where(m_t == NEG_INF, 0.0, m_t)
        p = jnp.exp2(s_tile - m_safe).astype(BF16)
        return m_t, _dot(v_tile, p)[0:HD + 8, :]

    def combine(parts):
        m_fin = parts[0][0]
        for m_t, _ in parts[1:]:
            m_fin = jnp.maximum(m_fin, m_t)
        acc = None
        for m_t, pv in parts:
            term = jnp.exp2(m_t - m_fin) * pv
            acc = term if acc is None else acc + term
        return m_fin, acc

    groups = range(gps)
    hs = [[gl * HPG + h for h in range(HPG)] for gl in groups]
    kcol = [slice(gl * 128, (gl + 1) * 128) for gl in groups]
    vrow = [slice(gl * VT_ROWS, (gl + 1) * VT_ROWS) for gl in groups]

    def stationary(q_all, hi_lo, block_rows):
        pad = jnp.where(srow == 0, MASK_BIG, 0.0)
        rest = jnp.zeros((128 - HD - KEY_PAD - 8, R), F32)
        tail = jnp.concatenate([hi_lo, block_rows, pad, rest], axis=0).astype(BF16)
        return jnp.concatenate([q_all, tail], axis=0)

    q_alls, hi_los, qms = [], [], []
    for gl in groups:
        q_alls.append(_heads_on_lanes([qt_ref[h * HD:(h + 1) * HD, :] for h in hs[gl]]))
        far = _heads_on_lanes([far_ref[h, 0:1, :] for h in hs[gl]])
        hi = far.astype(BF16).astype(F32)
        hi_los.append(jnp.where(srow == 0, hi, jnp.where(srow == 1, far - hi, 0.0)))
        qms.append(stationary(q_alls[gl], hi_los[gl], jnp.zeros((nblk, R), F32)))

    s_cs = [_dot(kc_ref[:, kcol[gl]], qms[gl]) for gl in groups]
    s_ws = [_dot(kw_ref[pl.ds(t0, wlen), kcol[gl]], qms[gl]) for gl in groups]
    s_ns = [_dot(ks_ref[pl.ds(near0, 2 * ATT_TK), kcol[gl]], qms[gl]) for gl in groups]

    o_cmps, scores = [], []
    for gl in groups:
        s_c = s_cs[gl] + _heads_on_lanes([bct_ref[h] for h in hs[gl]])
        m_c = jnp.max(s_c, axis=0, keepdims=True)
        m_c = jnp.where(m_c == NEG_INF, 0.0, m_c)
        e_c = jnp.exp2(s_c - m_c)
        p_c = e_c / jnp.maximum(jnp.sum(e_c, axis=0, keepdims=True), jnp.finfo(F32).tiny)
        o_cmps.append(_dot(vct_ref[gl * HD:(gl + 1) * HD, :], p_c.astype(BF16)))
        p_sum = p_c[:, 0:QB]
        for h in range(1, HPG):
            p_sum = p_sum + p_c[:, h * QB:(h + 1) * QB]
        p_hi = p_sum.astype(BF16)
        p_lo = (p_sum - p_hi.astype(F32)).astype(BF16)
        imp = _dot(ovt_ref[...], jnp.concatenate([p_hi, p_lo], axis=0))
        score = jnp.where(forced, FORCE_SCORE, imp)
        scores.append(jnp.where(jb <= qid, score, NEG_INF))

    nwt = wlen // ATT_TK
    o_wins = []
    for gl in groups:
        parts = []
        for i in range(nwt):
            d = nwt - 1 - i
            s_t = s_ws[gl][i * ATT_TK:(i + 1) * ATT_TK]
            if d in (0, 1):
                s_t = s_t + _heads_on_lanes([tp_ref[h, d] for h in hs[gl]])
            elif d == nwt - 1:
                s_t = s_t + _heads_on_lanes([tp_ref[h, 2] for h in hs[gl]])
            parts.append(tile_part(s_t, vwt_ref[vrow[gl], pl.ds(t0 + i * ATT_TK, ATT_TK)]))
        o_wins.append(finish(combine(parts)[1]))

    msels = [jnp.full((nblk, QB), NEG_INF, F32) for _ in groups]
    for _ in range(SLC_TOPK):
        for gl in groups:
            mx = jnp.max(scores[gl], axis=0, keepdims=True)
            first = jnp.min(jnp.where(scores[gl] == mx, jbf, float(nblk)), axis=0, keepdims=True)
            pick = jbf == first
            msels[gl] = jnp.where(pick & (mx > NEG_INF), 0.0, msels[gl])
            scores[gl] = jnp.where(pick, NEG_INF, scores[gl])
    carry0 = []
    for gl in groups:
        prev_mask = jnp.concatenate(
            [jnp.broadcast_to(jnp.max(jnp.where(jb == 2 * qb - 2 + r, msels[gl], NEG_INF), axis=0,
                                      keepdims=True), (SLC_BLOCK, QB)) for r in range(2)], axis=0)
        s_prev = s_ns[gl][0:ATT_TK] + _heads_on_lanes([tp_ref[h, 1] + prev_mask for h in hs[gl]])
        s_diag = s_ns[gl][ATT_TK:] + _heads_on_lanes([tp_ref[h, 0] for h in hs[gl]])
        carry0.append(combine([
            tile_part(s_diag, vst_ref[vrow[gl], pl.ds(near0 + ATT_TK, ATT_TK)]),
            tile_part(s_prev, vst_ref[vrow[gl], pl.ds(near0, ATT_TK)])]))

    qss = [stationary(q_alls[gl], hi_los[gl], _heads_on_lanes(
        [jnp.where(msels[gl] == 0.0, 0.0, MASK_BIG)] * HPG)) for gl in groups]

    def far_step(c, carry):
        tile0 = qb + (WINDOW - ATT_TK) // ATT_TK - (c + 1) * (FAR_CHUNK // ATT_TK)
        row0 = pl.multiple_of(tile0 * ATT_TK, ATT_TK)
        s_fs = [_dot(ks_ref[pl.ds(row0, FAR_CHUNK), kcol[gl]], qss[gl]) for gl in groups]
        out = []
        for gl in groups:
            parts = [carry[gl]]
            for j in range(FAR_CHUNK // ATT_TK):
                parts.append(tile_part(s_fs[gl][j * ATT_TK:(j + 1) * ATT_TK],
                                       vst_ref[vrow[gl], pl.ds(row0 + j * ATT_TK, ATT_TK)]))
            out.append(combine(parts))
        return tuple(out)

    n_far = (qb + 2) // 4
    sel = lax.fori_loop(0, n_far, far_step, tuple(carry0))

    gates = jax.nn.sigmoid(gt_ref[...])
    for gl in groups:
        o_cmp, o_win = o_cmps[gl], o_wins[gl]
        o_sel = finish(sel[gl][1])
        outs = []
        for h in range(HPG):
            lanes = slice(h * QB, (h + 1) * QB)
            r = gl * 16 + 3 * h
            outs.append(gates[r:r + 1, :] * o_cmp[:, lanes]
                        + gates[r + 1:r + 2, :] * o_sel[:, lanes]
                        + gates[r + 2:r + 3, :] * o_win[:, lanes])
        for pair in range(HPG // 2):
            two = jnp.concatenate(outs[2 * pair:2 * pair + 2], axis=0)
            c0 = gl * HPG * HD + pair * 2 * HD
            out_ref[:, c0:c0 + 2 * HD] = two.T.astype(out_ref.dtype)


def _overlap_matrix_t(nblk):
    start = np.arange(NCMP_PAD) * CMP_STRIDE
    sj = np.arange(nblk) * SLC_BLOCK
    ov = (np.minimum(start[None, :] + CMP_BLOCK, sj[:, None] + SLC_BLOCK)
          - np.maximum(start[None, :], sj[:, None]))
    ov = np.clip(ov, 0, None) / CMP_BLOCK
    ov[:, NCMP_PAD - 1] = 0.0
    return np.concatenate([ov, ov], axis=1).astype(np.float32)


def nsa_attn(qt, gt, kc, vct, ks, kw, vst, vwt, bias_ct, tiles, far, batch, seq, gps=ATT_GPS):
    QB, HD, HPG, G = ATT_QB, NSA_HD, NSA_HPG, NSA_GROUPS
    nblk = seq // SLC_BLOCK
    assert nblk % 8 == 0 and seq % QB == 0 and seq // CMP_STRIDE == NCMP_PAD and G % gps == 0
    assert WINDOW % ATT_TK == 0 and FAR_CHUNK == 4 * ATT_TK and KV_PAD >= FAR_CHUNK - ATT_TK
    assert KEY_BLK % 8 == 0 and KEY_BLK + nblk <= KEY_PAD and KEY_PAD + 8 <= 128 - HD
    nq = seq // QB
    sp = seq + KV_PAD
    ovt = jnp.asarray(_overlap_matrix_t(nblk), dtype=BF16)
    return pl.pallas_call(
        functools.partial(_nsa_body, gps),
        grid=(batch, G // gps, nq),
        in_specs=[pl.BlockSpec((None, gps * HPG * HD, QB), lambda b, g, i: (b, g, i)),
                  pl.BlockSpec((None, gps * 16, QB), lambda b, g, i: (b, g, i)),
                  pl.BlockSpec((None, NCMP_PAD, gps * 128), lambda b, g, i: (b, 0, g)),
                  pl.BlockSpec((None, gps * HD, NCMP_PAD), lambda b, g, i: (b, g, 0)),
                  pl.BlockSpec((None, sp, gps * 128), lambda b, g, i: (b, 0, g)),
                  pl.BlockSpec((None, sp, gps * 128), lambda b, g, i: (b, 0, g)),
                  pl.BlockSpec((None, gps * VT_ROWS, sp), lambda b, g, i: (b, g, 0)),
                  pl.BlockSpec((None, gps * VT_ROWS, sp), lambda b, g, i: (b, g, 0)),
                  pl.BlockSpec((gps * HPG, NCMP_PAD, QB), lambda b, g, i: (g, 0, i)),
                  pl.BlockSpec((gps * HPG, 3, ATT_TK, QB), lambda b, g, i: (g, 0, 0, 0)),
                  pl.BlockSpec((gps * HPG, 8, 128), lambda b, g, i: (g, 0, 0)),
                  pl.BlockSpec((nblk, 2 * NCMP_PAD), lambda b, g, i: (0, 0))],
        out_specs=pl.BlockSpec((QB, gps * HPG * HD), lambda b, g, i: (b * nq + i, g)),
        out_shape=jax.ShapeDtypeStruct((batch * seq, NSA_HEADS * HD), BF16),
        compiler_params=_cparams(("parallel", "parallel", "arbitrary")),
        name="nsa_attn",
    )(qt, gt, kc, vct, ks, kw, vst, vwt, bias_ct, tiles, far, ovt)


def _mlstm_layer(x2d, g_mix, sh, sc, w_in, b_if, g_out, batch, seq):
    nbig = 2 * ML_HEADS * ML_DQK + 2 * ML_HEADS * ML_DV
    ng = 2 * ML_HEADS
    wg = jnp.pad(w_in[:, nbig:], ((0, 0), (0, 128 - ng)))
    bg = jnp.pad(b_if, (0, 128 - ng))
    proj, gates = norm_proj(x2d, g_mix, sh, sc, [w_in[:, :nbig], wg], [None, bg], [BF16, F32])
    return mlstm_core(proj, gates, g_out, batch, seq)


def _nsa_shared(x2d, g_kv, kv_sh, kv_sc, w_kv, pos_k, w_k1, w_k2, pos_v, w_v1, w_v2,
                g_knorm, batch, seq):
    gw = NSA_GROUPS * NSA_HD
    part = lambda i: w_kv[:, i * gw:(i + 1) * gw]
    hd = NSA_HD
    wk = jnp.concatenate([p[:, g * hd:(g + 1) * hd] for g in range(NSA_GROUPS)
                          for p in (part(2), part(4))], axis=1).astype(BF16)
    gk = jnp.tile(jnp.concatenate([g_knorm[1], g_knorm[2]]), NSA_GROUPS).reshape(1, -1)
    wvt = jnp.concatenate([part(3), part(5)], axis=1).T.astype(BF16)
    wcc = jnp.concatenate([part(0), part(1)], axis=1).astype(BF16)
    ks, kw, vst, vwt, cc = kv_proj(x2d, g_kv, kv_sh, kv_sc, wk, wvt, wcc, gk, batch, seq)
    pos = jnp.stack([pos_k, pos_v])
    w1 = jnp.stack([w_k1, w_v1])
    kc, vct = compress(cc, pos, w1, w_k2.astype(BF16), w_v2.T.astype(BF16), g_knorm[0:1],
                       batch, seq)
    return kc, vct, ks, kw, vst, vwt


def _gate_weights_t(w_q, b_gate):
    nq = NSA_HEADS * NSA_HD
    per = 3 * NSA_HPG
    wg = w_q[:, nq:].T.reshape(NSA_GROUPS, per, -1)
    wg = jnp.pad(wg, ((0, 0), (0, 16 - per), (0, 0))).reshape(NSA_GROUPS * 16, -1)
    bg = jnp.pad(b_gate.reshape(NSA_GROUPS, per), ((0, 0), (0, 16 - per))).reshape(-1, 1)
    return wg.astype(BF16), bg


def _nsa_layer(x2d, g_mix, sh, sc, shared, w_q, b_gate, g_qnorm, bias_ct, tiles, far, batch, seq):
    nq = NSA_HEADS * NSA_HD
    wgt, bg = _gate_weights_t(w_q, b_gate)
    qt, gt = q_proj(x2d, g_mix, sh, sc, w_q[:, :nq].T.astype(BF16), wgt, bg,
                    g_qnorm.reshape(NSA_HD, 1), batch, seq)
    kc, vct, ks, kw, vst, vwt = shared
    return nsa_attn(qt, gt, kc, vct, ks, kw, vst, vwt, bias_ct, tiles, far, batch, seq)


def kernel(x, c, w_ada, b_ada, g_norm_mix, g_norm_ffn, w_ffn_in, w_ffn_out, w_a_in, b_a_if, g_a_out, w_a_out, w_kv_ada, b_kv_ada, g_kv_norm, w_kv, pos_cmp_k, w_cmp_k1, w_cmp_k2, pos_cmp_v, w_cmp_v1, w_cmp_v2, g_knorm, w_b_q, b_b_gate, g_qnorm, w_b_out, rel_table):
    B, S, D = x.shape
    depth = w_ada.shape[0]
    n_a = w_a_in.shape[0]
    x2d = x.reshape(B * S, D)
    mods = ada_mod(c, w_ada, b_ada)
    kv_mod = ada_mod(c, w_kv_ada[None], b_kv_ada[None])[0]
    shared = None
    bias_ct = tiles = far = None
    for layer in range(depth):
        sh1, sc1, ga1, sh2, sc2, ga2 = [mods[layer, :, i * D:(i + 1) * D].reshape(B, 1, D)
                                        for i in range(6)]
        if layer < n_a:
            mixed = _mlstm_layer(x2d, g_norm_mix[layer], sh1, sc1, w_a_in[layer], b_a_if[layer],
                                 g_a_out[layer], B, S)
            w_mix = (w_a_out, layer)
        else:
            j = layer - n_a
            if shared is None:
                kv_sh = kv_mod[:, :D].reshape(B, 1, D)
                kv_sc = kv_mod[:, D:].reshape(B, 1, D)
                shared = _nsa_shared(x2d, g_kv_norm, kv_sh, kv_sc, w_kv, pos_cmp_k, w_cmp_k1,
                                     w_cmp_k2, pos_cmp_v, w_cmp_v1, w_cmp_v2, g_knorm, B, S)
                bias_ct, tiles, far = bias_prep(rel_table, S)
            mixed = _nsa_layer(x2d, g_norm_mix[layer], sh1, sc1, shared, w_b_q[j], b_b_gate[j],
                               g_qnorm[j], bias_ct, tiles, far, B, S)
            w_mix = (w_b_out, j)
        x2d = mix_ffn(mixed, w_mix, x2d, ga1, g_norm_ffn[layer], sh2, sc2, ga2,
                      (w_ffn_in, layer), (w_ffn_out, layer))
    return x2d.reshape(B, S, D)
```

```python
import functools
import math

import jax
import jax.numpy as jnp
import numpy as np
from jax import lax
from jax.experimental import pallas as pl
from jax.experimental.pallas import tpu as pltpu

F32 = jnp.float32
BF16 = jnp.bfloat16
NEG_INF = float("-inf")
LOG2E = math.log2(math.e)

RMS_EPS = 1e-6

ML_HEADS = 4
ML_DQK = 128
ML_DV = 256
ML_LC = 256

NSA_HEADS = 16
NSA_GROUPS = 4
NSA_HPG = 4
NSA_HD = 64
CMP_BLOCK = 32
CMP_STRIDE = 16
SLC_BLOCK = 64
SLC_TOPK = 8
WINDOW = 512
FORCE_SCORE = 1e4
REL_BUCKETS = 32
REL_MAX_DIST = 128
ATT_QB = 128
ATT_TK = 128
ATT_GPS = 4
NCMP_PAD = 128
KV_PAD = WINDOW
FAR_CHUNK = 512
VT_ROWS = 80
KEY_BLK = 8
KEY_PAD = 40
MASK_BIG = -32768.0

VMEM_LIMIT = 56 * 1024 * 1024


def _cparams(sem):
    return pltpu.CompilerParams(dimension_semantics=sem, vmem_limit_bytes=VMEM_LIMIT)


def _dot(a, b):
    return jnp.dot(a, b, preferred_element_type=F32)


def _dot_nt(a, b):
    return lax.dot_general(a, b, (((1,), (1,)), ((), ())), preferred_element_type=F32)


def _dot_tn(a, b):
    return lax.dot_general(a, b, (((0,), (0,)), ((), ())), preferred_element_type=F32)


def _norm_mod(x, g, sh, sc):
    var = jnp.mean(x * x, axis=-1, keepdims=True)
    y = x * lax.rsqrt(var + RMS_EPS) * g
    return y * (1.0 + sc) + sh


def _ada_body(c_ref, w_ref, b_ref, o_ref):
    c = c_ref[...]
    ca = c * jax.nn.sigmoid(c)
    nb = ca.shape[0]
    a_hi = ca.astype(BF16).astype(F32)
    a_mid = (ca - a_hi).astype(BF16).astype(F32)
    a_lo = ca - a_hi - a_mid
    a3 = jnp.concatenate([a_hi, a_mid, a_lo, jnp.zeros_like(ca)], axis=0).astype(BF16)
    w = w_ref[...]
    w_hi = w.astype(BF16)
    w_lo = (w - w_hi.astype(F32)).astype(BF16)
    p = _dot(a3, w_hi)
    q = _dot(a3[0:2 * nb], w_lo)
    o_ref[...] = (p[0:nb] + p[nb:2 * nb] + p[2 * nb:3 * nb] + q[0:nb] + q[nb:2 * nb]) + b_ref[...]


def ada_mod(c, w, b, tn=2048):
    L, D, N = w.shape
    B = c.shape[0]
    return pl.pallas_call(
        _ada_body,
        grid=(L, N // tn),
        in_specs=[pl.BlockSpec((B, D), lambda l, j: (0, 0)),
                  pl.BlockSpec((None, D, tn), lambda l, j: (l, 0, j)),
                  pl.BlockSpec((None, 1, tn), lambda l, j: (l, 0, j))],
        out_specs=pl.BlockSpec((None, B, tn), lambda l, j: (l, 0, j)),
        out_shape=jax.ShapeDtypeStruct((L, B, N), F32),
        compiler_params=_cparams(("parallel", "parallel")),
        name="ada_mod",
    )(c, w, b.reshape(L, 1, N))


def _mlstm_proj_body(nbig, ng, tn, x_ref, g_ref, sh_ref, sc_ref, w_ref, b_ref, p_ref, gt_ref):
    tm = x_ref.shape[0]
    half = tm // 2
    for r0 in (0, half):
        rows = slice(r0, r0 + half)
        h = _norm_mod(x_ref[rows, :], g_ref[...], sh_ref[...], sc_ref[...]).astype(BF16)
        for n0 in range(0, nbig, tn):
            p_ref[rows, n0:n0 + tn] = _dot(h, w_ref[:, n0:n0 + tn].astype(BF16)).astype(p_ref.dtype)
        gates = _dot(h, w_ref[:, nbig:nbig + ng].astype(BF16)) + b_ref[...]
        gt_ref[rows, 0:ng] = gates
        gt_ref[rows, ng:] = jnp.zeros((half, gt_ref.shape[1] - ng), F32)


def mlstm_proj(x2d, g, sh, sc, w_stack, layer, b_if, tm=1024, tn=512):
    M, D = x2d.shape
    B = sh.shape[0]
    ng = b_if.shape[0]
    nbig = w_stack.shape[2] - ng
    tiles_per_batch = (M // B) // tm
    bvec = pl.BlockSpec((None, 1, D), lambda i: (i // tiles_per_batch, 0, 0))
    return pl.pallas_call(
        functools.partial(_mlstm_proj_body, nbig, ng, tn),
        grid=(M // tm,),
        in_specs=[pl.BlockSpec((tm, D), lambda i: (i, 0)),
                  pl.BlockSpec((1, D), lambda i: (0, 0)), bvec, bvec,
                  pl.BlockSpec((None,) + w_stack.shape[1:], lambda i: (layer, 0, 0),
                               pipeline_mode=pl.Buffered(1)),
                  pl.BlockSpec((1, ng), lambda i: (0, 0))],
        out_specs=[pl.BlockSpec((tm, nbig), lambda i: (i, 0)),
                   pl.BlockSpec((tm, 128), lambda i: (i, 0))],
        out_shape=[jax.ShapeDtypeStruct((M, nbig), BF16), jax.ShapeDtypeStruct((M, 128), F32)],
        compiler_params=_cparams(("parallel",)),
        name="mlstm_proj",
    )(x2d, g.reshape(1, D), sh, sc, w_stack, b_if.reshape(1, ng))


def _mix_ffn_body(tf, a_ref, wm_ref, x_ref, ga1_ref, g_ref, sh_ref, sc_ref, ga2_ref,
                  wi_ref, wo_ref, o_ref, act_s):
    F = wo_ref.shape[0]
    x1 = x_ref[...] + ga1_ref[...] * _dot(a_ref[...], wm_ref[...].astype(BF16))
    h = _norm_mod(x1, g_ref[...], sh_ref[...], sc_ref[...]).astype(BF16)
    for f0 in range(0, F, tf):
        gate = _dot(h, wi_ref[:, f0:f0 + tf].astype(BF16))
        up = _dot(h, wi_ref[:, F + f0:F + f0 + tf].astype(BF16))
        act_s[:, f0:f0 + tf] = (gate * jax.nn.sigmoid(gate) * up).astype(BF16)
    y = _dot(act_s[:, 0:tf], wo_ref[0:tf, :].astype(BF16))
    for f0 in range(tf, F, tf):
        y = y + _dot(act_s[:, f0:f0 + tf], wo_ref[f0:f0 + tf, :].astype(BF16))
    o_ref[...] = x1 + ga2_ref[...] * y


def mix_ffn(a, w_mix, x2d, ga1, g, sh, sc, ga2, w_in, w_out, tm=512, tf=256):
    M, D = x2d.shape
    K = a.shape[1]
    F = w_out[0].shape[1]
    B = sh.shape[0]
    tiles_per_batch = (M // B) // tm
    bvec = pl.BlockSpec((None, 1, D), lambda i: (i // tiles_per_batch, 0, 0))

    def resident(stacked_layer):
        w, layer = stacked_layer
        return pl.BlockSpec((None,) + w.shape[1:], lambda i: (layer, 0, 0),
                            pipeline_mode=pl.Buffered(1))

    return pl.pallas_call(
        functools.partial(_mix_ffn_body, tf),
        grid=(M // tm,),
        in_specs=[pl.BlockSpec((tm, K), lambda i: (i, 0)),
                  resident(w_mix),
                  pl.BlockSpec((tm, D), lambda i: (i, 0)),
                  bvec,
                  pl.BlockSpec((1, D), lambda i: (0, 0)),
                  bvec, bvec, bvec,
                  resident(w_in), resident(w_out)],
        out_specs=pl.BlockSpec((tm, D), lambda i: (i, 0)),
        out_shape=jax.ShapeDtypeStruct((M, D), F32),
        scratch_shapes=[pltpu.VMEM((tm, F), BF16)],
        compiler_params=_cparams(("parallel",)),
        name="mix_ffn",
    )(a, w_mix[0], x2d, ga1, g.reshape(1, D), sh, sc, ga2, w_in[0], w_out[0])


def _sublane_scan(x, op, fill):
    n = x.shape[0]
    row = lax.broadcasted_iota(jnp.int32, x.shape, 0)
    sh = 1
    while sh < n:
        x = op(x, jnp.where(row >= sh, pltpu.roll(x, sh, axis=0), fill))
        sh *= 2
    return x


def _mlstm_body(q_ref, k_ref, v_ref, o_ref, gc_ref, gout_ref, out_ref, c_s, n_s, m_s):
    c_idx = pl.program_id(1)
    LC = q_ref.shape[0]
    NH = ML_HEADS
    scale = ML_DQK ** -0.5
    log_scale = math.log(scale)

    @pl.when(c_idx == 0)
    def _():
        c_s[...] = jnp.zeros_like(c_s)
        n_s[...] = jnp.zeros_like(n_s)
        m_s[...] = jnp.zeros_like(m_s)

    gc = gc_ref[...]
    logf = jnp.minimum(gc, 0.0) - jnp.log1p(jnp.exp(-jnp.abs(gc)))
    bcum = pltpu.roll(_sublane_scan(logf, jnp.add, 0.0), 128 - NH, axis=1)
    a_c = gc - bcum
    cmax = _sublane_scan(a_c, jnp.maximum, NEG_INF)
    a_t = a_c.T

    row = lax.broadcasted_iota(jnp.int32, (LC, LC), 0)
    col = lax.broadcasted_iota(jnp.int32, (LC, LC), 1)
    causal = col <= row
    ones_col = jnp.ones((LC, 128), BF16)

    def lanes2(x):
        return jnp.concatenate([x, x], axis=1)

    for h in range(NH):
        a_rep = jnp.broadcast_to(a_c[:, h:h + 1], (LC, 128))
        b_rep = jnp.broadcast_to(bcum[:, h:h + 1], (LC, 128))
        cm_rep = jnp.broadcast_to(cmax[:, h:h + 1], (LC, 128))
        a_r = a_t[h:h + 1, :] + log_scale
        b_last = b_rep[LC - 1:LC, :]
        m_prev = m_s[h:h + 1, :]
        m_new = jnp.maximum(b_last + m_prev, b_last + cm_rep[LC - 1:LC, :])
        decay = jnp.exp(b_last + m_prev - m_new)
        e_rep = jnp.exp(b_last + a_rep - m_new)
        g_rep = jnp.maximum(m_prev, cm_rep)
        w_intra = jnp.exp(jnp.where(causal, a_r - lanes2(g_rep), NEG_INF))
        w_inter = jnp.exp(m_prev - g_rep) * scale
        floor = jnp.exp(-(b_rep + g_rep))

        qh = q_ref[:, h * ML_DQK:(h + 1) * ML_DQK]
        kh = k_ref[:, h * ML_DQK:(h + 1) * ML_DQK]
        vh = v_ref[:, h * ML_DV:(h + 1) * ML_DV]
        c_prev = c_s[h]
        n_prev = n_s[h:h + 1, :]

        s = (_dot_nt(qh, kh) * w_intra).astype(BF16)
        inter = _dot(qh, c_prev.astype(BF16))
        num = _dot(s, vh) + lanes2(w_inter) * inter
        n_rows = jnp.broadcast_to(n_prev, (128, ML_DQK)).astype(BF16)
        qn = _dot(s, ones_col) + w_inter * _dot_nt(qh, n_rows)
        inv = 1.0 / jnp.maximum(jnp.abs(qn), floor)

        ke = kh.astype(F32) * e_rep
        c_s[h] = lanes2(decay) * c_prev + _dot_tn(ke.astype(BF16), vh)
        n_s[h:h + 1, :] = decay * n_prev + jnp.sum(ke, axis=0, keepdims=True)
        m_s[h:h + 1, :] = m_new

        ssq = _dot((num * num).astype(BF16), jnp.ones((ML_DV, 128), BF16))
        rs = lax.rsqrt(ssq * (inv * inv) * (1.0 / ML_DV) + RMS_EPS) * inv
        hn = num * lanes2(rs) * gout_ref[:, h * ML_DV:(h + 1) * ML_DV]
        og = jax.nn.sigmoid(o_ref[:, h * ML_DV:(h + 1) * ML_DV])
        out_ref[:, h * ML_DV:(h + 1) * ML_DV] = hn.astype(BF16) * og


def mlstm_core(proj, gates, g_out, batch, seq):
    LC = ML_LC
    nc = seq // LC
    qk = ML_HEADS * ML_DQK
    vd = ML_HEADS * ML_DV
    row = lambda b, c: b * nc + c
    return pl.pallas_call(
        _mlstm_body,
        grid=(batch, nc),
        in_specs=[pl.BlockSpec((LC, qk), lambda b, c: (row(b, c), 0)),
                  pl.BlockSpec((LC, qk), lambda b, c: (row(b, c), 1)),
                  pl.BlockSpec((LC, vd), lambda b, c: (row(b, c), 1)),
                  pl.BlockSpec((LC, vd), lambda b, c: (row(b, c), 2)),
                  pl.BlockSpec((LC, 128), lambda b, c: (row(b, c), 0)),
                  pl.BlockSpec((1, vd), lambda b, c: (0, 0))],
        out_specs=pl.BlockSpec((LC, vd), lambda b, c: (row(b, c), 0)),
        out_shape=jax.ShapeDtypeStruct((batch * seq, vd), BF16),
        scratch_shapes=[pltpu.VMEM((ML_HEADS, ML_DQK, ML_DV), F32),
                        pltpu.VMEM((8, ML_DQK), F32),
                        pltpu.VMEM((8, 128), F32)],
        compiler_params=_cparams(("parallel", "arbitrary")),
        name="mlstm_core",
    )(proj, proj, proj, proj, gates, g_out.reshape(1, vd))


def _kv_proj_body(x_ref, g_ref, sh_ref, sc_ref, wk_ref, wvt_ref, wcc_ref, gk_ref, bd_ref,
                  ks_ref, kw_ref, vst_ref, vwt_ref, cc_ref):
    i = pl.program_id(1)
    G, HD = NSA_GROUPS, NSA_HD
    tm = x_ref.shape[0]
    lane = lax.broadcasted_iota(jnp.int32, (tm, G * 128), 1) % 128

    @pl.when(i == 0)
    def _():
        pad_rows = jnp.where(lane == HD + KEY_PAD, 1.0, 0.0).astype(BF16)
        ks_ref[...] = pad_rows
        kw_ref[...] = pad_rows
        vst_ref[...] = jnp.zeros_like(vst_ref)
        vwt_ref[...] = jnp.zeros_like(vwt_ref)

    @pl.when(i > 0)
    def _():
        h = _norm_mod(x_ref[...], g_ref[...], sh_ref[...], sc_ref[...]).astype(BF16)
        cc = _dot(h, wcc_ref[...])
        for q in range(cc_ref.shape[0]):
            cc_ref[q] = cc[:, q * 128:(q + 1) * 128]
        kk = _dot(h, wk_ref[...])
        ssq = _dot((kk * kk).astype(BF16), bd_ref[...])
        kn = kk * lax.rsqrt(ssq * (1.0 / HD) + RMS_EPS) * gk_ref[...]
        blk = (i - 1) * (tm // SLC_BLOCK) + lax.broadcasted_iota(
            jnp.int32, (tm, G * 128), 0) // SLC_BLOCK
        tail = jnp.where((lane == HD) | (lane == HD + 1) | (lane == HD + KEY_BLK + blk), 1.0, 0.0)
        ks_ref[...] = jnp.where(lane < HD, kn, tail).astype(BF16)
        kn_sw = jnp.concatenate(
            [pltpu.roll(kn[:, g * 128:(g + 1) * 128], HD, axis=1) for g in range(G)], axis=1)
        kw_ref[...] = jnp.where(lane < HD, kn_sw, tail).astype(BF16)
        vt = _dot_nt(wvt_ref[...], h)
        srow = lax.broadcasted_iota(jnp.int32, (VT_ROWS - HD, tm), 0)
        ones_blk = jnp.where(srow == 0, 1.0, 0.0).astype(BF16)
        for kind, o_ref in enumerate((vst_ref, vwt_ref)):
            for g in range(G):
                r0 = (kind * G + g) * HD
                o_ref[g * VT_ROWS:g * VT_ROWS + HD, :] = vt[r0:r0 + HD, :].astype(BF16)
                o_ref[g * VT_ROWS + HD:(g + 1) * VT_ROWS, :] = ones_blk


def _segment_ones(width, seg):
    idx = np.arange(width) // seg
    return (idx[:, None] == idx[None, :]).astype(np.float32)


def kv_proj(x2d, g, sh, sc, wk, wvt, wcc, gk, batch, seq, tm=KV_PAD):
    assert tm == KV_PAD and seq % tm == 0
    M, D = x2d.shape
    G = NSA_GROUPS
    nt = seq // tm
    sp = seq + KV_PAD
    ncs = wcc.shape[1] // 128
    bd = jnp.asarray(_segment_ones(G * 128, NSA_HD), dtype=BF16)
    xrow = lambda b, i: (b * nt + jnp.maximum(i - 1, 0), 0)
    bvec = pl.BlockSpec((None, 1, D), lambda b, i: (b, 0, 0))
    full = lambda a: pl.BlockSpec(a.shape, lambda b, i: (0,) * a.ndim)
    return pl.pallas_call(
        _kv_proj_body,
        grid=(batch, nt + 1),
        in_specs=[pl.BlockSpec((tm, D), xrow),
                  pl.BlockSpec((1, D), lambda b, i: (0, 0)),
                  bvec, bvec, full(wk), full(wvt), full(wcc), full(gk), full(bd)],
        out_specs=[pl.BlockSpec((None, tm, G * 128), lambda b, i: (b, i, 0)),
                   pl.BlockSpec((None, tm, G * 128), lambda b, i: (b, i, 0)),
                   pl.BlockSpec((None, G * VT_ROWS, tm), lambda b, i: (b, 0, i)),
                   pl.BlockSpec((None, G * VT_ROWS, tm), lambda b, i: (b, 0, i)),
                   pl.BlockSpec((ncs, tm, 128),
                                lambda b, i: (0, b * nt + jnp.maximum(i - 1, 0), 0))],
        out_shape=[jax.ShapeDtypeStruct((batch, sp, G * 128), BF16),
                   jax.ShapeDtypeStruct((batch, sp, G * 128), BF16),
                   jax.ShapeDtypeStruct((batch, G * VT_ROWS, sp), BF16),
                   jax.ShapeDtypeStruct((batch, G * VT_ROWS, sp), BF16),
                   jax.ShapeDtypeStruct((ncs, M, 128), F32)],
        compiler_params=_cparams(("parallel", "arbitrary")),
        name="kv_proj",
    )(x2d, g.reshape(1, D), sh, sc, wk, wvt, wcc, gk, bd)


def _gelu_tanh(x):
    c = math.sqrt(2.0 / math.pi)
    return 0.5 * x * (1.0 + jnp.tanh(c * (x + 0.044715 * (x * x * x))))


def _compress_body(cc0_ref, cc1_ref, cc2_ref, cc3_ref, pos_ref, w1_ref, w2_ref, w2t_ref, g_ref,
                   kc_ref, vct_ref):
    HD = NSA_HD
    nwin = kc_ref.shape[0]
    hid = w2_ref.shape[0]
    for p, cc_ref in enumerate((cc0_ref, cc1_ref, cc2_ref, cc3_ref)):
        kind = p // 2
        x = jnp.concatenate([cc_ref[pl.ds(i, nwin, stride=CMP_STRIDE), :]
                             for i in range(CMP_STRIDE)], axis=1)
        u = _dot((x + pos_ref[kind, 0:1, :]).astype(BF16), w1_ref[kind, 0])
        v = _dot((x + pos_ref[kind, 1:2, :]).astype(BF16), w1_ref[kind, 1])
        pre = u + pltpu.roll(v, nwin - 1, axis=0)
        for s in range(2):
            g = 2 * (p % 2) + s
            hmid = _gelu_tanh(pre[:, s * hid:(s + 1) * hid]).astype(BF16)
            if kind == 0:
                y = _dot(hmid, w2_ref[...])
                var = jnp.mean(y * y, axis=-1, keepdims=True)
                yn = y * lax.rsqrt(var + RMS_EPS) * g_ref[...]
                kc_ref[:, g * 128:g * 128 + HD] = yn.astype(BF16)
                kc_ref[:, g * 128 + HD:(g + 1) * 128] = jnp.zeros((nwin, 128 - HD), BF16)
            else:
                vct_ref[g * HD:(g + 1) * HD, :] = _dot_nt(w2t_ref[...], hmid).astype(BF16)


def _pair_expand(w):
    k, t, d, c = w.shape
    eye = jnp.eye(2, dtype=w.dtype)
    return jnp.einsum("ktdc,su->ktsduc", w, eye).reshape(k, t * 2 * d, 2 * c)


def compress(cc, pos, w1, w2k, w2vt, g, batch, seq):
    G, HD = NSA_GROUPS, NSA_HD
    nwin = seq // CMP_STRIDE
    hid = w1.shape[2]
    half = CMP_BLOCK // 2
    assert half == CMP_STRIDE
    w1r = w1.reshape(2, CMP_BLOCK, HD, hid)
    w1x = jnp.stack([_pair_expand(w1r[:, :half]), _pair_expand(w1r[:, half:])], axis=1).astype(BF16)
    posr = jnp.tile(pos.reshape(2, 2, half, 1, HD), (1, 1, 1, 2, 1)).reshape(2, 2, half * 2 * HD)
    resident = lambda a: pl.BlockSpec(a.shape, lambda b: (0,) * a.ndim, pipeline_mode=pl.Buffered(1))
    return pl.pallas_call(
        _compress_body,
        grid=(batch,),
        in_specs=[pl.BlockSpec((None, seq, 128), functools.partial(lambda q, b: (q, b, 0), q))
                  for q in range(4)]
                 + [resident(posr), resident(w1x), resident(w2k), resident(w2vt), resident(g)],
        out_specs=[pl.BlockSpec((None, nwin, G * 128), lambda b: (b, 0, 0)),
                   pl.BlockSpec((None, G * HD, nwin), lambda b: (b, 0, 0))],
        out_shape=[jax.ShapeDtypeStruct((batch, nwin, G * 128), BF16),
                   jax.ShapeDtypeStruct((batch, G * HD, nwin), BF16)],
        compiler_params=_cparams(("parallel",)),
        name="compress",
    )(cc, cc, cc, cc, posr, w1x, w2k, w2vt, g)


def _t5_bucket(dist):
    n = jnp.maximum(dist, 0)
    max_exact = REL_BUCKETS // 2
    nf = jnp.maximum(n, 1).astype(F32)
    large = max_exact + (jnp.log(nf / max_exact) / math.log(REL_MAX_DIST / max_exact)
                         * (REL_BUCKETS - max_exact)).astype(jnp.int32)
    large = jnp.minimum(large, REL_BUCKETS - 1)
    return jnp.where(n < max_exact, n, large)


def _table_lookup(bucket, tab_ref, h):
    out = jnp.zeros(bucket.shape, F32)
    for k in range(REL_BUCKETS):
        out = jnp.where(bucket == k, tab_ref[k, h], out)
    return out


def _bias_prep_body(tab_ref, bc_ref, tp_ref, far_ref):
    h = pl.program_id(0)
    S = bc_ref.shape[1]
    far = tab_ref[REL_BUCKETS - 1, h]
    far_ref[...] = jnp.full(far_ref.shape, far * LOG2E, F32)

    dist = lax.broadcasted_iota(jnp.int32, (8, S), 1)
    by_dist = _table_lookup(_t5_bucket(dist), tab_ref, h) * LOG2E
    shifted = pltpu.roll(jnp.broadcast_to(by_dist[0:1, :], (NCMP_PAD, S)), 0, axis=1,
                         stride=CMP_STRIDE, stride_axis=0)
    shifted = pltpu.roll(shifted, CMP_BLOCK - 1, axis=1)
    n = lax.broadcasted_iota(jnp.int32, (NCMP_PAD, S), 0)
    t = lax.broadcasted_iota(jnp.int32, (NCMP_PAD, S), 1)
    bc_ref[...] = jnp.where(t >= n * CMP_STRIDE + CMP_BLOCK - 1, shifted, NEG_INF)
    j = lax.broadcasted_iota(jnp.int32, (ATT_TK, ATT_QB), 0)
    i = lax.broadcasted_iota(jnp.int32, (ATT_TK, ATT_QB), 1)
    for d in range(2):
        dist = d * ATT_TK + i - j
        rel = (_table_lookup(_t5_bucket(dist), tab_ref, h) - far) * LOG2E
        tp_ref[d] = jnp.where(dist >= 0, rel, NEG_INF)
    tp_ref[2] = jnp.where(i < j, 0.0, NEG_INF)


def bias_prep(rel_table, seq):
    assert ATT_TK == ATT_QB and ATT_TK + 1 > 113
    return pl.pallas_call(
        _bias_prep_body,
        grid=(NSA_HEADS,),
        in_specs=[pl.BlockSpec(memory_space=pltpu.SMEM)],
        out_specs=[pl.BlockSpec((None, NCMP_PAD, seq), lambda h: (h, 0, 0)),
                   pl.BlockSpec((None, 3, ATT_TK, ATT_QB), lambda h: (h, 0, 0, 0)),
                   pl.BlockSpec((None, 8, 128), lambda h: (h, 0, 0))],
        out_shape=[jax.ShapeDtypeStruct((NSA_HEADS, NCMP_PAD, seq), F32),
                   jax.ShapeDtypeStruct((NSA_HEADS, 3, ATT_TK, ATT_QB), F32),
                   jax.ShapeDtypeStruct((NSA_HEADS, 8, 128), F32)],
        compiler_params=_cparams(("parallel",)),
        name="bias_prep",
    )(rel_table)


def _q_proj_body(x_ref, g_ref, sh_ref, sc_ref, wqt_ref, wgt_ref, bg_ref, gq_ref, qt_ref, gt_ref):
    HD = NSA_HD
    tm = x_ref.shape[0]
    half = tm // 2
    scale = gq_ref[...] * (HD ** -0.5 * LOG2E)
    for r0 in (0, half):
        h = _norm_mod(x_ref[r0:r0 + half, :], g_ref[...], sh_ref[...], sc_ref[...]).astype(BF16)
        gt_ref[:, r0:r0 + half] = _dot_nt(wgt_ref[...], h) + bg_ref[...]
        qt = _dot_nt(wqt_ref[...], h)
        for hh in range(NSA_HEADS):
            seg = qt[hh * HD:(hh + 1) * HD, :]
            var = jnp.mean(seg * seg, axis=0, keepdims=True)
            qt_ref[hh * HD:(hh + 1) * HD, r0:r0 + half] = (
                seg * lax.rsqrt(var + RMS_EPS) * scale).astype(BF16)


def q_proj(x2d, g, sh, sc, wqt, wgt, bg, gq, batch, seq, tm=1024):
    M, D = x2d.shape
    nt = seq // tm
    nq = wqt.shape[0]
    ng = wgt.shape[0]
    bvec = pl.BlockSpec((None, 1, D), lambda i: (i // nt, 0, 0))
    full = lambda a: pl.BlockSpec(a.shape, lambda i: (0,) * a.ndim)
    return pl.pallas_call(
        _q_proj_body,
        grid=(M // tm,),
        in_specs=[pl.BlockSpec((tm, D), lambda i: (i, 0)),
                  pl.BlockSpec((1, D), lambda i: (0, 0)),
                  bvec, bvec, full(wqt), full(wgt), full(bg), full(gq)],
        out_specs=[pl.BlockSpec((None, nq, tm), lambda i: (i // nt, 0, i % nt)),
                   pl.BlockSpec((None, ng, tm), lambda i: (i // nt, 0, i % nt))],
        out_shape=[jax.ShapeDtypeStruct((batch, nq, seq), BF16),
                   jax.ShapeDtypeStruct((batch, ng, seq), F32)],
        compiler_params=_cparams(("parallel",)),
        name="q_proj",
    )(x2d, g.reshape(1, D), sh, sc, wqt, wgt, bg, gq)


def _heads_on_lanes(pieces):
    return jnp.concatenate(pieces, axis=1)


def _nsa_body(gps, qt_ref, gt_ref, kc_ref, vct_ref, ks_ref, kw_ref, vst_ref, vwt_ref,
              bct_ref, tp_ref, far_ref, ovt_ref, out_ref):
    qb = pl.program_id(2)
    QB, HD, HPG = ATT_QB, NSA_HD, NSA_HPG
    R = HPG * QB
    t0 = pl.multiple_of(qb * QB, QB)
    near0 = pl.multiple_of(t0 + WINDOW - ATT_TK, ATT_TK)
    wlen = WINDOW + QB
    nblk = ovt_ref.shape[0]
    jb = lax.broadcasted_iota(jnp.int32, (nblk, QB), 0)
    jbf = jb.astype(F32)
    qid = jnp.right_shift(t0 + lax.broadcasted_iota(jnp.int32, (nblk, QB), 1), 6)
    forced = (jb == 0) | (jb == qid) | (jb == qid - 1)
    srow = lax.broadcasted_iota(jnp.int32, (8, R), 0)

    def finish(acc):
        return acc[0:HD, :] / acc[HD:HD + 1, :]

    def tile_part(s_tile, v_tile):
        m_t = jnp.max(s_tile, axis=0, keepdims=True)
        m_safe = jnp.where(m_t == NEG_INF, 0.0, m_t)
        p = jnp.exp2(s_tile - m_safe).astype(BF16)
        return m_t, _dot(v_tile, p)[0:HD + 8, :]

    def combine(parts):
        m_fin = parts[0][0]
        for m_t, _ in parts[1:]:
            m_fin = jnp.maximum(m_fin, m_t)
        acc = None
        for m_t, pv in parts:
            term = jnp.exp2(m_t - m_fin) * pv
            acc = term if acc is None else acc + term
        return m_fin, acc

    groups = range(gps)
    hs = [[gl * HPG + h for h in range(HPG)] for gl in groups]
    kcol = [slice(gl * 128, (gl + 1) * 128) for gl in groups]
    vrow = [slice(gl * VT_ROWS, (gl + 1) * VT_ROWS) for gl in groups]

    def stationary(q_all, hi_lo, block_rows):
        pad = jnp.where(srow == 0, MASK_BIG, 0.0)
        rest = jnp.zeros((128 - HD - KEY_PAD - 8, R), F32)
        tail = jnp.concatenate([hi_lo, block_rows, pad, rest], axis=0).astype(BF16)
        return jnp.concatenate([q_all, tail], axis=0)

    q_alls, hi_los, qms = [], [], []
    for gl in groups:
        q_alls.append(_heads_on_lanes([qt_ref[h * HD:(h + 1) * HD, :] for h in hs[gl]]))
        far = _heads_on_lanes([far_ref[h, 0:1, :] for h in hs[gl]])
        hi = far.astype(BF16).astype(F32)
        hi_los.append(jnp.where(srow == 0, hi, jnp.where(srow == 1, far - hi, 0.0)))
        qms.append(stationary(q_alls[gl], hi_los[gl], jnp.zeros((nblk, R), F32)))

    s_cs = [_dot(kc_ref[:, kcol[gl]], qms[gl]) for gl in groups]
    s_ws = [_dot(kw_ref[pl.ds(t0, wlen), kcol[gl]], qms[gl]) for gl in groups]
    s_ns = [_dot(ks_ref[pl.ds(near0, 2 * ATT_TK), kcol[gl]], qms[gl]) for gl in groups]

    o_cmps, scores = [], []
    for gl in groups:
        s_c = s_cs[gl] + _heads_on_lanes([bct_ref[h] for h in hs[gl]])
        m_c = jnp.max(s_c, axis=0, keepdims=True)
        m_c = jnp.where(m_c == NEG_INF, 0.0, m_c)
        e_c = jnp.exp2(s_c - m_c)
        p_c = e_c / jnp.maximum(jnp.sum(e_c, axis=0, keepdims=True), jnp.finfo(F32).tiny)
        o_cmps.append(_dot(vct_ref[gl * HD:(gl + 1) * HD, :], p_c.astype(BF16)))
        p_sum = p_c[:, 0:QB]
        for h in range(1, HPG):
            p_sum = p_sum + p_c[:, h * QB:(h + 1) * QB]
        p_hi = p_sum.astype(BF16)
        p_lo = (p_sum - p_hi.astype(F32)).astype(BF16)
        imp = _dot(ovt_ref[...], jnp.concatenate([p_hi, p_lo], axis=0))
        score = jnp.where(forced, FORCE_SCORE, imp)
        scores.append(jnp.where(jb <= qid, score, NEG_INF))

    nwt = wlen // ATT_TK
    o_wins = []
    for gl in groups:
        parts = []
        for i in range(nwt):
            d = nwt - 1 - i
            s_t = s_ws[gl][i * ATT_TK:(i + 1) * ATT_TK]
            if d in (0, 1):
                s_t = s_t + _heads_on_lanes([tp_ref[h, d] for h in hs[gl]])
            elif d == nwt - 1:
                s_t = s_t + _heads_on_lanes([tp_ref[h, 2] for h in hs[gl]])
            parts.append(tile_part(s_t, vwt_ref[vrow[gl], pl.ds(t0 + i * ATT_TK, ATT_TK)]))
        o_wins.append(finish(combine(parts)[1]))

    msels = [jnp.full((nblk, QB), NEG_INF, F32) for _ in groups]
    for _ in range(SLC_TOPK):
        for gl in groups:
            mx = jnp.max(scores[gl], axis=0, keepdims=True)
            first = jnp.min(jnp.where(scores[gl] == mx, jbf, float(nblk)), axis=0, keepdims=True)
            pick = jbf == first
            msels[gl] = jnp.where(pick & (mx > NEG_INF), 0.0, msels[gl])
            scores[gl] = jnp.where(pick, NEG_INF, scores[gl])
    carry0 = []
    for gl in groups:
        prev_mask = jnp.concatenate(
            [jnp.broadcast_to(jnp.max(jnp.where(jb == 2 * qb - 2 + r, msels[gl], NEG_INF), axis=0,
                                      keepdims=True), (SLC_BLOCK, QB)) for r in range(2)], axis=0)
        s_prev = s_ns[gl][0:ATT_TK] + _heads_on_lanes([tp_ref[h, 1] + prev_mask for h in hs[gl]])
        s_diag = s_ns[gl][ATT_TK:] + _heads_on_lanes([tp_ref[h, 0] for h in hs[gl]])
        carry0.append(combine([
            tile_part(s_diag, vst_ref[vrow[gl], pl.ds(near0 + ATT_TK, ATT_TK)]),
            tile_part(s_prev, vst_ref[vrow[gl], pl.ds(near0, ATT_TK)])]))

    qss = [stationary(q_alls[gl], hi_los[gl], _heads_on_lanes(
        [jnp.where(msels[gl] == 0.0, 0.0, MASK_BIG)] * HPG)) for gl in groups]

    def far_step(c, carry):
        tile0 = qb + (WINDOW - ATT_TK) // ATT_TK - (c + 1) * (FAR_CHUNK // ATT_TK)
        row0 = pl.multiple_of(tile0 * ATT_TK, ATT_TK)
        s_fs = [_dot(ks_ref[pl.ds(row0, FAR_CHUNK), kcol[gl]], qss[gl]) for gl in groups]
        out = []
        for gl in groups:
            parts = [carry[gl]]
            for j in range(FAR_CHUNK // ATT_TK):
                parts.append(tile_part(s_fs[gl][j * ATT_TK:(j + 1) * ATT_TK],
                                       vst_ref[vrow[gl], pl.ds(row0 + j * ATT_TK, ATT_TK)]))
            out.append(combine(parts))
        return tuple(out)

    n_far = (qb + 2) // 4
    sel = lax.fori_loop(0, n_far, far_step, tuple(carry0))

    gates = jax.nn.sigmoid(gt_ref[...])
    for gl in groups:
        o_cmp, o_win = o_cmps[gl], o_wins[gl]
        o_sel = finish(sel[gl][1])
        outs = []
        for h in range(HPG):
            lanes = slice(h * QB, (h + 1) * QB)
            r = gl * 16 + 3 * h
            outs.append(gates[r:r + 1, :] * o_cmp[:, lanes]
                        + gates[r + 1:r + 2, :] * o_sel[:, lanes]
                        + gates[r + 2:r + 3, :] * o_win[:, lanes])
        for pair in range(HPG // 2):
            two = jnp.concatenate(outs[2 * pair:2 * pair + 2], axis=0)
            c0 = gl * HPG * HD + pair * 2 * HD
            out_ref[:, c0:c0 + 2 * HD] = two.T.astype(out_ref.dtype)


def _overlap_matrix_t(nblk):
    start = np.arange(NCMP_PAD) * CMP_STRIDE
    sj = np.arange(nblk) * SLC_BLOCK
    ov = (np.minimum(start[None, :] + CMP_BLOCK, sj[:, None] + SLC_BLOCK)
          - np.maximum(start[None, :], sj[:, None]))
    ov = np.clip(ov, 0, None) / CMP_BLOCK
    ov[:, NCMP_PAD - 1] = 0.0
    return np.concatenate([ov, ov], axis=1).astype(np.float32)


def nsa_attn(qt, gt, kc, vct, ks, kw, vst, vwt, bias_ct, tiles, far, batch, seq, gps=ATT_GPS):
    QB, HD, HPG, G = ATT_QB, NSA_HD, NSA_HPG, NSA_GROUPS
    nblk = seq // SLC_BLOCK
    assert nblk % 8 == 0 and seq % QB == 0 and seq // CMP_STRIDE == NCMP_PAD and G % gps == 0
    assert WINDOW % ATT_TK == 0 and FAR_CHUNK == 4 * ATT_TK and KV_PAD >= FAR_CHUNK - ATT_TK
    assert KEY_BLK % 8 == 0 and KEY_BLK + nblk <= KEY_PAD and KEY_PAD + 8 <= 128 - HD
    nq = seq // QB
    sp = seq + KV_PAD
    ovt = jnp.asarray(_overlap_matrix_t(nblk), dtype=BF16)
    return pl.pallas_call(
        functools.partial(_nsa_body, gps),
        grid=(batch, G // gps, nq),
        in_specs=[pl.BlockSpec((None, gps * HPG * HD, QB), lambda b, g, i: (b, g, i)),
                  pl.BlockSpec((None, gps * 16, QB), lambda b, g, i: (b, g, i)),
                  pl.BlockSpec((None, NCMP_PAD, gps * 128), lambda b, g, i: (b, 0, g)),
                  pl.BlockSpec((None, gps * HD, NCMP_PAD), lambda b, g, i: (b, g, 0)),
                  pl.BlockSpec((None, sp, gps * 128), lambda b, g, i: (b, 0, g)),
                  pl.BlockSpec((None, sp, gps * 128), lambda b, g, i: (b, 0, g)),
                  pl.BlockSpec((None, gps * VT_ROWS, sp), lambda b, g, i: (b, g, 0)),
                  pl.BlockSpec((None, gps * VT_ROWS, sp), lambda b, g, i: (b, g, 0)),
                  pl.BlockSpec((gps * HPG, NCMP_PAD, QB), lambda b, g, i: (g, 0, i)),
                  pl.BlockSpec((gps * HPG, 3, ATT_TK, QB), lambda b, g, i: (g, 0, 0, 0)),
                  pl.BlockSpec((gps * HPG, 8, 128), lambda b, g, i: (g, 0, 0)),
                  pl.BlockSpec((nblk, 2 * NCMP_PAD), lambda b, g, i: (0, 0))],
        out_specs=pl.BlockSpec((QB, gps * HPG * HD), lambda b, g, i: (b * nq + i, g)),
        out_shape=jax.ShapeDtypeStruct((batch * seq, NSA_HEADS * HD), BF16),
        compiler_params=_cparams(("parallel", "parallel", "arbitrary")),
        name="nsa_attn",
    )(qt, gt, kc, vct, ks, kw, vst, vwt, bias_ct, tiles, far, ovt)


def _mlstm_layer(x2d, g_mix, sh, sc, w_stack, layer, b_if, g_out, batch, seq):
    proj, gates = mlstm_proj(x2d, g_mix, sh, sc, w_stack, layer, b_if)
    return mlstm_core(proj, gates, g_out, batch, seq)


def _nsa_shared(x2d, g_kv, kv_sh, kv_sc, w_kv, pos_k, w_k1, w_k2, pos_v, w_v1, w_v2,
                g_knorm, batch, seq):
    gw = NSA_GROUPS * NSA_HD
    part = lambda i: w_kv[:, i * gw:(i + 1) * gw]
    hd = NSA_HD
    wk = jnp.concatenate([p[:, g * hd:(g + 1) * hd] for g in range(NSA_GROUPS)
                          for p in (part(2), part(4))], axis=1).astype(BF16)
    gk = jnp.tile(jnp.concatenate([g_knorm[1], g_knorm[2]]), NSA_GROUPS).reshape(1, -1)
    wvt = jnp.concatenate([part(3), part(5)], axis=1).T.astype(BF16)
    wcc = jnp.concatenate([part(0), part(1)], axis=1).astype(BF16)
    ks, kw, vst, vwt, cc = kv_proj(x2d, g_kv, kv_sh, kv_sc, wk, wvt, wcc, gk, batch, seq)
    pos = jnp.stack([pos_k, pos_v])
    w1 = jnp.stack([w_k1, w_v1])
    kc, vct = compress(cc, pos, w1, w_k2.astype(BF16), w_v2.T.astype(BF16), g_knorm[0:1],
                       batch, seq)
    return kc, vct, ks, kw, vst, vwt


def _gate_weights_t(w_q, b_gate):
    nq = NSA_HEADS * NSA_HD
    per = 3 * NSA_HPG
    wg = w_q[:, nq:].T.reshape(NSA_GROUPS, per, -1)
    wg = jnp.pad(wg, ((0, 0), (0, 16 - per), (0, 0))).reshape(NSA_GROUPS * 16, -1)
    bg = jnp.pad(b_gate.reshape(NSA_GROUPS, per), ((0, 0), (0, 16 - per))).reshape(-1, 1)
    return wg.astype(BF16), bg


def _nsa_layer(x2d, g_mix, sh, sc, shared, w_q, b_gate, g_qnorm, bias_ct, tiles, far, batch, seq):
    nq = NSA_HEADS * NSA_HD
    wgt, bg = _gate_weights_t(w_q, b_gate)
    qt, gt = q_proj(x2d, g_mix, sh, sc, w_q[:, :nq].T.astype(BF16), wgt, bg,
                    g_qnorm.reshape(NSA_HD, 1), batch, seq)
    kc, vct, ks, kw, vst, vwt = shared
    return nsa_attn(qt, gt, kc, vct, ks, kw, vst, vwt, bias_ct, tiles, far, batch, seq)


def kernel(x, c, w_ada, b_ada, g_norm_mix, g_norm_ffn, w_ffn_in, w_ffn_out, w_a_in, b_a_if, g_a_out, w_a_out, w_kv_ada, b_kv_ada, g_kv_norm, w_kv, pos_cmp_k, w_cmp_k1, w_cmp_k2, pos_cmp_v, w_cmp_v1, w_cmp_v2, g_knorm, w_b_q, b_b_gate, g_qnorm, w_b_out, rel_table):
    B, S, D = x.shape
    depth = w_ada.shape[0]
    n_a = w_a_in.shape[0]
    x2d = x.reshape(B * S, D)
    mods = ada_mod(c, w_ada, b_ada)
    kv_mod = ada_mod(c, w_kv_ada[None], b_kv_ada[None])[0]
    shared = None
    bias_ct = tiles = far = None
    for layer in range(depth):
        sh1, sc1, ga1, sh2, sc2, ga2 = [mods[layer, :, i * D:(i + 1) * D].reshape(B, 1, D)
                                        for i in range(6)]
        if layer < n_a:
            mixed = _mlstm_layer(x2d, g_norm_mix[layer], sh1, sc1, w_a_in, layer, b_a_if[layer],
                                 g_a_out[layer], B, S)
            w_mix = (w_a_out, layer)
        else:
            j = layer - n_a
            if shared is None:
                kv_sh = kv_mod[:, :D].reshape(B, 1, D)
                kv_sc = kv_mod[:, D:].reshape(B, 1, D)
                shared = _nsa_shared(x2d, g_kv_norm, kv_sh, kv_sc, w_kv, pos_cmp_k, w_cmp_k1,
                                     w_cmp_k2, pos_cmp_v, w_cmp_v1, w_cmp_v2, g_knorm, B, S)
                bias_ct, tiles, far = bias_prep(rel_table, S)
            mixed = _nsa_layer(x2d, g_norm_mix[layer], sh1, sc1, shared, w_b_q[j], b_b_gate[j],
                               g_qnorm[j], bias_ct, tiles, far, B, S)
            w_mix = (w_b_out, j)
        x2d = mix_ffn(mixed, w_mix, x2d, ga1, g_norm_ffn[layer], sh2, sc2, ga2,
                      (w_ffn_in, layer), (w_ffn_out, layer))
    return x2d.reshape(B, S, D)
```

```python
import functools
import math

import jax
import jax.numpy as jnp
import numpy as np
from jax import lax
from jax.experimental import pallas as pl
from jax.experimental.pallas import tpu as pltpu

F32 = jnp.float32
BF16 = jnp.bfloat16
NEG_INF = float("-inf")
LOG2E = math.log2(math.e)

RMS_EPS = 1e-6

ML_HEADS = 4
ML_DQK = 128
ML_DV = 256
ML_LC = 256

NSA_HEADS = 16
NSA_GROUPS = 4
NSA_HPG = 4
NSA_HD = 64
CMP_BLOCK = 32
CMP_STRIDE = 16
SLC_BLOCK = 64
SLC_TOPK = 8
WINDOW = 512
FORCE_SCORE = 1e4
REL_BUCKETS = 32
REL_MAX_DIST = 128
ATT_QB = 128
ATT_TK = 128
ATT_GPS = 4
NCMP_PAD = 128
KV_PAD = WINDOW
FAR_CHUNK = 512
VT_ROWS = 80
KEY_BLK = 8
KEY_PAD = 40
MASK_BIG = -32768.0

VMEM_LIMIT = 56 * 1024 * 1024


def _cparams(sem):
    return pltpu.CompilerParams(dimension_semantics=sem, vmem_limit_bytes=VMEM_LIMIT)


def _dot(a, b):
    return jnp.dot(a, b, preferred_element_type=F32)


def _dot_nt(a, b):
    return lax.dot_general(a, b, (((1,), (1,)), ((), ())), preferred_element_type=F32)


def _dot_tn(a, b):
    return lax.dot_general(a, b, (((0,), (0,)), ((), ())), preferred_element_type=F32)


def _norm_mod(x, g, sh, sc):
    var = jnp.mean(x * x, axis=-1, keepdims=True)
    y = x * lax.rsqrt(var + RMS_EPS) * g
    return y * (1.0 + sc) + sh


def _ada_body(c_ref, w_ref, b_ref, o_ref):
    c = c_ref[...]
    ca = c * jax.nn.sigmoid(c)
    nb = ca.shape[0]
    a_hi = ca.astype(BF16).astype(F32)
    a_mid = (ca - a_hi).astype(BF16).astype(F32)
    a_lo = ca - a_hi - a_mid
    a3 = jnp.concatenate([a_hi, a_mid, a_lo, jnp.zeros_like(ca)], axis=0).astype(BF16)
    w = w_ref[...]
    w_hi = w.astype(BF16)
    w_lo = (w - w_hi.astype(F32)).astype(BF16)
    p = _dot(a3, w_hi)
    q = _dot(a3[0:2 * nb], w_lo)
    o_ref[...] = (p[0:nb] + p[nb:2 * nb] + p[2 * nb:3 * nb] + q[0:nb] + q[nb:2 * nb]) + b_ref[...]


def ada_mod(c, w, b, tn=2048):
    L, D, N = w.shape
    B = c.shape[0]
    return pl.pallas_call(
        _ada_body,
        grid=(L, N // tn),
        in_specs=[pl.BlockSpec((B, D), lambda l, j: (0, 0)),
                  pl.BlockSpec((None, D, tn), lambda l, j: (l, 0, j)),
                  pl.BlockSpec((None, 1, tn), lambda l, j: (l, 0, j))],
        out_specs=pl.BlockSpec((None, B, tn), lambda l, j: (l, 0, j)),
        out_shape=jax.ShapeDtypeStruct((L, B, N), F32),
        compiler_params=_cparams(("parallel", "parallel")),
        name="ada_mod",
    )(c, w, b.reshape(L, 1, N))


def _mlstm_proj_body(ng, tn, x_ref, g_ref, sh_ref, sc_ref, w_ref, wg_ref, b_ref, p_ref, gt_ref):
    tm = x_ref.shape[0]
    half = tm // 2
    for r0 in (0, half):
        rows = slice(r0, r0 + half)
        h = _norm_mod(x_ref[rows, :], g_ref[...], sh_ref[...], sc_ref[...]).astype(BF16)
        for n0 in range(0, w_ref.shape[1], tn):
            p_ref[rows, n0:n0 + tn] = _dot(h, w_ref[:, n0:n0 + tn].astype(BF16)).astype(p_ref.dtype)
        gates = _dot(h, wg_ref[:, 0:ng].astype(BF16)) + b_ref[...]
        gt_ref[rows, 0:ng] = gates
        gt_ref[rows, ng:] = jnp.zeros((half, gt_ref.shape[1] - ng), F32)


def mlstm_proj(x2d, g, sh, sc, w_stack, layer, b_if, tm=1024, tn=512):
    M, D = x2d.shape
    B = sh.shape[0]
    ng = b_if.shape[0]
    nbig = w_stack.shape[2] - ng
    assert nbig % 128 == 0 and ng <= 128
    tiles_per_batch = (M // B) // tm
    bvec = pl.BlockSpec((None, 1, D), lambda i: (i // tiles_per_batch, 0, 0))
    return pl.pallas_call(
        functools.partial(_mlstm_proj_body, ng, tn),
        grid=(M // tm,),
        in_specs=[pl.BlockSpec((tm, D), lambda i: (i, 0)),
                  pl.BlockSpec((1, D), lambda i: (0, 0)), bvec, bvec,
                  pl.BlockSpec((None, D, nbig), lambda i: (layer, 0, 0),
                               pipeline_mode=pl.Buffered(1)),
                  pl.BlockSpec((None, D, 128), lambda i: (layer, 0, nbig // 128),
                               pipeline_mode=pl.Buffered(1)),
                  pl.BlockSpec((1, ng), lambda i: (0, 0))],
        out_specs=[pl.BlockSpec((tm, nbig), lambda i: (i, 0)),
                   pl.BlockSpec((tm, 128), lambda i: (i, 0))],
        out_shape=[jax.ShapeDtypeStruct((M, nbig), BF16), jax.ShapeDtypeStruct((M, 128), F32)],
        compiler_params=_cparams(("parallel",)),
        name="mlstm_proj",
    )(x2d, g.reshape(1, D), sh, sc, w_stack, w_stack, b_if.reshape(1, ng))


def _mix_ffn_body(tf, a_ref, wm_ref, x_ref, ga1_ref, g_ref, sh_ref, sc_ref, ga2_ref,
                  wi_ref, wo_ref, o_ref, act_s):
    F = wo_ref.shape[0]
    x1 = x_ref[...] + ga1_ref[...] * _dot(a_ref[...], wm_ref[...].astype(BF16))
    h = _norm_mod(x1, g_ref[...], sh_ref[...], sc_ref[...]).astype(BF16)
    for f0 in range(0, F, tf):
        gate = _dot(h, wi_ref[:, f0:f0 + tf].astype(BF16))
        up = _dot(h, wi_ref[:, F + f0:F + f0 + tf].astype(BF16))
        act_s[:, f0:f0 + tf] = (gate * jax.nn.sigmoid(gate) * up).astype(BF16)
    y = _dot(act_s[:, 0:tf], wo_ref[0:tf, :].astype(BF16))
    for f0 in range(tf, F, tf):
        y = y + _dot(act_s[:, f0:f0 + tf], wo_ref[f0:f0 + tf, :].astype(BF16))
    o_ref[...] = x1 + ga2_ref[...] * y


def mix_ffn(a, w_mix, x2d, ga1, g, sh, sc, ga2, w_in, w_out, tm=512, tf=256):
    M, D = x2d.shape
    K = a.shape[1]
    F = w_out[0].shape[1]
    B = sh.shape[0]
    tiles_per_batch = (M // B) // tm
    bvec = pl.BlockSpec((None, 1, D), lambda i: (i // tiles_per_batch, 0, 0))

    def resident(stacked_layer):
        w, layer = stacked_layer
        return pl.BlockSpec((None,) + w.shape[1:], lambda i: (layer, 0, 0),
                            pipeline_mode=pl.Buffered(1))

    return pl.pallas_call(
        functools.partial(_mix_ffn_body, tf),
        grid=(M // tm,),
        in_specs=[pl.BlockSpec((tm, K), lambda i: (i, 0)),
                  resident(w_mix),
                  pl.BlockSpec((tm, D), lambda i: (i, 0)),
                  bvec,
                  pl.BlockSpec((1, D), lambda i: (0, 0)),
                  bvec, bvec, bvec,
                  resident(w_in), resident(w_out)],
        out_specs=pl.BlockSpec((tm, D), lambda i: (i, 0)),
        out_shape=jax.ShapeDtypeStruct((M, D), F32),
        scratch_shapes=[pltpu.VMEM((tm, F), BF16)],
        compiler_params=_cparams(("parallel",)),
        name="mix_ffn",
    )(a, w_mix[0], x2d, ga1, g.reshape(1, D), sh, sc, ga2, w_in[0], w_out[0])


def _sublane_scan(x, op, fill):
    n = x.shape[0]
    row = lax.broadcasted_iota(jnp.int32, x.shape, 0)
    sh = 1
    while sh < n:
        x = op(x, jnp.where(row >= sh, pltpu.roll(x, sh, axis=0), fill))
        sh *= 2
    return x


def _mlstm_body(q_ref, k_ref, v_ref, o_ref, gc_ref, gout_ref, out_ref, c_s, n_s, m_s):
    c_idx = pl.program_id(1)
    LC = q_ref.shape[0]
    NH = ML_HEADS
    scale = ML_DQK ** -0.5
    log_scale = math.log(scale)

    @pl.when(c_idx == 0)
    def _():
        c_s[...] = jnp.zeros_like(c_s)
        n_s[...] = jnp.zeros_like(n_s)
        m_s[...] = jnp.zeros_like(m_s)

    gc = gc_ref[...]
    logf = jnp.minimum(gc, 0.0) - jnp.log1p(jnp.exp(-jnp.abs(gc)))
    bcum = pltpu.roll(_sublane_scan(logf, jnp.add, 0.0), 128 - NH, axis=1)
    a_c = gc - bcum
    cmax = _sublane_scan(a_c, jnp.maximum, NEG_INF)
    a_t = a_c.T

    row = lax.broadcasted_iota(jnp.int32, (LC, LC), 0)
    col = lax.broadcasted_iota(jnp.int32, (LC, LC), 1)
    causal = col <= row
    ones_col = jnp.ones((LC, 128), BF16)

    def lanes2(x):
        return jnp.concatenate([x, x], axis=1)

    for h in range(NH):
        a_rep = jnp.broadcast_to(a_c[:, h:h + 1], (LC, 128))
        b_rep = jnp.broadcast_to(bcum[:, h:h + 1], (LC, 128))
        cm_rep = jnp.broadcast_to(cmax[:, h:h + 1], (LC, 128))
        a_r = a_t[h:h + 1, :] + log_scale
        b_last = b_rep[LC - 1:LC, :]
        m_prev = m_s[h:h + 1, :]
        m_new = jnp.maximum(b_last + m_prev, b_last + cm_rep[LC - 1:LC, :])
        decay = jnp.exp(b_last + m_prev - m_new)
        e_rep = jnp.exp(b_last + a_rep - m_new)
        g_rep = jnp.maximum(m_prev, cm_rep)
        w_intra = jnp.exp(jnp.where(causal, a_r - lanes2(g_rep), NEG_INF))
        w_inter = jnp.exp(m_prev - g_rep) * scale
        floor = jnp.exp(-(b_rep + g_rep))

        qh = q_ref[:, h * ML_DQK:(h + 1) * ML_DQK]
        kh = k_ref[:, h * ML_DQK:(h + 1) * ML_DQK]
        vh = v_ref[:, h * ML_DV:(h + 1) * ML_DV]
        c_prev = c_s[h]
        n_prev = n_s[h:h + 1, :]

        s = (_dot_nt(qh, kh) * w_intra).astype(BF16)
        inter = _dot(qh, c_prev.astype(BF16))
        num = _dot(s, vh) + lanes2(w_inter) * inter
        n_rows = jnp.broadcast_to(n_prev, (128, ML_DQK)).astype(BF16)
        qn = _dot(s, ones_col) + w_inter * _dot_nt(qh, n_rows)
        inv = 1.0 / jnp.maximum(jnp.abs(qn), floor)

        ke = kh.astype(F32) * e_rep
        c_s[h] = lanes2(decay) * c_prev + _dot_tn(ke.astype(BF16), vh)
        n_s[h:h + 1, :] = decay * n_prev + jnp.sum(ke, axis=0, keepdims=True)
        m_s[h:h + 1, :] = m_new

        ssq = _dot((num * num).astype(BF16), jnp.ones((ML_DV, 128), BF16))
        rs = lax.rsqrt(ssq * (inv * inv) * (1.0 / ML_DV) + RMS_EPS) * inv
        hn = num * lanes2(rs) * gout_ref[:, h * ML_DV:(h + 1) * ML_DV]
        og = jax.nn.sigmoid(o_ref[:, h * ML_DV:(h + 1) * ML_DV])
        out_ref[:, h * ML_DV:(h + 1) * ML_DV] = hn.astype(BF16) * og


def mlstm_core(proj, gates, g_out, batch, seq):
    LC = ML_LC
    nc = seq // LC
    qk = ML_HEADS * ML_DQK
    vd = ML_HEADS * ML_DV
    row = lambda b, c: b * nc + c
    return pl.pallas_call(
        _mlstm_body,
        grid=(batch, nc),
        in_specs=[pl.BlockSpec((LC, qk), lambda b, c: (row(b, c), 0)),
                  pl.BlockSpec((LC, qk), lambda b, c: (row(b, c), 1)),
                  pl.BlockSpec((LC, vd), lambda b, c: (row(b, c), 1)),
                  pl.BlockSpec((LC, vd), lambda b, c: (row(b, c), 2)),
                  pl.BlockSpec((LC, 128), lambda b, c: (row(b, c), 0)),
                  pl.BlockSpec((1, vd), lambda b, c: (0, 0))],
        out_specs=pl.BlockSpec((LC, vd), lambda b, c: (row(b, c), 0)),
        out_shape=jax.ShapeDtypeStruct((batch * seq, vd), BF16),
        scratch_shapes=[pltpu.VMEM((ML_HEADS, ML_DQK, ML_DV), F32),
                        pltpu.VMEM((8, ML_DQK), F32),
                        pltpu.VMEM((8, 128), F32)],
        compiler_params=_cparams(("parallel", "arbitrary")),
        name="mlstm_core",
    )(proj, proj, proj, proj, gates, g_out.reshape(1, vd))


def _kv_proj_body(x_ref, g_ref, sh_ref, sc_ref, wk_ref, wvt_ref, wcc_ref, gk_ref, bd_ref,
                  ks_ref, kw_ref, vst_ref, vwt_ref, cc_ref):
    i = pl.program_id(1)
    G, HD = NSA_GROUPS, NSA_HD
    tm = x_ref.shape[0]
    lane = lax.broadcasted_iota(jnp.int32, (tm, G * 128), 1) % 128

    @pl.when(i == 0)
    def _():
        pad_rows = jnp.where(lane == HD + KEY_PAD, 1.0, 0.0).astype(BF16)
        ks_ref[...] = pad_rows
        kw_ref[...] = pad_rows
        vst_ref[...] = jnp.zeros_like(vst_ref)
        vwt_ref[...] = jnp.zeros_like(vwt_ref)

    @pl.when(i > 0)
    def _():
        h = _norm_mod(x_ref[...], g_ref[...], sh_ref[...], sc_ref[...]).astype(BF16)
        cc = _dot(h, wcc_ref[...])
        for q in range(cc_ref.shape[0]):
            cc_ref[q] = cc[:, q * 128:(q + 1) * 128]
        kk = _dot(h, wk_ref[...])
        ssq = _dot((kk * kk).astype(BF16), bd_ref[...])
        kn = kk * lax.rsqrt(ssq * (1.0 / HD) + RMS_EPS) * gk_ref[...]
        blk = (i - 1) * (tm // SLC_BLOCK) + lax.broadcasted_iota(
            jnp.int32, (tm, G * 128), 0) // SLC_BLOCK
        tail = jnp.where((lane == HD) | (lane == HD + 1) | (lane == HD + KEY_BLK + blk), 1.0, 0.0)
        ks_ref[...] = jnp.where(lane < HD, kn, tail).astype(BF16)
        kn_sw = jnp.concatenate(
            [pltpu.roll(kn[:, g * 128:(g + 1) * 128], HD, axis=1) for g in range(G)], axis=1)
        kw_ref[...] = jnp.where(lane < HD, kn_sw, tail).astype(BF16)
        vt = _dot_nt(wvt_ref[...], h)
        srow = lax.broadcasted_iota(jnp.int32, (VT_ROWS - HD, tm), 0)
        ones_blk = jnp.where(srow == 0, 1.0, 0.0).astype(BF16)
        for kind, o_ref in enumerate((vst_ref, vwt_ref)):
            for g in range(G):
                r0 = (kind * G + g) * HD
                o_ref[g * VT_ROWS:g * VT_ROWS + HD, :] = vt[r0:r0 + HD, :].astype(BF16)
                o_ref[g * VT_ROWS + HD:(g + 1) * VT_ROWS, :] = ones_blk


def _segment_ones(width, seg):
    idx = np.arange(width) // seg
    return (idx[:, None] == idx[None, :]).astype(np.float32)


def kv_proj(x2d, g, sh, sc, wk, wvt, wcc, gk, batch, seq, tm=KV_PAD):
    assert tm == KV_PAD and seq % tm == 0
    M, D = x2d.shape
    G = NSA_GROUPS
    nt = seq // tm
    sp = seq + KV_PAD
    ncs = wcc.shape[1] // 128
    bd = jnp.asarray(_segment_ones(G * 128, NSA_HD), dtype=BF16)
    xrow = lambda b, i: (b * nt + jnp.maximum(i - 1, 0), 0)
    bvec = pl.BlockSpec((None, 1, D), lambda b, i: (b, 0, 0))
    full = lambda a: pl.BlockSpec(a.shape, lambda b, i: (0,) * a.ndim)
    return pl.pallas_call(
        _kv_proj_body,
        grid=(batch, nt + 1),
        in_specs=[pl.BlockSpec((tm, D), xrow),
                  pl.BlockSpec((1, D), lambda b, i: (0, 0)),
                  bvec, bvec, full(wk), full(wvt), full(wcc), full(gk), full(bd)],
        out_specs=[pl.BlockSpec((None, tm, G * 128), lambda b, i: (b, i, 0)),
                   pl.BlockSpec((None, tm, G * 128), lambda b, i: (b, i, 0)),
                   pl.BlockSpec((None, G * VT_ROWS, tm), lambda b, i: (b, 0, i)),
                   pl.BlockSpec((None, G * VT_ROWS, tm), lambda b, i: (b, 0, i)),
                   pl.BlockSpec((ncs, tm, 128),
                                lambda b, i: (0, b * nt + jnp.maximum(i - 1, 0), 0))],
        out_shape=[jax.ShapeDtypeStruct((batch, sp, G * 128), BF16),
                   jax.ShapeDtypeStruct((batch, sp, G * 128), BF16),
                   jax.ShapeDtypeStruct((batch, G * VT_ROWS, sp), BF16),
                   jax.ShapeDtypeStruct((batch, G * VT_ROWS, sp), BF16),
                   jax.ShapeDtypeStruct((ncs, M, 128), F32)],
        compiler_params=_cparams(("parallel", "arbitrary")),
        name="kv_proj",
    )(x2d, g.reshape(1, D), sh, sc, wk, wvt, wcc, gk, bd)


def _gelu_tanh(x):
    c = math.sqrt(2.0 / math.pi)
    return 0.5 * x * (1.0 + jnp.tanh(c * (x + 0.044715 * (x * x * x))))


def _compress_body(cc0_ref, cc1_ref, cc2_ref, cc3_ref, pos_ref, w1_ref, w2_ref, w2t_ref, g_ref,
                   kc_ref, vct_ref):
    HD = NSA_HD
    nwin = kc_ref.shape[0]
    hid = w2_ref.shape[0]
    for p, cc_ref in enumerate((cc0_ref, cc1_ref, cc2_ref, cc3_ref)):
        kind = p // 2
        x = jnp.concatenate([cc_ref[pl.ds(i, nwin, stride=CMP_STRIDE), :]
                             for i in range(CMP_STRIDE)], axis=1)
        u = _dot((x + pos_ref[kind, 0:1, :]).astype(BF16), w1_ref[kind, 0])
        v = _dot((x + pos_ref[kind, 1:2, :]).astype(BF16), w1_ref[kind, 1])
        pre = u + pltpu.roll(v, nwin - 1, axis=0)
        for s in range(2):
            g = 2 * (p % 2) + s
            hmid = _gelu_tanh(pre[:, s * hid:(s + 1) * hid]).astype(BF16)
            if kind == 0:
                y = _dot(hmid, w2_ref[...])
                var = jnp.mean(y * y, axis=-1, keepdims=True)
                yn = y * lax.rsqrt(var + RMS_EPS) * g_ref[...]
                kc_ref[:, g * 128:g * 128 + HD] = yn.astype(BF16)
                kc_ref[:, g * 128 + HD:(g + 1) * 128] = jnp.zeros((nwin, 128 - HD), BF16)
            else:
                vct_ref[g * HD:(g + 1) * HD, :] = _dot_nt(w2t_ref[...], hmid).astype(BF16)


def _pair_expand(w):
    k, t, d, c = w.shape
    eye = jnp.eye(2, dtype=w.dtype)
    return jnp.einsum("ktdc,su->ktsduc", w, eye).reshape(k, t * 2 * d, 2 * c)


def compress(cc, pos, w1, w2k, w2vt, g, batch, seq):
    G, HD = NSA_GROUPS, NSA_HD
    nwin = seq // CMP_STRIDE
    hid = w1.shape[2]
    half = CMP_BLOCK // 2
    assert half == CMP_STRIDE
    w1r = w1.reshape(2, CMP_BLOCK, HD, hid)
    w1x = jnp.stack([_pair_expand(w1r[:, :half]), _pair_expand(w1r[:, half:])], axis=1).astype(BF16)
    posr = jnp.tile(pos.reshape(2, 2, half, 1, HD), (1, 1, 1, 2, 1)).reshape(2, 2, half * 2 * HD)
    resident = lambda a: pl.BlockSpec(a.shape, lambda b: (0,) * a.ndim, pipeline_mode=pl.Buffered(1))
    return pl.pallas_call(
        _compress_body,
        grid=(batch,),
        in_specs=[pl.BlockSpec((None, seq, 128), functools.partial(lambda q, b: (q, b, 0), q))
                  for q in range(4)]
                 + [resident(posr), resident(w1x), resident(w2k), resident(w2vt), resident(g)],
        out_specs=[pl.BlockSpec((None, nwin, G * 128), lambda b: (b, 0, 0)),
                   pl.BlockSpec((None, G * HD, nwin), lambda b: (b, 0, 0))],
        out_shape=[jax.ShapeDtypeStruct((batch, nwin, G * 128), BF16),
                   jax.ShapeDtypeStruct((batch, G * HD, nwin), BF16)],
        compiler_params=_cparams(("parallel",)),
        name="compress",
    )(cc, cc, cc, cc, posr, w1x, w2k, w2vt, g)


def _t5_bucket(dist):
    n = jnp.maximum(dist, 0)
    max_exact = REL_BUCKETS // 2
    nf = jnp.maximum(n, 1).astype(F32)
    large = max_exact + (jnp.log(nf / max_exact) / math.log(REL_MAX_DIST / max_exact)
                         * (REL_BUCKETS - max_exact)).astype(jnp.int32)
    large = jnp.minimum(large, REL_BUCKETS - 1)
    return jnp.where(n < max_exact, n, large)


def _table_lookup(bucket, tab_ref, h):
    out = jnp.zeros(bucket.shape, F32)
    for k in range(REL_BUCKETS):
        out = jnp.where(bucket == k, tab_ref[k, h], out)
    return out


def _bias_prep_body(tab_ref, bc_ref, tp_ref, far_ref):
    h = pl.program_id(0)
    S = bc_ref.shape[1]
    far = tab_ref[REL_BUCKETS - 1, h]
    far_ref[...] = jnp.full(far_ref.shape, far * LOG2E, F32)

    dist = lax.broadcasted_iota(jnp.int32, (8, S), 1)
    by_dist = _table_lookup(_t5_bucket(dist), tab_ref, h) * LOG2E
    shifted = pltpu.roll(jnp.broadcast_to(by_dist[0:1, :], (NCMP_PAD, S)), 0, axis=1,
                         stride=CMP_STRIDE, stride_axis=0)
    shifted = pltpu.roll(shifted, CMP_BLOCK - 1, axis=1)
    n = lax.broadcasted_iota(jnp.int32, (NCMP_PAD, S), 0)
    t = lax.broadcasted_iota(jnp.int32, (NCMP_PAD, S), 1)
    bc_ref[...] = jnp.where(t >= n * CMP_STRIDE + CMP_BLOCK - 1, shifted, NEG_INF)
    j = lax.broadcasted_iota(jnp.int32, (ATT_TK, ATT_QB), 0)
    i = lax.broadcasted_iota(jnp.int32, (ATT_TK, ATT_QB), 1)
    for d in range(2):
        dist = d * ATT_TK + i - j
        rel = (_table_lookup(_t5_bucket(dist), tab_ref, h) - far) * LOG2E
        tp_ref[d] = jnp.where(dist >= 0, rel, NEG_INF)
    tp_ref[2] = jnp.where(i < j, 0.0, NEG_INF)


def bias_prep(rel_table, seq):
    assert ATT_TK == ATT_QB and ATT_TK + 1 > 113
    return pl.pallas_call(
        _bias_prep_body,
        grid=(NSA_HEADS,),
        in_specs=[pl.BlockSpec(memory_space=pltpu.SMEM)],
        out_specs=[pl.BlockSpec((None, NCMP_PAD, seq), lambda h: (h, 0, 0)),
                   pl.BlockSpec((None, 3, ATT_TK, ATT_QB), lambda h: (h, 0, 0, 0)),
                   pl.BlockSpec((None, 8, 128), lambda h: (h, 0, 0))],
        out_shape=[jax.ShapeDtypeStruct((NSA_HEADS, NCMP_PAD, seq), F32),
                   jax.ShapeDtypeStruct((NSA_HEADS, 3, ATT_TK, ATT_QB), F32),
                   jax.ShapeDtypeStruct((NSA_HEADS, 8, 128), F32)],
        compiler_params=_cparams(("parallel",)),
        name="bias_prep",
    )(rel_table)


def _q_proj_body(x_ref, g_ref, sh_ref, sc_ref, wqt_ref, wgt_ref, bg_ref, gq_ref, qt_ref, gt_ref):
    HD = NSA_HD
    tm = x_ref.shape[0]
    half = tm // 2
    scale = gq_ref[...] * (HD ** -0.5 * LOG2E)
    for r0 in (0, half):
        h = _norm_mod(x_ref[r0:r0 + half, :], g_ref[...], sh_ref[...], sc_ref[...]).astype(BF16)
        gt_ref[:, r0:r0 + half] = _dot_nt(wgt_ref[...], h) + bg_ref[...]
        qt = _dot_nt(wqt_ref[...], h)
        for hh in range(NSA_HEADS):
            seg = qt[hh * HD:(hh + 1) * HD, :]
            var = jnp.mean(seg * seg, axis=0, keepdims=True)
            qt_ref[hh * HD:(hh + 1) * HD, r0:r0 + half] = (
                seg * lax.rsqrt(var + RMS_EPS) * scale).astype(BF16)


def q_proj(x2d, g, sh, sc, wqt, wgt, bg, gq, batch, seq, tm=1024):
    M, D = x2d.shape
    nt = seq // tm
    nq = wqt.shape[0]
    ng = wgt.shape[0]
    bvec = pl.BlockSpec((None, 1, D), lambda i: (i // nt, 0, 0))
    full = lambda a: pl.BlockSpec(a.shape, lambda i: (0,) * a.ndim)
    return pl.pallas_call(
        _q_proj_body,
        grid=(M // tm,),
        in_specs=[pl.BlockSpec((tm, D), lambda i: (i, 0)),
                  pl.BlockSpec((1, D), lambda i: (0, 0)),
                  bvec, bvec, full(wqt), full(wgt), full(bg), full(gq)],
        out_specs=[pl.BlockSpec((None, nq, tm), lambda i: (i // nt, 0, i % nt)),
                   pl.BlockSpec((None, ng, tm), lambda i: (i // nt, 0, i % nt))],
        out_shape=[jax.ShapeDtypeStruct((batch, nq, seq), BF16),
                   jax.ShapeDtypeStruct((batch, ng, seq), F32)],
        compiler_params=_cparams(("parallel",)),
        name="q_proj",
    )(x2d, g.reshape(1, D), sh, sc, wqt, wgt, bg, gq)


def _heads_on_lanes(pieces):
    return jnp.concatenate(pieces, axis=1)


def _nsa_body(gps, qt_ref, gt_ref, kc_ref, vct_ref, ks_ref, kw_ref, vst_ref, vwt_ref,
              bct_ref, tp_ref, far_ref, ovt_ref, out_ref):
    qb = pl.program_id(2)
    QB, HD, HPG = ATT_QB, NSA_HD, NSA_HPG
    R = HPG * QB
    t0 = pl.multiple_of(qb * QB, QB)
    near0 = pl.multiple_of(t0 + WINDOW - ATT_TK, ATT_TK)
    wlen = WINDOW + QB
    nblk = ovt_ref.shape[0]
    jb = lax.broadcasted_iota(jnp.int32, (nblk, QB), 0)
    jbf = jb.astype(F32)
    qid = jnp.right_shift(t0 + lax.broadcasted_iota(jnp.int32, (nblk, QB), 1), 6)
    forced = (jb == 0) | (jb == qid) | (jb == qid - 1)
    srow = lax.broadcasted_iota(jnp.int32, (8, R), 0)

    def finish(acc):
        return acc[0:HD, :] / acc[HD:HD + 1, :]

    def tile_part(s_tile, v_tile):
        m_t = jnp.max(s_tile, axis=0, keepdims=True)
        m_safe = jnp.where(m_t == NEG_INF, 0.0, m_t)
        p = jnp.exp2(s_tile - m_safe).astype(BF16)
        return m_t, _dot(v_tile, p)[0:HD + 8, :]

    def combine(parts):
        m_fin = parts[0][0]
        for m_t, _ in parts[1:]:
            m_fin = jnp.maximum(m_fin, m_t)
        acc = None
        for m_t, pv in parts:
            term = jnp.exp2(m_t - m_fin) * pv
            acc = term if acc is None else acc + term
        return m_fin, acc

    groups = range(gps)
    hs = [[gl * HPG + h for h in range(HPG)] for gl in groups]
    kcol = [slice(gl * 128, (gl + 1) * 128) for gl in groups]
    vrow = [slice(gl * VT_ROWS, (gl + 1) * VT_ROWS) for gl in groups]

    def stationary(q_all, hi_lo, block_rows):
        pad = jnp.where(srow == 0, MASK_BIG, 0.0)
        rest = jnp.zeros((128 - HD - KEY_PAD - 8, R), F32)
        tail = jnp.concatenate([hi_lo, block_rows, pad, rest], axis=0).astype(BF16)
        return jnp.concatenate([q_all, tail], axis=0)

    q_alls, hi_los, qms = [], [], []
    for gl in groups:
        q_alls.append(_heads_on_lanes([qt_ref[h * HD:(h + 1) * HD, :] for h in hs[gl]]))
        far = _heads_on_lanes([far_ref[h, 0:1, :] for h in hs[gl]])
        hi = far.astype(BF16).astype(F32)
        hi_los.append(jnp.where(srow == 0, hi, jnp.where(srow == 1, far - hi, 0.0)))
        qms.append(stationary(q_alls[gl], hi_los[gl], jnp.zeros((nblk, R), F32)))

    s_cs = [_dot(kc_ref[:, kcol[gl]], qms[gl]) for gl in groups]
    s_ws = [_dot(kw_ref[pl.ds(t0, wlen), kcol[gl]], qms[gl]) for gl in groups]
    s_ns = [_dot(ks_ref[pl.ds(near0, 2 * ATT_TK), kcol[gl]], qms[gl]) for gl in groups]

    o_cmps, scores = [], []
    for gl in groups:
        s_c = s_cs[gl] + _heads_on_lanes([bct_ref[h] for h in hs[gl]])
        m_c = jnp.max(s_c, axis=0, keepdims=True)
        m_c = jnp.where(m_c == NEG_INF, 0.0, m_c)
        e_c = jnp.exp2(s_c - m_c)
        p_c = e_c / jnp.maximum(jnp.sum(e_c, axis=0, keepdims=True), jnp.finfo(F32).tiny)
        o_cmps.append(_dot(vct_ref[gl * HD:(gl + 1) * HD, :], p_c.astype(BF16)))
        p_sum = p_c[:, 0:QB]
        for h in range(1, HPG):
            p_sum = p_sum + p_c[:, h * QB:(h + 1) * QB]
        p_hi = p_sum.astype(BF16)
        p_lo = (p_sum - p_hi.astype(F32)).astype(BF16)
        imp = _dot(ovt_ref[...], jnp.concatenate([p_hi, p_lo], axis=0))
        score = jnp.where(forced, FORCE_SCORE, imp)
        scores.append(jnp.where(jb <= qid, score, NEG_INF))

    nwt = wlen // ATT_TK
    o_wins = []
    for gl in groups:
        parts = []
        for i in range(nwt):
            d = nwt - 1 - i
            s_t = s_ws[gl][i * ATT_TK:(i + 1) * ATT_TK]
            if d in (0, 1):
                s_t = s_t + _heads_on_lanes([tp_ref[h, d] for h in hs[gl]])
            elif d == nwt - 1:
                s_t = s_t + _heads_on_lanes([tp_ref[h, 2] for h in hs[gl]])
            parts.append(tile_part(s_t, vwt_ref[vrow[gl], pl.ds(t0 + i * ATT_TK, ATT_TK)]))
        o_wins.append(finish(combine(parts)[1]))

    msels = [jnp.full((nblk, QB), NEG_INF, F32) for _ in groups]
    for _ in range(SLC_TOPK):
        for gl in groups:
            mx = jnp.max(scores[gl], axis=0, keepdims=True)
            first = jnp.min(jnp.where(scores[gl] == mx, jbf, float(nblk)), axis=0, keepdims=True)
            pick = jbf == first
            msels[gl] = jnp.where(pick & (mx > NEG_INF), 0.0, msels[gl])
            scores[gl] = jnp.where(pick, NEG_INF, scores[gl])
    carry0 = []
    for gl in groups:
        prev_mask = jnp.concatenate(
            [jnp.broadcast_to(jnp.max(jnp.where(jb == 2 * qb - 2 + r, msels[gl], NEG_INF), axis=0,
                                      keepdims=True), (SLC_BLOCK, QB)) for r in range(2)], axis=0)
        s_prev = s_ns[gl][0:ATT_TK] + _heads_on_lanes([tp_ref[h, 1] + prev_mask for h in hs[gl]])
        s_diag = s_ns[gl][ATT_TK:] + _heads_on_lanes([tp_ref[h, 0] for h in hs[gl]])
        carry0.append(combine([
            tile_part(s_diag, vst_ref[vrow[gl], pl.ds(near0 + ATT_TK, ATT_TK)]),
            tile_part(s_prev, vst_ref[vrow[gl], pl.ds(near0, ATT_TK)])]))

    qss = [stationary(q_alls[gl], hi_los[gl], _heads_on_lanes(
        [jnp.where(msels[gl] == 0.0, 0.0, MASK_BIG)] * HPG)) for gl in groups]

    def far_step(c, carry):
        tile0 = qb + (WINDOW - ATT_TK) // ATT_TK - (c + 1) * (FAR_CHUNK // ATT_TK)
        row0 = pl.multiple_of(tile0 * ATT_TK, ATT_TK)
        s_fs = [_dot(ks_ref[pl.ds(row0, FAR_CHUNK), kcol[gl]], qss[gl]) for gl in groups]
        out = []
        for gl in groups:
            parts = [carry[gl]]
            for j in range(FAR_CHUNK // ATT_TK):
                parts.append(tile_part(s_fs[gl][j * ATT_TK:(j + 1) * ATT_TK],
                                       vst_ref[vrow[gl], pl.ds(row0 + j * ATT_TK, ATT_TK)]))
            out.append(combine(parts))
        return tuple(out)

    n_far = (qb + 2) // 4
    sel = lax.fori_loop(0, n_far, far_step, tuple(carry0))

    gates = jax.nn.sigmoid(gt_ref[...])
    for gl in groups:
        o_cmp, o_win = o_cmps[gl], o_wins[gl]
        o_sel = finish(sel[gl][1])
        outs = []
        for h in range(HPG):
            lanes = slice(h * QB, (h + 1) * QB)
            r = gl * 16 + 3 * h
            outs.append(gates[r:r + 1, :] * o_cmp[:, lanes]
                        + gates[r + 1:r + 2, :] * o_sel[:, lanes]
                        + gates[r + 2:r + 3, :] * o_win[:, lanes])
        for pair in range(HPG // 2):
            two = jnp.concatenate(outs[2 * pair:2 * pair + 2], axis=0)
            c0 = gl * HPG * HD + pair * 2 * HD
            out_ref[:, c0:c0 + 2 * HD] = two.T.astype(out_ref.dtype)


def _overlap_matrix_t(nblk):
    start = np.arange(NCMP_PAD) * CMP_STRIDE
    sj = np.arange(nblk) * SLC_BLOCK
    ov = (np.minimum(start[None, :] + CMP_BLOCK, sj[:, None] + SLC_BLOCK)
          - np.maximum(start[None, :], sj[:, None]))
    ov = np.clip(ov, 0, None) / CMP_BLOCK
    ov[:, NCMP_PAD - 1] = 0.0
    return np.concatenate([ov, ov], axis=1).astype(np.float32)


def nsa_attn(qt, gt, kc, vct, ks, kw, vst, vwt, bias_ct, tiles, far, batch, seq, gps=ATT_GPS):
    QB, HD, HPG, G = ATT_QB, NSA_HD, NSA_HPG, NSA_GROUPS
    nblk = seq // SLC_BLOCK
    assert nblk % 8 == 0 and seq % QB == 0 and seq // CMP_STRIDE == NCMP_PAD and G % gps == 0
    assert WINDOW % ATT_TK == 0 and FAR_CHUNK == 4 * ATT_TK and KV_PAD >= FAR_CHUNK - ATT_TK
    assert KEY_BLK % 8 == 0 and KEY_BLK + nblk <= KEY_PAD and KEY_PAD + 8 <= 128 - HD
    nq = seq // QB
    sp = seq + KV_PAD
    ovt = jnp.asarray(_overlap_matrix_t(nblk), dtype=BF16)
    return pl.pallas_call(
        functools.partial(_nsa_body, gps),
        grid=(batch, G // gps, nq),
        in_specs=[pl.BlockSpec((None, gps * HPG * HD, QB), lambda b, g, i: (b, g, i)),
                  pl.BlockSpec((None, gps * 16, QB), lambda b, g, i: (b, g, i)),
                  pl.BlockSpec((None, NCMP_PAD, gps * 128), lambda b, g, i: (b, 0, g)),
                  pl.BlockSpec((None, gps * HD, NCMP_PAD), lambda b, g, i: (b, g, 0)),
                  pl.BlockSpec((None, sp, gps * 128), lambda b, g, i: (b, 0, g)),
                  pl.BlockSpec((None, sp, gps * 128), lambda b, g, i: (b, 0, g)),
                  pl.BlockSpec((None, gps * VT_ROWS, sp), lambda b, g, i: (b, g, 0)),
                  pl.BlockSpec((None, gps * VT_ROWS, sp), lambda b, g, i: (b, g, 0)),
                  pl.BlockSpec((gps * HPG, NCMP_PAD, QB), lambda b, g, i: (g, 0, i)),
                  pl.BlockSpec((gps * HPG, 3, ATT_TK, QB), lambda b, g, i: (g, 0, 0, 0)),
                  pl.BlockSpec((gps * HPG, 8, 128), lambda b, g, i: (g, 0, 0)),
                  pl.BlockSpec((nblk, 2 * NCMP_PAD), lambda b, g, i: (0, 0))],
        out_specs=pl.BlockSpec((QB, gps * HPG * HD), lambda b, g, i: (b * nq + i, g)),
        out_shape=jax.ShapeDtypeStruct((batch * seq, NSA_HEADS * HD), BF16),
        compiler_params=_cparams(("parallel", "parallel", "arbitrary")),
        name="nsa_attn",
    )(qt, gt, kc, vct, ks, kw, vst, vwt, bias_ct, tiles, far, ovt)


def _mlstm_layer(x2d, g_mix, sh, sc, w_stack, layer, b_if, g_out, batch, seq):
    proj, gates = mlstm_proj(x2d, g_mix, sh, sc, w_stack, layer, b_if)
    return mlstm_core(proj, gates, g_out, batch, seq)


def _nsa_shared(x2d, g_kv, kv_sh, kv_sc, w_kv, pos_k, w_k1, w_k2, pos_v, w_v1, w_v2,
                g_knorm, batch, seq):
    gw = NSA_GROUPS * NSA_HD
    part = lambda i: w_kv[:, i * gw:(i + 1) * gw]
    hd = NSA_HD
    wk = jnp.concatenate([p[:, g * hd:(g + 1) * hd] for g in range(NSA_GROUPS)
                          for p in (part(2), part(4))], axis=1).astype(BF16)
    gk = jnp.tile(jnp.concatenate([g_knorm[1], g_knorm[2]]), NSA_GROUPS).reshape(1, -1)
    wvt = jnp.concatenate([part(3), part(5)], axis=1).T.astype(BF16)
    wcc = jnp.concatenate([part(0), part(1)], axis=1).astype(BF16)
    ks, kw, vst, vwt, cc = kv_proj(x2d, g_kv, kv_sh, kv_sc, wk, wvt, wcc, gk, batch, seq)
    pos = jnp.stack([pos_k, pos_v])
    w1 = jnp.stack([w_k1, w_v1])
    kc, vct = compress(cc, pos, w1, w_k2.astype(BF16), w_v2.T.astype(BF16), g_knorm[0:1],
                       batch, seq)
    return kc, vct, ks, kw, vst, vwt


def _gate_weights_t(w_q, b_gate):
    nq = NSA_HEADS * NSA_HD
    per = 3 * NSA_HPG
    wg = w_q[:, nq:].T.reshape(NSA_GROUPS, per, -1)
    wg = jnp.pad(wg, ((0, 0), (0, 16 - per), (0, 0))).reshape(NSA_GROUPS * 16, -1)
    bg = jnp.pad(b_gate.reshape(NSA_GROUPS, per), ((0, 0), (0, 16 - per))).reshape(-1, 1)
    return wg.astype(BF16), bg


def _nsa_layer(x2d, g_mix, sh, sc, shared, w_q, b_gate, g_qnorm, bias_ct, tiles, far, batch, seq):
    nq = NSA_HEADS * NSA_HD
    wgt, bg = _gate_weights_t(w_q, b_gate)
    qt, gt = q_proj(x2d, g_mix, sh, sc, w_q[:, :nq].T.astype(BF16), wgt, bg,
                    g_qnorm.reshape(NSA_HD, 1), batch, seq)
    kc, vct, ks, kw, vst, vwt = shared
    return nsa_attn(qt, gt, kc, vct, ks, kw, vst, vwt, bias_ct, tiles, far, batch, seq)


def kernel(x, c, w_ada, b_ada, g_norm_mix, g_norm_ffn, w_ffn_in, w_ffn_out, w_a_in, b_a_if, g_a_out, w_a_out, w_kv_ada, b_kv_ada, g_kv_norm, w_kv, pos_cmp_k, w_cmp_k1, w_cmp_k2, pos_cmp_v, w_cmp_v1, w_cmp_v2, g_knorm, w_b_q, b_b_gate, g_qnorm, w_b_out, rel_table):
    B, S, D = x.shape
    depth = w_ada.shape[0]
    n_a = w_a_in.shape[0]
    x2d = x.reshape(B * S, D)
    mods = ada_mod(c, w_ada, b_ada)
    kv_mod = ada_mod(c, w_kv_ada[None], b_kv_ada[None])[0]
    shared = None
    bias_ct = tiles = far = None
    for layer in range(depth):
        sh1, sc1, ga1, sh2, sc2, ga2 = [mods[layer, :, i * D:(i + 1) * D].reshape(B, 1, D)
                                        for i in range(6)]
        if layer < n_a:
            mixed = _mlstm_layer(x2d, g_norm_mix[layer], sh1, sc1, w_a_in, layer, b_a_if[layer],
                                 g_a_out[layer], B, S)
            w_mix = (w_a_out, layer)
        else:
            j = layer - n_a
            if shared is None:
                kv_sh = kv_mod[:, :D].reshape(B, 1, D)
                kv_sc = kv_mod[:, D:].reshape(B, 1, D)
                shared = _nsa_shared(x2d, g_kv_norm, kv_sh, kv_sc, w_kv, pos_cmp_k, w_cmp_k1,
                                     w_cmp_k2, pos_cmp_v, w_cmp_v1, w_cmp_v2, g_knorm, B, S)
                bias_ct, tiles, far = bias_prep(rel_table, S)
            mixed = _nsa_layer(x2d, g_norm_mix[layer], sh1, sc1, shared, w_b_q[j], b_b_gate[j],
                               g_qnorm[j], bias_ct, tiles, far, B, S)
            w_mix = (w_b_out, j)
        x2d = mix_ffn(mixed, w_mix, x2d, ga1, g_norm_ffn[layer], sh2, sc2, ga2,
                      (w_ffn_in, layer), (w_ffn_out, layer))
    return x2d.reshape(B, S, D)
```

```python
import functools
import math

import jax
import jax.numpy as jnp
import numpy as np
from jax import lax
from jax.experimental import pallas as pl
from jax.experimental.pallas import tpu as pltpu

F32 = jnp.float32
BF16 = jnp.bfloat16
NEG_INF = float("-inf")
LOG2E = math.log2(math.e)

RMS_EPS = 1e-6

ML_HEADS = 4
ML_DQK = 128
ML_DV = 256
ML_LC = 256
ML_CPS = 4

NSA_HEADS = 16
NSA_GROUPS = 4
NSA_HPG = 4
NSA_HD = 64
CMP_BLOCK = 32
CMP_STRIDE = 16
SLC_BLOCK = 64
SLC_TOPK = 8
WINDOW = 512
FORCE_SCORE = 1e4
REL_BUCKETS = 32
REL_MAX_DIST = 128
ATT_QB = 128
ATT_TK = 128
ATT_GPS = 4
NCMP_PAD = 128
KV_PAD = WINDOW
FAR_CHUNK = 512
VT_ROWS = 80
KEY_BLK = 8
KEY_PAD = 40
MASK_BIG = -32768.0

VMEM_LIMIT = 56 * 1024 * 1024


def _cparams(sem):
    return pltpu.CompilerParams(dimension_semantics=sem, vmem_limit_bytes=VMEM_LIMIT)


def _dot(a, b):
    return jnp.dot(a, b, preferred_element_type=F32)


def _dot_nt(a, b):
    return lax.dot_general(a, b, (((1,), (1,)), ((), ())), preferred_element_type=F32)


def _dot_tn(a, b):
    return lax.dot_general(a, b, (((0,), (0,)), ((), ())), preferred_element_type=F32)


def _norm_mod(x, g, sh, sc):
    var = jnp.mean(x * x, axis=-1, keepdims=True)
    y = x * lax.rsqrt(var + RMS_EPS) * g
    return y * (1.0 + sc) + sh


def _ada_body(c_ref, w_ref, b_ref, o_ref):
    c = c_ref[...]
    ca = c * jax.nn.sigmoid(c)
    nb = ca.shape[0]
    a_hi = ca.astype(BF16).astype(F32)
    a_mid = (ca - a_hi).astype(BF16).astype(F32)
    a_lo = ca - a_hi - a_mid
    a3 = jnp.concatenate([a_hi, a_mid, a_lo, jnp.zeros_like(ca)], axis=0).astype(BF16)
    w = w_ref[...]
    w_hi = w.astype(BF16)
    w_lo = (w - w_hi.astype(F32)).astype(BF16)
    p = _dot(a3, w_hi)
    q = _dot(a3[0:2 * nb], w_lo)
    o_ref[...] = (p[0:nb] + p[nb:2 * nb] + p[2 * nb:3 * nb] + q[0:nb] + q[nb:2 * nb]) + b_ref[...]


def ada_mod(c, w, b, tn=2048):
    L, D, N = w.shape
    B = c.shape[0]
    return pl.pallas_call(
        _ada_body,
        grid=(L, N // tn),
        in_specs=[pl.BlockSpec((B, D), lambda l, j: (0, 0)),
                  pl.BlockSpec((None, D, tn), lambda l, j: (l, 0, j)),
                  pl.BlockSpec((None, 1, tn), lambda l, j: (l, 0, j))],
        out_specs=pl.BlockSpec((None, B, tn), lambda l, j: (l, 0, j)),
        out_shape=jax.ShapeDtypeStruct((L, B, N), F32),
        compiler_params=_cparams(("parallel", "parallel")),
        name="ada_mod",
    )(c, w, b.reshape(L, 1, N))


def _mlstm_proj_body(ng, tn, x_ref, g_ref, sh_ref, sc_ref, w_ref, wg_ref, b_ref, p_ref, gt_ref):
    tm = x_ref.shape[0]
    half = tm // 2
    for r0 in (0, half):
        rows = slice(r0, r0 + half)
        h = _norm_mod(x_ref[rows, :], g_ref[...], sh_ref[...], sc_ref[...]).astype(BF16)
        for n0 in range(0, w_ref.shape[1], tn):
            p_ref[rows, n0:n0 + tn] = _dot(h, w_ref[:, n0:n0 + tn].astype(BF16)).astype(p_ref.dtype)
        gates = _dot(h, wg_ref[:, 0:ng].astype(BF16)) + b_ref[...]
        gt_ref[rows, 0:ng] = gates
        gt_ref[rows, ng:] = jnp.zeros((half, gt_ref.shape[1] - ng), F32)


def mlstm_proj(x2d, g, sh, sc, w_stack, layer, b_if, tm=1024, tn=512):
    M, D = x2d.shape
    B = sh.shape[0]
    ng = b_if.shape[0]
    nbig = w_stack.shape[2] - ng
    assert nbig % 128 == 0 and ng <= 128
    tiles_per_batch = (M // B) // tm
    bvec = pl.BlockSpec((None, 1, D), lambda i: (i // tiles_per_batch, 0, 0))
    return pl.pallas_call(
        functools.partial(_mlstm_proj_body, ng, tn),
        grid=(M // tm,),
        in_specs=[pl.BlockSpec((tm, D), lambda i: (i, 0)),
                  pl.BlockSpec((1, D), lambda i: (0, 0)), bvec, bvec,
                  pl.BlockSpec((None, D, nbig), lambda i: (layer, 0, 0),
                               pipeline_mode=pl.Buffered(1)),
                  pl.BlockSpec((None, D, 128), lambda i: (layer, 0, nbig // 128),
                               pipeline_mode=pl.Buffered(1)),
                  pl.BlockSpec((1, ng), lambda i: (0, 0))],
        out_specs=[pl.BlockSpec((tm, nbig), lambda i: (i, 0)),
                   pl.BlockSpec((tm, 128), lambda i: (i, 0))],
        out_shape=[jax.ShapeDtypeStruct((M, nbig), BF16), jax.ShapeDtypeStruct((M, 128), F32)],
        compiler_params=_cparams(("parallel",)),
        name="mlstm_proj",
    )(x2d, g.reshape(1, D), sh, sc, w_stack, w_stack, b_if.reshape(1, ng))


def _mix_ffn_body(tf, a_ref, wm_ref, x_ref, ga1_ref, g_ref, sh_ref, sc_ref, ga2_ref,
                  wi_ref, wo_ref, o_ref, act_s):
    F = wo_ref.shape[0]
    x1 = x_ref[...] + ga1_ref[...] * _dot(a_ref[...], wm_ref[...].astype(BF16))
    h = _norm_mod(x1, g_ref[...], sh_ref[...], sc_ref[...]).astype(BF16)
    for f0 in range(0, F, tf):
        gate = _dot(h, wi_ref[:, f0:f0 + tf].astype(BF16))
        up = _dot(h, wi_ref[:, F + f0:F + f0 + tf].astype(BF16))
        act_s[:, f0:f0 + tf] = (gate * jax.nn.sigmoid(gate) * up).astype(BF16)
    y = _dot(act_s[:, 0:tf], wo_ref[0:tf, :].astype(BF16))
    for f0 in range(tf, F, tf):
        y = y + _dot(act_s[:, f0:f0 + tf], wo_ref[f0:f0 + tf, :].astype(BF16))
    o_ref[...] = x1 + ga2_ref[...] * y


def mix_ffn(a, w_mix, x2d, ga1, g, sh, sc, ga2, w_in, w_out, tm=512, tf=256):
    M, D = x2d.shape
    K = a.shape[1]
    F = w_out[0].shape[1]
    B = sh.shape[0]
    tiles_per_batch = (M // B) // tm
    bvec = pl.BlockSpec((None, 1, D), lambda i: (i // tiles_per_batch, 0, 0))

    def resident(stacked_layer):
        w, layer = stacked_layer
        return pl.BlockSpec((None,) + w.shape[1:], lambda i: (layer, 0, 0),
                            pipeline_mode=pl.Buffered(1))

    return pl.pallas_call(
        functools.partial(_mix_ffn_body, tf),
        grid=(M // tm,),
        in_specs=[pl.BlockSpec((tm, K), lambda i: (i, 0)),
                  resident(w_mix),
                  pl.BlockSpec((tm, D), lambda i: (i, 0)),
                  bvec,
                  pl.BlockSpec((1, D), lambda i: (0, 0)),
                  bvec, bvec, bvec,
                  resident(w_in), resident(w_out)],
        out_specs=pl.BlockSpec((tm, D), lambda i: (i, 0)),
        out_shape=jax.ShapeDtypeStruct((M, D), F32),
        scratch_shapes=[pltpu.VMEM((tm, F), BF16)],
        compiler_params=_cparams(("parallel",)),
        name="mix_ffn",
    )(a, w_mix[0], x2d, ga1, g.reshape(1, D), sh, sc, ga2, w_in[0], w_out[0])


def _sublane_scan(x, op, fill):
    n = x.shape[0]
    row = lax.broadcasted_iota(jnp.int32, x.shape, 0)
    sh = 1
    while sh < n:
        x = op(x, jnp.where(row >= sh, pltpu.roll(x, sh, axis=0), fill))
        sh *= 2
    return x


def _mlstm_body(q_ref, k_ref, v_ref, o_ref, gc_ref, gout_ref, out_ref, c_s, n_s, m_s):
    @pl.when(pl.program_id(1) == 0)
    def _():
        c_s[...] = jnp.zeros_like(c_s)
        n_s[...] = jnp.zeros_like(n_s)
        m_s[...] = jnp.zeros_like(m_s)

    for j in range(q_ref.shape[0] // ML_LC):
        rows = pl.ds(j * ML_LC, ML_LC)
        _mlstm_chunk(q_ref.at[rows], k_ref.at[rows], v_ref.at[rows], o_ref.at[rows], gc_ref.at[rows],
                     gout_ref, out_ref.at[rows], c_s, n_s, m_s)


def _mlstm_chunk(q_ref, k_ref, v_ref, o_ref, gc_ref, gout_ref, out_ref, c_s, n_s, m_s):
    LC = q_ref.shape[0]
    NH = ML_HEADS
    scale = ML_DQK ** -0.5
    log_scale = math.log(scale)

    gc = gc_ref[...]
    logf = jnp.minimum(gc, 0.0) - jnp.log1p(jnp.exp(-jnp.abs(gc)))
    bcum = pltpu.roll(_sublane_scan(logf, jnp.add, 0.0), 128 - NH, axis=1)
    a_c = gc - bcum
    cmax = _sublane_scan(a_c, jnp.maximum, NEG_INF)
    a_t = a_c.T

    row = lax.broadcasted_iota(jnp.int32, (LC, LC), 0)
    col = lax.broadcasted_iota(jnp.int32, (LC, LC), 1)
    causal = col <= row
    ones_col = jnp.ones((LC, 128), BF16)

    def lanes2(x):
        return jnp.concatenate([x, x], axis=1)

    for h in range(NH):
        a_rep = jnp.broadcast_to(a_c[:, h:h + 1], (LC, 128))
        b_rep = jnp.broadcast_to(bcum[:, h:h + 1], (LC, 128))
        cm_rep = jnp.broadcast_to(cmax[:, h:h + 1], (LC, 128))
        a_r = a_t[h:h + 1, :] + log_scale
        b_last = b_rep[LC - 1:LC, :]
        m_prev = m_s[h:h + 1, :]
        m_new = jnp.maximum(b_last + m_prev, b_last + cm_rep[LC - 1:LC, :])
        decay = jnp.exp(b_last + m_prev - m_new)
        e_rep = jnp.exp(b_last + a_rep - m_new)
        g_rep = jnp.maximum(m_prev, cm_rep)
        w_intra = jnp.exp(jnp.where(causal, a_r - lanes2(g_rep), NEG_INF))
        w_inter = jnp.exp(m_prev - g_rep) * scale
        floor = jnp.exp(-(b_rep + g_rep))

        qh = q_ref[:, h * ML_DQK:(h + 1) * ML_DQK]
        kh = k_ref[:, h * ML_DQK:(h + 1) * ML_DQK]
        vh = v_ref[:, h * ML_DV:(h + 1) * ML_DV]
        c_prev = c_s[h]
        n_prev = n_s[h:h + 1, :]

        s = (_dot_nt(qh, kh) * w_intra).astype(BF16)
        inter = _dot(qh, c_prev.astype(BF16))
        num = _dot(s, vh) + lanes2(w_inter) * inter
        n_rows = jnp.broadcast_to(n_prev, (128, ML_DQK)).astype(BF16)
        qn = _dot(s, ones_col) + w_inter * _dot_nt(qh, n_rows)
        inv = 1.0 / jnp.maximum(jnp.abs(qn), floor)

        ke = kh.astype(F32) * e_rep
        c_s[h] = lanes2(decay) * c_prev + _dot_tn(ke.astype(BF16), vh)
        n_s[h:h + 1, :] = decay * n_prev + jnp.sum(ke, axis=0, keepdims=True)
        m_s[h:h + 1, :] = m_new

        ssq = _dot((num * num).astype(BF16), jnp.ones((ML_DV, 128), BF16))
        rs = lax.rsqrt(ssq * (inv * inv) * (1.0 / ML_DV) + RMS_EPS) * inv
        hn = num * lanes2(rs) * gout_ref[:, h * ML_DV:(h + 1) * ML_DV]
        og = jax.nn.sigmoid(o_ref[:, h * ML_DV:(h + 1) * ML_DV])
        out_ref[:, h * ML_DV:(h + 1) * ML_DV] = hn.astype(BF16) * og


def mlstm_core(proj, gates, g_out, batch, seq):
    LC = ML_LC * ML_CPS
    nc = seq // LC
    qk = ML_HEADS * ML_DQK
    vd = ML_HEADS * ML_DV
    row = lambda b, c: b * nc + c
    return pl.pallas_call(
        _mlstm_body,
        grid=(batch, nc),
        in_specs=[pl.BlockSpec((LC, qk), lambda b, c: (row(b, c), 0)),
                  pl.BlockSpec((LC, qk), lambda b, c: (row(b, c), 1)),
                  pl.BlockSpec((LC, vd), lambda b, c: (row(b, c), 1)),
                  pl.BlockSpec((LC, vd), lambda b, c: (row(b, c), 2)),
                  pl.BlockSpec((LC, 128), lambda b, c: (row(b, c), 0)),
                  pl.BlockSpec((1, vd), lambda b, c: (0, 0))],
        out_specs=pl.BlockSpec((LC, vd), lambda b, c: (row(b, c), 0)),
        out_shape=jax.ShapeDtypeStruct((batch * seq, vd), BF16),
        scratch_shapes=[pltpu.VMEM((ML_HEADS, ML_DQK, ML_DV), F32),
                        pltpu.VMEM((8, ML_DQK), F32),
                        pltpu.VMEM((8, 128), F32)],
        compiler_params=_cparams(("parallel", "arbitrary")),
        name="mlstm_core",
    )(proj, proj, proj, proj, gates, g_out.reshape(1, vd))


def _kv_proj_body(x_ref, g_ref, sh_ref, sc_ref, wk_ref, wvt_ref, wcc_ref, gk_ref, bd_ref,
                  ks_ref, kw_ref, vst_ref, vwt_ref, cc_ref):
    i = pl.program_id(1)
    G, HD = NSA_GROUPS, NSA_HD
    tm = x_ref.shape[0]
    lane = lax.broadcasted_iota(jnp.int32, (tm, G * 128), 1) % 128

    @pl.when(i == 0)
    def _():
        pad_rows = jnp.where(lane == HD + KEY_PAD, 1.0, 0.0).astype(BF16)
        ks_ref[...] = pad_rows
        kw_ref[...] = pad_rows
        vst_ref[...] = jnp.zeros_like(vst_ref)
        vwt_ref[...] = jnp.zeros_like(vwt_ref)

    @pl.when(i > 0)
    def _():
        h = _norm_mod(x_ref[...], g_ref[...], sh_ref[...], sc_ref[...]).astype(BF16)
        cc = _dot(h, wcc_ref[...])
        for q in range(cc_ref.shape[0]):
            cc_ref[q] = cc[:, q * 128:(q + 1) * 128]
        kk = _dot(h, wk_ref[...])
        ssq = _dot((kk * kk).astype(BF16), bd_ref[...])
        kn = kk * lax.rsqrt(ssq * (1.0 / HD) + RMS_EPS) * gk_ref[...]
        blk = (i - 1) * (tm // SLC_BLOCK) + lax.broadcasted_iota(
            jnp.int32, (tm, G * 128), 0) // SLC_BLOCK
        tail = jnp.where((lane == HD) | (lane == HD + 1) | (lane == HD + KEY_BLK + blk), 1.0, 0.0)
        ks_ref[...] = jnp.where(lane < HD, kn, tail).astype(BF16)
        kn_sw = jnp.concatenate(
            [pltpu.roll(kn[:, g * 128:(g + 1) * 128], HD, axis=1) for g in range(G)], axis=1)
        kw_ref[...] = jnp.where(lane < HD, kn_sw, tail).astype(BF16)
        vt = _dot_nt(wvt_ref[...], h)
        srow = lax.broadcasted_iota(jnp.int32, (VT_ROWS - HD, tm), 0)
        ones_blk = jnp.where(srow == 0, 1.0, 0.0).astype(BF16)
        for kind, o_ref in enumerate((vst_ref, vwt_ref)):
            for g in range(G):
                r0 = (kind * G + g) * HD
                o_ref[g * VT_ROWS:g * VT_ROWS + HD, :] = vt[r0:r0 + HD, :].astype(BF16)
                o_ref[g * VT_ROWS + HD:(g + 1) * VT_ROWS, :] = ones_blk


def _segment_ones(width, seg):
    idx = np.arange(width) // seg
    return (idx[:, None] == idx[None, :]).astype(np.float32)


def kv_proj(x2d, g, sh, sc, wk, wvt, wcc, gk, batch, seq, tm=KV_PAD):
    assert tm == KV_PAD and seq % tm == 0
    M, D = x2d.shape
    G = NSA_GROUPS
    nt = seq // tm
    sp = seq + KV_PAD
    ncs = wcc.shape[1] // 128
    bd = jnp.asarray(_segment_ones(G * 128, NSA_HD), dtype=BF16)
    xrow = lambda b, i: (b * nt + jnp.maximum(i - 1, 0), 0)
    bvec = pl.BlockSpec((None, 1, D), lambda b, i: (b, 0, 0))
    full = lambda a: pl.BlockSpec(a.shape, lambda b, i: (0,) * a.ndim)
    return pl.pallas_call(
        _kv_proj_body,
        grid=(batch, nt + 1),
        in_specs=[pl.BlockSpec((tm, D), xrow),
                  pl.BlockSpec((1, D), lambda b, i: (0, 0)),
                  bvec, bvec, full(wk), full(wvt), full(wcc), full(gk), full(bd)],
        out_specs=[pl.BlockSpec((None, tm, G * 128), lambda b, i: (b, i, 0)),
                   pl.BlockSpec((None, tm, G * 128), lambda b, i: (b, i, 0)),
                   pl.BlockSpec((None, G * VT_ROWS, tm), lambda b, i: (b, 0, i)),
                   pl.BlockSpec((None, G * VT_ROWS, tm), lambda b, i: (b, 0, i)),
                   pl.BlockSpec((ncs, tm, 128),
                                lambda b, i: (0, b * nt + jnp.maximum(i - 1, 0), 0))],
        out_shape=[jax.ShapeDtypeStruct((batch, sp, G * 128), BF16),
                   jax.ShapeDtypeStruct((batch, sp, G * 128), BF16),
                   jax.ShapeDtypeStruct((batch, G * VT_ROWS, sp), BF16),
                   jax.ShapeDtypeStruct((batch, G * VT_ROWS, sp), BF16),
                   jax.ShapeDtypeStruct((ncs, M, 128), F32)],
        compiler_params=_cparams(("parallel", "arbitrary")),
        name="kv_proj",
    )(x2d, g.reshape(1, D), sh, sc, wk, wvt, wcc, gk, bd)


def _gelu_tanh(x):
    c = math.sqrt(2.0 / math.pi)
    return 0.5 * x * (1.0 + jnp.tanh(c * (x + 0.044715 * (x * x * x))))


def _compress_body(cc0_ref, cc1_ref, cc2_ref, cc3_ref, pos_ref, w1_ref, w2_ref, w2t_ref, g_ref,
                   kc_ref, vct_ref):
    HD = NSA_HD
    nwin = kc_ref.shape[0]
    hid = w2_ref.shape[0]
    for p, cc_ref in enumerate((cc0_ref, cc1_ref, cc2_ref, cc3_ref)):
        kind = p // 2
        x = jnp.concatenate([cc_ref[pl.ds(i, nwin, stride=CMP_STRIDE), :]
                             for i in range(CMP_STRIDE)], axis=1)
        u = _dot((x + pos_ref[kind, 0:1, :]).astype(BF16), w1_ref[kind, 0])
        v = _dot((x + pos_ref[kind, 1:2, :]).astype(BF16), w1_ref[kind, 1])
        pre = u + pltpu.roll(v, nwin - 1, axis=0)
        for s in range(2):
            g = 2 * (p % 2) + s
            hmid = _gelu_tanh(pre[:, s * hid:(s + 1) * hid]).astype(BF16)
            if kind == 0:
                y = _dot(hmid, w2_ref[...])
                var = jnp.mean(y * y, axis=-1, keepdims=True)
                yn = y * lax.rsqrt(var + RMS_EPS) * g_ref[...]
                kc_ref[:, g * 128:g * 128 + HD] = yn.astype(BF16)
                kc_ref[:, g * 128 + HD:(g + 1) * 128] = jnp.zeros((nwin, 128 - HD), BF16)
            else:
                vct_ref[g * HD:(g + 1) * HD, :] = _dot_nt(w2t_ref[...], hmid).astype(BF16)


def _pair_expand(w):
    k, t, d, c = w.shape
    eye = jnp.eye(2, dtype=w.dtype)
    return jnp.einsum("ktdc,su->ktsduc", w, eye).reshape(k, t * 2 * d, 2 * c)


def compress(cc, pos, w1, w2k, w2vt, g, batch, seq):
    G, HD = NSA_GROUPS, NSA_HD
    nwin = seq // CMP_STRIDE
    hid = w1.shape[2]
    half = CMP_BLOCK // 2
    assert half == CMP_STRIDE
    w1r = w1.reshape(2, CMP_BLOCK, HD, hid)
    w1x = jnp.stack([_pair_expand(w1r[:, :half]), _pair_expand(w1r[:, half:])], axis=1).astype(BF16)
    posr = jnp.tile(pos.reshape(2, 2, half, 1, HD), (1, 1, 1, 2, 1)).reshape(2, 2, half * 2 * HD)
    resident = lambda a: pl.BlockSpec(a.shape, lambda b: (0,) * a.ndim, pipeline_mode=pl.Buffered(1))
    return pl.pallas_call(
        _compress_body,
        grid=(batch,),
        in_specs=[pl.BlockSpec((None, seq, 128), functools.partial(lambda q, b: (q, b, 0), q))
                  for q in range(4)]
                 + [resident(posr), resident(w1x), resident(w2k), resident(w2vt), resident(g)],
        out_specs=[pl.BlockSpec((None, nwin, G * 128), lambda b: (b, 0, 0)),
                   pl.BlockSpec((None, G * HD, nwin), lambda b: (b, 0, 0))],
        out_shape=[jax.ShapeDtypeStruct((batch, nwin, G * 128), BF16),
                   jax.ShapeDtypeStruct((batch, G * HD, nwin), BF16)],
        compiler_params=_cparams(("parallel",)),
        name="compress",
    )(cc, cc, cc, cc, posr, w1x, w2k, w2vt, g)


def _t5_bucket(dist):
    n = jnp.maximum(dist, 0)
    max_exact = REL_BUCKETS // 2
    nf = jnp.maximum(n, 1).astype(F32)
    large = max_exact + (jnp.log(nf / max_exact) / math.log(REL_MAX_DIST / max_exact)
                         * (REL_BUCKETS - max_exact)).astype(jnp.int32)
    large = jnp.minimum(large, REL_BUCKETS - 1)
    return jnp.where(n < max_exact, n, large)


def _table_lookup(bucket, tab_ref, h):
    out = jnp.zeros(bucket.shape, F32)
    for k in range(REL_BUCKETS):
        out = jnp.where(bucket == k, tab_ref[k, h], out)
    return out


def _bias_prep_body(tab_ref, bc_ref, tp_ref, far_ref):
    h = pl.program_id(0)
    S = bc_ref.shape[1]
    far = tab_ref[REL_BUCKETS - 1, h]
    far_ref[...] = jnp.full(far_ref.shape, far * LOG2E, F32)

    dist = lax.broadcasted_iota(jnp.int32, (8, S), 1)
    by_dist = _table_lookup(_t5_bucket(dist), tab_ref, h) * LOG2E
    shifted = pltpu.roll(jnp.broadcast_to(by_dist[0:1, :], (NCMP_PAD, S)), 0, axis=1,
                         stride=CMP_STRIDE, stride_axis=0)
    shifted = pltpu.roll(shifted, CMP_BLOCK - 1, axis=1)
    n = lax.broadcasted_iota(jnp.int32, (NCMP_PAD, S), 0)
    t = lax.broadcasted_iota(jnp.int32, (NCMP_PAD, S), 1)
    bc_ref[...] = jnp.where(t >= n * CMP_STRIDE + CMP_BLOCK - 1, shifted, NEG_INF)
    j = lax.broadcasted_iota(jnp.int32, (ATT_TK, ATT_QB), 0)
    i = lax.broadcasted_iota(jnp.int32, (ATT_TK, ATT_QB), 1)
    for d in range(2):
        dist = d * ATT_TK + i - j
        rel = (_table_lookup(_t5_bucket(dist), tab_ref, h) - far) * LOG2E
        tp_ref[d] = jnp.where(dist >= 0, rel, NEG_INF)
    tp_ref[2] = jnp.where(i < j, 0.0, NEG_INF)


def bias_prep(rel_table, seq):
    assert ATT_TK == ATT_QB and ATT_TK + 1 > 113
    return pl.pallas_call(
        _bias_prep_body,
        grid=(NSA_HEADS,),
        in_specs=[pl.BlockSpec(memory_space=pltpu.SMEM)],
        out_specs=[pl.BlockSpec((None, NCMP_PAD, seq), lambda h: (h, 0, 0)),
                   pl.BlockSpec((None, 3, ATT_TK, ATT_QB), lambda h: (h, 0, 0, 0)),
                   pl.BlockSpec((None, 8, 128), lambda h: (h, 0, 0))],
        out_shape=[jax.ShapeDtypeStruct((NSA_HEADS, NCMP_PAD, seq), F32),
                   jax.ShapeDtypeStruct((NSA_HEADS, 3, ATT_TK, ATT_QB), F32),
                   jax.ShapeDtypeStruct((NSA_HEADS, 8, 128), F32)],
        compiler_params=_cparams(("parallel",)),
        name="bias_prep",
    )(rel_table)


def _q_proj_body(x_ref, g_ref, sh_ref, sc_ref, wqt_ref, wgt_ref, bg_ref, gq_ref, qt_ref, gt_ref):
    HD = NSA_HD
    tm = x_ref.shape[0]
    half = tm // 2
    scale = gq_ref[...] * (HD ** -0.5 * LOG2E)
    for r0 in (0, half):
        h = _norm_mod(x_ref[r0:r0 + half, :], g_ref[...], sh_ref[...], sc_ref[...]).astype(BF16)
        gt_ref[:, r0:r0 + half] = _dot_nt(wgt_ref[...], h) + bg_ref[...]
        qt = _dot_nt(wqt_ref[...], h)
        for hh in range(NSA_HEADS):
            seg = qt[hh * HD:(hh + 1) * HD, :]
            var = jnp.mean(seg * seg, axis=0, keepdims=True)
            qt_ref[hh * HD:(hh + 1) * HD, r0:r0 + half] = (
                seg * lax.rsqrt(var + RMS_EPS) * scale).astype(BF16)


def q_proj(x2d, g, sh, sc, wqt, wgt, bg, gq, batch, seq, tm=1024):
    M, D = x2d.shape
    nt = seq // tm
    nq = wqt.shape[0]
    ng = wgt.shape[0]
    bvec = pl.BlockSpec((None, 1, D), lambda i: (i // nt, 0, 0))
    full = lambda a: pl.BlockSpec(a.shape, lambda i: (0,) * a.ndim)
    return pl.pallas_call(
        _q_proj_body,
        grid=(M // tm,),
        in_specs=[pl.BlockSpec((tm, D), lambda i: (i, 0)),
                  pl.BlockSpec((1, D), lambda i: (0, 0)),
                  bvec, bvec, full(wqt), full(wgt), full(bg), full(gq)],
        out_specs=[pl.BlockSpec((None, nq, tm), lambda i: (i // nt, 0, i % nt)),
                   pl.BlockSpec((None, ng, tm), lambda i: (i // nt, 0, i % nt))],
        out_shape=[jax.ShapeDtypeStruct((batch, nq, seq), BF16),
                   jax.ShapeDtypeStruct((batch, ng, seq), F32)],
        compiler_params=_cparams(("parallel",)),
        name="q_proj",
    )(x2d, g.reshape(1, D), sh, sc, wqt, wgt, bg, gq)


def _heads_on_lanes(pieces):
    return jnp.concatenate(pieces, axis=1)


def _nsa_body(gps, qt_ref, gt_ref, kc_ref, vct_ref, ks_ref, kw_ref, vst_ref, vwt_ref,
              bct_ref, tp_ref, far_ref, ovt_ref, out_ref):
    qb = pl.program_id(2)
    QB, HD, HPG = ATT_QB, NSA_HD, NSA_HPG
    R = HPG * QB
    t0 = pl.multiple_of(qb * QB, QB)
    near0 = pl.multiple_of(t0 + WINDOW - ATT_TK, ATT_TK)
    wlen = WINDOW + QB
    nblk = ovt_ref.shape[0]
    jb = lax.broadcasted_iota(jnp.int32, (nblk, QB), 0)
    jbf = jb.astype(F32)
    qid = jnp.right_shift(t0 + lax.broadcasted_iota(jnp.int32, (nblk, QB), 1), 6)
    forced = (jb == 0) | (jb == qid) | (jb == qid - 1)
    srow = lax.broadcasted_iota(jnp.int32, (8, R), 0)

    def finish(acc):
        return acc[0:HD, :] / acc[HD:HD + 1, :]

    def tile_part(s_tile, v_tile):
        m_t = jnp.max(s_tile, axis=0, keepdims=True)
        m_safe = jnp.where(m_t == NEG_INF, 0.0, m_t)
        p = jnp.exp2(s_tile - m_safe).astype(BF16)
        return m_t, _dot(v_tile, p)[0:HD + 8, :]

    def combine(parts):
        m_fin = parts[0][0]
        for m_t, _ in parts[1:]:
            m_fin = jnp.maximum(m_fin, m_t)
        acc = None
        for m_t, pv in parts:
            term = jnp.exp2(m_t - m_fin) * pv
            acc = term if acc is None else acc + term
        return m_fin, acc

    groups = range(gps)
    hs = [[gl * HPG + h for h in range(HPG)] for gl in groups]
    kcol = [slice(gl * 128, (gl + 1) * 128) for gl in groups]
    vrow = [slice(gl * VT_ROWS, (gl + 1) * VT_ROWS) for gl in groups]

    def stationary(q_all, hi_lo, block_rows):
        pad = jnp.where(srow == 0, MASK_BIG, 0.0)
        rest = jnp.zeros((128 - HD - KEY_PAD - 8, R), F32)
        tail = jnp.concatenate([hi_lo, block_rows, pad, rest], axis=0).astype(BF16)
        return jnp.concatenate([q_all, tail], axis=0)

    q_alls, hi_los, qms = [], [], []
    for gl in groups:
        q_alls.append(_heads_on_lanes([qt_ref[h * HD:(h + 1) * HD, :] for h in hs[gl]]))
        far = _heads_on_lanes([far_ref[h, 0:1, :] for h in hs[gl]])
        hi = far.astype(BF16).astype(F32)
        hi_los.append(jnp.where(srow == 0, hi, jnp.where(srow == 1, far - hi, 0.0)))
        qms.append(stationary(q_alls[gl], hi_los[gl], jnp.zeros((nblk, R), F32)))

    s_cs = [_dot(kc_ref[:, kcol[gl]], qms[gl]) for gl in groups]
    s_ws = [_dot(kw_ref[pl.ds(t0, wlen), kcol[gl]], qms[gl]) for gl in groups]
    s_ns = [_dot(ks_ref[pl.ds(near0, 2 * ATT_TK), kcol[gl]], qms[gl]) for gl in groups]

    o_cmps, scores = [], []
    for gl in groups:
        s_c = s_cs[gl] + _heads_on_lanes([bct_ref[h] for h in hs[gl]])
        m_c = jnp.max(s_c, axis=0, keepdims=True)
        m_c = jnp.where(m_c == NEG_INF, 0.0, m_c)
        e_c = jnp.exp2(s_c - m_c)
        p_c = e_c / jnp.maximum(jnp.sum(e_c, axis=0, keepdims=True), jnp.finfo(F32).tiny)
        o_cmps.append(_dot(vct_ref[gl * HD:(gl + 1) * HD, :], p_c.astype(BF16)))
        p_sum = p_c[:, 0:QB]
        for h in range(1, HPG):
            p_sum = p_sum + p_c[:, h * QB:(h + 1) * QB]
        p_hi = p_sum.astype(BF16)
        p_lo = (p_sum - p_hi.astype(F32)).astype(BF16)
        imp = _dot(ovt_ref[...], jnp.concatenate([p_hi, p_lo], axis=0))
        score = jnp.where(forced, FORCE_SCORE, imp)
        scores.append(jnp.where(jb <= qid, score, NEG_INF))

    nwt = wlen // ATT_TK
    o_wins = []
    for gl in groups:
        parts = []
        for i in range(nwt):
            d = nwt - 1 - i
            s_t = s_ws[gl][i * ATT_TK:(i + 1) * ATT_TK]
            if d in (0, 1):
                s_t = s_t + _heads_on_lanes([tp_ref[h, d] for h in hs[gl]])
            elif d == nwt - 1:
                s_t = s_t + _heads_on_lanes([tp_ref[h, 2] for h in hs[gl]])
            parts.append(tile_part(s_t, vwt_ref[vrow[gl], pl.ds(t0 + i * ATT_TK, ATT_TK)]))
        o_wins.append(finish(combine(parts)[1]))

    msels = [jnp.full((nblk, QB), NEG_INF, F32) for _ in groups]
    for _ in range(SLC_TOPK):
        for gl in groups:
            mx = jnp.max(scores[gl], axis=0, keepdims=True)
            first = jnp.min(jnp.where(scores[gl] == mx, jbf, float(nblk)), axis=0, keepdims=True)
            pick = jbf == first
            msels[gl] = jnp.where(pick & (mx > NEG_INF), 0.0, msels[gl])
            scores[gl] = jnp.where(pick, NEG_INF, scores[gl])
    carry0 = []
    for gl in groups:
        prev_mask = jnp.concatenate(
            [jnp.broadcast_to(jnp.max(jnp.where(jb == 2 * qb - 2 + r, msels[gl], NEG_INF), axis=0,
                                      keepdims=True), (SLC_BLOCK, QB)) for r in range(2)], axis=0)
        s_prev = s_ns[gl][0:ATT_TK] + _heads_on_lanes([tp_ref[h, 1] + prev_mask for h in hs[gl]])
        s_diag = s_ns[gl][ATT_TK:] + _heads_on_lanes([tp_ref[h, 0] for h in hs[gl]])
        carry0.append(combine([
            tile_part(s_diag, vst_ref[vrow[gl], pl.ds(near0 + ATT_TK, ATT_TK)]),
            tile_part(s_prev, vst_ref[vrow[gl], pl.ds(near0, ATT_TK)])]))

    qss = [stationary(q_alls[gl], hi_los[gl], _heads_on_lanes(
        [jnp.where(msels[gl] == 0.0, 0.0, MASK_BIG)] * HPG)) for gl in groups]

    def far_step(c, carry):
        tile0 = qb + (WINDOW - ATT_TK) // ATT_TK - (c + 1) * (FAR_CHUNK // ATT_TK)
        row0 = pl.multiple_of(tile0 * ATT_TK, ATT_TK)
        s_fs = [_dot(ks_ref[pl.ds(row0, FAR_CHUNK), kcol[gl]], qss[gl]) for gl in groups]
        out = []
        for gl in groups:
            parts = [carry[gl]]
            for j in range(FAR_CHUNK // ATT_TK):
                parts.append(tile_part(s_fs[gl][j * ATT_TK:(j + 1) * ATT_TK],
                                       vst_ref[vrow[gl], pl.ds(row0 + j * ATT_TK, ATT_TK)]))
            out.append(combine(parts))
        return tuple(out)

    n_far = (qb + 2) // 4
    sel = lax.fori_loop(0, n_far, far_step, tuple(carry0))

    gates = jax.nn.sigmoid(gt_ref[...])
    for gl in groups:
        o_cmp, o_win = o_cmps[gl], o_wins[gl]
        o_sel = finish(sel[gl][1])
        outs = []
        for h in range(HPG):
            lanes = slice(h * QB, (h + 1) * QB)
            r = gl * 16 + 3 * h
            outs.append(gates[r:r + 1, :] * o_cmp[:, lanes]
                        + gates[r + 1:r + 2, :] * o_sel[:, lanes]
                        + gates[r + 2:r + 3, :] * o_win[:, lanes])
        for pair in range(HPG // 2):
            two = jnp.concatenate(outs[2 * pair:2 * pair + 2], axis=0)
            c0 = gl * HPG * HD + pair * 2 * HD
            out_ref[:, c0:c0 + 2 * HD] = two.T.astype(out_ref.dtype)


def _overlap_matrix_t(nblk):
    start = np.arange(NCMP_PAD) * CMP_STRIDE
    sj = np.arange(nblk) * SLC_BLOCK
    ov = (np.minimum(start[None, :] + CMP_BLOCK, sj[:, None] + SLC_BLOCK)
          - np.maximum(start[None, :], sj[:, None]))
    ov = np.clip(ov, 0, None) / CMP_BLOCK
    ov[:, NCMP_PAD - 1] = 0.0
    return np.concatenate([ov, ov], axis=1).astype(np.float32)


def nsa_attn(qt, gt, kc, vct, ks, kw, vst, vwt, bias_ct, tiles, far, batch, seq, gps=ATT_GPS):
    QB, HD, HPG, G = ATT_QB, NSA_HD, NSA_HPG, NSA_GROUPS
    nblk = seq // SLC_BLOCK
    assert nblk % 8 == 0 and seq % QB == 0 and seq // CMP_STRIDE == NCMP_PAD and G % gps == 0
    assert WINDOW % ATT_TK == 0 and FAR_CHUNK == 4 * ATT_TK and KV_PAD >= FAR_CHUNK - ATT_TK
    assert KEY_BLK % 8 == 0 and KEY_BLK + nblk <= KEY_PAD and KEY_PAD + 8 <= 128 - HD
    nq = seq // QB
    sp = seq + KV_PAD
    ovt = jnp.asarray(_overlap_matrix_t(nblk), dtype=BF16)
    return pl.pallas_call(
        functools.partial(_nsa_body, gps),
        grid=(batch, G // gps, nq),
        in_specs=[pl.BlockSpec((None, gps * HPG * HD, QB), lambda b, g, i: (b, g, i)),
                  pl.BlockSpec((None, gps * 16, QB), lambda b, g, i: (b, g, i)),
                  pl.BlockSpec((None, NCMP_PAD, gps * 128), lambda b, g, i: (b, 0, g)),
                  pl.BlockSpec((None, gps * HD, NCMP_PAD), lambda b, g, i: (b, g, 0)),
                  pl.BlockSpec((None, sp, gps * 128), lambda b, g, i: (b, 0, g)),
                  pl.BlockSpec((None, sp, gps * 128), lambda b, g, i: (b, 0, g)),
                  pl.BlockSpec((None, gps * VT_ROWS, sp), lambda b, g, i: (b, g, 0)),
                  pl.BlockSpec((None, gps * VT_ROWS, sp), lambda b, g, i: (b, g, 0)),
                  pl.BlockSpec((gps * HPG, NCMP_PAD, QB), lambda b, g, i: (g, 0, i)),
                  pl.BlockSpec((gps * HPG, 3, ATT_TK, QB), lambda b, g, i: (g, 0, 0, 0)),
                  pl.BlockSpec((gps * HPG, 8, 128), lambda b, g, i: (g, 0, 0)),
                  pl.BlockSpec((nblk, 2 * NCMP_PAD), lambda b, g, i: (0, 0))],
        out_specs=pl.BlockSpec((QB, gps * HPG * HD), lambda b, g, i: (b * nq + i, g)),
        out_shape=jax.ShapeDtypeStruct((batch * seq, NSA_HEADS * HD), BF16),
        compiler_params=_cparams(("parallel", "parallel", "arbitrary")),
        name="nsa_attn",
    )(qt, gt, kc, vct, ks, kw, vst, vwt, bias_ct, tiles, far, ovt)


def _mlstm_layer(x2d, g_mix, sh, sc, w_stack, layer, b_if, g_out, batch, seq):
    proj, gates = mlstm_proj(x2d, g_mix, sh, sc, w_stack, layer, b_if)
    return mlstm_core(proj, gates, g_out, batch, seq)


def _nsa_shared(x2d, g_kv, kv_sh, kv_sc, w_kv, pos_k, w_k1, w_k2, pos_v, w_v1, w_v2,
                g_knorm, batch, seq):
    gw = NSA_GROUPS * NSA_HD
    part = lambda i: w_kv[:, i * gw:(i + 1) * gw]
    hd = NSA_HD
    wk = jnp.concatenate([p[:, g * hd:(g + 1) * hd] for g in range(NSA_GROUPS)
                          for p in (part(2), part(4))], axis=1).astype(BF16)
    gk = jnp.tile(jnp.concatenate([g_knorm[1], g_knorm[2]]), NSA_GROUPS).reshape(1, -1)
    wvt = jnp.concatenate([part(3), part(5)], axis=1).T.astype(BF16)
    wcc = jnp.concatenate([part(0), part(1)], axis=1).astype(BF16)
    ks, kw, vst, vwt, cc = kv_proj(x2d, g_kv, kv_sh, kv_sc, wk, wvt, wcc, gk, batch, seq)
    pos = jnp.stack([pos_k, pos_v])
    w1 = jnp.stack([w_k1, w_v1])
    kc, vct = compress(cc, pos, w1, w_k2.astype(BF16), w_v2.T.astype(BF16), g_knorm[0:1],
                       batch, seq)
    return kc, vct, ks, kw, vst, vwt


def _gate_weights_t(w_q, b_gate):
    nq = NSA_HEADS * NSA_HD
    per = 3 * NSA_HPG
    wg = w_q[:, nq:].T.reshape(NSA_GROUPS, per, -1)
    wg = jnp.pad(wg, ((0, 0), (0, 16 - per), (0, 0))).reshape(NSA_GROUPS * 16, -1)
    bg = jnp.pad(b_gate.reshape(NSA_GROUPS, per), ((0, 0), (0, 16 - per))).reshape(-1, 1)
    return wg.astype(BF16), bg


def _nsa_layer(x2d, g_mix, sh, sc, shared, w_q, b_gate, g_qnorm, bias_ct, tiles, far, batch, seq):
    nq = NSA_HEADS * NSA_HD
    wgt, bg = _gate_weights_t(w_q, b_gate)
    qt, gt = q_proj(x2d, g_mix, sh, sc, w_q[:, :nq].T.astype(BF16), wgt, bg,
                    g_qnorm.reshape(NSA_HD, 1), batch, seq)
    kc, vct, ks, kw, vst, vwt = shared
    return nsa_attn(qt, gt, kc, vct, ks, kw, vst, vwt, bias_ct, tiles, far, batch, seq)


def kernel(x, c, w_ada, b_ada, g_norm_mix, g_norm_ffn, w_ffn_in, w_ffn_out, w_a_in, b_a_if, g_a_out, w_a_out, w_kv_ada, b_kv_ada, g_kv_norm, w_kv, pos_cmp_k, w_cmp_k1, w_cmp_k2, pos_cmp_v, w_cmp_v1, w_cmp_v2, g_knorm, w_b_q, b_b_gate, g_qnorm, w_b_out, rel_table):
    B, S, D = x.shape
    depth = w_ada.shape[0]
    n_a = w_a_in.shape[0]
    x2d = x.reshape(B * S, D)
    mods = ada_mod(c, w_ada, b_ada)
    kv_mod = ada_mod(c, w_kv_ada[None], b_kv_ada[None])[0]
    shared = None
    bias_ct = tiles = far = None
    for layer in range(depth):
        sh1, sc1, ga1, sh2, sc2, ga2 = [mods[layer, :, i * D:(i + 1) * D].reshape(B, 1, D)
                                        for i in range(6)]
        if layer < n_a:
            mixed = _mlstm_layer(x2d, g_norm_mix[layer], sh1, sc1, w_a_in, layer, b_a_if[layer],
                                 g_a_out[layer], B, S)
            w_mix = (w_a_out, layer)
        else:
            j = layer - n_a
            if shared is None:
                kv_sh = kv_mod[:, :D].reshape(B, 1, D)
                kv_sc = kv_mod[:, D:].reshape(B, 1, D)
                shared = _nsa_shared(x2d, g_kv_norm, kv_sh, kv_sc, w_kv, pos_cmp_k, w_cmp_k1,
                                     w_cmp_k2, pos_cmp_v, w_cmp_v1, w_cmp_v2, g_knorm, B, S)
                bias_ct, tiles, far = bias_prep(rel_table, S)
            mixed = _nsa_layer(x2d, g_norm_mix[layer], sh1, sc1, shared, w_b_q[j], b_b_gate[j],
                               g_qnorm[j], bias_ct, tiles, far, B, S)
            w_mix = (w_b_out, j)
        x2d = mix_ffn(mixed, w_mix, x2d, ga1, g_norm_ffn[layer], sh2, sc2, ga2,
                      (w_ffn_in, layer), (w_ffn_out, layer))
    return x2d.reshape(B, S, D)
```

```python
import functools
import math

import jax
import jax.numpy as jnp
import numpy as np
from jax import lax
from jax.experimental import pallas as pl
from jax.experimental.pallas import tpu as pltpu

F32 = jnp.float32
BF16 = jnp.bfloat16
NEG_INF = float("-inf")
LOG2E = math.log2(math.e)

RMS_EPS = 1e-6

ML_HEADS = 4
ML_DQK = 128
ML_DV = 256
ML_LC = 256
ML_CPS = 4

NSA_HEADS = 16
NSA_GROUPS = 4
NSA_HPG = 4
NSA_HD = 64
CMP_BLOCK = 32
CMP_STRIDE = 16
SLC_BLOCK = 64
SLC_TOPK = 8
WINDOW = 512
FORCE_SCORE = 1e4
REL_BUCKETS = 32
REL_MAX_DIST = 128
ATT_QB = 128
ATT_TK = 128
ATT_GPS = 4
NCMP_PAD = 128
KV_PAD = WINDOW
FAR_CHUNK = 512
VT_ROWS = 80
KEY_BLK = 8
KEY_PAD = 40
MASK_BIG = -32768.0

VMEM_LIMIT = 56 * 1024 * 1024


def _cparams(sem):
    return pltpu.CompilerParams(dimension_semantics=sem, vmem_limit_bytes=VMEM_LIMIT)


def _dot(a, b):
    return jnp.dot(a, b, preferred_element_type=F32)


def _dot_nt(a, b):
    return lax.dot_general(a, b, (((1,), (1,)), ((), ())), preferred_element_type=F32)


def _dot_tn(a, b):
    return lax.dot_general(a, b, (((0,), (0,)), ((), ())), preferred_element_type=F32)


def _norm_mod(x, g, sh, sc):
    var = jnp.mean(x * x, axis=-1, keepdims=True)
    y = x * lax.rsqrt(var + RMS_EPS) * g
    return y * (1.0 + sc) + sh


def _ada_body(c_ref, w_ref, b_ref, o_ref):
    c = c_ref[...]
    ca = c * jax.nn.sigmoid(c)
    nb = ca.shape[0]
    a_hi = ca.astype(BF16).astype(F32)
    a_mid = (ca - a_hi).astype(BF16).astype(F32)
    a_lo = ca - a_hi - a_mid
    a3 = jnp.concatenate([a_hi, a_mid, a_lo, jnp.zeros_like(ca)], axis=0).astype(BF16)
    w = w_ref[...]
    w_hi = w.astype(BF16)
    w_lo = (w - w_hi.astype(F32)).astype(BF16)
    p = _dot(a3, w_hi)
    q = _dot(a3[0:2 * nb], w_lo)
    o_ref[...] = (p[0:nb] + p[nb:2 * nb] + p[2 * nb:3 * nb] + q[0:nb] + q[nb:2 * nb]) + b_ref[...]


def ada_mod(c, w, b, tn=2048):
    L, D, N = w.shape
    B = c.shape[0]
    return pl.pallas_call(
        _ada_body,
        grid=(L, N // tn),
        in_specs=[pl.BlockSpec((B, D), lambda l, j: (0, 0)),
                  pl.BlockSpec((None, D, tn), lambda l, j: (l, 0, j)),
                  pl.BlockSpec((None, 1, tn), lambda l, j: (l, 0, j))],
        out_specs=pl.BlockSpec((None, B, tn), lambda l, j: (l, 0, j)),
        out_shape=jax.ShapeDtypeStruct((L, B, N), F32),
        compiler_params=_cparams(("parallel", "parallel")),
        name="ada_mod",
    )(c, w, b.reshape(L, 1, N))


def _mlstm_proj_body(ng, tn, x_ref, g_ref, sh_ref, sc_ref, w_ref, wg_ref, b_ref,
                     q_ref, k_ref, v_ref, o_ref, gt_ref):
    tm = x_ref.shape[0]
    half = tm // 2
    for r0 in (0, half):
        rows = slice(r0, r0 + half)
        h = _norm_mod(x_ref[rows, :], g_ref[...], sh_ref[...], sc_ref[...]).astype(BF16)
        n0 = 0
        for p_ref in (q_ref, k_ref, v_ref, o_ref):
            for c0 in range(0, p_ref.shape[1], tn):
                p_ref[rows, c0:c0 + tn] = _dot(
                    h, w_ref[:, n0 + c0:n0 + c0 + tn].astype(BF16)).astype(p_ref.dtype)
            n0 += p_ref.shape[1]
        gates = _dot(h, wg_ref[:, 0:ng].astype(BF16)) + b_ref[...]
        gt_ref[rows, 0:ng] = gates
        gt_ref[rows, ng:] = jnp.zeros((half, gt_ref.shape[1] - ng), F32)


def mlstm_proj(x2d, g, sh, sc, w_stack, layer, b_if, tm=1024, tn=512):
    M, D = x2d.shape
    B = sh.shape[0]
    ng = b_if.shape[0]
    nbig = w_stack.shape[2] - ng
    widths = (ML_HEADS * ML_DQK, ML_HEADS * ML_DQK, ML_HEADS * ML_DV, ML_HEADS * ML_DV)
    assert nbig == sum(widths) and ng <= 128
    tiles_per_batch = (M // B) // tm
    bvec = pl.BlockSpec((None, 1, D), lambda i: (i // tiles_per_batch, 0, 0))
    return pl.pallas_call(
        functools.partial(_mlstm_proj_body, ng, tn),
        grid=(M // tm,),
        in_specs=[pl.BlockSpec((tm, D), lambda i: (i, 0)),
                  pl.BlockSpec((1, D), lambda i: (0, 0)), bvec, bvec,
                  pl.BlockSpec((None, D, nbig), lambda i: (layer, 0, 0),
                               pipeline_mode=pl.Buffered(1)),
                  pl.BlockSpec((None, D, 128), lambda i: (layer, 0, nbig // 128),
                               pipeline_mode=pl.Buffered(1)),
                  pl.BlockSpec((1, ng), lambda i: (0, 0))],
        out_specs=[pl.BlockSpec((tm, n), lambda i: (i, 0)) for n in widths + (128,)],
        out_shape=[jax.ShapeDtypeStruct((M, n), BF16) for n in widths]
                  + [jax.ShapeDtypeStruct((M, 128), F32)],
        compiler_params=_cparams(("parallel",)),
        name="mlstm_proj",
    )(x2d, g.reshape(1, D), sh, sc, w_stack, w_stack, b_if.reshape(1, ng))


def _mix_ffn_body(tf, a_ref, wm_ref, x_ref, ga1_ref, g_ref, sh_ref, sc_ref, ga2_ref,
                  wi_ref, wo_ref, o_ref, act_s):
    F = wo_ref.shape[0]
    x1 = x_ref[...] + ga1_ref[...] * _dot(a_ref[...], wm_ref[...].astype(BF16))
    h = _norm_mod(x1, g_ref[...], sh_ref[...], sc_ref[...]).astype(BF16)
    for f0 in range(0, F, tf):
        gate = _dot(h, wi_ref[:, f0:f0 + tf].astype(BF16))
        up = _dot(h, wi_ref[:, F + f0:F + f0 + tf].astype(BF16))
        act_s[:, f0:f0 + tf] = (gate * jax.nn.sigmoid(gate) * up).astype(BF16)
    y = _dot(act_s[:, 0:tf], wo_ref[0:tf, :].astype(BF16))
    for f0 in range(tf, F, tf):
        y = y + _dot(act_s[:, f0:f0 + tf], wo_ref[f0:f0 + tf, :].astype(BF16))
    o_ref[...] = x1 + ga2_ref[...] * y


def mix_ffn(a, w_mix, x2d, ga1, g, sh, sc, ga2, w_in, w_out, tm=512, tf=256):
    M, D = x2d.shape
    K = a.shape[1]
    F = w_out[0].shape[1]
    B = sh.shape[0]
    tiles_per_batch = (M // B) // tm
    bvec = pl.BlockSpec((None, 1, D), lambda i: (i // tiles_per_batch, 0, 0))

    def resident(stacked_layer):
        w, layer = stacked_layer
        return pl.BlockSpec((None,) + w.shape[1:], lambda i: (layer, 0, 0),
                            pipeline_mode=pl.Buffered(1))

    return pl.pallas_call(
        functools.partial(_mix_ffn_body, tf),
        grid=(M // tm,),
        in_specs=[pl.BlockSpec((tm, K), lambda i: (i, 0)),
                  resident(w_mix),
                  pl.BlockSpec((tm, D), lambda i: (i, 0)),
                  bvec,
                  pl.BlockSpec((1, D), lambda i: (0, 0)),
                  bvec, bvec, bvec,
                  resident(w_in), resident(w_out)],
        out_specs=pl.BlockSpec((tm, D), lambda i: (i, 0)),
        out_shape=jax.ShapeDtypeStruct((M, D), F32),
        scratch_shapes=[pltpu.VMEM((tm, F), BF16)],
        compiler_params=_cparams(("parallel",)),
        name="mix_ffn",
    )(a, w_mix[0], x2d, ga1, g.reshape(1, D), sh, sc, ga2, w_in[0], w_out[0])


def _sublane_scan(x, op, fill):
    n = x.shape[0]
    row = lax.broadcasted_iota(jnp.int32, x.shape, 0)
    sh = 1
    while sh < n:
        x = op(x, jnp.where(row >= sh, pltpu.roll(x, sh, axis=0), fill))
        sh *= 2
    return x


def _mlstm_body(q_ref, k_ref, v_ref, o_ref, gc_ref, gout_ref, out_ref, c_s, n_s, m_s):
    @pl.when(pl.program_id(1) == 0)
    def _():
        c_s[...] = jnp.zeros_like(c_s)
        n_s[...] = jnp.zeros_like(n_s)
        m_s[...] = jnp.zeros_like(m_s)

    for j in range(q_ref.shape[0] // ML_LC):
        rows = pl.ds(j * ML_LC, ML_LC)
        _mlstm_chunk(q_ref.at[rows], k_ref.at[rows], v_ref.at[rows], o_ref.at[rows], gc_ref.at[rows],
                     gout_ref, out_ref.at[rows], c_s, n_s, m_s)


def _mlstm_chunk(q_ref, k_ref, v_ref, o_ref, gc_ref, gout_ref, out_ref, c_s, n_s, m_s):
    LC = q_ref.shape[0]
    NH = ML_HEADS
    scale = ML_DQK ** -0.5
    log_scale = math.log(scale)

    gc = gc_ref[...]
    logf = jnp.minimum(gc, 0.0) - jnp.log1p(jnp.exp(-jnp.abs(gc)))
    bcum = pltpu.roll(_sublane_scan(logf, jnp.add, 0.0), 128 - NH, axis=1)
    a_c = gc - bcum
    cmax = _sublane_scan(a_c, jnp.maximum, NEG_INF)
    a_t = a_c.T

    row = lax.broadcasted_iota(jnp.int32, (LC, LC), 0)
    col = lax.broadcasted_iota(jnp.int32, (LC, LC), 1)
    causal = col <= row
    ones_col = jnp.ones((LC, 128), BF16)

    def lanes2(x):
        return jnp.concatenate([x, x], axis=1)

    for h in range(NH):
        a_rep = jnp.broadcast_to(a_c[:, h:h + 1], (LC, 128))
        b_rep = jnp.broadcast_to(bcum[:, h:h + 1], (LC, 128))
        cm_rep = jnp.broadcast_to(cmax[:, h:h + 1], (LC, 128))
        a_r = a_t[h:h + 1, :] + log_scale
        b_last = b_rep[LC - 1:LC, :]
        m_prev = m_s[h:h + 1, :]
        m_new = jnp.maximum(b_last + m_prev, b_last + cm_rep[LC - 1:LC, :])
        decay = jnp.exp(b_last + m_prev - m_new)
        e_rep = jnp.exp(b_last + a_rep - m_new)
        g_rep = jnp.maximum(m_prev, cm_rep)
        w_intra = jnp.exp(jnp.where(causal, a_r - lanes2(g_rep), NEG_INF))
        w_inter = jnp.exp(m_prev - g_rep) * scale
        floor = jnp.exp(-(b_rep + g_rep))

        qh = q_ref[:, h * ML_DQK:(h + 1) * ML_DQK]
        kh = k_ref[:, h * ML_DQK:(h + 1) * ML_DQK]
        vh = v_ref[:, h * ML_DV:(h + 1) * ML_DV]
        c_prev = c_s[h]
        n_prev = n_s[h:h + 1, :]

        s = (_dot_nt(qh, kh) * w_intra).astype(BF16)
        inter = _dot(qh, c_prev.astype(BF16))
        num = _dot(s, vh) + lanes2(w_inter) * inter
        n_rows = jnp.broadcast_to(n_prev, (128, ML_DQK)).astype(BF16)
        qn = _dot(s, ones_col) + w_inter * _dot_nt(qh, n_rows)
        inv = 1.0 / jnp.maximum(jnp.abs(qn), floor)

        ke = kh.astype(F32) * e_rep
        c_s[h] = lanes2(decay) * c_prev + _dot_tn(ke.astype(BF16), vh)
        n_s[h:h + 1, :] = decay * n_prev + jnp.sum(ke, axis=0, keepdims=True)
        m_s[h:h + 1, :] = m_new

        ssq = _dot((num * num).astype(BF16), jnp.ones((ML_DV, 128), BF16))
        rs = lax.rsqrt(ssq * (inv * inv) * (1.0 / ML_DV) + RMS_EPS) * inv
        hn = num * lanes2(rs) * gout_ref[:, h * ML_DV:(h + 1) * ML_DV]
        og = jax.nn.sigmoid(o_ref[:, h * ML_DV:(h + 1) * ML_DV])
        out_ref[:, h * ML_DV:(h + 1) * ML_DV] = hn.astype(BF16) * og


def mlstm_core(q, k, v, o, gates, g_out, batch, seq):
    LC = ML_LC * ML_CPS
    nc = seq // LC
    qk = ML_HEADS * ML_DQK
    vd = ML_HEADS * ML_DV
    row = lambda b, c: b * nc + c
    return pl.pallas_call(
        _mlstm_body,
        grid=(batch, nc),
        in_specs=[pl.BlockSpec((LC, qk), lambda b, c: (row(b, c), 0)),
                  pl.BlockSpec((LC, qk), lambda b, c: (row(b, c), 0)),
                  pl.BlockSpec((LC, vd), lambda b, c: (row(b, c), 0)),
                  pl.BlockSpec((LC, vd), lambda b, c: (row(b, c), 0)),
                  pl.BlockSpec((LC, 128), lambda b, c: (row(b, c), 0)),
                  pl.BlockSpec((1, vd), lambda b, c: (0, 0))],
        out_specs=pl.BlockSpec((LC, vd), lambda b, c: (row(b, c), 0)),
        out_shape=jax.ShapeDtypeStruct((batch * seq, vd), BF16),
        scratch_shapes=[pltpu.VMEM((ML_HEADS, ML_DQK, ML_DV), F32),
                        pltpu.VMEM((8, ML_DQK), F32),
                        pltpu.VMEM((8, 128), F32)],
        compiler_params=_cparams(("parallel", "arbitrary")),
        name="mlstm_core",
    )(q, k, v, o, gates, g_out.reshape(1, vd))


def _kv_proj_body(x_ref, g_ref, sh_ref, sc_ref, wk_ref, wvt_ref, wcc_ref, gk_ref, bd_ref,
                  ks_ref, kw_ref, vst_ref, vwt_ref, cc_ref):
    i = pl.program_id(1)
    G, HD = NSA_GROUPS, NSA_HD
    tm = x_ref.shape[0]
    lane = lax.broadcasted_iota(jnp.int32, (tm, G * 128), 1) % 128

    @pl.when(i == 0)
    def _():
        pad_rows = jnp.where(lane == HD + KEY_PAD, 1.0, 0.0).astype(BF16)
        ks_ref[...] = pad_rows
        kw_ref[...] = pad_rows
        vst_ref[...] = jnp.zeros_like(vst_ref)
        vwt_ref[...] = jnp.zeros_like(vwt_ref)

    @pl.when(i > 0)
    def _():
        h = _norm_mod(x_ref[...], g_ref[...], sh_ref[...], sc_ref[...]).astype(BF16)
        cc = _dot(h, wcc_ref[...])
        for q in range(cc_ref.shape[0]):
            cc_ref[q] = cc[:, q * 128:(q + 1) * 128]
        kk = _dot(h, wk_ref[...])
        ssq = _dot((kk * kk).astype(BF16), bd_ref[...])
        kn = kk * lax.rsqrt(ssq * (1.0 / HD) + RMS_EPS) * gk_ref[...]
        blk = (i - 1) * (tm // SLC_BLOCK) + lax.broadcasted_iota(
            jnp.int32, (tm, G * 128), 0) // SLC_BLOCK
        tail = jnp.where((lane == HD) | (lane == HD + 1) | (lane == HD + KEY_BLK + blk), 1.0, 0.0)
        ks_ref[...] = jnp.where(lane < HD, kn, tail).astype(BF16)
        kn_sw = jnp.concatenate(
            [pltpu.roll(kn[:, g * 128:(g + 1) * 128], HD, axis=1) for g in range(G)], axis=1)
        kw_ref[...] = jnp.where(lane < HD, kn_sw, tail).astype(BF16)
        vt = _dot_nt(wvt_ref[...], h)
        srow = lax.broadcasted_iota(jnp.int32, (VT_ROWS - HD, tm), 0)
        ones_blk = jnp.where(srow == 0, 1.0, 0.0).astype(BF16)
        for kind, o_ref in enumerate((vst_ref, vwt_ref)):
            for g in range(G):
                r0 = (kind * G + g) * HD
                o_ref[g * VT_ROWS:g * VT_ROWS + HD, :] = vt[r0:r0 + HD, :].astype(BF16)
                o_ref[g * VT_ROWS + HD:(g + 1) * VT_ROWS, :] = ones_blk


def _segment_ones(width, seg):
    idx = np.arange(width) // seg
    return (idx[:, None] == idx[None, :]).astype(np.float32)


def kv_proj(x2d, g, sh, sc, wk, wvt, wcc, gk, batch, seq, tm=KV_PAD):
    assert tm == KV_PAD and seq % tm == 0
    M, D = x2d.shape
    G = NSA_GROUPS
    nt = seq // tm
    sp = seq + KV_PAD
    ncs = wcc.shape[1] // 128
    bd = jnp.asarray(_segment_ones(G * 128, NSA_HD), dtype=BF16)
    xrow = lambda b, i: (b * nt + jnp.maximum(i - 1, 0), 0)
    bvec = pl.BlockSpec((None, 1, D), lambda b, i: (b, 0, 0))
    full = lambda a: pl.BlockSpec(a.shape, lambda b, i: (0,) * a.ndim)
    return pl.pallas_call(
        _kv_proj_body,
        grid=(batch, nt + 1),
        in_specs=[pl.BlockSpec((tm, D), xrow),
                  pl.BlockSpec((1, D), lambda b, i: (0, 0)),
                  bvec, bvec, full(wk), full(wvt), full(wcc), full(gk), full(bd)],
        out_specs=[pl.BlockSpec((None, tm, G * 128), lambda b, i: (b, i, 0)),
                   pl.BlockSpec((None, tm, G * 128), lambda b, i: (b, i, 0)),
                   pl.BlockSpec((None, G * VT_ROWS, tm), lambda b, i: (b, 0, i)),
                   pl.BlockSpec((None, G * VT_ROWS, tm), lambda b, i: (b, 0, i)),
                   pl.BlockSpec((ncs, tm, 128),
                                lambda b, i: (0, b * nt + jnp.maximum(i - 1, 0), 0))],
        out_shape=[jax.ShapeDtypeStruct((batch, sp, G * 128), BF16),
                   jax.ShapeDtypeStruct((batch, sp, G * 128), BF16),
                   jax.ShapeDtypeStruct((batch, G * VT_ROWS, sp), BF16),
                   jax.ShapeDtypeStruct((batch, G * VT_ROWS, sp), BF16),
                   jax.ShapeDtypeStruct((ncs, M, 128), F32)],
        compiler_params=_cparams(("parallel", "arbitrary")),
        name="kv_proj",
    )(x2d, g.reshape(1, D), sh, sc, wk, wvt, wcc, gk, bd)


def _gelu_tanh(x):
    c = math.sqrt(2.0 / math.pi)
    return 0.5 * x * (1.0 + jnp.tanh(c * (x + 0.044715 * (x * x * x))))


def _compress_body(cc0_ref, cc1_ref, cc2_ref, cc3_ref, pos_ref, w1_ref, w2_ref, w2t_ref, g_ref,
                   kc_ref, vct_ref):
    HD = NSA_HD
    nwin = kc_ref.shape[0]
    hid = w2_ref.shape[0]
    for p, cc_ref in enumerate((cc0_ref, cc1_ref, cc2_ref, cc3_ref)):
        kind = p // 2
        x = jnp.concatenate([cc_ref[pl.ds(i, nwin, stride=CMP_STRIDE), :]
                             for i in range(CMP_STRIDE)], axis=1)
        u = _dot((x + pos_ref[kind, 0:1, :]).astype(BF16), w1_ref[kind, 0])
        v = _dot((x + pos_ref[kind, 1:2, :]).astype(BF16), w1_ref[kind, 1])
        pre = u + pltpu.roll(v, nwin - 1, axis=0)
        for s in range(2):
            g = 2 * (p % 2) + s
            hmid = _gelu_tanh(pre[:, s * hid:(s + 1) * hid]).astype(BF16)
            if kind == 0:
                y = _dot(hmid, w2_ref[...])
                var = jnp.mean(y * y, axis=-1, keepdims=True)
                yn = y * lax.rsqrt(var + RMS_EPS) * g_ref[...]
                kc_ref[:, g * 128:g * 128 + HD] = yn.astype(BF16)
                kc_ref[:, g * 128 + HD:(g + 1) * 128] = jnp.zeros((nwin, 128 - HD), BF16)
            else:
                vct_ref[g * HD:(g + 1) * HD, :] = _dot_nt(w2t_ref[...], hmid).astype(BF16)


def _pair_expand(w):
    k, t, d, c = w.shape
    eye = jnp.eye(2, dtype=w.dtype)
    return jnp.einsum("ktdc,su->ktsduc", w, eye).reshape(k, t * 2 * d, 2 * c)


def compress(cc, pos, w1, w2k, w2vt, g, batch, seq):
    G, HD = NSA_GROUPS, NSA_HD
    nwin = seq // CMP_STRIDE
    hid = w1.shape[2]
    half = CMP_BLOCK // 2
    assert half == CMP_STRIDE
    w1r = w1.reshape(2, CMP_BLOCK, HD, hid)
    w1x = jnp.stack([_pair_expand(w1r[:, :half]), _pair_expand(w1r[:, half:])], axis=1).astype(BF16)
    posr = jnp.tile(pos.reshape(2, 2, half, 1, HD), (1, 1, 1, 2, 1)).reshape(2, 2, half * 2 * HD)
    resident = lambda a: pl.BlockSpec(a.shape, lambda b: (0,) * a.ndim, pipeline_mode=pl.Buffered(1))
    return pl.pallas_call(
        _compress_body,
        grid=(batch,),
        in_specs=[pl.BlockSpec((None, seq, 128), functools.partial(lambda q, b: (q, b, 0), q))
                  for q in range(4)]
                 + [resident(posr), resident(w1x), resident(w2k), resident(w2vt), resident(g)],
        out_specs=[pl.BlockSpec((None, nwin, G * 128), lambda b: (b, 0, 0)),
                   pl.BlockSpec((None, G * HD, nwin), lambda b: (b, 0, 0))],
        out_shape=[jax.ShapeDtypeStruct((batch, nwin, G * 128), BF16),
                   jax.ShapeDtypeStruct((batch, G * HD, nwin), BF16)],
        compiler_params=_cparams(("parallel",)),
        name="compress",
    )(cc, cc, cc, cc, posr, w1x, w2k, w2vt, g)


def _t5_bucket(dist):
    n = jnp.maximum(dist, 0)
    max_exact = REL_BUCKETS // 2
    nf = jnp.maximum(n, 1).astype(F32)
    large = max_exact + (jnp.log(nf / max_exact) / math.log(REL_MAX_DIST / max_exact)
                         * (REL_BUCKETS - max_exact)).astype(jnp.int32)
    large = jnp.minimum(large, REL_BUCKETS - 1)
    return jnp.where(n < max_exact, n, large)


def _table_lookup(bucket, tab_ref, h):
    out = jnp.zeros(bucket.shape, F32)
    for k in range(REL_BUCKETS):
        out = jnp.where(bucket == k, tab_ref[k, h], out)
    return out


def _bias_prep_body(tab_ref, bc_ref, tp_ref, far_ref):
    h = pl.program_id(0)
    S = bc_ref.shape[0] * bc_ref.shape[2]
    far = tab_ref[REL_BUCKETS - 1, h]
    far_ref[...] = jnp.full(far_ref.shape, far * LOG2E, F32)

    dist = lax.broadcasted_iota(jnp.int32, (8, S), 1)
    by_dist = _table_lookup(_t5_bucket(dist), tab_ref, h) * LOG2E
    shifted = pltpu.roll(jnp.broadcast_to(by_dist[0:1, :], (NCMP_PAD, S)), 0, axis=1,
                         stride=CMP_STRIDE, stride_axis=0)
    shifted = pltpu.roll(shifted, CMP_BLOCK - 1, axis=1)
    n = lax.broadcasted_iota(jnp.int32, (NCMP_PAD, S), 0)
    t = lax.broadcasted_iota(jnp.int32, (NCMP_PAD, S), 1)
    bias_c = jnp.where(t >= n * CMP_STRIDE + CMP_BLOCK - 1, shifted, NEG_INF)
    for jq in range(bc_ref.shape[0]):
        bc_ref[jq] = bias_c[:, jq * ATT_QB:(jq + 1) * ATT_QB]
    j = lax.broadcasted_iota(jnp.int32, (ATT_TK, ATT_QB), 0)
    i = lax.broadcasted_iota(jnp.int32, (ATT_TK, ATT_QB), 1)
    for d in range(2):
        dist = d * ATT_TK + i - j
        rel = (_table_lookup(_t5_bucket(dist), tab_ref, h) - far) * LOG2E
        tp_ref[d] = jnp.where(dist >= 0, rel, NEG_INF)
    tp_ref[2] = jnp.where(i < j, 0.0, NEG_INF)


def bias_prep(rel_table, seq):
    assert ATT_TK == ATT_QB and ATT_TK + 1 > 113
    return pl.pallas_call(
        _bias_prep_body,
        grid=(NSA_HEADS,),
        in_specs=[pl.BlockSpec(memory_space=pltpu.SMEM)],
        out_specs=[pl.BlockSpec((seq // ATT_QB, None, NCMP_PAD, ATT_QB), lambda h: (0, h, 0, 0)),
                   pl.BlockSpec((None, 3, ATT_TK, ATT_QB), lambda h: (h, 0, 0, 0)),
                   pl.BlockSpec((None, 8, 128), lambda h: (h, 0, 0))],
        out_shape=[jax.ShapeDtypeStruct((seq // ATT_QB, NSA_HEADS, NCMP_PAD, ATT_QB), F32),
                   jax.ShapeDtypeStruct((NSA_HEADS, 3, ATT_TK, ATT_QB), F32),
                   jax.ShapeDtypeStruct((NSA_HEADS, 8, 128), F32)],
        compiler_params=_cparams(("parallel",)),
        name="bias_prep",
    )(rel_table)


def _q_proj_body(x_ref, g_ref, sh_ref, sc_ref, wqt_ref, wgt_ref, bg_ref, gq_ref, qt_ref, gt_ref):
    HD = NSA_HD
    tm = x_ref.shape[0]
    half = tm // 2
    scale = gq_ref[...] * (HD ** -0.5 * LOG2E)
    for r0 in (0, half):
        h = _norm_mod(x_ref[r0:r0 + half, :], g_ref[...], sh_ref[...], sc_ref[...]).astype(BF16)
        gt = _dot_nt(wgt_ref[...], h) + bg_ref[...]
        qt = _dot_nt(wqt_ref[...], h)
        for jq in range(half // ATT_QB):
            cols = slice(jq * ATT_QB, (jq + 1) * ATT_QB)
            gt_ref[r0 // ATT_QB + jq] = gt[:, cols]
        for hh in range(NSA_HEADS):
            seg = qt[hh * HD:(hh + 1) * HD, :]
            var = jnp.mean(seg * seg, axis=0, keepdims=True)
            qn = (seg * lax.rsqrt(var + RMS_EPS) * scale).astype(BF16)
            for jq in range(half // ATT_QB):
                qt_ref[r0 // ATT_QB + jq, hh * HD:(hh + 1) * HD, :] = qn[:, jq * ATT_QB:(jq + 1) * ATT_QB]


def q_proj(x2d, g, sh, sc, wqt, wgt, bg, gq, batch, seq, tm=1024):
    M, D = x2d.shape
    nt = seq // tm
    nq = wqt.shape[0]
    ng = wgt.shape[0]
    bvec = pl.BlockSpec((None, 1, D), lambda i: (i // nt, 0, 0))
    full = lambda a: pl.BlockSpec(a.shape, lambda i: (0,) * a.ndim)
    return pl.pallas_call(
        _q_proj_body,
        grid=(M // tm,),
        in_specs=[pl.BlockSpec((tm, D), lambda i: (i, 0)),
                  pl.BlockSpec((1, D), lambda i: (0, 0)),
                  bvec, bvec, full(wqt), full(wgt), full(bg), full(gq)],
        out_specs=[pl.BlockSpec((None, tm // ATT_QB, nq, ATT_QB), lambda i: (i // nt, i % nt, 0, 0)),
                   pl.BlockSpec((None, tm // ATT_QB, ng, ATT_QB), lambda i: (i // nt, i % nt, 0, 0))],
        out_shape=[jax.ShapeDtypeStruct((batch, seq // ATT_QB, nq, ATT_QB), BF16),
                   jax.ShapeDtypeStruct((batch, seq // ATT_QB, ng, ATT_QB), F32)],
        compiler_params=_cparams(("parallel",)),
        name="q_proj",
    )(x2d, g.reshape(1, D), sh, sc, wqt, wgt, bg, gq)


def _heads_on_lanes(pieces):
    return jnp.concatenate(pieces, axis=1)


def _nsa_body(gps, qt_ref, gt_ref, kc_ref, vct_ref, ks_ref, kw_ref, vst_ref, vwt_ref,
              bct_ref, tp_ref, far_ref, ovt_ref, out_ref):
    qb = pl.program_id(2)
    QB, HD, HPG = ATT_QB, NSA_HD, NSA_HPG
    R = HPG * QB
    t0 = pl.multiple_of(qb * QB, QB)
    near0 = pl.multiple_of(t0 + WINDOW - ATT_TK, ATT_TK)
    wlen = WINDOW + QB
    nblk = ovt_ref.shape[0]
    jb = lax.broadcasted_iota(jnp.int32, (nblk, QB), 0)
    jbf = jb.astype(F32)
    qid = jnp.right_shift(t0 + lax.broadcasted_iota(jnp.int32, (nblk, QB), 1), 6)
    forced = (jb == 0) | (jb == qid) | (jb == qid - 1)
    srow = lax.broadcasted_iota(jnp.int32, (8, R), 0)

    def finish(acc):
        return acc[0:HD, :] / acc[HD:HD + 1, :]

    def tile_part(s_tile, v_tile):
        m_t = jnp.max(s_tile, axis=0, keepdims=True)
        m_safe = jnp.where(m_t == NEG_INF, 0.0, m_t)
        p = jnp.exp2(s_tile - m_safe).astype(BF16)
        return m_t, _dot(v_tile, p)[0:HD + 8, :]

    def combine(parts):
        m_fin = parts[0][0]
        for m_t, _ in parts[1:]:
            m_fin = jnp.maximum(m_fin, m_t)
        acc = None
        for m_t, pv in parts:
            term = jnp.exp2(m_t - m_fin) * pv
            acc = term if acc is None else acc + term
        return m_fin, acc

    groups = range(gps)
    hs = [[gl * HPG + h for h in range(HPG)] for gl in groups]
    kcol = [slice(gl * 128, (gl + 1) * 128) for gl in groups]
    vrow = [slice(gl * VT_ROWS, (gl + 1) * VT_ROWS) for gl in groups]

    def stationary(q_all, hi_lo, block_rows):
        pad = jnp.where(srow == 0, MASK_BIG, 0.0)
        rest = jnp.zeros((128 - HD - KEY_PAD - 8, R), F32)
        tail = jnp.concatenate([hi_lo, block_rows, pad, rest], axis=0).astype(BF16)
        return jnp.concatenate([q_all, tail], axis=0)

    q_alls, hi_los, qms = [], [], []
    for gl in groups:
        q_alls.append(_heads_on_lanes([qt_ref[h * HD:(h + 1) * HD, :] for h in hs[gl]]))
        far = _heads_on_lanes([far_ref[h, 0:1, :] for h in hs[gl]])
        hi = far.astype(BF16).astype(F32)
        hi_los.append(jnp.where(srow == 0, hi, jnp.where(srow == 1, far - hi, 0.0)))
        qms.append(stationary(q_alls[gl], hi_los[gl], jnp.zeros((nblk, R), F32)))

    s_cs = [_dot(kc_ref[:, kcol[gl]], qms[gl]) for gl in groups]
    s_ws = [_dot(kw_ref[pl.ds(t0, wlen), kcol[gl]], qms[gl]) for gl in groups]
    s_ns = [_dot(ks_ref[pl.ds(near0, 2 * ATT_TK), kcol[gl]], qms[gl]) for gl in groups]

    o_cmps, scores = [], []
    for gl in groups:
        s_c = s_cs[gl] + _heads_on_lanes([bct_ref[h] for h in hs[gl]])
        m_c = jnp.max(s_c, axis=0, keepdims=True)
        m_c = jnp.where(m_c == NEG_INF, 0.0, m_c)
        e_c = jnp.exp2(s_c - m_c)
        p_c = e_c / jnp.maximum(jnp.sum(e_c, axis=0, keepdims=True), jnp.finfo(F32).tiny)
        o_cmps.append(_dot(vct_ref[gl * HD:(gl + 1) * HD, :], p_c.astype(BF16)))
        p_sum = p_c[:, 0:QB]
        for h in range(1, HPG):
            p_sum = p_sum + p_c[:, h * QB:(h + 1) * QB]
        p_hi = p_sum.astype(BF16)
        p_lo = (p_sum - p_hi.astype(F32)).astype(BF16)
        imp = _dot(ovt_ref[...], jnp.concatenate([p_hi, p_lo], axis=0))
        score = jnp.where(forced, FORCE_SCORE, imp)
        scores.append(jnp.where(jb <= qid, score, NEG_INF))

    nwt = wlen // ATT_TK
    o_wins = []
    for gl in groups:
        parts = []
        for i in range(nwt):
            d = nwt - 1 - i
            s_t = s_ws[gl][i * ATT_TK:(i + 1) * ATT_TK]
            if d in (0, 1):
                s_t = s_t + _heads_on_lanes([tp_ref[h, d] for h in hs[gl]])
            elif d == nwt - 1:
                s_t = s_t + _heads_on_lanes([tp_ref[h, 2] for h in hs[gl]])
            parts.append(tile_part(s_t, vwt_ref[vrow[gl], pl.ds(t0 + i * ATT_TK, ATT_TK)]))
        o_wins.append(finish(combine(parts)[1]))

    msels = [jnp.full((nblk, QB), NEG_INF, F32) for _ in groups]
    for _ in range(SLC_TOPK):
        for gl in groups:
            mx = jnp.max(scores[gl], axis=0, keepdims=True)
            first = jnp.min(jnp.where(scores[gl] == mx, jbf, float(nblk)), axis=0, keepdims=True)
            pick = jbf == first
            msels[gl] = jnp.where(pick & (mx > NEG_INF), 0.0, msels[gl])
            scores[gl] = jnp.where(pick, NEG_INF, scores[gl])
    carry0 = []
    for gl in groups:
        prev_mask = jnp.concatenate(
            [jnp.broadcast_to(jnp.max(jnp.where(jb == 2 * qb - 2 + r, msels[gl], NEG_INF), axis=0,
                                      keepdims=True), (SLC_BLOCK, QB)) for r in range(2)], axis=0)
        s_prev = s_ns[gl][0:ATT_TK] + _heads_on_lanes([tp_ref[h, 1] + prev_mask for h in hs[gl]])
        s_diag = s_ns[gl][ATT_TK:] + _heads_on_lanes([tp_ref[h, 0] for h in hs[gl]])
        carry0.append(combine([
            tile_part(s_diag, vst_ref[vrow[gl], pl.ds(near0 + ATT_TK, ATT_TK)]),
            tile_part(s_prev, vst_ref[vrow[gl], pl.ds(near0, ATT_TK)])]))

    qss = [stationary(q_alls[gl], hi_los[gl], _heads_on_lanes(
        [jnp.where(msels[gl] == 0.0, 0.0, MASK_BIG)] * HPG)) for gl in groups]

    def far_step(c, carry):
        tile0 = qb + (WINDOW - ATT_TK) // ATT_TK - (c + 1) * (FAR_CHUNK // ATT_TK)
        row0 = pl.multiple_of(tile0 * ATT_TK, ATT_TK)
        s_fs = [_dot(ks_ref[pl.ds(row0, FAR_CHUNK), kcol[gl]], qss[gl]) for gl in groups]
        out = []
        for gl in groups:
            parts = [carry[gl]]
            for j in range(FAR_CHUNK // ATT_TK):
                parts.append(tile_part(s_fs[gl][j * ATT_TK:(j + 1) * ATT_TK],
                                       vst_ref[vrow[gl], pl.ds(row0 + j * ATT_TK, ATT_TK)]))
            out.append(combine(parts))
        return tuple(out)

    n_far = (qb + 2) // 4
    sel = lax.fori_loop(0, n_far, far_step, tuple(carry0))

    gates = jax.nn.sigmoid(gt_ref[...])
    for gl in groups:
        o_cmp, o_win = o_cmps[gl], o_wins[gl]
        o_sel = finish(sel[gl][1])
        outs = []
        for h in range(HPG):
            lanes = slice(h * QB, (h + 1) * QB)
            r = gl * 16 + 3 * h
            outs.append(gates[r:r + 1, :] * o_cmp[:, lanes]
                        + gates[r + 1:r + 2, :] * o_sel[:, lanes]
                        + gates[r + 2:r + 3, :] * o_win[:, lanes])
        for pair in range(HPG // 2):
            two = jnp.concatenate(outs[2 * pair:2 * pair + 2], axis=0)
            c0 = gl * HPG * HD + pair * 2 * HD
            out_ref[:, c0:c0 + 2 * HD] = two.T.astype(out_ref.dtype)


def _overlap_matrix_t(nblk):
    start = np.arange(NCMP_PAD) * CMP_STRIDE
    sj = np.arange(nblk) * SLC_BLOCK
    ov = (np.minimum(start[None, :] + CMP_BLOCK, sj[:, None] + SLC_BLOCK)
          - np.maximum(start[None, :], sj[:, None]))
    ov = np.clip(ov, 0, None) / CMP_BLOCK
    ov[:, NCMP_PAD - 1] = 0.0
    return np.concatenate([ov, ov], axis=1).astype(np.float32)


def nsa_attn(qt, gt, kc, vct, ks, kw, vst, vwt, bias_ct, tiles, far, batch, seq, gps=ATT_GPS):
    QB, HD, HPG, G = ATT_QB, NSA_HD, NSA_HPG, NSA_GROUPS
    nblk = seq // SLC_BLOCK
    assert nblk % 8 == 0 and seq % QB == 0 and seq // CMP_STRIDE == NCMP_PAD and G % gps == 0
    assert WINDOW % ATT_TK == 0 and FAR_CHUNK == 4 * ATT_TK and KV_PAD >= FAR_CHUNK - ATT_TK
    assert KEY_BLK % 8 == 0 and KEY_BLK + nblk <= KEY_PAD and KEY_PAD + 8 <= 128 - HD
    nq = seq // QB
    sp = seq + KV_PAD
    ovt = jnp.asarray(_overlap_matrix_t(nblk), dtype=BF16)
    return pl.pallas_call(
        functools.partial(_nsa_body, gps),
        grid=(batch, G // gps, nq),
        in_specs=[pl.BlockSpec((None, None, gps * HPG * HD, QB), lambda b, g, i: (b, i, g, 0)),
                  pl.BlockSpec((None, None, gps * 16, QB), lambda b, g, i: (b, i, g, 0)),
                  pl.BlockSpec((None, NCMP_PAD, gps * 128), lambda b, g, i: (b, 0, g)),
                  pl.BlockSpec((None, gps * HD, NCMP_PAD), lambda b, g, i: (b, g, 0)),
                  pl.BlockSpec((None, sp, gps * 128), lambda b, g, i: (b, 0, g)),
                  pl.BlockSpec((None, sp, gps * 128), lambda b, g, i: (b, 0, g)),
                  pl.BlockSpec((None, gps * VT_ROWS, sp), lambda b, g, i: (b, g, 0)),
                  pl.BlockSpec((None, gps * VT_ROWS, sp), lambda b, g, i: (b, g, 0)),
                  pl.BlockSpec((None, gps * HPG, NCMP_PAD, QB), lambda b, g, i: (i, g, 0, 0)),
                  pl.BlockSpec((gps * HPG, 3, ATT_TK, QB), lambda b, g, i: (g, 0, 0, 0)),
                  pl.BlockSpec((gps * HPG, 8, 128), lambda b, g, i: (g, 0, 0)),
                  pl.BlockSpec((nblk, 2 * NCMP_PAD), lambda b, g, i: (0, 0))],
        out_specs=pl.BlockSpec((QB, gps * HPG * HD), lambda b, g, i: (b * nq + i, g)),
        out_shape=jax.ShapeDtypeStruct((batch * seq, NSA_HEADS * HD), BF16),
        compiler_params=_cparams(("parallel", "parallel", "arbitrary")),
        name="nsa_attn",
    )(qt, gt, kc, vct, ks, kw, vst, vwt, bias_ct, tiles, far, ovt)


def _mlstm_layer(x2d, g_mix, sh, sc, w_stack, layer, b_if, g_out, batch, seq):
    q, k, v, o, gates = mlstm_proj(x2d, g_mix, sh, sc, w_stack, layer, b_if)
    return mlstm_core(q, k, v, o, gates, g_out, batch, seq)


def _nsa_shared(x2d, g_kv, kv_sh, kv_sc, w_kv, pos_k, w_k1, w_k2, pos_v, w_v1, w_v2,
                g_knorm, batch, seq):
    gw = NSA_GROUPS * NSA_HD
    part = lambda i: w_kv[:, i * gw:(i + 1) * gw]
    hd = NSA_HD
    wk = jnp.concatenate([p[:, g * hd:(g + 1) * hd] for g in range(NSA_GROUPS)
                          for p in (part(2), part(4))], axis=1).astype(BF16)
    gk = jnp.tile(jnp.concatenate([g_knorm[1], g_knorm[2]]), NSA_GROUPS).reshape(1, -1)
    wvt = jnp.concatenate([part(3), part(5)], axis=1).T.astype(BF16)
    wcc = jnp.concatenate([part(0), part(1)], axis=1).astype(BF16)
    ks, kw, vst, vwt, cc = kv_proj(x2d, g_kv, kv_sh, kv_sc, wk, wvt, wcc, gk, batch, seq)
    pos = jnp.stack([pos_k, pos_v])
    w1 = jnp.stack([w_k1, w_v1])
    kc, vct = compress(cc, pos, w1, w_k2.astype(BF16), w_v2.T.astype(BF16), g_knorm[0:1],
                       batch, seq)
    return kc, vct, ks, kw, vst, vwt


def _gate_weights_t(w_q, b_gate):
    nq = NSA_HEADS * NSA_HD
    per = 3 * NSA_HPG
    wg = w_q[:, nq:].T.reshape(NSA_GROUPS, per, -1)
    wg = jnp.pad(wg, ((0, 0), (0, 16 - per), (0, 0))).reshape(NSA_GROUPS * 16, -1)
    bg = jnp.pad(b_gate.reshape(NSA_GROUPS, per), ((0, 0), (0, 16 - per))).reshape(-1, 1)
    return wg.astype(BF16), bg


def _nsa_layer(x2d, g_mix, sh, sc, shared, w_q, b_gate, g_qnorm, bias_ct, tiles, far, batch, seq):
    nq = NSA_HEADS * NSA_HD
    wgt, bg = _gate_weights_t(w_q, b_gate)
    qt, gt = q_proj(x2d, g_mix, sh, sc, w_q[:, :nq].T.astype(BF16), wgt, bg,
                    g_qnorm.reshape(NSA_HD, 1), batch, seq)
    kc, vct, ks, kw, vst, vwt = shared
    return nsa_attn(qt, gt, kc, vct, ks, kw, vst, vwt, bias_ct, tiles, far, batch, seq)


def kernel(x, c, w_ada, b_ada, g_norm_mix, g_norm_ffn, w_ffn_in, w_ffn_out, w_a_in, b_a_if, g_a_out, w_a_out, w_kv_ada, b_kv_ada, g_kv_norm, w_kv, pos_cmp_k, w_cmp_k1, w_cmp_k2, pos_cmp_v, w_cmp_v1, w_cmp_v2, g_knorm, w_b_q, b_b_gate, g_qnorm, w_b_out, rel_table):
    B, S, D = x.shape
    depth = w_ada.shape[0]
    n_a = w_a_in.shape[0]
    x2d = x.reshape(B * S, D)
    mods = ada_mod(c, w_ada, b_ada)
    kv_mod = ada_mod(c, w_kv_ada[None], b_kv_ada[None])[0]
    shared = None
    bias_ct = tiles = far = None
    for layer in range(depth):
        sh1, sc1, ga1, sh2, sc2, ga2 = [mods[layer, :, i * D:(i + 1) * D].reshape(B, 1, D)
                                        for i in range(6)]
        if layer < n_a:
            mixed = _mlstm_layer(x2d, g_norm_mix[layer], sh1, sc1, w_a_in, layer, b_a_if[layer],
                                 g_a_out[layer], B, S)
            w_mix = (w_a_out, layer)
        else:
            j = layer - n_a
            if shared is None:
                kv_sh = kv_mod[:, :D].reshape(B, 1, D)
                kv_sc = kv_mod[:, D:].reshape(B, 1, D)
                shared = _nsa_shared(x2d, g_kv_norm, kv_sh, kv_sc, w_kv, pos_cmp_k, w_cmp_k1,
                                     w_cmp_k2, pos_cmp_v, w_cmp_v1, w_cmp_v2, g_knorm, B, S)
                bias_ct, tiles, far = bias_prep(rel_table, S)
            mixed = _nsa_layer(x2d, g_norm_mix[layer], sh1, sc1, shared, w_b_q[j], b_b_gate[j],
                               g_qnorm[j], bias_ct, tiles, far, B, S)
            w_mix = (w_b_out, j)
        x2d = mix_ffn(mixed, w_mix, x2d, ga1, g_norm_ffn[layer], sh2, sc2, ga2,
                      (w_ffn_in, layer), (w_ffn_out, layer))
    return x2d.reshape(B, S, D)
```
